```python
import math
import jax
import jax.numpy as jnp
from jax import lax
import numpy as np

D_MODEL = 1024
BATCH = 2
SEQ = 8192
DEPTH = 1
DEC_BATCH = 128
DEC_SEQ = 4
PAST_LEN = 8192
PAGE_SIZE = 128

SSD_HEADS = 8
SSD_HEAD_DIM = 64
SSD_INNER = SSD_HEADS * SSD_HEAD_DIM
SSD_GROUPS = 2
SSD_STATE = 128
CONV_WIDTH = 4
SSD_CHUNK = 128
CONV_DIM = SSD_INNER + 2 * SSD_GROUPS * SSD_STATE
MLA_HEADS = 8
QK_NOPE = 64
QK_ROPE = 32
V_DIM = 64
KV_RANK = 256
Q_RANK = 384
MLA_INNER = MLA_HEADS * V_DIM
ROPE_THETA = 10000.0
ATTN_SCALE = 1.0 / math.sqrt(QK_NOPE + QK_ROPE)
Q_BLOCK = 128
MIX_WIDTH = SSD_INNER + MLA_INNER
D_FF = 4 * D_MODEL
IN_PROJ_DIM = SSD_INNER + CONV_DIM + SSD_HEADS + Q_RANK + KV_RANK + QK_ROPE
EPS = 1e-6

kernel_name = "hymba_ssd_mla_adaln_decode_step"


def rmsnorm(x, g):
    xf = x.astype(jnp.float32)
    y = xf * lax.rsqrt(jnp.mean(xf * xf, axis=-1, keepdims=True) + EPS)
    return (y * g.astype(jnp.float32)).astype(x.dtype)


def modulate(x, g, shift, scale):
    return rmsnorm(x, g) * (1.0 + scale[:, None, :]) + shift[:, None, :]


def rope_tables(pos):
    inv = 1.0 / (ROPE_THETA ** (jnp.arange(0, QK_ROPE, 2, dtype=jnp.float32) / QK_ROPE))
    ang = pos.astype(jnp.float32)[:, None] * inv[None, :]
    return jnp.cos(ang), jnp.sin(ang)


def apply_rope(x, cos, sin):
    x1, x2 = jnp.split(x.astype(jnp.float32), 2, axis=-1)
    return jnp.concatenate([x1 * cos - x2 * sin, x1 * sin + x2 * cos], axis=-1).astype(x.dtype)


def causal_conv(xp, w, b):
    y = lax.conv_general_dilated(xp, w[:, None, :], window_strides=(1,), padding="VALID",
                                 dimension_numbers=("NWC", "WIO", "NWC"), feature_group_count=CONV_DIM)
    return y + b


def ssd_scan(xs, dt, a, bm, cm, h0, chunk):
    b, L, H, P = xs.shape
    nc = L // chunk
    rep = H // SSD_GROUPS
    f32 = jnp.float32
    xc = xs.astype(f32).reshape(b, nc, chunk, H, P)
    bc = jnp.repeat(bm.astype(f32), rep, axis=2).reshape(b, nc, chunk, H, SSD_STATE)
    cc = jnp.repeat(cm.astype(f32), rep, axis=2).reshape(b, nc, chunk, H, SSD_STATE)
    dtc = dt.reshape(b, nc, chunk, H)
    acs = jnp.cumsum(dtc * a, axis=2)
    seg = acs[:, :, :, None, :] - acs[:, :, None, :, :]
    causal = jnp.tril(jnp.ones((chunk, chunk), bool))[None, None, :, :, None]
    decay = jnp.exp(jnp.where(causal, seg, -jnp.inf))
    scores = jnp.einsum("bcihn,bcjhn->bcijh", cc, bc) * decay * dtc[:, :, None, :, :]
    y_diag = jnp.einsum("bcijh,bcjhp->bcihp", scores, xc)
    to_end = jnp.exp(acs[:, :, -1:, :] - acs) * dtc
    s_chunk = jnp.einsum("bcjhn,bcjhp->bchpn", bc, xc * to_end[..., None])
    chunk_decay = jnp.exp(acs[:, :, -1, :])

    def step(h, inp):
        dec, s = inp
        return dec[:, :, None, None] * h + s, h

    h_fin, h_prev = lax.scan(step, h0.astype(f32),
                             (jnp.moveaxis(chunk_decay, 1, 0), jnp.moveaxis(s_chunk, 1, 0)))
    h_prev = jnp.moveaxis(h_prev, 0, 1)
    y_off = jnp.einsum("bcihn,bchpn->bcihp", cc, h_prev) * jnp.exp(acs)[..., None]
    return (y_diag + y_off).reshape(b, L, H, P), h_fin


def ssd_mixer(z, xbc_pad, dt_raw, h0, lw):
    b, L, _ = z.shape
    xbc = jax.nn.silu(causal_conv(xbc_pad, lw["conv_w"], lw["conv_b"]))
    xs, bm, cm = jnp.split(xbc, [SSD_INNER, SSD_INNER + SSD_GROUPS * SSD_STATE], axis=-1)
    xs = xs.reshape(b, L, SSD_HEADS, SSD_HEAD_DIM)
    bm = bm.reshape(b, L, SSD_GROUPS, SSD_STATE)
    cm = cm.reshape(b, L, SSD_GROUPS, SSD_STATE)
    dt = jax.nn.softplus(dt_raw.astype(jnp.float32) + lw["dt_bias"].astype(jnp.float32))
    a = -jnp.exp(lw["a_log"].astype(jnp.float32))
    chunk = SSD_CHUNK if L % SSD_CHUNK == 0 else L
    y, h_fin = ssd_scan(xs, dt, a, bm, cm, h0, chunk)
    y = y + lw["d_skip"].astype(jnp.float32)[:, None] * xs.astype(jnp.float32)
    y = y.reshape(b, L, SSD_INNER) * jax.nn.silu(z.astype(jnp.float32))
    return rmsnorm(y, lw["norm_ssd_g"]).astype(z.dtype), h_fin.astype(z.dtype)


def mla_project(q_lat, kv_lat, kr_raw, pos, lw):
    b, L, _ = q_lat.shape
    q = (rmsnorm(q_lat, lw["q_norm_g"]) @ lw["w_uq"]).reshape(b, L, MLA_HEADS, QK_NOPE + QK_ROPE)
    q_nope, q_rope = jnp.split(q, [QK_NOPE], axis=-1)
    cos, sin = rope_tables(pos)
    q_rope = apply_rope(q_rope, cos[None, :, None, :], sin[None, :, None, :])
    kr = apply_rope(kr_raw, cos[None], sin[None])
    ckv = rmsnorm(kv_lat, lw["kv_norm_g"])
    q_abs = jnp.einsum("blhd,rhd->blhr", q_nope, lw["w_uk"])
    return q_abs, q_rope, ckv, kr


def mla_attend_prompt(q_abs, q_rope, ckv, kr):
    b, L = q_abs.shape[:2]
    nb = L // Q_BLOCK
    qa = jnp.moveaxis(q_abs.reshape(b, nb, Q_BLOCK, MLA_HEADS, KV_RANK), 1, 0)
    qr = jnp.moveaxis(q_rope.reshape(b, nb, Q_BLOCK, MLA_HEADS, QK_ROPE), 1, 0)
    kpos = jnp.arange(L, dtype=jnp.int32)

    def one_block(args):
        qa_b, qr_b, start = args
        s = jnp.einsum("bqhr,bkr->bhqk", qa_b, ckv) + jnp.einsum("bqhd,bkd->bhqk", qr_b, kr)
        qpos = start + jnp.arange(Q_BLOCK, dtype=jnp.int32)
        s = jnp.where(kpos[None, :] <= qpos[:, None], s.astype(jnp.float32) * ATTN_SCALE, -jnp.inf)
        p = jax.nn.softmax(s, axis=-1).astype(ckv.dtype)
        return jnp.einsum("bhqk,bkr->bqhr", p, ckv)

    o = lax.map(one_block, (qa, qr, jnp.arange(nb, dtype=jnp.int32) * Q_BLOCK))
    return jnp.moveaxis(o, 0, 1).reshape(b, L, MLA_HEADS, KV_RANK)


def mla_attend_sample(q_abs, q_rope, ckv, kr, ckv_past, kr_past):
    t = q_abs.shape[1]
    past = ckv_past.shape[1]
    s_past = jnp.einsum("bqhr,bkr->bhqk", q_abs, ckv_past) + jnp.einsum("bqhd,bkd->bhqk", q_rope, kr_past)
    s_new = jnp.einsum("bqhr,bkr->bhqk", q_abs, ckv) + jnp.einsum("bqhd,bkd->bhqk", q_rope, kr)
    causal = jnp.tril(jnp.ones((t, t), bool))
    s_new = jnp.where(causal, s_new.astype(jnp.float32) * ATTN_SCALE, -jnp.inf)
    s = jnp.concatenate([s_past.astype(jnp.float32) * ATTN_SCALE, s_new], axis=-1)
    p = jax.nn.softmax(s, axis=-1).astype(ckv.dtype)
    return (jnp.einsum("bhqk,bkr->bqhr", p[..., :past], ckv_past)
            + jnp.einsum("bhqk,bkr->bqhr", p[..., past:], ckv))


def mixer(h, pos, conv_prev, h0, ckv_past, kr_past, lw):
    b, L, _ = h.shape
    proj = h @ lw["w_in"]
    c1 = SSD_INNER
    c2 = c1 + CONV_DIM
    c3 = c2 + SSD_HEADS
    c4 = c3 + Q_RANK
    c5 = c4 + KV_RANK
    z, xbc, dt_raw, q_lat, kv_lat, kr_raw = jnp.split(proj, [c1, c2, c3, c4, c5], axis=-1)
    xbc_pad = jnp.concatenate([conv_prev.astype(xbc.dtype), xbc], axis=1)
    new_conv = xbc_pad[:, L:]
    y_ssd, h_fin = ssd_mixer(z, xbc_pad, dt_raw, h0, lw)
    q_abs, q_rope, ckv, kr = mla_project(q_lat, kv_lat, kr_raw, pos, lw)
    if ckv_past is None:
        o_lat = mla_attend_prompt(q_abs, q_rope, ckv, kr)
    else:
        o_lat = mla_attend_sample(q_abs, q_rope, ckv, kr, ckv_past, kr_past)
    o = jnp.einsum("blhr,rhd->blhd", o_lat, lw["w_uv"]).reshape(b, L, MLA_INNER)
    y_attn = rmsnorm(o, lw["norm_attn_g"])
    mix = jnp.concatenate([y_ssd, y_attn], axis=-1) @ lw["w_out"]
    return mix, ckv, kr, new_conv, h_fin


def layer_block(x, c, pos, conv_prev, h0, ckv_past, kr_past, lw):
    ada = jax.nn.silu(c) @ lw["w_ada"] + lw["b_ada"]
    sh1, sc1, g1, sh2, sc2, g2 = jnp.split(ada, 6, axis=-1)
    mix, ckv, kr, new_conv, h_fin = mixer(modulate(x, lw["norm_mix_g"], sh1, sc1), pos,
                                          conv_prev, h0, ckv_past, kr_past, lw)
    x = x + g1[:, None, :] * mix
    u = jax.nn.relu(modulate(x, lw["norm_mlp_g"], sh2, sc2) @ lw["w_up"])
    x = x + g2[:, None, :] * (jnp.square(u) @ lw["w_down"])
    return x, ckv, kr, new_conv, h_fin


def final_norm(x, c, w_ada_final, b_ada_final, norm_final_g):
    ada = jax.nn.silu(c) @ w_ada_final + b_ada_final
    shift, scale = jnp.split(ada, 2, axis=-1)
    return modulate(x, norm_final_g, shift, scale)


def setup_inputs(seed: int = 0) -> dict:
    key = jax.random.key(seed)
    ks = jax.random.split(key, 32)
    f32 = jnp.float32
    n_pages = PAST_LEN // PAGE_SIZE
    n_phys = (DEC_BATCH * n_pages * 5) // 4

    def nrm(k, shape, scale):
        return jax.random.normal(k, shape, f32) * scale

    def gain(k, n):
        return 1.0 + nrm(k, (DEPTH, n), 0.02)

    x_prompt = nrm(ks[0], (BATCH, SEQ, D_MODEL), 1.0)
    x_sample = nrm(ks[1], (DEC_BATCH, DEC_SEQ, D_MODEL), 1.0)
    cache_kv_latent = nrm(ks[2], (DEPTH, n_phys, PAGE_SIZE, KV_RANK), 1.0)
    cache_k_rope = nrm(ks[3], (DEPTH, n_phys, PAGE_SIZE, QK_ROPE), 1.0)
    state_conv = nrm(ks[4], (DEPTH, DEC_BATCH, CONV_WIDTH - 1, CONV_DIM), 1.0)
    state_ssm = nrm(ks[5], (DEPTH, DEC_BATCH, SSD_HEADS, SSD_HEAD_DIM, SSD_STATE), 0.5)
    page_table = jax.random.permutation(ks[6], n_phys)[: DEC_BATCH * n_pages].reshape(DEC_BATCH, n_pages).astype(jnp.int32)
    c_prompt = nrm(ks[7], (BATCH, D_MODEL), 1.0)
    c_sample = nrm(ks[8], (DEC_BATCH, D_MODEL), 1.0)
    w_ada = nrm(ks[9], (DEPTH, D_MODEL, 6 * D_MODEL), 0.5 * D_MODEL ** -0.5)
    b_ada = nrm(ks[10], (DEPTH, 6 * D_MODEL), 0.02)
    norm_mix_g = gain(ks[11], D_MODEL)
    w_in = nrm(ks[12], (DEPTH, D_MODEL, IN_PROJ_DIM), D_MODEL ** -0.5)
    conv_w = nrm(ks[13], (DEPTH, CONV_WIDTH, CONV_DIM), CONV_WIDTH ** -0.5)
    conv_b = nrm(ks[14], (DEPTH, CONV_DIM), 0.02)
    dt0 = jnp.exp(jax.random.uniform(ks[15], (DEPTH, SSD_HEADS), f32, math.log(1e-3), math.log(1e-1)))
    dt_bias = dt0 + jnp.log(-jnp.expm1(-dt0))
    a_log = jnp.log(jax.random.uniform(ks[16], (DEPTH, SSD_HEADS), f32, 1.0, 16.0))
    d_skip = gain(ks[17], SSD_HEADS)
    norm_ssd_g = gain(ks[18], SSD_INNER)
    q_norm_g = gain(ks[19], Q_RANK)
    kv_norm_g = gain(ks[20], KV_RANK)
    w_uq = nrm(ks[21], (DEPTH, Q_RANK, MLA_HEADS * (QK_NOPE + QK_ROPE)), Q_RANK ** -0.5)
    w_uk = nrm(ks[22], (DEPTH, KV_RANK, MLA_HEADS, QK_NOPE), KV_RANK ** -0.5)
    w_uv = nrm(ks[23], (DEPTH, KV_RANK, MLA_HEADS, V_DIM), KV_RANK ** -0.5)
    norm_attn_g = gain(ks[24], MLA_INNER)
    w_out = nrm(ks[25], (DEPTH, MIX_WIDTH, D_MODEL), MIX_WIDTH ** -0.5)
    norm_mlp_g = gain(ks[26], D_MODEL)
    w_up = nrm(ks[27], (DEPTH, D_MODEL, D_FF), D_MODEL ** -0.5)
    w_down = nrm(ks[28], (DEPTH, D_FF, D_MODEL), D_FF ** -0.5)
    w_ada_final = nrm(ks[29], (D_MODEL, 2 * D_MODEL), 0.5 * D_MODEL ** -0.5)
    b_ada_final = nrm(ks[30], (2 * D_MODEL,), 0.02)
    norm_final_g = 1.0 + nrm(ks[31], (D_MODEL,), 0.02)
    return {"x_prompt": x_prompt, "x_sample": x_sample,
            "cache_kv_latent": cache_kv_latent, "cache_k_rope": cache_k_rope,
            "state_conv": state_conv, "state_ssm": state_ssm, "page_table": page_table,
            "c_prompt": c_prompt, "c_sample": c_sample,
            "w_ada": w_ada, "b_ada": b_ada, "norm_mix_g": norm_mix_g, "w_in": w_in,
            "conv_w": conv_w, "conv_b": conv_b, "dt_bias": dt_bias, "a_log": a_log,
            "d_skip": d_skip, "norm_ssd_g": norm_ssd_g, "q_norm_g": q_norm_g,
            "kv_norm_g": kv_norm_g, "w_uq": w_uq, "w_uk": w_uk, "w_uv": w_uv,
            "norm_attn_g": norm_attn_g, "w_out": w_out, "norm_mlp_g": norm_mlp_g,
            "w_up": w_up, "w_down": w_down, "w_ada_final": w_ada_final,
            "b_ada_final": b_ada_final, "norm_final_g": norm_final_g}


def reference(x_prompt, x_sample, cache_kv_latent, cache_k_rope, state_conv, state_ssm, page_table,
              c_prompt, c_sample, w_ada, b_ada, norm_mix_g, w_in, conv_w, conv_b, dt_bias, a_log,
              d_skip, norm_ssd_g, q_norm_g, kv_norm_g, w_uq, w_uk, w_uv, norm_attn_g, w_out,
              norm_mlp_g, w_up, w_down, w_ada_final, b_ada_final, norm_final_g):
    b_p, seq = x_prompt.shape[0], x_prompt.shape[1]
    n_seq = page_table.shape[0]
    past_len = page_table.shape[1] * PAGE_SIZE
    pos_p = jnp.arange(seq, dtype=jnp.int32)
    pos_s = past_len + jnp.arange(x_sample.shape[1], dtype=jnp.int32)
    xp, xs = x_prompt, x_sample
    kvp, krp, cvp, ssp = [], [], [], []
    kvs, krs, cvs, sss = [], [], [], []
    for l in range(DEPTH):
        lw = {"w_ada": w_ada[l], "b_ada": b_ada[l], "norm_mix_g": norm_mix_g[l], "w_in": w_in[l],
              "conv_w": conv_w[l], "conv_b": conv_b[l], "dt_bias": dt_bias[l], "a_log": a_log[l],
              "d_skip": d_skip[l], "norm_ssd_g": norm_ssd_g[l], "q_norm_g": q_norm_g[l],
              "kv_norm_g": kv_norm_g[l], "w_uq": w_uq[l], "w_uk": w_uk[l], "w_uv": w_uv[l],
              "norm_attn_g": norm_attn_g[l], "w_out": w_out[l], "norm_mlp_g": norm_mlp_g[l],
              "w_up": w_up[l], "w_down": w_down[l]}
        conv0 = jnp.zeros((b_p, CONV_WIDTH - 1, CONV_DIM), x_prompt.dtype)
        h0 = jnp.zeros((b_p, SSD_HEADS, SSD_HEAD_DIM, SSD_STATE), jnp.float32)
        xp, ckv, kr, cv, hs = layer_block(xp, c_prompt, pos_p, conv0, h0, None, None, lw)
        kvp.append(ckv)
        krp.append(kr)
        cvp.append(cv)
        ssp.append(hs)
        ckv_past = cache_kv_latent[l, page_table].reshape(n_seq, past_len, KV_RANK)
        kr_past = cache_k_rope[l, page_table].reshape(n_seq, past_len, QK_ROPE)
        xs, ckv, kr, cv, hs = layer_block(xs, c_sample, pos_s, state_conv[l], state_ssm[l],
                                          ckv_past, kr_past, lw)
        kvs.append(ckv)
        krs.append(kr)
        cvs.append(cv)
        sss.append(hs)
    y_prompt = final_norm(xp, c_prompt, w_ada_final, b_ada_final, norm_final_g)
    y_sample = final_norm(xs, c_sample, w_ada_final, b_ada_final, norm_final_g)
    return (y_prompt, y_sample, jnp.stack(kvp), jnp.stack(krp), jnp.stack(cvp), jnp.stack(ssp),
            jnp.stack(kvs), jnp.stack(krs), jnp.stack(cvs), jnp.stack(sss))
```

```python
import functools
import math

import jax
import jax.numpy as jnp
from jax import lax
from jax.experimental import pallas as pl
from jax.experimental.pallas import tpu as pltpu

F32 = jnp.float32
BF16 = jnp.bfloat16
SDS = jax.ShapeDtypeStruct

D_MODEL = 1024
SSD_HEADS = 8
SSD_HEAD_DIM = 64
SSD_INNER = SSD_HEADS * SSD_HEAD_DIM
SSD_GROUPS = 2
SSD_STATE = 128
CONV_WIDTH = 4
SSD_CHUNK = 128
CONV_DIM = SSD_INNER + 2 * SSD_GROUPS * SSD_STATE
MLA_HEADS = 8
QK_NOPE = 64
QK_ROPE = 32
V_DIM = 64
KV_RANK = 256
Q_RANK = 384
MLA_INNER = MLA_HEADS * V_DIM
ROPE_THETA = 10000.0
ATTN_SCALE = 1.0 / math.sqrt(QK_NOPE + QK_ROPE)
PAGE_SIZE = 128
D_FF = 4 * D_MODEL
EPS = 1e-6

LANES = 128
C_Z = 0
C_XBC = C_Z + SSD_INNER
C_QLAT = C_XBC + CONV_DIM
C_KVLAT = C_QLAT + Q_RANK
C_MISC = C_KVLAT + KV_RANK
W_IN_COLS = C_MISC + LANES
MISC_KR = 32
MISC_KRSW = 64
C_QROPE = MLA_HEADS * KV_RANK
C_QROPE_SW = C_QROPE + MLA_HEADS * QK_ROPE
WQ_COLS = C_QROPE_SW + MLA_HEADS * QK_ROPE

VMEM_LIMIT = 52 * 1024 * 1024


def _dot(a, b):
    return jnp.dot(a, b, preferred_element_type=F32)


def _dot_nt(a, b):
    return lax.dot_general(a, b, (((1,), (1,)), ((), ())), preferred_element_type=F32)


def _dot_tn(a, b):
    return lax.dot_general(a, b, (((0,), (0,)), ((), ())), preferred_element_type=F32)


def _silu(x):
    return x * jax.nn.sigmoid(x)


def _rms(x, g):
    return x * lax.rsqrt(jnp.mean(x * x, axis=-1, keepdims=True) + EPS) * g


def _split3_dot(mask_bf16, v):
    v1 = v.astype(BF16)
    r1 = v - v1.astype(F32)
    v2 = r1.astype(BF16)
    v3 = (r1 - v2.astype(F32)).astype(BF16)
    return _dot(mask_bf16, v1) + _dot(mask_bf16, v2) + _dot(mask_bf16, v3)


def _params(sem, vmem=VMEM_LIMIT):
    return pltpu.CompilerParams(dimension_semantics=sem, vmem_limit_bytes=vmem)


def _ada_body(c_ref, w_ref, b_ref, o_ref):
    s = _silu(c_ref[...]).astype(BF16)
    o_ref[...] = _dot(s, w_ref[...].astype(BF16)) + b_ref[...]


def _ada(c, w, b):
    bsz, d = c.shape
    n = w.shape[1]
    tn = 1024
    return pl.pallas_call(
        _ada_body,
        grid=(n // tn,),
        in_specs=[pl.BlockSpec((bsz, d), lambda j: (0, 0)),
                  pl.BlockSpec((d, tn), lambda j: (0, j)),
                  pl.BlockSpec((1, tn), lambda j: (0, j))],
        out_specs=pl.BlockSpec((bsz, tn), lambda j: (0, j)),
        out_shape=SDS((bsz, n), F32),
        compiler_params=_params(("arbitrary",)),
        name="ada",
    )(c, w, b.reshape(1, n))


def _fold_body(wq_ref, wk_ref, o_ref):
    a = wq_ref[:, 0:QK_NOPE].astype(BF16)
    o_ref[...] = _dot_nt(a, wk_ref[...].astype(BF16)).astype(BF16)


def _fold(wq_h, wk_h):
    return pl.pallas_call(
        _fold_body,
        grid=(MLA_HEADS,),
        in_specs=[pl.BlockSpec((None, Q_RANK, QK_NOPE + QK_ROPE), lambda h: (h, 0, 0)),
                  pl.BlockSpec((None, KV_RANK, QK_NOPE), lambda h: (h, 0, 0))],
        out_specs=pl.BlockSpec((Q_RANK, KV_RANK), lambda h: (0, h)),
        out_shape=SDS((Q_RANK, MLA_HEADS * KV_RANK), BF16),
        compiler_params=_params(("arbitrary",)),
        name="fold",
    )(wq_h, wk_h)


def _inproj_body(x_ref, sh_ref, sc_ref, gmix_ref, win_ref, qg_ref, kvg_ref, wq_ref, tq_ref, tk_ref,
                 z_ref, xbc_ref, dt_ref, ckv_ref, kc_ref, kr_ref, krb_ref, qa_ref, qr_ref, *, head_major):
    h = _rms(x_ref[...], gmix_ref[...]) * (1.0 + sc_ref[...]) + sh_ref[...]
    proj = _dot(h.astype(BF16), win_ref[...])
    z_ref[...] = proj[:, C_Z:C_XBC]
    xbc_ref[...] = proj[:, C_XBC:C_QLAT]
    qn = _rms(proj[:, C_QLAT:C_KVLAT], qg_ref[...]) * ATTN_SCALE
    ckv = _rms(proj[:, C_KVLAT:C_MISC], kvg_ref[...])
    ckv_ref[...] = ckv
    kc_ref[...] = ckv.astype(BF16)
    misc = proj[:, C_MISC:W_IN_COLS]
    lane = lax.broadcasted_iota(jnp.int32, misc.shape, 1)
    dt_ref[...] = jnp.where(lane < SSD_HEADS, misc, 0.0)
    pr = misc * tk_ref[...]
    kr = pr[:, MISC_KR:MISC_KR + QK_ROPE] + pr[:, MISC_KRSW:MISC_KRSW + QK_ROPE]
    kr_ref[...] = kr
    krb_ref[...] = kr.astype(BF16)
    q = _dot(qn.astype(BF16), wq_ref[...])
    tq = tq_ref[...]
    nr = MLA_HEADS * QK_ROPE
    qrot = (q[:, C_QROPE:C_QROPE_SW] * tq[:, :nr] + q[:, C_QROPE_SW:WQ_COLS] * tq[:, nr:]).astype(BF16)
    if head_major:
        for hh in range(MLA_HEADS):
            qa_ref[hh] = q[:, hh * KV_RANK:(hh + 1) * KV_RANK].astype(BF16)
            qr_ref[hh] = qrot[:, hh * QK_ROPE:(hh + 1) * QK_ROPE]
    else:
        qa_ref[...] = q[:, :C_QROPE].astype(BF16)
        qr_ref[...] = qrot


def _inproj(x3, mod3, gmix, win, qg, kvg, wq, tabq, tabk, *, head_major, tm):
    nb, L, d = x3.shape
    per_tok = mod3.shape[1] != 1
    tm = min(tm, L)
    nt = L // tm

    def mod_spec(k):
        if per_tok:
            return pl.BlockSpec((None, tm, d), lambda b, i: (b, i, k))
        return pl.BlockSpec((None, 1, d), lambda b, i: (b, 0, k))

    def tok(width):
        return pl.BlockSpec((None, tm, width), lambda b, i: (b, i, 0))

    def const(a):
        return pl.BlockSpec(a.shape, lambda b, i: (0,) * a.ndim)

    nr = MLA_HEADS * QK_ROPE
    if head_major:
        qa_spec = pl.BlockSpec((None, MLA_HEADS, tm, KV_RANK), lambda b, i: (b, 0, i, 0))
        qr_spec = pl.BlockSpec((None, MLA_HEADS, tm, QK_ROPE), lambda b, i: (b, 0, i, 0))
        qa_shape = SDS((nb, MLA_HEADS, L, KV_RANK), BF16)
        qr_shape = SDS((nb, MLA_HEADS, L, QK_ROPE), BF16)
    else:
        qa_spec, qr_spec = tok(C_QROPE), tok(nr)
        qa_shape = SDS((nb, L, C_QROPE), BF16)
        qr_shape = SDS((nb, L, nr), BF16)
    return pl.pallas_call(
        functools.partial(_inproj_body, head_major=head_major),
        grid=(nb, nt),
        in_specs=[tok(d), mod_spec(0), mod_spec(1), const(gmix), const(win), const(qg), const(kvg), const(wq),
                  pl.BlockSpec((tm, 2 * nr), lambda b, i: (i, 0)),
                  pl.BlockSpec((tm, LANES), lambda b, i: (i, 0))],
        out_specs=[tok(SSD_INNER), tok(CONV_DIM), tok(LANES), tok(KV_RANK), tok(KV_RANK), tok(QK_ROPE),
                   tok(QK_ROPE), qa_spec, qr_spec],
        out_shape=[SDS((nb, L, SSD_INNER), F32), SDS((nb, L, CONV_DIM), F32), SDS((nb, L, LANES), F32),
                   SDS((nb, L, KV_RANK), F32), SDS((nb, L, KV_RANK), BF16), SDS((nb, L, QK_ROPE), F32),
                   SDS((nb, L, QK_ROPE), BF16), qa_shape, qr_shape],
        compiler_params=_params(("arbitrary", "arbitrary")),
        name="inproj_hm" if head_major else "inproj_tm",
    )(x3, mod3, mod3, gmix, win, qg, kvg, wq, tabq, tabk)


def _ssd_body(*refs, Q, seg, carry):
    if carry:
        (xin_ref, z_ref, dt_ref, cw_ref, cb_ref, dtb_ref, alog_ref, dsk_ref, g_ref,
         y_ref, hfin_ref, xbuf, hT) = refs
    else:
        (xin_ref, dt_ref, cw_ref, cb_ref, dtb_ref, alog_ref, dsk_ref,
         ypre_ref, eacs_ref, xw_ref, dec_ref, bm_ref, cm_ref) = refs
    cw = cw_ref[...]
    acc = jnp.broadcast_to(cb_ref[...], (Q, CONV_DIM))
    if carry:
        c = pl.program_id(1)

        @pl.when(c == 0)
        def _():
            xbuf[0:8, :] = jnp.zeros((8, CONV_DIM), F32)
            hT[...] = jnp.zeros(hT.shape, F32)

        xbuf[8:8 + Q, :] = xin_ref[...]
        for k in range(CONV_WIDTH):
            acc = acc + cw[k:k + 1, :] * xbuf[pl.ds(8 - (CONV_WIDTH - 1) + k, Q), :]
        xbuf[0:8, :] = xbuf[Q:Q + 8, :]
    else:
        for k in range(CONV_WIDTH):
            acc = acc + cw[k:k + 1, :] * xin_ref[k]
    xc = _silu(acc)
    xs = xc[:, :SSD_INNER]
    gs = SSD_GROUPS * SSD_STATE
    bm = xc[:, SSD_INNER:SSD_INNER + gs]
    cm = xc[:, SSD_INNER + gs:]
    bm_b = bm.astype(BF16)
    cm_b = cm.astype(BF16)

    lane = lax.broadcasted_iota(jnp.int32, (Q, LANES), 1)
    v = dt_ref[...] + dtb_ref[...]
    dt = jnp.maximum(v, 0.0) + jnp.log1p(jnp.exp(-jnp.abs(v)))
    dt = jnp.where(lane < SSD_HEADS, dt, 0.0)
    dA = dt * (-jnp.exp(alog_ref[...]))
    ri = lax.broadcasted_iota(jnp.int32, (Q, Q), 0)
    ci = lax.broadcasted_iota(jnp.int32, (Q, Q), 1)
    if seg == Q:
        mask = ci <= ri
    else:
        same = (ri // seg) == (ci // seg)
        mask = jnp.logical_and(same, ci <= ri)
    acs = _split3_dot(jnp.where(mask, 1.0, 0.0).astype(BF16), dA)
    if seg == Q:
        acs_last = acs[Q - 1:Q, :]
    else:
        acs_last = _split3_dot(jnp.where(same, 1.0, 0.0).astype(BF16), dA)
    to_end = jnp.exp(acs_last - acs) * dt
    acsT = acs.T
    dtT = dt.T

    G = [_dot_nt(cm_b[:, g * SSD_STATE:(g + 1) * SSD_STATE], bm_b[:, g * SSD_STATE:(g + 1) * SSD_STATE])
         for g in range(SSD_GROUPS)]
    lane_lo = lane < SSD_HEAD_DIM
    heads_per_group = SSD_HEADS // SSD_GROUPS
    ypairs, epairs, xwpairs, decpairs = [], [], [], []
    for k in range(SSD_HEADS // 2):
        g = (2 * k) // heads_per_group
        xp = xs[:, k * LANES:(k + 1) * LANES]
        xhalf = (jnp.where(lane_lo, xp, 0.0).astype(BF16), jnp.where(lane_lo, 0.0, xp).astype(BF16))
        yk = jnp.zeros((Q, LANES), F32)
        for s in range(2):
            hh = 2 * k + s
            segm = acs[:, hh:hh + 1] - acsT[hh:hh + 1, :]
            m = G[g] * jnp.exp(jnp.where(mask, segm, -jnp.inf)) * dtT[hh:hh + 1, :]
            yk = yk + _dot(m.astype(BF16), xhalf[s])

        def pair(a):
            return jnp.where(lane_lo[:a.shape[0]], a[:, 2 * k:2 * k + 1], a[:, 2 * k + 1:2 * k + 2])

        e_p = jnp.exp(pair(acs))
        xw = xp * pair(to_end)
        dec = jnp.exp(pair(acs_last))
        if carry:
            h_prev = hT[k]
            yk = yk + _dot(cm_b[:, g * SSD_STATE:(g + 1) * SSD_STATE], h_prev.astype(BF16)) * e_p
            hT[k] = dec * h_prev + _dot_tn(bm_b[:, g * SSD_STATE:(g + 1) * SSD_STATE], xw.astype(BF16))
        else:
            epairs.append(e_p)
            xwpairs.append(xw)
            decpairs.append(dec)
        ypairs.append(yk)
    y = jnp.concatenate(ypairs, axis=1) + dsk_ref[...] * xs
    if carry:
        y = y * _silu(z_ref[...])
        y_ref[...] = _rms(y, g_ref[...]).astype(BF16)

        @pl.when(c == pl.num_programs(1) - 1)
        def _():
            for k in range(SSD_HEADS // 2):
                hfin_ref[k * LANES:(k + 1) * LANES, :] = hT[k].T
    else:
        ypre_ref[...] = y
        eacs_ref[...] = jnp.concatenate(epairs, axis=1)
        xw_ref[...] = jnp.concatenate(xwpairs, axis=1)
        dec_ref[...] = jnp.concatenate(decpairs, axis=1)
        bm_ref[...] = bm
        cm_ref[...] = cm


def _row(a, n):
    return jnp.pad(a.reshape(1, -1).astype(F32), ((0, 0), (0, n - a.size)))


def _ssd_prompt(xbc, z, dt, cw, cb, dtb, alog, dsk, g):
    nb, L, _ = xbc.shape
    Q = SSD_CHUNK if L % SSD_CHUNK == 0 else L
    nc = L // Q

    def tok(width):
        return pl.BlockSpec((None, Q, width), lambda b, c: (b, c, 0))

    def const(a):
        return pl.BlockSpec(a.shape, lambda b, c: (0,) * a.ndim)

    consts = (cw, cb, dtb, alog, dsk, g)
    return pl.pallas_call(
        functools.partial(_ssd_body, Q=Q, seg=Q, carry=True),
        grid=(nb, nc),
        in_specs=[tok(CONV_DIM), tok(SSD_INNER), tok(LANES)] + [const(a) for a in consts],
        out_specs=[tok(SSD_INNER), pl.BlockSpec((None, SSD_INNER, SSD_STATE), lambda b, c: (b, 0, 0))],
        out_shape=[SDS((nb, L, SSD_INNER), BF16), SDS((nb, SSD_INNER, SSD_STATE), F32)],
        scratch_shapes=[pltpu.VMEM((Q + 8, CONV_DIM), F32), pltpu.VMEM((SSD_HEADS // 2, SSD_STATE, LANES), F32)],
        compiler_params=_params(("arbitrary", "arbitrary")),
        name="ssd_prompt",
    )(xbc, z, dt, *consts)


def _ssd_sample(xsh, dt, cw, cb, dtb, alog, dsk, *, seg):
    _, T, _ = xsh.shape
    consts = (cw, cb, dtb, alog, dsk)

    def full(a):
        return pl.BlockSpec(a.shape, lambda i: (0,) * a.ndim)

    outs = [SDS((T, SSD_INNER), F32)] * 4 + [SDS((T, SSD_GROUPS * SSD_STATE), F32)] * 2
    return pl.pallas_call(
        functools.partial(_ssd_body, Q=T, seg=seg, carry=False),
        grid=(1,),
        in_specs=[full(xsh), full(dt)] + [full(a) for a in consts],
        out_specs=[full(o) for o in outs],
        out_shape=outs,
        compiler_params=_params(("arbitrary",)),
        name="ssd_sample",
    )(xsh, dt, *consts)


def _sstate_body(cm_ref, bm_ref, ypre_ref, eacs_ref, xw_ref, dec_ref, z_ref, s0_ref, g_ref, y_ref, sn_ref):
    s0 = s0_ref[...]
    s0b = s0.astype(BF16)
    cm = cm_ref[...].astype(BF16)
    bm = bm_ref[...].astype(BF16)
    rows = SSD_INNER // SSD_GROUPS
    yo = jnp.concatenate(
        [jnp.einsum("btn,bqn->btq", cm[:, :, g * SSD_STATE:(g + 1) * SSD_STATE],
                    s0b[:, g * rows:(g + 1) * rows, :], preferred_element_type=F32)
         for g in range(SSD_GROUPS)], axis=-1)
    y = (ypre_ref[...] + yo * eacs_ref[...]) * _silu(z_ref[...])
    y_ref[...] = _rms(y, g_ref[...]).astype(BF16)
    dec = dec_ref[...]
    hi = dec.astype(BF16)
    lo = (dec - hi.astype(F32)).astype(BF16)
    sel = jnp.where(lax.broadcasted_iota(jnp.int32, (dec.shape[0], dec.shape[1], SSD_STATE), 1) == 0,
                    1.0, 0.0).astype(BF16)
    dmat = (jnp.einsum("bjq,bjn->bqn", hi, sel, preferred_element_type=F32)
            + jnp.einsum("bjq,bjn->bqn", lo, sel, preferred_element_type=F32))
    xw = xw_ref[...].astype(BF16)
    upd = jnp.concatenate(
        [jnp.einsum("bjq,bjn->bqn", xw[:, :, g * rows:(g + 1) * rows],
                    bm[:, :, g * SSD_STATE:(g + 1) * SSD_STATE], preferred_element_type=F32)
         for g in range(SSD_GROUPS)], axis=1)
    sn_ref[...] = dmat * s0 + upd


def _sstate(cm, bm, ypre, eacs, xw, dec, z, s0, g, *, bs):
    nseq, t, _ = cm.shape
    bs = min(bs, nseq)

    def blk(a):
        return pl.BlockSpec((bs,) + a.shape[1:], lambda i: (i, 0, 0))

    ins = (cm, bm, ypre, eacs, xw, dec, z, s0)
    return pl.pallas_call(
        _sstate_body,
        grid=(nseq // bs,),
        in_specs=[blk(a) for a in ins] + [pl.BlockSpec(g.shape, lambda i: (0, 0))],
        out_specs=[blk(ypre), blk(s0)],
        out_shape=[SDS(ypre.shape, BF16), SDS(s0.shape, F32)],
        compiler_params=_params(("arbitrary",)),
        name="sstate",
    )(*ins, g)


def _attn_body(qa_ref, qr_ref, kc_ref, kr_ref, wuv_ref, g_ref, o_ref, m_sc, l_sc, acc_sc, *, tq, tk):
    i = pl.program_id(1)
    rows = MLA_HEADS * tq
    q = qa_ref[...].reshape(rows, KV_RANK)
    qr = qr_ref[...].reshape(rows, QK_ROPE)
    m_sc[...] = jnp.full(m_sc.shape, -jnp.inf, F32)
    l_sc[...] = jnp.zeros(l_sc.shape, F32)
    acc_sc[...] = jnp.zeros(acc_sc.shape, F32)

    def step(j, masked):
        k0 = pl.multiple_of(j * tk, tk)
        kc = kc_ref[pl.ds(k0, tk), :]
        kr = kr_ref[pl.ds(k0, tk), :]
        s = _dot_nt(q, kc) + _dot_nt(qr, kr)
        if masked:
            tok = i * tq + (lax.broadcasted_iota(jnp.int32, (rows, tk), 0) & (tq - 1))
            col = k0 + lax.broadcasted_iota(jnp.int32, (rows, tk), 1)
            s = jnp.where(col <= tok, s, -jnp.inf)
        m_prev = m_sc[...]
        m_new = jnp.maximum(m_prev, jnp.max(s, axis=-1, keepdims=True))
        alpha = jnp.exp(m_prev - m_new)
        p = jnp.exp(s - m_new)
        l_sc[...] = alpha * l_sc[...] + jnp.sum(p, axis=-1, keepdims=True)
        acc_sc[...] = alpha * acc_sc[...] + _dot(p.astype(BF16), kc)
        m_sc[...] = m_new

    nfull = (i * tq) // tk

    def body(j, carry):
        step(j, False)
        return carry

    lax.fori_loop(0, nfull, body, 0)
    step(nfull, True)

    inv = 1.0 / l_sc[...]
    o = jnp.zeros((tq, MLA_INNER), F32)
    for hh in range(MLA_HEADS):
        oh = acc_sc[hh * tq:(hh + 1) * tq, :] * inv[hh * tq:(hh + 1) * tq, :]
        o = o + _dot(oh.astype(BF16), wuv_ref[hh])
    o_ref[...] = _rms(o, g_ref[...]).astype(BF16)


def _attn_prompt(qa, qr, kc, krb, wuv_pad, g, *, tq, tk):
    nb, _, L, _ = qa.shape
    assert tq & (tq - 1) == 0 and L % tk == 0 and tk % tq == 0
    rows = MLA_HEADS * tq
    return pl.pallas_call(
        functools.partial(_attn_body, tq=tq, tk=tk),
        grid=(nb, L // tq),
        in_specs=[pl.BlockSpec((None, MLA_HEADS, tq, KV_RANK), lambda b, i: (b, 0, i, 0)),
                  pl.BlockSpec((None, MLA_HEADS, tq, QK_ROPE), lambda b, i: (b, 0, i, 0)),
                  pl.BlockSpec((None, L, KV_RANK), lambda b, i: (b, 0, 0)),
                  pl.BlockSpec((None, L, QK_ROPE), lambda b, i: (b, 0, 0)),
                  pl.BlockSpec(wuv_pad.shape, lambda b, i: (0, 0, 0)),
                  pl.BlockSpec(g.shape, lambda b, i: (0, 0))],
        out_specs=pl.BlockSpec((None, tq, MLA_INNER), lambda b, i: (b, i, 0)),
        out_shape=SDS((nb, L, MLA_INNER), BF16),
        scratch_shapes=[pltpu.VMEM((rows, 1), F32), pltpu.VMEM((rows, 1), F32), pltpu.VMEM((rows, KV_RANK), F32)],
        compiler_params=_params(("arbitrary", "arbitrary")),
        name="attn_prompt",
    )(qa, qr, kc, krb, wuv_pad, g)


def _sattn_body(pt_ref, qa_ref, qr_ref, kn_ref, rn_ref, ckv_hbm, kr_hbm, o_ref, kbuf, rbuf, sem, *, npages, t_new):
    b = pl.program_id(0)
    nseq = pl.num_programs(0)

    def copies(seq_page, slot, p):
        return (pltpu.make_async_copy(ckv_hbm.at[seq_page], kbuf.at[slot, p], sem.at[0, slot]),
                pltpu.make_async_copy(kr_hbm.at[seq_page], rbuf.at[slot, p], sem.at[1, slot]))

    def start_fetch(seq, slot):
        def body(p, carry):
            for cp in copies(pt_ref[seq * npages + p], slot, p):
                cp.start()
            return carry
        lax.fori_loop(0, npages, body, 0)

    def wait_fetch(slot):
        def body(p, carry):
            for cp in copies(0, slot, p):
                cp.wait()
            return carry
        lax.fori_loop(0, npages, body, 0)

    @pl.when(b == 0)
    def _():
        start_fetch(0, 0)

    @pl.when(b + 1 < nseq)
    def _():
        start_fetch(b + 1, (b + 1) % 2)

    slot = b % 2
    wait_fetch(slot)
    past = npages * PAGE_SIZE
    kp = kbuf[slot].reshape(past, KV_RANK).astype(BF16)
    rp = rbuf[slot].reshape(past, QK_ROPE).astype(BF16)
    q = qa_ref[...]
    qr = qr_ref[...]
    kn = kn_ref[...]
    rn = rn_ref[...]
    s_p = _dot_nt(q, kp) + _dot_nt(qr, rp)
    s_n = _dot_nt(q, kn) + _dot_nt(qr, rn)
    rows = t_new * MLA_HEADS
    tok = lax.broadcasted_iota(jnp.int32, (rows, t_new), 0) // MLA_HEADS
    col = lax.broadcasted_iota(jnp.int32, (rows, t_new), 1)
    s_n = jnp.where(col <= tok, s_n, -jnp.inf)
    m = jnp.maximum(jnp.max(s_p, axis=-1, keepdims=True), jnp.max(s_n, axis=-1, keepdims=True))
    pp = jnp.exp(s_p - m)
    pn = jnp.exp(s_n - m)
    l = jnp.sum(pp, axis=-1, keepdims=True) + jnp.sum(pn, axis=-1, keepdims=True)
    o = _dot(pp.astype(BF16), kp) + _dot(pn.astype(BF16), kn)
    o_ref[...] = o / l


def _sattn(page_table, qa, qr, kn, rn, cache_kv, cache_kr):
    nseq, npages = page_table.shape
    t_new = kn.shape[1]
    rows = t_new * MLA_HEADS
    grid_spec = pltpu.PrefetchScalarGridSpec(
        num_scalar_prefetch=1,
        grid=(nseq,),
        in_specs=[pl.BlockSpec((rows, KV_RANK), lambda b, pt: (b, 0)),
                  pl.BlockSpec((rows, QK_ROPE), lambda b, pt: (b, 0)),
                  pl.BlockSpec((None, t_new, KV_RANK), lambda b, pt: (b, 0, 0)),
                  pl.BlockSpec((None, t_new, QK_ROPE), lambda b, pt: (b, 0, 0)),
                  pl.BlockSpec(memory_space=pl.ANY),
                  pl.BlockSpec(memory_space=pl.ANY)],
        out_specs=pl.BlockSpec((rows, KV_RANK), lambda b, pt: (b, 0)),
        scratch_shapes=[pltpu.VMEM((2, npages, PAGE_SIZE, KV_RANK), F32),
                        pltpu.VMEM((2, npages, PAGE_SIZE, QK_ROPE), F32),
                        pltpu.SemaphoreType.DMA((2, 2))],
    )
    return pl.pallas_call(
        functools.partial(_sattn_body, npages=npages, t_new=t_new),
        grid_spec=grid_spec,
        out_shape=SDS((nseq * rows, KV_RANK), F32),
        compiler_params=_params(("arbitrary",)),
        name="attn_sample",
    )(page_table.reshape(-1), qa, qr, kn, rn, cache_kv, cache_kr)


def _apost_body(o_ref, wuv_ref, g_ref, y_ref):
    tm = o_ref.shape[0]
    o = jnp.zeros((tm, MLA_INNER), F32)
    for hh in range(MLA_HEADS):
        o = o + _dot(o_ref[:, hh * KV_RANK:(hh + 1) * KV_RANK].astype(BF16), wuv_ref[hh])
    y_ref[...] = _rms(o, g_ref[...]).astype(BF16)


def _apost(o_lat, wuv_pad, g):
    T = o_lat.shape[0]
    return pl.pallas_call(
        _apost_body,
        grid=(1,),
        in_specs=[pl.BlockSpec(o_lat.shape, lambda i: (0, 0)),
                  pl.BlockSpec(wuv_pad.shape, lambda i: (0, 0, 0)),
                  pl.BlockSpec(g.shape, lambda i: (0, 0))],
        out_specs=pl.BlockSpec((T, MLA_INNER), lambda i: (0, 0)),
        out_shape=SDS((T, MLA_INNER), BF16),
        compiler_params=_params(("arbitrary",)),
        name="attn_post",
    )(o_lat, wuv_pad, g)


def _mlp_body(x_ref, ys_ref, ya_ref, g1_ref, sh2_ref, sc2_ref, g2_ref, shf_ref, scf_ref,
              wout_ref, gmlp_ref, wup_ref, wdn_ref, gfin_ref, o_ref, x1_sc, h2_sc, acc_sc, *, final):
    j = pl.program_id(2)

    @pl.when(j == 0)
    def _():
        yy = jnp.concatenate([ys_ref[...], ya_ref[...]], axis=-1)
        x1 = x_ref[...] + g1_ref[...] * _dot(yy, wout_ref[...])
        x1_sc[...] = x1
        h2_sc[...] = (_rms(x1, gmlp_ref[...]) * (1.0 + sc2_ref[...]) + sh2_ref[...]).astype(BF16)
        acc_sc[...] = jnp.zeros(acc_sc.shape, F32)

    u = jnp.maximum(_dot(h2_sc[...], wup_ref[...]), 0.0)
    acc_sc[...] += _dot((u * u).astype(BF16), wdn_ref[...])

    @pl.when(j == pl.num_programs(2) - 1)
    def _():
        x2 = x1_sc[...] + g2_ref[...] * acc_sc[...]
        if final:
            x2 = _rms(x2, gfin_ref[...]) * (1.0 + scf_ref[...]) + shf_ref[...]
        o_ref[...] = x2


def _mlp(x3, ys, ya, mod3, wout, gmlp, wup, wdn, gfin, *, final, tm, tf):
    nb, L, d = x3.shape
    per_tok = mod3.shape[1] != 1
    tm = min(tm, L)
    nj = D_FF // tf

    def mod_spec(k):
        if per_tok:
            return pl.BlockSpec((None, tm, d), lambda b, i, j: (b, i, k))
        return pl.BlockSpec((None, 1, d), lambda b, i, j: (b, 0, k))

    def tok(width):
        return pl.BlockSpec((None, tm, width), lambda b, i, j: (b, i, 0))

    def const(a):
        return pl.BlockSpec(a.shape, lambda b, i, j: (0,) * a.ndim)

    return pl.pallas_call(
        functools.partial(_mlp_body, final=final),
        grid=(nb, L // tm, nj),
        in_specs=[tok(d), tok(SSD_INNER), tok(MLA_INNER)] + [mod_spec(k) for k in (2, 3, 4, 5, 6, 7)]
                 + [const(wout), const(gmlp),
                    pl.BlockSpec((d, tf), lambda b, i, j: (0, j)),
                    pl.BlockSpec((tf, d), lambda b, i, j: (j, 0)),
                    const(gfin)],
        out_specs=tok(d),
        out_shape=SDS((nb, L, d), F32),
        scratch_shapes=[pltpu.VMEM((tm, d), F32), pltpu.VMEM((tm, d), BF16), pltpu.VMEM((tm, d), F32)],
        compiler_params=_params(("arbitrary", "arbitrary", "arbitrary")),
        name="mlp",
    )(x3, ys, ya, mod3, mod3, mod3, mod3, mod3, mod3, wout, gmlp, wup, wdn, gfin)


def _rope_tables(pos):
    inv = 1.0 / (ROPE_THETA ** (jnp.arange(0, QK_ROPE, 2, dtype=F32) / QK_ROPE))
    ang = pos.astype(F32)[:, None] * inv[None, :]
    cos, sin = jnp.cos(ang), jnp.sin(ang)
    c32 = jnp.concatenate([cos, cos], axis=-1)
    s32 = jnp.concatenate([-sin, sin], axis=-1)
    n = pos.shape[0]
    tabq = jnp.concatenate([jnp.tile(c32, (1, MLA_HEADS)), jnp.tile(s32, (1, MLA_HEADS))], axis=-1)
    tabk = jnp.concatenate([jnp.zeros((n, MISC_KR), F32), c32, s32,
                            jnp.zeros((n, LANES - MISC_KRSW - QK_ROPE), F32)], axis=-1)
    return tabq, tabk


def _swap_halves(w):
    half = w.shape[-1] // 2
    return jnp.concatenate([w[..., half:], w[..., :half]], axis=-1)


def kernel(x_prompt, x_sample, cache_kv_latent, cache_k_rope, state_conv, state_ssm, page_table,
           c_prompt, c_sample, w_ada, b_ada, norm_mix_g, w_in, conv_w, conv_b, dt_bias, a_log,
           d_skip, norm_ssd_g, q_norm_g, kv_norm_g, w_uq, w_uk, w_uv, norm_attn_g, w_out,
           norm_mlp_g, w_up, w_down, w_ada_final, b_ada_final, norm_final_g):
    depth = w_in.shape[0]
    b_p, seq, d = x_prompt.shape
    n_seq, t_new, _ = x_sample.shape
    n_tok_s = n_seq * t_new
    past_len = page_table.shape[1] * PAGE_SIZE

    c_all = jnp.concatenate([c_prompt, c_sample], axis=0)
    ada_fin = _ada(c_all, w_ada_final, b_ada_final)
    tabq_p, tabk_p = _rope_tables(jnp.arange(seq, dtype=jnp.int32))
    tabq_s, tabk_s = _rope_tables(past_len + jnp.arange(t_new, dtype=jnp.int32))
    tabq_s = jnp.tile(tabq_s, (n_seq, 1))
    tabk_s = jnp.tile(tabk_s, (n_seq, 1))

    xp = x_prompt
    xs = x_sample.reshape(1, n_tok_s, d)
    outs_p, outs_s = [], []
    for l in range(depth):
        final = l == depth - 1
        wi = w_in[l]
        c1 = SSD_INNER
        c2 = c1 + CONV_DIM
        c3 = c2 + SSD_HEADS
        c4 = c3 + Q_RANK
        c5 = c4 + KV_RANK
        w_kr = wi[:, c5:]
        win = jnp.concatenate(
            [wi[:, :c2], wi[:, c3:c5], wi[:, c2:c3], jnp.zeros((d, MISC_KR - SSD_HEADS), F32), w_kr,
             _swap_halves(w_kr), jnp.zeros((d, LANES - MISC_KRSW - QK_ROPE), F32)], axis=1).astype(BF16)
        wq_h = w_uq[l].reshape(Q_RANK, MLA_HEADS, QK_NOPE + QK_ROPE)
        w_rope = wq_h[:, :, QK_NOPE:]
        wfold = _fold(jnp.transpose(wq_h, (1, 0, 2)), jnp.transpose(w_uk[l], (1, 0, 2)))
        wq = jnp.concatenate([wfold, w_rope.reshape(Q_RANK, -1).astype(BF16),
                              _swap_halves(w_rope).reshape(Q_RANK, -1).astype(BF16)], axis=1)
        wuv_pad = jnp.zeros((MLA_HEADS, KV_RANK, MLA_HEADS, V_DIM), F32)
        wuv_pad = wuv_pad.at[jnp.arange(MLA_HEADS), :, jnp.arange(MLA_HEADS), :].set(
            jnp.transpose(w_uv[l], (1, 0, 2)))
        wuv_pad = wuv_pad.reshape(MLA_HEADS, KV_RANK, MLA_INNER).astype(BF16)
        wout = w_out[l].astype(BF16)
        wup = w_up[l].astype(BF16)
        wdn = w_down[l].astype(BF16)
        gmix = norm_mix_g[l].reshape(1, d)
        gmlp = norm_mlp_g[l].reshape(1, d)
        gfin = norm_final_g.reshape(1, d)
        qg = q_norm_g[l].reshape(1, Q_RANK)
        kvg = kv_norm_g[l].reshape(1, KV_RANK)
        gssd = norm_ssd_g[l].reshape(1, SSD_INNER)
        gattn = norm_attn_g[l].reshape(1, MLA_INNER)
        cw = conv_w[l]
        cb = conv_b[l].reshape(1, CONV_DIM)
        dtb = _row(dt_bias[l], LANES)
        alog = _row(a_log[l], LANES)
        dsk = jnp.repeat(d_skip[l].astype(F32), SSD_HEAD_DIM).reshape(1, SSD_INNER)

        ada = _ada(c_all, w_ada[l], b_ada[l])
        mod = jnp.concatenate([ada, ada_fin], axis=1)
        mod_p = mod[:b_p].reshape(b_p, 1, 8 * d)
        mod_s = jnp.repeat(mod[b_p:], t_new, axis=0).reshape(1, n_tok_s, 8 * d)

        z, xbc, dtr, ckv, kc, kr, krb, qa, qr = _inproj(
            xp, mod_p, gmix, win, qg, kvg, wq, tabq_p, tabk_p, head_major=True, tm=256)
        y_ssd, hfin = _ssd_prompt(xbc, z, dtr, cw, cb, dtb, alog, dsk, gssd)
        y_attn = _attn_prompt(qa, qr, kc, krb, wuv_pad, gattn, tq=128, tk=512)
        xp = _mlp(xp, y_ssd, y_attn, mod_p, wout, gmlp, wup, wdn, gfin, final=final, tm=512, tf=1024)
        tail = min(seq, CONV_WIDTH - 1)
        conv_tail = jnp.concatenate([jnp.zeros((b_p, CONV_WIDTH - 1 - tail, CONV_DIM), F32),
                                     xbc[:, seq - tail:]], axis=1)
        outs_p.append((ckv, kr, conv_tail, hfin.reshape(b_p, SSD_HEADS, SSD_HEAD_DIM, SSD_STATE)))

        z, xbc, dtr, ckv, kc, kr, krb, qa, qr = _inproj(
            xs, mod_s, gmix, win, qg, kvg, wq, tabq_s, tabk_s, head_major=False, tm=256)
        xbc3 = xbc.reshape(n_seq, t_new, CONV_DIM)
        xpad = jnp.concatenate([state_conv[l], xbc3], axis=1)
        xsh = jnp.stack([xpad[:, k:k + t_new].reshape(n_tok_s, CONV_DIM) for k in range(CONV_WIDTH)])
        ypre, eacs, xw, dec, bm, cm = _ssd_sample(xsh, dtr[0], cw, cb, dtb, alog, dsk, seg=t_new)

        def seqs(a):
            return a.reshape(n_seq, t_new, a.shape[-1])

        y_ssd, s_new = _sstate(seqs(cm), seqs(bm), seqs(ypre), seqs(eacs), seqs(xw), seqs(dec), seqs(z[0]),
                               state_ssm[l].reshape(n_seq, SSD_INNER, SSD_STATE), gssd, bs=8)
        o_lat = _sattn(page_table, qa.reshape(n_tok_s * MLA_HEADS, KV_RANK),
                       qr.reshape(n_tok_s * MLA_HEADS, QK_ROPE), seqs(kc[0]), seqs(krb[0]),
                       cache_kv_latent[l], cache_k_rope[l])
        y_attn = _apost(o_lat.reshape(n_tok_s, MLA_HEADS * KV_RANK), wuv_pad, gattn)
        xs = _mlp(xs, y_ssd.reshape(1, n_tok_s, SSD_INNER), y_attn.reshape(1, n_tok_s, MLA_INNER), mod_s,
                  wout, gmlp, wup, wdn, gfin, final=final, tm=n_tok_s, tf=1024)
        outs_s.append((seqs(ckv[0]), seqs(kr[0]), xpad[:, t_new:],
                       s_new.reshape(n_seq, SSD_HEADS, SSD_HEAD_DIM, SSD_STATE)))

    def stack(outs, k):
        return jnp.stack([o[k] for o in outs])

    return (xp, xs.reshape(n_seq, t_new, d),
            stack(outs_p, 0), stack(outs_p, 1), stack(outs_p, 2), stack(outs_p, 3),
            stack(outs_s, 0), stack(outs_s, 1), stack(outs_s, 2), stack(outs_s, 3))
```

```python
import functools
import math

import jax
import jax.numpy as jnp
from jax import lax
from jax.experimental import pallas as pl
from jax.experimental.pallas import tpu as pltpu

F32 = jnp.float32
BF16 = jnp.bfloat16
SDS = jax.ShapeDtypeStruct

D_MODEL = 1024
SSD_HEADS = 8
SSD_HEAD_DIM = 64
SSD_INNER = SSD_HEADS * SSD_HEAD_DIM
SSD_GROUPS = 2
SSD_STATE = 128
CONV_WIDTH = 4
SSD_CHUNK = 128
CONV_DIM = SSD_INNER + 2 * SSD_GROUPS * SSD_STATE
MLA_HEADS = 8
QK_NOPE = 64
QK_ROPE = 32
V_DIM = 64
KV_RANK = 256
Q_RANK = 384
MLA_INNER = MLA_HEADS * V_DIM
ROPE_THETA = 10000.0
ATTN_SCALE = 1.0 / math.sqrt(QK_NOPE + QK_ROPE)
PAGE_SIZE = 128
D_FF = 4 * D_MODEL
EPS = 1e-6

LANES = 128
TQ = 128
QK_PAD = 384
C_Z = 0
C_XBC = C_Z + SSD_INNER
C_QLAT = C_XBC + CONV_DIM
C_KVLAT = C_QLAT + Q_RANK
C_MISC = C_KVLAT + KV_RANK
W_IN_COLS = C_MISC + LANES
MISC_KR = 32
MISC_KRSW = 64
C_QROPE = MLA_HEADS * KV_RANK
C_QROPE_SW = C_QROPE + MLA_HEADS * QK_ROPE
WQ_COLS = C_QROPE_SW + MLA_HEADS * QK_ROPE

VMEM_LIMIT = 52 * 1024 * 1024


def _dot(a, b):
    return jnp.dot(a, b, preferred_element_type=F32)


def _dot_nt(a, b):
    return lax.dot_general(a, b, (((1,), (1,)), ((), ())), preferred_element_type=F32)


def _dot_tn(a, b):
    return lax.dot_general(a, b, (((0,), (0,)), ((), ())), preferred_element_type=F32)


def _silu(x):
    return x * jax.nn.sigmoid(x)


def _rms(x, g):
    return x * lax.rsqrt(jnp.mean(x * x, axis=-1, keepdims=True) + EPS) * g


def _split3_dot(mask_bf16, v):
    v1 = v.astype(BF16)
    r1 = v - v1.astype(F32)
    v2 = r1.astype(BF16)
    v3 = (r1 - v2.astype(F32)).astype(BF16)
    return _dot(mask_bf16, v1) + _dot(mask_bf16, v2) + _dot(mask_bf16, v3)


def _params(sem, vmem=VMEM_LIMIT):
    return pltpu.CompilerParams(dimension_semantics=sem, vmem_limit_bytes=vmem)


def _ada_body(c_ref, w_ref, b_ref, o_ref):
    s = _silu(c_ref[...]).astype(BF16)
    o_ref[...] = _dot(s, w_ref[...].astype(BF16)) + b_ref[...]


def _ada(c, w, b):
    bsz, d = c.shape
    n = w.shape[1]
    tn = 1024
    return pl.pallas_call(
        _ada_body,
        grid=(n // tn,),
        in_specs=[pl.BlockSpec((bsz, d), lambda j: (0, 0)),
                  pl.BlockSpec((d, tn), lambda j: (0, j)),
                  pl.BlockSpec((1, tn), lambda j: (0, j))],
        out_specs=pl.BlockSpec((bsz, tn), lambda j: (0, j)),
        out_shape=SDS((bsz, n), F32),
        compiler_params=_params(("arbitrary",)),
        name="ada",
    )(c, w, b.reshape(1, n))


def _fold_body(wq_ref, wk_ref, o_ref):
    a = wq_ref[:, 0:QK_NOPE].astype(BF16)
    o_ref[...] = _dot_nt(a, wk_ref[...].astype(BF16)).astype(BF16)


def _fold(wq_h, wk_h):
    return pl.pallas_call(
        _fold_body,
        grid=(MLA_HEADS,),
        in_specs=[pl.BlockSpec((None, Q_RANK, QK_NOPE + QK_ROPE), lambda h: (h, 0, 0)),
                  pl.BlockSpec((None, KV_RANK, QK_NOPE), lambda h: (h, 0, 0))],
        out_specs=pl.BlockSpec((Q_RANK, KV_RANK), lambda h: (0, h)),
        out_shape=SDS((Q_RANK, MLA_HEADS * KV_RANK), BF16),
        compiler_params=_params(("arbitrary",)),
        name="fold",
    )(wq_h, wk_h)


def _inproj_body(*refs, q_transposed):
    (x_ref, sh_ref, sc_ref, gmix_ref, win_ref, qg_ref, kvg_ref, wq_ref, tq_ref, tk_ref,
     z_ref, xbc_ref, dt_ref, ckv_ref, kr_ref) = refs[:15]
    h = _rms(x_ref[...], gmix_ref[...]) * (1.0 + sc_ref[...]) + sh_ref[...]
    proj = _dot(h.astype(BF16), win_ref[...])
    tm = proj.shape[0]
    z_ref[...] = proj[:, C_Z:C_XBC]
    xbc_ref[...] = proj[:, C_XBC:C_QLAT]
    qn = (_rms(proj[:, C_QLAT:C_KVLAT], qg_ref[...]) * ATTN_SCALE).astype(BF16)
    ckv = _rms(proj[:, C_KVLAT:C_MISC], kvg_ref[...])
    ckv_ref[...] = ckv
    misc = proj[:, C_MISC:W_IN_COLS]
    lane = lax.broadcasted_iota(jnp.int32, misc.shape, 1)
    dt_ref[...] = jnp.where(lane < SSD_HEADS, misc, 0.0)
    pr = misc * tk_ref[...]
    kr = pr[:, MISC_KR:MISC_KR + QK_ROPE] + pr[:, MISC_KRSW:MISC_KRSW + QK_ROPE]
    kr_ref[...] = kr
    nr = MLA_HEADS * QK_ROPE
    if q_transposed:
        kf_ref, qt_ref = refs[15:]
        npad = QK_PAD - KV_RANK - QK_ROPE
        kf_ref[:, 0:KV_RANK] = ckv.astype(BF16)
        kf_ref[:, KV_RANK:QK_PAD] = jnp.concatenate([kr, jnp.zeros((tm, npad), F32)], axis=1).astype(BF16)
        qt = _dot_nt(wq_ref[...], qn)
        tq = tq_ref[...]
        rot = (qt[C_QROPE:C_QROPE_SW] * tq[:nr] + qt[C_QROPE_SW:WQ_COLS] * tq[nr:]).astype(BF16)
        zpad = jnp.zeros((npad, TQ), BF16)
        for c in range(tm // TQ):
            toks = slice(c * TQ, (c + 1) * TQ)
            for hh in range(MLA_HEADS):
                cols = slice(hh * TQ, (hh + 1) * TQ)
                qt_ref[c, 0:KV_RANK, cols] = qt[hh * KV_RANK:(hh + 1) * KV_RANK, toks].astype(BF16)
                qt_ref[c, KV_RANK:KV_RANK + QK_ROPE, cols] = rot[hh * QK_ROPE:(hh + 1) * QK_ROPE, toks]
                qt_ref[c, KV_RANK + QK_ROPE:QK_PAD, cols] = zpad
    else:
        kc_ref, krb_ref, qa_ref, qr_ref = refs[15:]
        kc_ref[...] = ckv.astype(BF16)
        krb_ref[...] = kr.astype(BF16)
        q = _dot(qn, wq_ref[...])
        tq = tq_ref[...]
        qa_ref[...] = q[:, :C_QROPE].astype(BF16)
        qr_ref[...] = (q[:, C_QROPE:C_QROPE_SW] * tq[:, :nr] + q[:, C_QROPE_SW:WQ_COLS] * tq[:, nr:]).astype(BF16)


def _inproj(x3, mod3, gmix, win, qg, kvg, wq, tabq, tabk, *, q_transposed, tm):
    nb, L, d = x3.shape
    per_tok = mod3.shape[1] != 1
    tm = min(tm, L)
    nt = L // tm

    def mod_spec(k):
        if per_tok:
            return pl.BlockSpec((None, tm, d), lambda b, i: (b, i, k))
        return pl.BlockSpec((None, 1, d), lambda b, i: (b, 0, k))

    def tok(width):
        return pl.BlockSpec((None, tm, width), lambda b, i: (b, i, 0))

    def const(a):
        return pl.BlockSpec(a.shape, lambda b, i: (0,) * a.ndim)

    nr = MLA_HEADS * QK_ROPE
    out_specs = [tok(SSD_INNER), tok(CONV_DIM), tok(LANES), tok(KV_RANK), tok(QK_ROPE)]
    out_shape = [SDS((nb, L, SSD_INNER), F32), SDS((nb, L, CONV_DIM), F32), SDS((nb, L, LANES), F32),
                 SDS((nb, L, KV_RANK), F32), SDS((nb, L, QK_ROPE), F32)]
    if q_transposed:
        assert tm % TQ == 0
        tabq_spec = pl.BlockSpec((2 * nr, tm), lambda b, i: (0, i))
        out_specs += [tok(QK_PAD),
                      pl.BlockSpec((None, tm // TQ, QK_PAD, MLA_HEADS * TQ), lambda b, i: (b, i, 0, 0))]
        out_shape += [SDS((nb, L, QK_PAD), BF16), SDS((nb, L // TQ, QK_PAD, MLA_HEADS * TQ), BF16)]
    else:
        tabq_spec = pl.BlockSpec((tm, 2 * nr), lambda b, i: (i, 0))
        out_specs += [tok(KV_RANK), tok(QK_ROPE), tok(C_QROPE), tok(nr)]
        out_shape += [SDS((nb, L, KV_RANK), BF16), SDS((nb, L, QK_ROPE), BF16),
                      SDS((nb, L, C_QROPE), BF16), SDS((nb, L, nr), BF16)]
    return pl.pallas_call(
        functools.partial(_inproj_body, q_transposed=q_transposed),
        grid=(nb, nt),
        in_specs=[tok(d), mod_spec(0), mod_spec(1), const(gmix), const(win), const(qg), const(kvg), const(wq),
                  tabq_spec, pl.BlockSpec((tm, LANES), lambda b, i: (i, 0))],
        out_specs=out_specs,
        out_shape=out_shape,
        compiler_params=_params(("arbitrary", "arbitrary")),
        name="inproj_prompt" if q_transposed else "inproj_sample",
    )(x3, mod3, mod3, gmix, win, qg, kvg, wq, tabq, tabk)


def _ssd_body(*refs, Q, seg, carry):
    if carry:
        (xin_ref, z_ref, dt_ref, cw_ref, cb_ref, dtb_ref, alog_ref, dsk_ref, g_ref,
         y_ref, hfin_ref, xbuf, hT) = refs
    else:
        (xin_ref, dt_ref, cw_ref, cb_ref, dtb_ref, alog_ref, dsk_ref,
         ypre_ref, eacs_ref, xw_ref, dec_ref, bm_ref, cm_ref) = refs
    cw = cw_ref[...]
    acc = jnp.broadcast_to(cb_ref[...], (Q, CONV_DIM))
    if carry:
        c = pl.program_id(1)

        @pl.when(c == 0)
        def _():
            xbuf[0:8, :] = jnp.zeros((8, CONV_DIM), F32)
            hT[...] = jnp.zeros(hT.shape, F32)

        xbuf[8:8 + Q, :] = xin_ref[...]
        for k in range(CONV_WIDTH):
            acc = acc + cw[k:k + 1, :] * xbuf[pl.ds(8 - (CONV_WIDTH - 1) + k, Q), :]
        xbuf[0:8, :] = xbuf[Q:Q + 8, :]
    else:
        for k in range(CONV_WIDTH):
            acc = acc + cw[k:k + 1, :] * xin_ref[k]
    xc = _silu(acc)
    xs = xc[:, :SSD_INNER]
    gs = SSD_GROUPS * SSD_STATE
    bm = xc[:, SSD_INNER:SSD_INNER + gs]
    cm = xc[:, SSD_INNER + gs:]
    bm_b = bm.astype(BF16)
    cm_b = cm.astype(BF16)

    lane = lax.broadcasted_iota(jnp.int32, (Q, LANES), 1)
    v = dt_ref[...] + dtb_ref[...]
    dt = jnp.maximum(v, 0.0) + jnp.log1p(jnp.exp(-jnp.abs(v)))
    dt = jnp.where(lane < SSD_HEADS, dt, 0.0)
    dA = dt * (-jnp.exp(alog_ref[...]))
    ri = lax.broadcasted_iota(jnp.int32, (Q, Q), 0)
    ci = lax.broadcasted_iota(jnp.int32, (Q, Q), 1)
    if seg == Q:
        mask = ci <= ri
    else:
        same = (ri // seg) == (ci // seg)
        mask = jnp.logical_and(same, ci <= ri)
    acs = _split3_dot(jnp.where(mask, 1.0, 0.0).astype(BF16), dA)
    if seg == Q:
        acs_last = acs[Q - 1:Q, :]
    else:
        acs_last = _split3_dot(jnp.where(same, 1.0, 0.0).astype(BF16), dA)
    to_end = jnp.exp(acs_last - acs) * dt
    acsT = acs.T
    dtT = dt.T

    G = [_dot_nt(cm_b[:, g * SSD_STATE:(g + 1) * SSD_STATE], bm_b[:, g * SSD_STATE:(g + 1) * SSD_STATE])
         for g in range(SSD_GROUPS)]
    lane_lo = lane < SSD_HEAD_DIM
    heads_per_group = SSD_HEADS // SSD_GROUPS
    ypairs, epairs, xwpairs, decpairs = [], [], [], []
    for k in range(SSD_HEADS // 2):
        g = (2 * k) // heads_per_group
        xp = xs[:, k * LANES:(k + 1) * LANES]
        xhalf = (jnp.where(lane_lo, xp, 0.0).astype(BF16), jnp.where(lane_lo, 0.0, xp).astype(BF16))
        yk = jnp.zeros((Q, LANES), F32)
        for s in range(2):
            hh = 2 * k + s
            segm = acs[:, hh:hh + 1] - acsT[hh:hh + 1, :]
            m = G[g] * jnp.exp(jnp.where(mask, segm, -jnp.inf)) * dtT[hh:hh + 1, :]
            yk = yk + _dot(m.astype(BF16), xhalf[s])

        def pair(a):
            return jnp.where(lane_lo[:a.shape[0]], a[:, 2 * k:2 * k + 1], a[:, 2 * k + 1:2 * k + 2])

        e_p = jnp.exp(pair(acs))
        xw = xp * pair(to_end)
        dec = jnp.exp(pair(acs_last))
        if carry:
            h_prev = hT[k]
            yk = yk + _dot(cm_b[:, g * SSD_STATE:(g + 1) * SSD_STATE], h_prev.astype(BF16)) * e_p
            hT[k] = dec * h_prev + _dot_tn(bm_b[:, g * SSD_STATE:(g + 1) * SSD_STATE], xw.astype(BF16))
        else:
            epairs.append(e_p)
            xwpairs.append(xw)
            decpairs.append(dec)
        ypairs.append(yk)
    y = jnp.concatenate(ypairs, axis=1) + dsk_ref[...] * xs
    if carry:
        y = y * _silu(z_ref[...])
        y_ref[...] = _rms(y, g_ref[...]).astype(BF16)

        @pl.when(c == pl.num_programs(1) - 1)
        def _():
            for k in range(SSD_HEADS // 2):
                hfin_ref[k * LANES:(k + 1) * LANES, :] = hT[k].T
    else:
        ypre_ref[...] = y
        eacs_ref[...] = jnp.concatenate(epairs, axis=1)
        xw_ref[...] = jnp.concatenate(xwpairs, axis=1)
        dec_ref[...] = jnp.concatenate(decpairs, axis=1)
        bm_ref[...] = bm
        cm_ref[...] = cm


def _row(a, n):
    return jnp.pad(a.reshape(1, -1).astype(F32), ((0, 0), (0, n - a.size)))


def _ssd_prompt(xbc, z, dt, cw, cb, dtb, alog, dsk, g):
    nb, L, _ = xbc.shape
    Q = SSD_CHUNK if L % SSD_CHUNK == 0 else L
    nc = L // Q

    def tok(width):
        return pl.BlockSpec((None, Q, width), lambda b, c: (b, c, 0))

    def const(a):
        return pl.BlockSpec(a.shape, lambda b, c: (0,) * a.ndim)

    consts = (cw, cb, dtb, alog, dsk, g)
    return pl.pallas_call(
        functools.partial(_ssd_body, Q=Q, seg=Q, carry=True),
        grid=(nb, nc),
        in_specs=[tok(CONV_DIM), tok(SSD_INNER), tok(LANES)] + [const(a) for a in consts],
        out_specs=[tok(SSD_INNER), pl.BlockSpec((None, SSD_INNER, SSD_STATE), lambda b, c: (b, 0, 0))],
        out_shape=[SDS((nb, L, SSD_INNER), BF16), SDS((nb, SSD_INNER, SSD_STATE), F32)],
        scratch_shapes=[pltpu.VMEM((Q + 8, CONV_DIM), F32), pltpu.VMEM((SSD_HEADS // 2, SSD_STATE, LANES), F32)],
        compiler_params=_params(("arbitrary", "arbitrary")),
        name="ssd_prompt",
    )(xbc, z, dt, *consts)


def _ssd_sample(xsh, dt, cw, cb, dtb, alog, dsk, *, seg):
    _, T, _ = xsh.shape
    consts = (cw, cb, dtb, alog, dsk)

    def full(a):
        return pl.BlockSpec(a.shape, lambda i: (0,) * a.ndim)

    outs = [SDS((T, SSD_INNER), F32)] * 4 + [SDS((T, SSD_GROUPS * SSD_STATE), F32)] * 2
    return pl.pallas_call(
        functools.partial(_ssd_body, Q=T, seg=seg, carry=False),
        grid=(1,),
        in_specs=[full(xsh), full(dt)] + [full(a) for a in consts],
        out_specs=[full(o) for o in outs],
        out_shape=outs,
        compiler_params=_params(("arbitrary",)),
        name="ssd_sample",
    )(xsh, dt, *consts)


def _sstate_body(cm_ref, bm_ref, ypre_ref, eacs_ref, xw_ref, dec_ref, z_ref, s0_ref, g_ref, y_ref, sn_ref):
    s0 = s0_ref[...]
    s0b = s0.astype(BF16)
    cm = cm_ref[...].astype(BF16)
    bm = bm_ref[...].astype(BF16)
    rows = SSD_INNER // SSD_GROUPS
    yo = jnp.concatenate(
        [jnp.einsum("btn,bqn->btq", cm[:, :, g * SSD_STATE:(g + 1) * SSD_STATE],
                    s0b[:, g * rows:(g + 1) * rows, :], preferred_element_type=F32)
         for g in range(SSD_GROUPS)], axis=-1)
    y = (ypre_ref[...] + yo * eacs_ref[...]) * _silu(z_ref[...])
    y_ref[...] = _rms(y, g_ref[...]).astype(BF16)
    dec = dec_ref[...]
    hi = dec.astype(BF16)
    lo = (dec - hi.astype(F32)).astype(BF16)
    sel = jnp.where(lax.broadcasted_iota(jnp.int32, (dec.shape[0], dec.shape[1], SSD_STATE), 1) == 0,
                    1.0, 0.0).astype(BF16)
    dmat = (jnp.einsum("bjq,bjn->bqn", hi, sel, preferred_element_type=F32)
            + jnp.einsum("bjq,bjn->bqn", lo, sel, preferred_element_type=F32))
    xw = xw_ref[...].astype(BF16)
    upd = jnp.concatenate(
        [jnp.einsum("bjq,bjn->bqn", xw[:, :, g * rows:(g + 1) * rows],
                    bm[:, :, g * SSD_STATE:(g + 1) * SSD_STATE], preferred_element_type=F32)
         for g in range(SSD_GROUPS)], axis=1)
    sn_ref[...] = dmat * s0 + upd


def _sstate(cm, bm, ypre, eacs, xw, dec, z, s0, g, *, bs):
    nseq, t, _ = cm.shape
    bs = min(bs, nseq)

    def blk(a):
        return pl.BlockSpec((bs,) + a.shape[1:], lambda i: (i, 0, 0))

    ins = (cm, bm, ypre, eacs, xw, dec, z, s0)
    return pl.pallas_call(
        _sstate_body,
        grid=(nseq // bs,),
        in_specs=[blk(a) for a in ins] + [pl.BlockSpec(g.shape, lambda i: (0, 0))],
        out_specs=[blk(ypre), blk(s0)],
        out_shape=[SDS(ypre.shape, BF16), SDS(s0.shape, F32)],
        compiler_params=_params(("arbitrary",)),
        name="sstate",
    )(*ins, g)


def _attn_body(qt_ref, kf_ref, kct_ref, wuvt_ref, g_ref, o_ref, m_sc, l_sc, acc_sc, *, tk, ct):
    i = pl.program_id(1)
    ncols = MLA_HEADS * TQ
    m_sc[...] = jnp.full(m_sc.shape, -jnp.inf, F32)
    l_sc[...] = jnp.zeros(l_sc.shape, F32)
    acc_sc[...] = jnp.zeros(acc_sc.shape, F32)

    def step(j, masked):
        k0 = pl.multiple_of(j * tk, tk)
        kf = kf_ref[pl.ds(k0, tk), :]
        kct = kct_ref[j]
        tiles = [slice(n * ct, (n + 1) * ct) for n in range(ncols // ct)]
        scores = [_dot(kf, qt_ref[:, cs]) for cs in tiles]
        for cs, s in zip(tiles, scores):
            if masked:
                key = k0 + lax.broadcasted_iota(jnp.int32, (tk, ct), 0)
                tok = i * TQ + (lax.broadcasted_iota(jnp.int32, (tk, ct), 1) & (TQ - 1))
                s = jnp.where(key <= tok, s, -jnp.inf)
            m_prev = m_sc[:, cs]
            m_new = jnp.maximum(m_prev, jnp.max(s, axis=0, keepdims=True))
            alpha = jnp.exp(m_prev - m_new)
            p = jnp.exp(s - m_new)
            l_sc[:, cs] = alpha * l_sc[:, cs] + jnp.sum(p, axis=0, keepdims=True)
            acc_sc[:, cs] = alpha * acc_sc[:, cs] + _dot(kct, p.astype(BF16))
            m_sc[:, cs] = m_new

    nfull = (i * TQ) // tk

    def body(j, carry):
        step(j, False)
        return carry

    lax.fori_loop(0, nfull, body, 0)
    step(nfull, True)

    inv = 1.0 / l_sc[...]
    ys = []
    for hh in range(MLA_HEADS):
        cs = slice(hh * TQ, (hh + 1) * TQ)
        ys.append(_dot(wuvt_ref[hh], (acc_sc[:, cs] * inv[:, cs]).astype(BF16)))
    yt = jnp.concatenate(ys, axis=0)
    yt = yt * lax.rsqrt(jnp.mean(yt * yt, axis=0, keepdims=True) + EPS)
    o_ref[...] = (yt.T * g_ref[...]).astype(BF16)


def _attn_prompt(qt, kf, kct, wuvt, g, *, tk, ct):
    nb, nq, _, ncols = qt.shape
    L = nq * TQ
    assert TQ & (TQ - 1) == 0 and L % tk == 0 and tk % TQ == 0 and ct % TQ == 0 and ncols % ct == 0
    return pl.pallas_call(
        functools.partial(_attn_body, tk=tk, ct=ct),
        grid=(nb, nq),
        in_specs=[pl.BlockSpec((None, None, QK_PAD, ncols), lambda b, i: (b, i, 0, 0)),
                  pl.BlockSpec((None, L, QK_PAD), lambda b, i: (b, 0, 0)),
                  pl.BlockSpec((None, L // tk, KV_RANK, tk), lambda b, i: (b, 0, 0, 0)),
                  pl.BlockSpec(wuvt.shape, lambda b, i: (0, 0, 0)),
                  pl.BlockSpec(g.shape, lambda b, i: (0, 0))],
        out_specs=pl.BlockSpec((None, TQ, MLA_INNER), lambda b, i: (b, i, 0)),
        out_shape=SDS((nb, L, MLA_INNER), BF16),
        scratch_shapes=[pltpu.VMEM((1, ncols), F32), pltpu.VMEM((1, ncols), F32),
                        pltpu.VMEM((KV_RANK, ncols), F32)],
        compiler_params=_params(("arbitrary", "arbitrary")),
        name="attn_prompt",
    )(qt, kf, kct, wuvt, g)


def _sattn_body(pt_ref, qa_ref, qr_ref, kn_ref, rn_ref, ckv_hbm, krt_hbm, o_ref, kbuf, rbuf, sem, *, npages, t_new):
    b = pl.program_id(0)
    nseq = pl.num_programs(0)

    def copies(seq_page, slot, p):
        off = pl.multiple_of(p * PAGE_SIZE, PAGE_SIZE)
        return (pltpu.make_async_copy(ckv_hbm.at[seq_page], kbuf.at[slot, p], sem.at[0, slot]),
                pltpu.make_async_copy(krt_hbm.at[seq_page], rbuf.at[slot, :, pl.ds(off, PAGE_SIZE)],
                                      sem.at[1, slot]))

    def start_fetch(seq, slot):
        def body(p, carry):
            for cp in copies(pt_ref[seq * npages + p], slot, p):
                cp.start()
            return carry
        lax.fori_loop(0, npages, body, 0)

    def wait_fetch(slot):
        def body(p, carry):
            for cp in copies(0, slot, p):
                cp.wait()
            return carry
        lax.fori_loop(0, npages, body, 0)

    @pl.when(b == 0)
    def _():
        start_fetch(0, 0)

    @pl.when(b + 1 < nseq)
    def _():
        start_fetch(b + 1, (b + 1) % 2)

    slot = b % 2
    wait_fetch(slot)
    past = npages * PAGE_SIZE
    kp = kbuf[slot].reshape(past, KV_RANK).astype(BF16)
    rpt = rbuf[slot].astype(BF16)
    q = qa_ref[...]
    qr = qr_ref[...]
    kn = kn_ref[...]
    rn = rn_ref[...]
    s_p = _dot_nt(q, kp) + _dot(qr, rpt)
    s_n = _dot_nt(q, kn) + _dot_nt(qr, rn)
    rows = t_new * MLA_HEADS
    tok = lax.broadcasted_iota(jnp.int32, (rows, t_new), 0) // MLA_HEADS
    col = lax.broadcasted_iota(jnp.int32, (rows, t_new), 1)
    s_n = jnp.where(col <= tok, s_n, -jnp.inf)
    m = jnp.maximum(jnp.max(s_p, axis=-1, keepdims=True), jnp.max(s_n, axis=-1, keepdims=True))
    pp = jnp.exp(s_p - m)
    pn = jnp.exp(s_n - m)
    l = jnp.sum(pp, axis=-1, keepdims=True) + jnp.sum(pn, axis=-1, keepdims=True)
    o = _dot(pp.astype(BF16), kp) + _dot(pn.astype(BF16), kn)
    o_ref[...] = o / l


def _sattn(page_table, qa, qr, kn, rn, cache_kv, cache_krt):
    nseq, npages = page_table.shape
    t_new = kn.shape[1]
    rows = t_new * MLA_HEADS
    grid_spec = pltpu.PrefetchScalarGridSpec(
        num_scalar_prefetch=1,
        grid=(nseq,),
        in_specs=[pl.BlockSpec((rows, KV_RANK), lambda b, pt: (b, 0)),
                  pl.BlockSpec((rows, QK_ROPE), lambda b, pt: (b, 0)),
                  pl.BlockSpec((None, t_new, KV_RANK), lambda b, pt: (b, 0, 0)),
                  pl.BlockSpec((None, t_new, QK_ROPE), lambda b, pt: (b, 0, 0)),
                  pl.BlockSpec(memory_space=pl.ANY),
                  pl.BlockSpec(memory_space=pl.ANY)],
        out_specs=pl.BlockSpec((rows, KV_RANK), lambda b, pt: (b, 0)),
        scratch_shapes=[pltpu.VMEM((2, npages, PAGE_SIZE, KV_RANK), F32),
                        pltpu.VMEM((2, QK_ROPE, npages * PAGE_SIZE), F32),
                        pltpu.SemaphoreType.DMA((2, 2))],
    )
    return pl.pallas_call(
        functools.partial(_sattn_body, npages=npages, t_new=t_new),
        grid_spec=grid_spec,
        out_shape=SDS((nseq * rows, KV_RANK), F32),
        compiler_params=_params(("arbitrary",)),
        name="attn_sample",
    )(page_table.reshape(-1), qa, qr, kn, rn, cache_kv, cache_krt)


def _apost_body(o_ref, wuv_ref, g_ref, y_ref):
    tm = o_ref.shape[0]
    o = jnp.zeros((tm, MLA_INNER), F32)
    for hh in range(MLA_HEADS):
        o = o + _dot(o_ref[:, hh * KV_RANK:(hh + 1) * KV_RANK].astype(BF16), wuv_ref[hh])
    y_ref[...] = _rms(o, g_ref[...]).astype(BF16)


def _apost(o_lat, wuv_pad, g):
    T = o_lat.shape[0]
    return pl.pallas_call(
        _apost_body,
        grid=(1,),
        in_specs=[pl.BlockSpec(o_lat.shape, lambda i: (0, 0)),
                  pl.BlockSpec(wuv_pad.shape, lambda i: (0, 0, 0)),
                  pl.BlockSpec(g.shape, lambda i: (0, 0))],
        out_specs=pl.BlockSpec((T, MLA_INNER), lambda i: (0, 0)),
        out_shape=SDS((T, MLA_INNER), BF16),
        compiler_params=_params(("arbitrary",)),
        name="attn_post",
    )(o_lat, wuv_pad, g)


def _mlp_body(x_ref, ys_ref, ya_ref, g1_ref, sh2_ref, sc2_ref, g2_ref, shf_ref, scf_ref,
              wout_ref, gmlp_ref, wup_ref, wdn_ref, gfin_ref, o_ref, x1_sc, h2_sc, acc_sc, *, final):
    j = pl.program_id(2)

    @pl.when(j == 0)
    def _():
        yy = jnp.concatenate([ys_ref[...], ya_ref[...]], axis=-1)
        x1 = x_ref[...] + g1_ref[...] * _dot(yy, wout_ref[...])
        x1_sc[...] = x1
        h2_sc[...] = (_rms(x1, gmlp_ref[...]) * (1.0 + sc2_ref[...]) + sh2_ref[...]).astype(BF16)
        acc_sc[...] = jnp.zeros(acc_sc.shape, F32)

    u = jnp.maximum(_dot(h2_sc[...], wup_ref[...]), 0.0)
    acc_sc[...] += _dot((u * u).astype(BF16), wdn_ref[...])

    @pl.when(j == pl.num_programs(2) - 1)
    def _():
        x2 = x1_sc[...] + g2_ref[...] * acc_sc[...]
        if final:
            x2 = _rms(x2, gfin_ref[...]) * (1.0 + scf_ref[...]) + shf_ref[...]
        o_ref[...] = x2


def _mlp(x3, ys, ya, mod3, wout, gmlp, wup, wdn, gfin, *, final, tm, tf):
    nb, L, d = x3.shape
    per_tok = mod3.shape[1] != 1
    tm = min(tm, L)
    nj = D_FF // tf

    def mod_spec(k):
        if per_tok:
            return pl.BlockSpec((None, tm, d), lambda b, i, j: (b, i, k))
        return pl.BlockSpec((None, 1, d), lambda b, i, j: (b, 0, k))

    def tok(width):
        return pl.BlockSpec((None, tm, width), lambda b, i, j: (b, i, 0))

    def const(a):
        return pl.BlockSpec(a.shape, lambda b, i, j: (0,) * a.ndim)

    return pl.pallas_call(
        functools.partial(_mlp_body, final=final),
        grid=(nb, L // tm, nj),
        in_specs=[tok(d), tok(SSD_INNER), tok(MLA_INNER)] + [mod_spec(k) for k in (2, 3, 4, 5, 6, 7)]
                 + [const(wout), const(gmlp),
                    pl.BlockSpec((d, tf), lambda b, i, j: (0, j)),
                    pl.BlockSpec((tf, d), lambda b, i, j: (j, 0)),
                    const(gfin)],
        out_specs=tok(d),
        out_shape=SDS((nb, L, d), F32),
        scratch_shapes=[pltpu.VMEM((tm, d), F32), pltpu.VMEM((tm, d), BF16), pltpu.VMEM((tm, d), F32)],
        compiler_params=_params(("arbitrary", "arbitrary", "arbitrary")),
        name="mlp",
    )(x3, ys, ya, mod3, mod3, mod3, mod3, mod3, mod3, wout, gmlp, wup, wdn, gfin)


def _rope_tables(pos):
    inv = 1.0 / (ROPE_THETA ** (jnp.arange(0, QK_ROPE, 2, dtype=F32) / QK_ROPE))
    ang = pos.astype(F32)[:, None] * inv[None, :]
    cos, sin = jnp.cos(ang), jnp.sin(ang)
    c32 = jnp.concatenate([cos, cos], axis=-1)
    s32 = jnp.concatenate([-sin, sin], axis=-1)
    n = pos.shape[0]
    tabq = jnp.concatenate([jnp.tile(c32, (1, MLA_HEADS)), jnp.tile(s32, (1, MLA_HEADS))], axis=-1)
    tabk = jnp.concatenate([jnp.zeros((n, MISC_KR), F32), c32, s32,
                            jnp.zeros((n, LANES - MISC_KRSW - QK_ROPE), F32)], axis=-1)
    return tabq, tabk


def _swap_halves(w):
    half = w.shape[-1] // 2
    return jnp.concatenate([w[..., half:], w[..., :half]], axis=-1)


def kernel(x_prompt, x_sample, cache_kv_latent, cache_k_rope, state_conv, state_ssm, page_table,
           c_prompt, c_sample, w_ada, b_ada, norm_mix_g, w_in, conv_w, conv_b, dt_bias, a_log,
           d_skip, norm_ssd_g, q_norm_g, kv_norm_g, w_uq, w_uk, w_uv, norm_attn_g, w_out,
           norm_mlp_g, w_up, w_down, w_ada_final, b_ada_final, norm_final_g):
    depth = w_in.shape[0]
    b_p, seq, d = x_prompt.shape
    n_seq, t_new, _ = x_sample.shape
    n_tok_s = n_seq * t_new
    past_len = page_table.shape[1] * PAGE_SIZE

    c_all = jnp.concatenate([c_prompt, c_sample], axis=0)
    ada_fin = _ada(c_all, w_ada_final, b_ada_final)
    tabq_p, tabk_p = _rope_tables(jnp.arange(seq, dtype=jnp.int32))
    tabq_s, tabk_s = _rope_tables(past_len + jnp.arange(t_new, dtype=jnp.int32))
    tabq_s = jnp.tile(tabq_s, (n_seq, 1))
    tabk_s = jnp.tile(tabk_s, (n_seq, 1))

    xp = x_prompt
    xs = x_sample.reshape(1, n_tok_s, d)
    outs_p, outs_s = [], []
    for l in range(depth):
        final = l == depth - 1
        wi = w_in[l]
        c1 = SSD_INNER
        c2 = c1 + CONV_DIM
        c3 = c2 + SSD_HEADS
        c4 = c3 + Q_RANK
        c5 = c4 + KV_RANK
        w_kr = wi[:, c5:]
        win = jnp.concatenate(
            [wi[:, :c2], wi[:, c3:c5], wi[:, c2:c3], jnp.zeros((d, MISC_KR - SSD_HEADS), F32), w_kr,
             _swap_halves(w_kr), jnp.zeros((d, LANES - MISC_KRSW - QK_ROPE), F32)], axis=1).astype(BF16)
        wq_h = w_uq[l].reshape(Q_RANK, MLA_HEADS, QK_NOPE + QK_ROPE)
        w_rope = wq_h[:, :, QK_NOPE:]
        wfold = _fold(jnp.transpose(wq_h, (1, 0, 2)), jnp.transpose(w_uk[l], (1, 0, 2)))
        wq = jnp.concatenate([wfold, w_rope.reshape(Q_RANK, -1).astype(BF16),
                              _swap_halves(w_rope).reshape(Q_RANK, -1).astype(BF16)], axis=1)
        wuv_pad = jnp.zeros((MLA_HEADS, KV_RANK, MLA_HEADS, V_DIM), F32)
        wuv_pad = wuv_pad.at[jnp.arange(MLA_HEADS), :, jnp.arange(MLA_HEADS), :].set(
            jnp.transpose(w_uv[l], (1, 0, 2)))
        wuv_pad = wuv_pad.reshape(MLA_HEADS, KV_RANK, MLA_INNER).astype(BF16)
        wuvt = jnp.transpose(w_uv[l], (1, 2, 0)).astype(BF16)
        wout = w_out[l].astype(BF16)
        wup = w_up[l].astype(BF16)
        wdn = w_down[l].astype(BF16)
        gmix = norm_mix_g[l].reshape(1, d)
        gmlp = norm_mlp_g[l].reshape(1, d)
        gfin = norm_final_g.reshape(1, d)
        qg = q_norm_g[l].reshape(1, Q_RANK)
        kvg = kv_norm_g[l].reshape(1, KV_RANK)
        gssd = norm_ssd_g[l].reshape(1, SSD_INNER)
        gattn = norm_attn_g[l].reshape(1, MLA_INNER)
        cw = conv_w[l]
        cb = conv_b[l].reshape(1, CONV_DIM)
        dtb = _row(dt_bias[l], LANES)
        alog = _row(a_log[l], LANES)
        dsk = jnp.repeat(d_skip[l].astype(F32), SSD_HEAD_DIM).reshape(1, SSD_INNER)

        ada = _ada(c_all, w_ada[l], b_ada[l])
        mod = jnp.concatenate([ada, ada_fin], axis=1)
        mod_p = mod[:b_p].reshape(b_p, 1, 8 * d)
        mod_s = jnp.repeat(mod[b_p:], t_new, axis=0).reshape(1, n_tok_s, 8 * d)

        z, xbc, dtr, ckv, kr, kf, qt = _inproj(
            xp, mod_p, gmix, win, qg, kvg, wq.T, tabq_p.T, tabk_p, q_transposed=True, tm=256)
        y_ssd, hfin = _ssd_prompt(xbc, z, dtr, cw, cb, dtb, alog, dsk, gssd)
        tk = min(512, seq)
        kct = jnp.swapaxes(kf[:, :, :KV_RANK].reshape(b_p, seq // tk, tk, KV_RANK), -1, -2)
        y_attn = _attn_prompt(qt, kf, kct, wuvt, gattn, tk=tk, ct=512)
        xp = _mlp(xp, y_ssd, y_attn, mod_p, wout, gmlp, wup, wdn, gfin, final=final, tm=512, tf=1024)
        tail = min(seq, CONV_WIDTH - 1)
        conv_tail = jnp.concatenate([jnp.zeros((b_p, CONV_WIDTH - 1 - tail, CONV_DIM), F32),
                                     xbc[:, seq - tail:]], axis=1)
        outs_p.append((ckv, kr, conv_tail, hfin.reshape(b_p, SSD_HEADS, SSD_HEAD_DIM, SSD_STATE)))

        z, xbc, dtr, ckv, kr, kc, krb, qa, qr = _inproj(
            xs, mod_s, gmix, win, qg, kvg, wq, tabq_s, tabk_s, q_transposed=False, tm=256)
        xbc3 = xbc.reshape(n_seq, t_new, CONV_DIM)
        xpad = jnp.concatenate([state_conv[l], xbc3], axis=1)
        xsh = jnp.stack([xpad[:, k:k + t_new].reshape(n_tok_s, CONV_DIM) for k in range(CONV_WIDTH)])
        ypre, eacs, xw, dec, bm, cm = _ssd_sample(xsh, dtr[0], cw, cb, dtb, alog, dsk, seg=t_new)

        def seqs(a):
            return a.reshape(n_seq, t_new, a.shape[-1])

        y_ssd, s_new = _sstate(seqs(cm), seqs(bm), seqs(ypre), seqs(eacs), seqs(xw), seqs(dec), seqs(z[0]),
                               state_ssm[l].reshape(n_seq, SSD_INNER, SSD_STATE), gssd, bs=8)
        o_lat = _sattn(page_table, qa.reshape(n_tok_s * MLA_HEADS, KV_RANK),
                       qr.reshape(n_tok_s * MLA_HEADS, QK_ROPE), seqs(kc[0]), seqs(krb[0]),
                       cache_kv_latent[l], jnp.swapaxes(cache_k_rope[l], -1, -2))
        y_attn = _apost(o_lat.reshape(n_tok_s, MLA_HEADS * KV_RANK), wuv_pad, gattn)
        xs = _mlp(xs, y_ssd.reshape(1, n_tok_s, SSD_INNER), y_attn.reshape(1, n_tok_s, MLA_INNER), mod_s,
                  wout, gmlp, wup, wdn, gfin, final=final, tm=n_tok_s, tf=1024)
        outs_s.append((seqs(ckv[0]), seqs(kr[0]), xpad[:, t_new:],
                       s_new.reshape(n_seq, SSD_HEADS, SSD_HEAD_DIM, SSD_STATE)))

    def stack(outs, k):
        return jnp.stack([o[k] for o in outs])

    return (xp, xs.reshape(n_seq, t_new, d),
            stack(outs_p, 0), stack(outs_p, 1), stack(outs_p, 2), stack(outs_p, 3),
            stack(outs_s, 0), stack(outs_s, 1), stack(outs_s, 2), stack(outs_s, 3))
```

```python
import functools
import math

import jax
import jax.numpy as jnp
from jax import lax
from jax.experimental import pallas as pl
from jax.experimental.pallas import tpu as pltpu

F32 = jnp.float32
BF16 = jnp.bfloat16
SDS = jax.ShapeDtypeStruct

D_MODEL = 1024
SSD_HEADS = 8
SSD_HEAD_DIM = 64
SSD_INNER = SSD_HEADS * SSD_HEAD_DIM
SSD_GROUPS = 2
SSD_STATE = 128
CONV_WIDTH = 4
SSD_CHUNK = 128
CONV_DIM = SSD_INNER + 2 * SSD_GROUPS * SSD_STATE
MLA_HEADS = 8
QK_NOPE = 64
QK_ROPE = 32
V_DIM = 64
KV_RANK = 256
Q_RANK = 384
MLA_INNER = MLA_HEADS * V_DIM
ROPE_THETA = 10000.0
ATTN_SCALE = 1.0 / math.sqrt(QK_NOPE + QK_ROPE)
PAGE_SIZE = 128
D_FF = 4 * D_MODEL
EPS = 1e-6

LANES = 128
TQ = 128
QK_PAD = 384
SPLIT = 2
C_Z = 0
C_XBC = C_Z + SSD_INNER
C_QLAT = C_XBC + CONV_DIM
C_KVLAT = C_QLAT + Q_RANK
C_MISC = C_KVLAT + KV_RANK
W_IN_COLS = C_MISC + LANES
MISC_KR = 32
MISC_KRSW = 64
C_QROPE = MLA_HEADS * KV_RANK
C_QROPE_SW = C_QROPE + MLA_HEADS * QK_ROPE
WQ_COLS = C_QROPE_SW + MLA_HEADS * QK_ROPE

VMEM_LIMIT = 52 * 1024 * 1024


def _dot(a, b):
    return jnp.dot(a, b, preferred_element_type=F32)


def _dot_nt(a, b):
    return lax.dot_general(a, b, (((1,), (1,)), ((), ())), preferred_element_type=F32)


def _dot_tn(a, b):
    return lax.dot_general(a, b, (((0,), (0,)), ((), ())), preferred_element_type=F32)


def _silu(x):
    return x * jax.nn.sigmoid(x)


def _rms(x, g):
    return x * lax.rsqrt(jnp.mean(x * x, axis=-1, keepdims=True) + EPS) * g


def _split3_dot(mask_bf16, v):
    v1 = v.astype(BF16)
    r1 = v - v1.astype(F32)
    v2 = r1.astype(BF16)
    v3 = (r1 - v2.astype(F32)).astype(BF16)
    return _dot(mask_bf16, v1) + _dot(mask_bf16, v2) + _dot(mask_bf16, v3)


def _params(sem, vmem=VMEM_LIMIT):
    return pltpu.CompilerParams(dimension_semantics=sem, vmem_limit_bytes=vmem)


def _ada_body(c_ref, w_ref, b_ref, o_ref):
    s = _silu(c_ref[...]).astype(BF16)
    o_ref[...] = _dot(s, w_ref[...].astype(BF16)) + b_ref[...]


def _ada(c, w, b):
    bsz, d = c.shape
    n = w.shape[1]
    tn = 1024
    return pl.pallas_call(
        _ada_body,
        grid=(n // tn,),
        in_specs=[pl.BlockSpec((bsz, d), lambda j: (0, 0)),
                  pl.BlockSpec((d, tn), lambda j: (0, j)),
                  pl.BlockSpec((1, tn), lambda j: (0, j))],
        out_specs=pl.BlockSpec((bsz, tn), lambda j: (0, j)),
        out_shape=SDS((bsz, n), F32),
        compiler_params=_params(("arbitrary",)),
        name="ada",
    )(c, w, b.reshape(1, n))


def _fold_body(wq_ref, wk_ref, o_ref):
    a = wq_ref[:, 0:QK_NOPE].astype(BF16)
    o_ref[...] = _dot_nt(a, wk_ref[...].astype(BF16)).astype(BF16)


def _fold(wq_h, wk_h):
    return pl.pallas_call(
        _fold_body,
        grid=(MLA_HEADS,),
        in_specs=[pl.BlockSpec((None, Q_RANK, QK_NOPE + QK_ROPE), lambda h: (h, 0, 0)),
                  pl.BlockSpec((None, KV_RANK, QK_NOPE), lambda h: (h, 0, 0))],
        out_specs=pl.BlockSpec((Q_RANK, KV_RANK), lambda h: (0, h)),
        out_shape=SDS((Q_RANK, MLA_HEADS * KV_RANK), BF16),
        compiler_params=_params(("arbitrary",)),
        name="fold",
    )(wq_h, wk_h)


def _inproj_body(*refs, q_transposed):
    (x_ref, sh_ref, sc_ref, gmix_ref, win_ref, qg_ref, kvg_ref, wq_ref, tq_ref, tk_ref,
     z_ref, xbc_ref, dt_ref, ckv_ref, kr_ref) = refs[:15]
    h = _rms(x_ref[...], gmix_ref[...]) * (1.0 + sc_ref[...]) + sh_ref[...]
    proj = _dot(h.astype(BF16), win_ref[...])
    tm = proj.shape[0]
    z_ref[...] = proj[:, C_Z:C_XBC]
    xbc_ref[...] = proj[:, C_XBC:C_QLAT]
    qn = (_rms(proj[:, C_QLAT:C_KVLAT], qg_ref[...]) * ATTN_SCALE).astype(BF16)
    ckv = _rms(proj[:, C_KVLAT:C_MISC], kvg_ref[...])
    ckv_ref[...] = ckv
    misc = proj[:, C_MISC:W_IN_COLS]
    lane = lax.broadcasted_iota(jnp.int32, misc.shape, 1)
    dt_ref[...] = jnp.where(lane < SSD_HEADS, misc, 0.0)
    pr = misc * tk_ref[...]
    kr = pr[:, MISC_KR:MISC_KR + QK_ROPE] + pr[:, MISC_KRSW:MISC_KRSW + QK_ROPE]
    kr_ref[...] = kr
    nr = MLA_HEADS * QK_ROPE
    if q_transposed:
        kf_ref, qt_ref = refs[15:]
        npad = QK_PAD - KV_RANK - QK_ROPE
        kf_ref[:, 0:KV_RANK] = ckv.astype(BF16)
        kf_ref[:, KV_RANK:QK_PAD] = jnp.concatenate([kr, jnp.zeros((tm, npad), F32)], axis=1).astype(BF16)
        qt = _dot_nt(wq_ref[...], qn)
        tq = tq_ref[...]
        cos_t = jnp.concatenate([tq[:QK_ROPE]] * MLA_HEADS, axis=0)
        sin_t = jnp.concatenate([tq[QK_ROPE:]] * MLA_HEADS, axis=0)
        rot = (qt[C_QROPE:C_QROPE_SW] * cos_t + qt[C_QROPE_SW:WQ_COLS] * sin_t).astype(BF16)
        zpad = jnp.zeros((npad, TQ), BF16)
        for c in range(tm // TQ):
            toks = slice(c * TQ, (c + 1) * TQ)
            for hh in range(MLA_HEADS):
                cols = slice(hh * TQ, (hh + 1) * TQ)
                qt_ref[c, 0:KV_RANK, cols] = qt[hh * KV_RANK:(hh + 1) * KV_RANK, toks].astype(BF16)
                qt_ref[c, KV_RANK:KV_RANK + QK_ROPE, cols] = rot[hh * QK_ROPE:(hh + 1) * QK_ROPE, toks]
                qt_ref[c, KV_RANK + QK_ROPE:QK_PAD, cols] = zpad
    else:
        kc_ref, krb_ref, qa_ref, qr_ref = refs[15:]
        kc_ref[...] = ckv.astype(BF16)
        krb_ref[...] = kr.astype(BF16)
        q = _dot(qn, wq_ref[...])
        tq = tq_ref[...]
        qa_ref[...] = q[:, :C_QROPE].astype(BF16)
        qr_ref[...] = (q[:, C_QROPE:C_QROPE_SW] * tq[:, :nr] + q[:, C_QROPE_SW:WQ_COLS] * tq[:, nr:]).astype(BF16)


def _inproj(x3, mod3, gmix, win, qg, kvg, wq, tabq, tabk, *, q_transposed, tm):
    nb, L, d = x3.shape
    per_tok = mod3.shape[1] != 1
    tm = min(tm, L)
    nt = L // tm

    def mod_spec(k):
        if per_tok:
            return pl.BlockSpec((None, tm, d), lambda b, i: (b, i, k))
        return pl.BlockSpec((None, 1, d), lambda b, i: (b, 0, k))

    def tok(width):
        return pl.BlockSpec((None, tm, width), lambda b, i: (b, i, 0))

    def const(a):
        return pl.BlockSpec(a.shape, lambda b, i: (0,) * a.ndim)

    nr = MLA_HEADS * QK_ROPE
    out_specs = [tok(SSD_INNER), tok(CONV_DIM), tok(LANES), tok(KV_RANK), tok(QK_ROPE)]
    out_shape = [SDS((nb, L, SSD_INNER), F32), SDS((nb, L, CONV_DIM), F32), SDS((nb, L, LANES), F32),
                 SDS((nb, L, KV_RANK), F32), SDS((nb, L, QK_ROPE), F32)]
    if q_transposed:
        assert tm % TQ == 0
        tabq_spec = pl.BlockSpec((2 * QK_ROPE, tm), lambda b, i: (0, i))
        out_specs += [tok(QK_PAD),
                      pl.BlockSpec((None, tm // TQ, QK_PAD, MLA_HEADS * TQ), lambda b, i: (b, i, 0, 0))]
        out_shape += [SDS((nb, L, QK_PAD), BF16), SDS((nb, L // TQ, QK_PAD, MLA_HEADS * TQ), BF16)]
    else:
        tabq_spec = pl.BlockSpec((tm, 2 * nr), lambda b, i: (i, 0))
        out_specs += [tok(KV_RANK), tok(QK_ROPE), tok(C_QROPE), tok(nr)]
        out_shape += [SDS((nb, L, KV_RANK), BF16), SDS((nb, L, QK_ROPE), BF16),
                      SDS((nb, L, C_QROPE), BF16), SDS((nb, L, nr), BF16)]
    return pl.pallas_call(
        functools.partial(_inproj_body, q_transposed=q_transposed),
        grid=(nb, nt),
        in_specs=[tok(d), mod_spec(0), mod_spec(1), const(gmix), const(win), const(qg), const(kvg), const(wq),
                  tabq_spec, pl.BlockSpec((tm, LANES), lambda b, i: (i, 0))],
        out_specs=out_specs,
        out_shape=out_shape,
        compiler_params=_params(("arbitrary", "arbitrary")),
        name="inproj_prompt" if q_transposed else "inproj_sample",
    )(x3, mod3, mod3, gmix, win, qg, kvg, wq, tabq, tabk)


def _ssd_body(*refs, Q, seg, carry):
    if carry:
        (xin_ref, z_ref, dt_ref, cw_ref, cb_ref, dtb_ref, alog_ref, dsk_ref, g_ref,
         y_ref, hfin_ref, xbuf, hT) = refs
    else:
        (xin_ref, dt_ref, cw_ref, cb_ref, dtb_ref, alog_ref, dsk_ref,
         ypre_ref, eacs_ref, xw_ref, dec_ref, bm_ref, cm_ref) = refs
    cw = cw_ref[...]
    acc = jnp.broadcast_to(cb_ref[...], (Q, CONV_DIM))
    if carry:
        c = pl.program_id(1)

        @pl.when(c == 0)
        def _():
            xbuf[0:8, :] = jnp.zeros((8, CONV_DIM), F32)
            hT[...] = jnp.zeros(hT.shape, F32)

        xbuf[8:8 + Q, :] = xin_ref[...]
        for k in range(CONV_WIDTH):
            acc = acc + cw[k:k + 1, :] * xbuf[pl.ds(8 - (CONV_WIDTH - 1) + k, Q), :]
        xbuf[0:8, :] = xbuf[Q:Q + 8, :]
    else:
        for k in range(CONV_WIDTH):
            acc = acc + cw[k:k + 1, :] * xin_ref[k]
    xc = _silu(acc)
    xs = xc[:, :SSD_INNER]
    gs = SSD_GROUPS * SSD_STATE
    bm = xc[:, SSD_INNER:SSD_INNER + gs]
    cm = xc[:, SSD_INNER + gs:]
    bm_b = bm.astype(BF16)
    cm_b = cm.astype(BF16)

    lane = lax.broadcasted_iota(jnp.int32, (Q, LANES), 1)
    v = dt_ref[...] + dtb_ref[...]
    dt = jnp.maximum(v, 0.0) + jnp.log1p(jnp.exp(-jnp.abs(v)))
    dt = jnp.where(lane < SSD_HEADS, dt, 0.0)
    dA = dt * (-jnp.exp(alog_ref[...]))
    ri = lax.broadcasted_iota(jnp.int32, (Q, Q), 0)
    ci = lax.broadcasted_iota(jnp.int32, (Q, Q), 1)
    if seg == Q:
        mask = ci <= ri
    else:
        same = (ri // seg) == (ci // seg)
        mask = jnp.logical_and(same, ci <= ri)
    acs = _split3_dot(jnp.where(mask, 1.0, 0.0).astype(BF16), dA)
    if seg == Q:
        acs_last = acs[Q - 1:Q, :]
    else:
        acs_last = _split3_dot(jnp.where(same, 1.0, 0.0).astype(BF16), dA)
    to_end = jnp.exp(acs_last - acs) * dt
    acsT = acs.T
    dtT = dt.T

    G = [_dot_nt(cm_b[:, g * SSD_STATE:(g + 1) * SSD_STATE], bm_b[:, g * SSD_STATE:(g + 1) * SSD_STATE])
         for g in range(SSD_GROUPS)]
    lane_lo = lane < SSD_HEAD_DIM
    heads_per_group = SSD_HEADS // SSD_GROUPS
    ypairs, epairs, xwpairs, decpairs = [], [], [], []
    for k in range(SSD_HEADS // 2):
        g = (2 * k) // heads_per_group
        xp = xs[:, k * LANES:(k + 1) * LANES]
        xhalf = (jnp.where(lane_lo, xp, 0.0).astype(BF16), jnp.where(lane_lo, 0.0, xp).astype(BF16))
        yk = jnp.zeros((Q, LANES), F32)
        for s in range(2):
            hh = 2 * k + s
            segm = acs[:, hh:hh + 1] - acsT[hh:hh + 1, :]
            m = G[g] * jnp.exp(jnp.where(mask, segm, -jnp.inf)) * dtT[hh:hh + 1, :]
            yk = yk + _dot(m.astype(BF16), xhalf[s])

        def pair(a):
            return jnp.where(lane_lo[:a.shape[0]], a[:, 2 * k:2 * k + 1], a[:, 2 * k + 1:2 * k + 2])

        e_p = jnp.exp(pair(acs))
        xw = xp * pair(to_end)
        dec = jnp.exp(pair(acs_last))
        if carry:
            h_prev = hT[k]
            yk = yk + _dot(cm_b[:, g * SSD_STATE:(g + 1) * SSD_STATE], h_prev.astype(BF16)) * e_p
            hT[k] = dec * h_prev + _dot_tn(bm_b[:, g * SSD_STATE:(g + 1) * SSD_STATE], xw.astype(BF16))
        else:
            epairs.append(e_p)
            xwpairs.append(xw)
            decpairs.append(dec)
        ypairs.append(yk)
    y = jnp.concatenate(ypairs, axis=1) + dsk_ref[...] * xs
    if carry:
        y = y * _silu(z_ref[...])
        y_ref[...] = _rms(y, g_ref[...]).astype(BF16)

        @pl.when(c == pl.num_programs(1) - 1)
        def _():
            for k in range(SSD_HEADS // 2):
                hfin_ref[k * LANES:(k + 1) * LANES, :] = hT[k].T
    else:
        ypre_ref[...] = y
        eacs_ref[...] = jnp.concatenate(epairs, axis=1)
        xw_ref[...] = jnp.concatenate(xwpairs, axis=1)
        dec_ref[...] = jnp.concatenate(decpairs, axis=1)
        bm_ref[...] = bm
        cm_ref[...] = cm


def _row(a, n):
    return jnp.pad(a.reshape(1, -1).astype(F32), ((0, 0), (0, n - a.size)))


def _ssd_prompt(xbc, z, dt, cw, cb, dtb, alog, dsk, g):
    nb, L, _ = xbc.shape
    Q = SSD_CHUNK if L % SSD_CHUNK == 0 else L
    nc = L // Q

    def tok(width):
        return pl.BlockSpec((None, Q, width), lambda b, c: (b, c, 0))

    def const(a):
        return pl.BlockSpec(a.shape, lambda b, c: (0,) * a.ndim)

    consts = (cw, cb, dtb, alog, dsk, g)
    return pl.pallas_call(
        functools.partial(_ssd_body, Q=Q, seg=Q, carry=True),
        grid=(nb, nc),
        in_specs=[tok(CONV_DIM), tok(SSD_INNER), tok(LANES)] + [const(a) for a in consts],
        out_specs=[tok(SSD_INNER), pl.BlockSpec((None, SSD_INNER, SSD_STATE), lambda b, c: (b, 0, 0))],
        out_shape=[SDS((nb, L, SSD_INNER), BF16), SDS((nb, SSD_INNER, SSD_STATE), F32)],
        scratch_shapes=[pltpu.VMEM((Q + 8, CONV_DIM), F32), pltpu.VMEM((SSD_HEADS // 2, SSD_STATE, LANES), F32)],
        compiler_params=_params(("arbitrary", "arbitrary")),
        name="ssd_prompt",
    )(xbc, z, dt, *consts)


def _ssd_sample(xsh, dt, cw, cb, dtb, alog, dsk, *, seg):
    _, T, _ = xsh.shape
    consts = (cw, cb, dtb, alog, dsk)

    def full(a):
        return pl.BlockSpec(a.shape, lambda i: (0,) * a.ndim)

    outs = [SDS((T, SSD_INNER), F32)] * 4 + [SDS((T, SSD_GROUPS * SSD_STATE), F32)] * 2
    return pl.pallas_call(
        functools.partial(_ssd_body, Q=T, seg=seg, carry=False),
        grid=(1,),
        in_specs=[full(xsh), full(dt)] + [full(a) for a in consts],
        out_specs=[full(o) for o in outs],
        out_shape=outs,
        compiler_params=_params(("arbitrary",)),
        name="ssd_sample",
    )(xsh, dt, *consts)


def _sstate_body(cm_ref, bm_ref, ypre_ref, eacs_ref, xw_ref, dec_ref, z_ref, s0_ref, g_ref, y_ref, sn_ref):
    s0 = s0_ref[...]
    s0b = s0.astype(BF16)
    cm = cm_ref[...].astype(BF16)
    bm = bm_ref[...].astype(BF16)
    rows = SSD_INNER // SSD_GROUPS
    yo = jnp.concatenate(
        [jnp.einsum("btn,bqn->btq", cm[:, :, g * SSD_STATE:(g + 1) * SSD_STATE],
                    s0b[:, g * rows:(g + 1) * rows, :], preferred_element_type=F32)
         for g in range(SSD_GROUPS)], axis=-1)
    y = (ypre_ref[...] + yo * eacs_ref[...]) * _silu(z_ref[...])
    y_ref[...] = _rms(y, g_ref[...]).astype(BF16)
    dec = dec_ref[...]
    hi = dec.astype(BF16)
    lo = (dec - hi.astype(F32)).astype(BF16)
    sel = jnp.where(lax.broadcasted_iota(jnp.int32, (dec.shape[0], dec.shape[1], SSD_STATE), 1) == 0,
                    1.0, 0.0).astype(BF16)
    dmat = (jnp.einsum("bjq,bjn->bqn", hi, sel, preferred_element_type=F32)
            + jnp.einsum("bjq,bjn->bqn", lo, sel, preferred_element_type=F32))
    xw = xw_ref[...].astype(BF16)
    upd = jnp.concatenate(
        [jnp.einsum("bjq,bjn->bqn", xw[:, :, g * rows:(g + 1) * rows],
                    bm[:, :, g * SSD_STATE:(g + 1) * SSD_STATE], preferred_element_type=F32)
         for g in range(SSD_GROUPS)], axis=1)
    sn_ref[...] = dmat * s0 + upd


def _sstate(cm, bm, ypre, eacs, xw, dec, z, s0, g, *, bs):
    nseq, t, _ = cm.shape
    bs = min(bs, nseq)

    def blk(a):
        return pl.BlockSpec((bs,) + a.shape[1:], lambda i: (i, 0, 0))

    ins = (cm, bm, ypre, eacs, xw, dec, z, s0)
    return pl.pallas_call(
        _sstate_body,
        grid=(nseq // bs,),
        in_specs=[blk(a) for a in ins] + [pl.BlockSpec(g.shape, lambda i: (0, 0))],
        out_specs=[blk(ypre), blk(s0)],
        out_shape=[SDS(ypre.shape, BF16), SDS(s0.shape, F32)],
        compiler_params=_params(("arbitrary",)),
        name="sstate",
    )(*ins, g)


def _attn_body(qt_ref, kf_ref, kct_ref, wuvt_ref, g_ref, o_ref, m_sc, l_sc, acc_sc, s_a, s_b, *, tk, ct):
    i = pl.program_id(1)
    ncols = MLA_HEADS * TQ
    m_sc[...] = jnp.full(m_sc.shape, -jnp.inf, F32)
    l_sc[...] = jnp.zeros(l_sc.shape, F32)
    acc_sc[...] = jnp.zeros(acc_sc.shape, F32)
    tiles = [slice(n * ct, (n + 1) * ct) for n in range(ncols // ct)]

    def scores(j, s_ref):
        kf = kf_ref[pl.ds(pl.multiple_of(j * tk, tk), tk), :]
        for cs in tiles:
            s_ref[:, cs] = _dot(kf, qt_ref[:, cs])

    def softmax_pv(j, s_ref, masked):
        kct = kct_ref[j]
        for cs in tiles:
            s = s_ref[:, cs]
            if masked:
                key = j * tk + lax.broadcasted_iota(jnp.int32, (tk, ct), 0)
                tok = i * TQ + (lax.broadcasted_iota(jnp.int32, (tk, ct), 1) & (TQ - 1))
                s = jnp.where(key <= tok, s, -jnp.inf)
            m_prev = m_sc[:, cs]
            m_new = jnp.maximum(m_prev, jnp.max(s, axis=0, keepdims=True))
            alpha = jnp.exp(m_prev - m_new)
            p = jnp.exp(s - m_new)
            l_sc[:, cs] = alpha * l_sc[:, cs] + jnp.sum(p, axis=0, keepdims=True)
            acc_sc[:, cs] = alpha * acc_sc[:, cs] + _dot(kct, p.astype(BF16))
            m_sc[:, cs] = m_new

    nfull = (i * TQ) // tk
    scores(0, s_a)

    def pair(p, carry):
        scores(2 * p + 1, s_b)
        softmax_pv(2 * p, s_a, False)
        scores(2 * p + 2, s_a)
        softmax_pv(2 * p + 1, s_b, False)
        return carry

    lax.fori_loop(0, nfull // 2, pair, 0)
    odd = (nfull % 2) == 1

    @pl.when(odd)
    def _():
        scores(nfull, s_b)
        softmax_pv(nfull - 1, s_a, False)
        softmax_pv(nfull, s_b, True)

    @pl.when(jnp.logical_not(odd))
    def _():
        softmax_pv(nfull, s_a, True)

    inv = 1.0 / l_sc[...]
    ys = []
    for hh in range(MLA_HEADS):
        cs = slice(hh * TQ, (hh + 1) * TQ)
        ys.append(_dot(wuvt_ref[hh], (acc_sc[:, cs] * inv[:, cs]).astype(BF16)))
    yt = jnp.concatenate(ys, axis=0)
    yt = yt * lax.rsqrt(jnp.mean(yt * yt, axis=0, keepdims=True) + EPS)
    o_ref[...] = (yt.T * g_ref[...]).astype(BF16)


def _attn_prompt(qt, kf, kct, wuvt, g, *, tk, ct):
    nb, nq, _, ncols = qt.shape
    L = nq * TQ
    assert TQ & (TQ - 1) == 0 and L % tk == 0 and tk % TQ == 0 and ct % TQ == 0 and ncols % ct == 0
    return pl.pallas_call(
        functools.partial(_attn_body, tk=tk, ct=ct),
        grid=(nb, nq),
        in_specs=[pl.BlockSpec((None, None, QK_PAD, ncols), lambda b, i: (b, i, 0, 0)),
                  pl.BlockSpec((None, L, QK_PAD), lambda b, i: (b, 0, 0)),
                  pl.BlockSpec((None, L // tk, KV_RANK, tk), lambda b, i: (b, 0, 0, 0)),
                  pl.BlockSpec(wuvt.shape, lambda b, i: (0, 0, 0)),
                  pl.BlockSpec(g.shape, lambda b, i: (0, 0))],
        out_specs=pl.BlockSpec((None, TQ, MLA_INNER), lambda b, i: (b, i, 0)),
        out_shape=SDS((nb, L, MLA_INNER), BF16),
        scratch_shapes=[pltpu.VMEM((1, ncols), F32), pltpu.VMEM((1, ncols), F32),
                        pltpu.VMEM((KV_RANK, ncols), F32),
                        pltpu.VMEM((tk, ncols), F32), pltpu.VMEM((tk, ncols), F32)],
        compiler_params=_params(("arbitrary", "arbitrary")),
        name="attn_prompt",
    )(qt, kf, kct, wuvt, g)


def _sattn_body(pt_ref, qa_ref, qr_ref, kn_ref, rn_ref, ckv_hbm, krt_hbm, o_ref, kbuf, rbuf, sem, *, npages, t_new):
    b = pl.program_id(0)
    nseq = pl.num_programs(0)

    def copies(seq_page, slot, p):
        off = p * PAGE_SIZE
        return (pltpu.make_async_copy(ckv_hbm.at[seq_page], kbuf.at[slot, p], sem.at[0, slot]),
                pltpu.make_async_copy(krt_hbm.at[seq_page], rbuf.at[slot, :, pl.ds(off, PAGE_SIZE)],
                                      sem.at[1, slot]))

    def start_fetch(seq, slot):
        for p in range(npages):
            for cp in copies(pt_ref[seq * npages + p], slot, p):
                cp.start()

    def wait_fetch(slot):
        pltpu.make_async_copy(ckv_hbm.at[pl.ds(0, npages)], kbuf.at[slot], sem.at[0, slot]).wait()
        pltpu.make_async_copy(rbuf.at[slot], rbuf.at[slot], sem.at[1, slot]).wait()

    @pl.when(b == 0)
    def _():
        start_fetch(0, 0)

    @pl.when(b + 1 < nseq)
    def _():
        start_fetch(b + 1, (b + 1) % 2)

    slot = b % 2
    wait_fetch(slot)
    pg = npages // SPLIT
    part = pg * PAGE_SIZE
    kps = [kbuf[slot, pl.ds(h * pg, pg)].reshape(part, KV_RANK).astype(BF16) for h in range(SPLIT)]
    rpt = rbuf[slot].astype(BF16)
    q = qa_ref[...]
    qr = qr_ref[...]
    kn = kn_ref[...]
    rn = rn_ref[...]
    s_main = [_dot_nt(q, kps[h]) for h in range(SPLIT)]
    s_rope = [_dot(qr, rpt[:, h * part:(h + 1) * part]) for h in range(SPLIT)]
    s_p = [a + b for a, b in zip(s_main, s_rope)]
    s_n = _dot_nt(q, kn) + _dot_nt(qr, rn)
    rows = t_new * MLA_HEADS
    tok = lax.broadcasted_iota(jnp.int32, (rows, t_new), 0) // MLA_HEADS
    col = lax.broadcasted_iota(jnp.int32, (rows, t_new), 1)
    s_n = jnp.where(col <= tok, s_n, -jnp.inf)
    m = jnp.max(s_n, axis=-1, keepdims=True)
    for s in s_p:
        m = jnp.maximum(m, jnp.max(s, axis=-1, keepdims=True))
    pn = jnp.exp(s_n - m)
    pp = [jnp.exp(s - m) for s in s_p]
    l = jnp.sum(pn, axis=-1, keepdims=True)
    for p in pp:
        l = l + jnp.sum(p, axis=-1, keepdims=True)
    inv = 1.0 / l
    parts = [_dot(pp[h].astype(BF16), kps[h]) for h in range(SPLIT)]
    o = _dot(pn.astype(BF16), kn) * inv
    for part_o in parts:
        o = o + part_o * inv
    o_ref[...] = o


def _sattn(page_table, qa, qr, kn, rn, cache_kv, cache_krt):
    nseq, npages = page_table.shape
    t_new = kn.shape[1]
    rows = t_new * MLA_HEADS
    grid_spec = pltpu.PrefetchScalarGridSpec(
        num_scalar_prefetch=1,
        grid=(nseq,),
        in_specs=[pl.BlockSpec((rows, KV_RANK), lambda b, pt: (b, 0)),
                  pl.BlockSpec((rows, QK_ROPE), lambda b, pt: (b, 0)),
                  pl.BlockSpec((None, t_new, KV_RANK), lambda b, pt: (b, 0, 0)),
                  pl.BlockSpec((None, t_new, QK_ROPE), lambda b, pt: (b, 0, 0)),
                  pl.BlockSpec(memory_space=pl.ANY),
                  pl.BlockSpec(memory_space=pl.ANY)],
        out_specs=pl.BlockSpec((rows, KV_RANK), lambda b, pt: (b, 0)),
        scratch_shapes=[pltpu.VMEM((2, npages, PAGE_SIZE, KV_RANK), F32),
                        pltpu.VMEM((2, QK_ROPE, npages * PAGE_SIZE), F32),
                        pltpu.SemaphoreType.DMA((2, 2))],
    )
    return pl.pallas_call(
        functools.partial(_sattn_body, npages=npages, t_new=t_new),
        grid_spec=grid_spec,
        out_shape=SDS((nseq * rows, KV_RANK), F32),
        compiler_params=_params(("arbitrary",)),
        name="attn_sample",
    )(page_table.reshape(-1), qa, qr, kn, rn, cache_kv, cache_krt)


def _apost_body(o_ref, wuv_ref, g_ref, y_ref):
    tm = o_ref.shape[0]
    o = jnp.zeros((tm, MLA_INNER), F32)
    for hh in range(MLA_HEADS):
        o = o + _dot(o_ref[:, hh * KV_RANK:(hh + 1) * KV_RANK].astype(BF16), wuv_ref[hh])
    y_ref[...] = _rms(o, g_ref[...]).astype(BF16)


def _apost(o_lat, wuv_pad, g):
    T = o_lat.shape[0]
    return pl.pallas_call(
        _apost_body,
        grid=(1,),
        in_specs=[pl.BlockSpec(o_lat.shape, lambda i: (0, 0)),
                  pl.BlockSpec(wuv_pad.shape, lambda i: (0, 0, 0)),
                  pl.BlockSpec(g.shape, lambda i: (0, 0))],
        out_specs=pl.BlockSpec((T, MLA_INNER), lambda i: (0, 0)),
        out_shape=SDS((T, MLA_INNER), BF16),
        compiler_params=_params(("arbitrary",)),
        name="attn_post",
    )(o_lat, wuv_pad, g)


def _mlp_body(x_ref, ys_ref, ya_ref, g1_ref, sh2_ref, sc2_ref, g2_ref, shf_ref, scf_ref,
              wout_ref, gmlp_ref, wup_ref, wdn_ref, gfin_ref, o_ref, x1_sc, h2_sc, acc_sc, *, final):
    j = pl.program_id(2)

    @pl.when(j == 0)
    def _():
        yy = jnp.concatenate([ys_ref[...], ya_ref[...]], axis=-1)
        x1 = x_ref[...] + g1_ref[...] * _dot(yy, wout_ref[...])
        x1_sc[...] = x1
        h2_sc[...] = (_rms(x1, gmlp_ref[...]) * (1.0 + sc2_ref[...]) + sh2_ref[...]).astype(BF16)
        acc_sc[...] = jnp.zeros(acc_sc.shape, F32)

    u = jnp.maximum(_dot(h2_sc[...], wup_ref[...]), 0.0)
    acc_sc[...] += _dot((u * u).astype(BF16), wdn_ref[...])

    @pl.when(j == pl.num_programs(2) - 1)
    def _():
        x2 = x1_sc[...] + g2_ref[...] * acc_sc[...]
        if final:
            x2 = _rms(x2, gfin_ref[...]) * (1.0 + scf_ref[...]) + shf_ref[...]
        o_ref[...] = x2


def _mlp(x3, ys, ya, mod3, wout, gmlp, wup, wdn, gfin, *, final, tm, tf):
    nb, L, d = x3.shape
    per_tok = mod3.shape[1] != 1
    tm = min(tm, L)
    nj = D_FF // tf

    def mod_spec(k):
        if per_tok:
            return pl.BlockSpec((None, tm, d), lambda b, i, j: (b, i, k))
        return pl.BlockSpec((None, 1, d), lambda b, i, j: (b, 0, k))

    def tok(width):
        return pl.BlockSpec((None, tm, width), lambda b, i, j: (b, i, 0))

    def const(a):
        return pl.BlockSpec(a.shape, lambda b, i, j: (0,) * a.ndim)

    return pl.pallas_call(
        functools.partial(_mlp_body, final=final),
        grid=(nb, L // tm, nj),
        in_specs=[tok(d), tok(SSD_INNER), tok(MLA_INNER)] + [mod_spec(k) for k in (2, 3, 4, 5, 6, 7)]
                 + [const(wout), const(gmlp),
                    pl.BlockSpec((d, tf), lambda b, i, j: (0, j)),
                    pl.BlockSpec((tf, d), lambda b, i, j: (j, 0)),
                    const(gfin)],
        out_specs=tok(d),
        out_shape=SDS((nb, L, d), F32),
        scratch_shapes=[pltpu.VMEM((tm, d), F32), pltpu.VMEM((tm, d), BF16), pltpu.VMEM((tm, d), F32)],
        compiler_params=_params(("arbitrary", "arbitrary", "arbitrary")),
        name="mlp",
    )(x3, ys, ya, mod3, mod3, mod3, mod3, mod3, mod3, wout, gmlp, wup, wdn, gfin)


def _rope_tables(pos):
    inv = 1.0 / (ROPE_THETA ** (jnp.arange(0, QK_ROPE, 2, dtype=F32) / QK_ROPE))
    ang = pos.astype(F32)[:, None] * inv[None, :]
    cos, sin = jnp.cos(ang), jnp.sin(ang)
    c32 = jnp.concatenate([cos, cos], axis=-1)
    s32 = jnp.concatenate([-sin, sin], axis=-1)
    n = pos.shape[0]
    tab1 = jnp.concatenate([c32, s32], axis=-1)
    tabq = jnp.concatenate([jnp.tile(c32, (1, MLA_HEADS)), jnp.tile(s32, (1, MLA_HEADS))], axis=-1)
    tabk = jnp.concatenate([jnp.zeros((n, MISC_KR), F32), c32, s32,
                            jnp.zeros((n, LANES - MISC_KRSW - QK_ROPE), F32)], axis=-1)
    return tab1, tabq, tabk


def _swap_halves(w):
    half = w.shape[-1] // 2
    return jnp.concatenate([w[..., half:], w[..., :half]], axis=-1)


def kernel(x_prompt, x_sample, cache_kv_latent, cache_k_rope, state_conv, state_ssm, page_table,
           c_prompt, c_sample, w_ada, b_ada, norm_mix_g, w_in, conv_w, conv_b, dt_bias, a_log,
           d_skip, norm_ssd_g, q_norm_g, kv_norm_g, w_uq, w_uk, w_uv, norm_attn_g, w_out,
           norm_mlp_g, w_up, w_down, w_ada_final, b_ada_final, norm_final_g):
    depth = w_in.shape[0]
    b_p, seq, d = x_prompt.shape
    n_seq, t_new, _ = x_sample.shape
    n_tok_s = n_seq * t_new
    past_len = page_table.shape[1] * PAGE_SIZE

    c_all = jnp.concatenate([c_prompt, c_sample], axis=0)
    ada_fin = _ada(c_all, w_ada_final, b_ada_final)
    tab1_p, _, tabk_p = _rope_tables(jnp.arange(seq, dtype=jnp.int32))
    _, tabq_s, tabk_s = _rope_tables(past_len + jnp.arange(t_new, dtype=jnp.int32))
    tabq_s = jnp.tile(tabq_s, (n_seq, 1))
    tabk_s = jnp.tile(tabk_s, (n_seq, 1))

    xp = x_prompt
    xs = x_sample.reshape(1, n_tok_s, d)
    outs_p, outs_s = [], []
    for l in range(depth):
        final = l == depth - 1
        wi = w_in[l]
        c1 = SSD_INNER
        c2 = c1 + CONV_DIM
        c3 = c2 + SSD_HEADS
        c4 = c3 + Q_RANK
        c5 = c4 + KV_RANK
        w_kr = wi[:, c5:]
        win = jnp.concatenate(
            [wi[:, :c2], wi[:, c3:c5], wi[:, c2:c3], jnp.zeros((d, MISC_KR - SSD_HEADS), F32), w_kr,
             _swap_halves(w_kr), jnp.zeros((d, LANES - MISC_KRSW - QK_ROPE), F32)], axis=1).astype(BF16)
        wq_h = w_uq[l].reshape(Q_RANK, MLA_HEADS, QK_NOPE + QK_ROPE)
        w_rope = wq_h[:, :, QK_NOPE:]
        wfold = _fold(jnp.transpose(wq_h, (1, 0, 2)), jnp.transpose(w_uk[l], (1, 0, 2)))
        wq = jnp.concatenate([wfold, w_rope.reshape(Q_RANK, -1).astype(BF16),
                              _swap_halves(w_rope).reshape(Q_RANK, -1).astype(BF16)], axis=1)
        wuv_pad = jnp.zeros((MLA_HEADS, KV_RANK, MLA_HEADS, V_DIM), F32)
        wuv_pad = wuv_pad.at[jnp.arange(MLA_HEADS), :, jnp.arange(MLA_HEADS), :].set(
            jnp.transpose(w_uv[l], (1, 0, 2)))
        wuv_pad = wuv_pad.reshape(MLA_HEADS, KV_RANK, MLA_INNER).astype(BF16)
        wuvt = jnp.transpose(w_uv[l], (1, 2, 0)).astype(BF16)
        wout = w_out[l].astype(BF16)
        wup = w_up[l].astype(BF16)
        wdn = w_down[l].astype(BF16)
        gmix = norm_mix_g[l].reshape(1, d)
        gmlp = norm_mlp_g[l].reshape(1, d)
        gfin = norm_final_g.reshape(1, d)
        qg = q_norm_g[l].reshape(1, Q_RANK)
        kvg = kv_norm_g[l].reshape(1, KV_RANK)
        gssd = norm_ssd_g[l].reshape(1, SSD_INNER)
        gattn = norm_attn_g[l].reshape(1, MLA_INNER)
        cw = conv_w[l]
        cb = conv_b[l].reshape(1, CONV_DIM)
        dtb = _row(dt_bias[l], LANES)
        alog = _row(a_log[l], LANES)
        dsk = jnp.repeat(d_skip[l].astype(F32), SSD_HEAD_DIM).reshape(1, SSD_INNER)

        ada = _ada(c_all, w_ada[l], b_ada[l])
        mod = jnp.concatenate([ada, ada_fin], axis=1)
        mod_p = mod[:b_p].reshape(b_p, 1, 8 * d)
        mod_s = jnp.repeat(mod[b_p:], t_new, axis=0).reshape(1, n_tok_s, 8 * d)

        z, xbc, dtr, ckv, kr, kf, qt = _inproj(
            xp, mod_p, gmix, win, qg, kvg, wq.T, tab1_p.T, tabk_p, q_transposed=True, tm=256)
        y_ssd, hfin = _ssd_prompt(xbc, z, dtr, cw, cb, dtb, alog, dsk, gssd)
        tk = min(512, seq)
        kct = jnp.swapaxes(kf[:, :, :KV_RANK].reshape(b_p, seq // tk, tk, KV_RANK), -1, -2)
        y_attn = _attn_prompt(qt, kf, kct, wuvt, gattn, tk=tk, ct=512)
        xp = _mlp(xp, y_ssd, y_attn, mod_p, wout, gmlp, wup, wdn, gfin, final=final, tm=512, tf=2048)
        tail = min(seq, CONV_WIDTH - 1)
        conv_tail = jnp.concatenate([jnp.zeros((b_p, CONV_WIDTH - 1 - tail, CONV_DIM), F32),
                                     xbc[:, seq - tail:]], axis=1)
        outs_p.append((ckv, kr, conv_tail, hfin.reshape(b_p, SSD_HEADS, SSD_HEAD_DIM, SSD_STATE)))

        z, xbc, dtr, ckv, kr, kc, krb, qa, qr = _inproj(
            xs, mod_s, gmix, win, qg, kvg, wq, tabq_s, tabk_s, q_transposed=False, tm=256)
        xbc3 = xbc.reshape(n_seq, t_new, CONV_DIM)
        xpad = jnp.concatenate([state_conv[l], xbc3], axis=1)
        xsh = jnp.stack([xpad[:, k:k + t_new].reshape(n_tok_s, CONV_DIM) for k in range(CONV_WIDTH)])
        ypre, eacs, xw, dec, bm, cm = _ssd_sample(xsh, dtr[0], cw, cb, dtb, alog, dsk, seg=t_new)

        def seqs(a):
            return a.reshape(n_seq, t_new, a.shape[-1])

        y_ssd, s_new = _sstate(seqs(cm), seqs(bm), seqs(ypre), seqs(eacs), seqs(xw), seqs(dec), seqs(z[0]),
                               state_ssm[l].reshape(n_seq, SSD_INNER, SSD_STATE), gssd, bs=8)
        o_lat = _sattn(page_table, qa.reshape(n_tok_s * MLA_HEADS, KV_RANK),
                       qr.reshape(n_tok_s * MLA_HEADS, QK_ROPE), seqs(kc[0]), seqs(krb[0]),
                       cache_kv_latent[l], jnp.swapaxes(cache_k_rope[l], -1, -2))
        y_attn = _apost(o_lat.reshape(n_tok_s, MLA_HEADS * KV_RANK), wuv_pad, gattn)
        xs = _mlp(xs, y_ssd.reshape(1, n_tok_s, SSD_INNER), y_attn.reshape(1, n_tok_s, MLA_INNER), mod_s,
                  wout, gmlp, wup, wdn, gfin, final=final, tm=n_tok_s, tf=2048)
        outs_s.append((seqs(ckv[0]), seqs(kr[0]), xpad[:, t_new:],
                       s_new.reshape(n_seq, SSD_HEADS, SSD_HEAD_DIM, SSD_STATE)))

    def stack(outs, k):
        return jnp.stack([o[k] for o in outs])

    return (xp, xs.reshape(n_seq, t_new, d),
            stack(outs_p, 0), stack(outs_p, 1), stack(outs_p, 2), stack(outs_p, 3),
            stack(outs_s, 0), stack(outs_s, 1), stack(outs_s, 2), stack(outs_s, 3))
```

```python
import functools
import math

import jax
import jax.numpy as jnp
import numpy as np
from jax import lax
from jax.experimental import pallas as pl
from jax.experimental.pallas import tpu as pltpu

F32 = jnp.float32
BF16 = jnp.bfloat16
SDS = jax.ShapeDtypeStruct

D_MODEL = 1024
SSD_HEADS = 8
SSD_HEAD_DIM = 64
SSD_INNER = SSD_HEADS * SSD_HEAD_DIM
SSD_GROUPS = 2
SSD_STATE = 128
CONV_WIDTH = 4
SSD_CHUNK = 128
CONV_DIM = SSD_INNER + 2 * SSD_GROUPS * SSD_STATE
MLA_HEADS = 8
QK_NOPE = 64
QK_ROPE = 32
V_DIM = 64
KV_RANK = 256
Q_RANK = 384
MLA_INNER = MLA_HEADS * V_DIM
ROPE_THETA = 10000.0
ATTN_SCALE = 1.0 / math.sqrt(QK_NOPE + QK_ROPE)
LOG2E = math.log2(math.e)
PAGE_SIZE = 128
D_FF = 4 * D_MODEL
EPS = 1e-6

LANES = 128
TQ = 128
QK_PAD = 384
SPLIT = 2
C_Z = 0
C_XBC = C_Z + SSD_INNER
C_QLAT = C_XBC + CONV_DIM
C_KVLAT = C_QLAT + Q_RANK
C_MISC = C_KVLAT + KV_RANK
W_IN_COLS = C_MISC + LANES
MISC_KR = 32
MISC_KRSW = 64
C_QROPE = MLA_HEADS * KV_RANK
C_QROPE_SW = C_QROPE + MLA_HEADS * QK_ROPE
WQ_COLS = C_QROPE_SW + MLA_HEADS * QK_ROPE

VMEM_LIMIT = 52 * 1024 * 1024


def _dot(a, b):
    return jnp.dot(a, b, preferred_element_type=F32)


def _dot_nt(a, b):
    return lax.dot_general(a, b, (((1,), (1,)), ((), ())), preferred_element_type=F32)


def _dot_tn(a, b):
    return lax.dot_general(a, b, (((0,), (0,)), ((), ())), preferred_element_type=F32)


def _silu(x):
    return x * jax.nn.sigmoid(x)


def _rms(x, g):
    return x * lax.rsqrt(jnp.mean(x * x, axis=-1, keepdims=True) + EPS) * g


def _split3_dot(mask_bf16, v):
    v1 = v.astype(BF16)
    r1 = v - v1.astype(F32)
    v2 = r1.astype(BF16)
    v3 = (r1 - v2.astype(F32)).astype(BF16)
    return _dot(mask_bf16, v1) + _dot(mask_bf16, v2) + _dot(mask_bf16, v3)


def _params(sem, vmem=VMEM_LIMIT):
    return pltpu.CompilerParams(dimension_semantics=sem, vmem_limit_bytes=vmem)


def _ada_body(c_ref, w_ref, b_ref, o_ref):
    s = _silu(c_ref[...]).astype(BF16)
    o_ref[...] = _dot(s, w_ref[...].astype(BF16)) + b_ref[...]


def _ada(c, w, b):
    bsz, d = c.shape
    n = w.shape[1]
    tn = 1024
    return pl.pallas_call(
        _ada_body,
        grid=(n // tn,),
        in_specs=[pl.BlockSpec((bsz, d), lambda j: (0, 0)),
                  pl.BlockSpec((d, tn), lambda j: (0, j)),
                  pl.BlockSpec((1, tn), lambda j: (0, j))],
        out_specs=pl.BlockSpec((bsz, tn), lambda j: (0, j)),
        out_shape=SDS((bsz, n), F32),
        compiler_params=_params(("arbitrary",)),
        name="ada",
    )(c, w, b.reshape(1, n))


def _fold_body(wq_ref, wk_ref, o_ref):
    a = wq_ref[:, 0:QK_NOPE].astype(BF16)
    o_ref[...] = _dot_nt(a, wk_ref[...].astype(BF16)).astype(BF16)


def _fold(wq_h, wk_h):
    return pl.pallas_call(
        _fold_body,
        grid=(MLA_HEADS,),
        in_specs=[pl.BlockSpec((None, Q_RANK, QK_NOPE + QK_ROPE), lambda h: (h, 0, 0)),
                  pl.BlockSpec((None, KV_RANK, QK_NOPE), lambda h: (h, 0, 0))],
        out_specs=pl.BlockSpec((Q_RANK, KV_RANK), lambda h: (0, h)),
        out_shape=SDS((Q_RANK, MLA_HEADS * KV_RANK), BF16),
        compiler_params=_params(("arbitrary",)),
        name="fold",
    )(wq_h, wk_h)


def _inproj_body(*refs, q_transposed):
    (x_ref, sh_ref, sc_ref, gmix_ref, win_ref, qg_ref, kvg_ref, wq_ref, tq_ref, tk_ref,
     z_ref, xbc_ref, dt_ref, ckv_ref, kr_ref) = refs[:15]
    h = _rms(x_ref[...], gmix_ref[...]) * (1.0 + sc_ref[...]) + sh_ref[...]
    proj = _dot(h.astype(BF16), win_ref[...])
    tm = proj.shape[0]
    z_ref[...] = proj[:, C_Z:C_XBC]
    xbc_ref[...] = proj[:, C_XBC:C_QLAT]
    q_scale = ATTN_SCALE * LOG2E if q_transposed else ATTN_SCALE
    qn = (_rms(proj[:, C_QLAT:C_KVLAT], qg_ref[...]) * q_scale).astype(BF16)
    ckv = _rms(proj[:, C_KVLAT:C_MISC], kvg_ref[...])
    ckv_ref[...] = ckv
    misc = proj[:, C_MISC:W_IN_COLS]
    lane = lax.broadcasted_iota(jnp.int32, misc.shape, 1)
    dt_ref[...] = jnp.where(lane < SSD_HEADS, misc, 0.0)
    tk = tk_ref[...]
    kr = (misc[:, MISC_KR:MISC_KR + QK_ROPE] * tk[:, :QK_ROPE]
          + misc[:, MISC_KRSW:MISC_KRSW + QK_ROPE] * tk[:, QK_ROPE:])
    kr_ref[...] = kr
    nr = MLA_HEADS * QK_ROPE
    if q_transposed:
        kf_ref, kct_ref, qt_ref = refs[15:]
        npad = QK_PAD - KV_RANK - QK_ROPE
        kf_ref[:, 0:KV_RANK] = ckv.astype(BF16)
        kct_ref[...] = ckv.T.astype(BF16)
        kf_ref[:, KV_RANK:QK_PAD] = jnp.concatenate([kr, jnp.zeros((tm, npad), F32)], axis=1).astype(BF16)
        qt = _dot_nt(wq_ref[...], qn)
        tq = tq_ref[...]
        cos_t = jnp.concatenate([tq[:QK_ROPE]] * MLA_HEADS, axis=0)
        sin_t = jnp.concatenate([tq[QK_ROPE:]] * MLA_HEADS, axis=0)
        rot = (qt[C_QROPE:C_QROPE_SW] * cos_t + qt[C_QROPE_SW:WQ_COLS] * sin_t).astype(BF16)
        zpad = jnp.zeros((npad, TQ), BF16)
        for c in range(tm // TQ):
            toks = slice(c * TQ, (c + 1) * TQ)
            for hh in range(MLA_HEADS):
                cols = slice(hh * TQ, (hh + 1) * TQ)
                qt_ref[c, 0:KV_RANK, cols] = qt[hh * KV_RANK:(hh + 1) * KV_RANK, toks].astype(BF16)
                qt_ref[c, KV_RANK:KV_RANK + QK_ROPE, cols] = rot[hh * QK_ROPE:(hh + 1) * QK_ROPE, toks]
                qt_ref[c, KV_RANK + QK_ROPE:QK_PAD, cols] = zpad
    else:
        kc_ref, krb_ref, qa_ref, qr_ref = refs[15:]
        kc_ref[...] = ckv.astype(BF16)
        krb_ref[...] = kr.astype(BF16)
        q = _dot(qn, wq_ref[...])
        tq = tq_ref[...]
        qa_ref[...] = q[:, :C_QROPE].astype(BF16)
        qr_ref[...] = (q[:, C_QROPE:C_QROPE_SW] * tq[:, :nr] + q[:, C_QROPE_SW:WQ_COLS] * tq[:, nr:]).astype(BF16)


def _inproj(x3, mod3, gmix, win, qg, kvg, wq, tabq, tabk, *, q_transposed, tm, tk=None):
    nb, L, d = x3.shape
    per_tok = mod3.shape[1] != 1
    tm = min(tm, L)
    nt = L // tm

    def mod_spec(k):
        if per_tok:
            return pl.BlockSpec((None, tm, d), lambda b, i: (b, i, k))
        return pl.BlockSpec((None, 1, d), lambda b, i: (b, 0, k))

    def tok(width):
        return pl.BlockSpec((None, tm, width), lambda b, i: (b, i, 0))

    def const(a):
        return pl.BlockSpec(a.shape, lambda b, i: (0,) * a.ndim)

    nr = MLA_HEADS * QK_ROPE
    out_specs = [tok(SSD_INNER), tok(CONV_DIM), tok(LANES), tok(KV_RANK), tok(QK_ROPE)]
    out_shape = [SDS((nb, L, SSD_INNER), F32), SDS((nb, L, CONV_DIM), F32), SDS((nb, L, LANES), F32),
                 SDS((nb, L, KV_RANK), F32), SDS((nb, L, QK_ROPE), F32)]
    if q_transposed:
        assert tm % TQ == 0 and tk % tm == 0
        r = tk // tm
        tabq_spec = pl.BlockSpec((2 * QK_ROPE, tm), lambda b, i: (0, i))
        out_specs += [tok(QK_PAD),
                      pl.BlockSpec((None, None, KV_RANK, tm), lambda b, i: (b, i // r, 0, i % r)),
                      pl.BlockSpec((None, tm // TQ, QK_PAD, MLA_HEADS * TQ), lambda b, i: (b, i, 0, 0))]
        out_shape += [SDS((nb, L, QK_PAD), BF16), SDS((nb, L // tk, KV_RANK, tk), BF16),
                      SDS((nb, L // TQ, QK_PAD, MLA_HEADS * TQ), BF16)]
    else:
        tabq_spec = pl.BlockSpec((tm, 2 * nr), lambda b, i: (i, 0))
        out_specs += [tok(KV_RANK), tok(QK_ROPE), tok(C_QROPE), tok(nr)]
        out_shape += [SDS((nb, L, KV_RANK), BF16), SDS((nb, L, QK_ROPE), BF16),
                      SDS((nb, L, C_QROPE), BF16), SDS((nb, L, nr), BF16)]
    return pl.pallas_call(
        functools.partial(_inproj_body, q_transposed=q_transposed),
        grid=(nb, nt),
        in_specs=[tok(d), mod_spec(0), mod_spec(1), const(gmix), const(win), const(qg), const(kvg), const(wq),
                  tabq_spec, pl.BlockSpec((tm, 2 * QK_ROPE), lambda b, i: (i, 0))],
        out_specs=out_specs,
        out_shape=out_shape,
        compiler_params=_params(("arbitrary", "arbitrary")),
        name="inproj_prompt" if q_transposed else "inproj_sample",
    )(x3, mod3, mod3, gmix, win, qg, kvg, wq, tabq, tabk)


def _ssd_body(*refs, Q, seg, carry):
    if carry:
        (xin_ref, z_ref, dt_ref, cw_ref, cb_ref, dtb_ref, alog_ref, dsk_ref, g_ref,
         y_ref, hfin_ref, xbuf, hT) = refs
    else:
        (xin_ref, dt_ref, cw_ref, cb_ref, dtb_ref, alog_ref, dsk_ref,
         ypre_ref, eacs_ref, xw_ref, dec_ref, bm_ref, cm_ref) = refs
    cw = cw_ref[...]
    acc = jnp.broadcast_to(cb_ref[...], (Q, CONV_DIM))
    if carry:
        c = pl.program_id(1)

        @pl.when(c == 0)
        def _():
            xbuf[0:8, :] = jnp.zeros((8, CONV_DIM), F32)
            hT[...] = jnp.zeros(hT.shape, F32)

        xbuf[8:8 + Q, :] = xin_ref[...]
        for k in range(CONV_WIDTH):
            acc = acc + cw[k:k + 1, :] * xbuf[pl.ds(8 - (CONV_WIDTH - 1) + k, Q), :]
        xbuf[0:8, :] = xbuf[Q:Q + 8, :]
    else:
        for k in range(CONV_WIDTH):
            acc = acc + cw[k:k + 1, :] * xin_ref[k]
    xc = _silu(acc)
    xs = xc[:, :SSD_INNER]
    gs = SSD_GROUPS * SSD_STATE
    bm = xc[:, SSD_INNER:SSD_INNER + gs]
    cm = xc[:, SSD_INNER + gs:]
    bm_b = bm.astype(BF16)
    cm_b = cm.astype(BF16)

    lane = lax.broadcasted_iota(jnp.int32, (Q, LANES), 1)
    v = dt_ref[...] + dtb_ref[...]
    dt = jnp.maximum(v, 0.0) + jnp.log1p(jnp.exp(-jnp.abs(v)))
    dt = jnp.where(lane < SSD_HEADS, dt, 0.0)
    dA = dt * (-jnp.exp(alog_ref[...]))
    ri = lax.broadcasted_iota(jnp.int32, (Q, Q), 0)
    ci = lax.broadcasted_iota(jnp.int32, (Q, Q), 1)
    if seg == Q:
        mask = ci <= ri
    else:
        same = (ri // seg) == (ci // seg)
        mask = jnp.logical_and(same, ci <= ri)
    acs = _split3_dot(jnp.where(mask, 1.0, 0.0).astype(BF16), dA)
    if seg == Q:
        acs_last = acs[Q - 1:Q, :]
    else:
        acs_last = _split3_dot(jnp.where(same, 1.0, 0.0).astype(BF16), dA)
    to_end = jnp.exp(acs_last - acs) * dt
    acsT = acs.T
    dtT = dt.T

    G = [_dot_nt(cm_b[:, g * SSD_STATE:(g + 1) * SSD_STATE], bm_b[:, g * SSD_STATE:(g + 1) * SSD_STATE])
         for g in range(SSD_GROUPS)]
    lane_lo = lane < SSD_HEAD_DIM
    heads_per_group = SSD_HEADS // SSD_GROUPS
    ypairs, epairs, xwpairs, decpairs = [], [], [], []
    for k in range(SSD_HEADS // 2):
        g = (2 * k) // heads_per_group
        xp = xs[:, k * LANES:(k + 1) * LANES]
        xhalf = (jnp.where(lane_lo, xp, 0.0).astype(BF16), jnp.where(lane_lo, 0.0, xp).astype(BF16))
        yk = jnp.zeros((Q, LANES), F32)
        for s in range(2):
            hh = 2 * k + s
            segm = acs[:, hh:hh + 1] - acsT[hh:hh + 1, :]
            m = G[g] * jnp.exp(jnp.where(mask, segm, -jnp.inf)) * dtT[hh:hh + 1, :]
            yk = yk + _dot(m.astype(BF16), xhalf[s])

        def pair(a):
            return jnp.where(lane_lo[:a.shape[0]], a[:, 2 * k:2 * k + 1], a[:, 2 * k + 1:2 * k + 2])

        e_p = jnp.exp(pair(acs))
        xw = xp * pair(to_end)
        dec = jnp.exp(pair(acs_last))
        if carry:
            h_prev = hT[k]
            yk = yk + _dot(cm_b[:, g * SSD_STATE:(g + 1) * SSD_STATE], h_prev.astype(BF16)) * e_p
            hT[k] = dec * h_prev + _dot_tn(bm_b[:, g * SSD_STATE:(g + 1) * SSD_STATE], xw.astype(BF16))
        else:
            epairs.append(e_p)
            xwpairs.append(xw)
            decpairs.append(dec)
        ypairs.append(yk)
    y = jnp.concatenate(ypairs, axis=1) + dsk_ref[...] * xs
    if carry:
        y = y * _silu(z_ref[...])
        y_ref[...] = _rms(y, g_ref[...]).astype(BF16)

        @pl.when(c == pl.num_programs(1) - 1)
        def _():
            for k in range(SSD_HEADS // 2):
                hfin_ref[k * LANES:(k + 1) * LANES, :] = hT[k].T
    else:
        ypre_ref[...] = y
        eacs_ref[...] = jnp.concatenate(epairs, axis=1)
        xw_ref[...] = jnp.concatenate(xwpairs, axis=1)
        dec_ref[...] = jnp.concatenate(decpairs, axis=1)
        bm_ref[...] = bm
        cm_ref[...] = cm


def _row(a, n):
    return jnp.pad(a.reshape(1, -1).astype(F32), ((0, 0), (0, n - a.size)))


def _ssd_prompt(xbc, z, dt, cw, cb, dtb, alog, dsk, g):
    nb, L, _ = xbc.shape
    Q = SSD_CHUNK if L % SSD_CHUNK == 0 else L
    nc = L // Q

    def tok(width):
        return pl.BlockSpec((None, Q, width), lambda b, c: (b, c, 0))

    def const(a):
        return pl.BlockSpec(a.shape, lambda b, c: (0,) * a.ndim)

    consts = (cw, cb, dtb, alog, dsk, g)
    return pl.pallas_call(
        functools.partial(_ssd_body, Q=Q, seg=Q, carry=True),
        grid=(nb, nc),
        in_specs=[tok(CONV_DIM), tok(SSD_INNER), tok(LANES)] + [const(a) for a in consts],
        out_specs=[tok(SSD_INNER), pl.BlockSpec((None, SSD_INNER, SSD_STATE), lambda b, c: (b, 0, 0))],
        out_shape=[SDS((nb, L, SSD_INNER), BF16), SDS((nb, SSD_INNER, SSD_STATE), F32)],
        scratch_shapes=[pltpu.VMEM((Q + 8, CONV_DIM), F32), pltpu.VMEM((SSD_HEADS // 2, SSD_STATE, LANES), F32)],
        compiler_params=_params(("arbitrary", "arbitrary")),
        name="ssd_prompt",
    )(xbc, z, dt, *consts)


def _ssd_sample(xsh, dt, cw, cb, dtb, alog, dsk, *, seg):
    _, T, _ = xsh.shape
    consts = (cw, cb, dtb, alog, dsk)

    def full(a):
        return pl.BlockSpec(a.shape, lambda i: (0,) * a.ndim)

    outs = [SDS((T, SSD_INNER), F32)] * 4 + [SDS((T, SSD_GROUPS * SSD_STATE), F32)] * 2
    return pl.pallas_call(
        functools.partial(_ssd_body, Q=T, seg=seg, carry=False),
        grid=(1,),
        in_specs=[full(xsh), full(dt)] + [full(a) for a in consts],
        out_specs=[full(o) for o in outs],
        out_shape=outs,
        compiler_params=_params(("arbitrary",)),
        name="ssd_sample",
    )(xsh, dt, *consts)


def _sstate_body(cm_ref, bm_ref, ypre_ref, eacs_ref, xw_ref, dec_ref, z_ref, s0_ref, g_ref, y_ref, sn_ref):
    s0 = s0_ref[...]
    s0b = s0.astype(BF16)
    cm = cm_ref[...].astype(BF16)
    bm = bm_ref[...].astype(BF16)
    rows = SSD_INNER // SSD_GROUPS
    yo = jnp.concatenate(
        [jnp.einsum("btn,bqn->btq", cm[:, :, g * SSD_STATE:(g + 1) * SSD_STATE],
                    s0b[:, g * rows:(g + 1) * rows, :], preferred_element_type=F32)
         for g in range(SSD_GROUPS)], axis=-1)
    y = (ypre_ref[...] + yo * eacs_ref[...]) * _silu(z_ref[...])
    y_ref[...] = _rms(y, g_ref[...]).astype(BF16)
    dec = dec_ref[...]
    hi = dec.astype(BF16)
    lo = (dec - hi.astype(F32)).astype(BF16)
    sel = jnp.where(lax.broadcasted_iota(jnp.int32, (dec.shape[0], dec.shape[1], SSD_STATE), 1) == 0,
                    1.0, 0.0).astype(BF16)
    dmat = (jnp.einsum("bjq,bjn->bqn", hi, sel, preferred_element_type=F32)
            + jnp.einsum("bjq,bjn->bqn", lo, sel, preferred_element_type=F32))
    xw = xw_ref[...].astype(BF16)
    upd = jnp.concatenate(
        [jnp.einsum("bjq,bjn->bqn", xw[:, :, g * rows:(g + 1) * rows],
                    bm[:, :, g * SSD_STATE:(g + 1) * SSD_STATE], preferred_element_type=F32)
         for g in range(SSD_GROUPS)], axis=1)
    sn_ref[...] = dmat * s0 + upd


def _sstate(cm, bm, ypre, eacs, xw, dec, z, s0, g, *, bs):
    nseq, t, _ = cm.shape
    bs = min(bs, nseq)

    def blk(a):
        return pl.BlockSpec((bs,) + a.shape[1:], lambda i: (i, 0, 0))

    ins = (cm, bm, ypre, eacs, xw, dec, z, s0)
    return pl.pallas_call(
        _sstate_body,
        grid=(nseq // bs,),
        in_specs=[blk(a) for a in ins] + [pl.BlockSpec(g.shape, lambda i: (0, 0))],
        out_specs=[blk(ypre), blk(s0)],
        out_shape=[SDS(ypre.shape, BF16), SDS(s0.shape, F32)],
        compiler_params=_params(("arbitrary",)),
        name="sstate",
    )(*ins, g)


def _attn_body(qt_ref, qn_ref, kf_ref, kct_ref, wuvt_ref, g_ref, o_ref, m_sc, l_sc, acc_sc,
               s_a, s_b, s_c, smax_a, smax_b, smax_c, *, tk, ct):
    i = pl.program_id(1)
    ncols = MLA_HEADS * TQ
    m_sc[...] = jnp.full(m_sc.shape, -jnp.inf, F32)
    l_sc[...] = jnp.zeros(l_sc.shape, F32)
    acc_sc[...] = jnp.zeros(acc_sc.shape, F32)
    tiles = [slice(n * ct, (n + 1) * ct) for n in range(ncols // ct)]

    def scores(j, s_ref, smax_ref, q_ref=qt_ref):
        kf = kf_ref[pl.ds(pl.multiple_of(j * tk, tk), tk), :]
        for cs in tiles:
            s = _dot(kf, q_ref[:, cs])
            s_ref[:, cs] = s
            smax_ref[:, cs] = jnp.max(s, axis=0, keepdims=True)

    def softmax_pv(j, s_ref, smax_ref, masked):
        kct = kct_ref[j]
        for cs in tiles:
            s = s_ref[:, cs]
            if masked:
                key = j * tk + lax.broadcasted_iota(jnp.int32, (tk, ct), 0)
                tok = i * TQ + (lax.broadcasted_iota(jnp.int32, (tk, ct), 1) & (TQ - 1))
                s = jnp.where(key <= tok, s, -jnp.inf)
                smax = jnp.max(s, axis=0, keepdims=True)
            else:
                smax = smax_ref[:, cs]
            m_prev = m_sc[:, cs]
            m_new = jnp.maximum(m_prev, smax)
            alpha = jnp.exp2(m_prev - m_new)
            p = jnp.exp2(s - m_new)
            l_sc[:, cs] = alpha * l_sc[:, cs] + jnp.sum(p, axis=0, keepdims=True)
            acc_sc[:, cs] = alpha * acc_sc[:, cs] + _dot(kct, p.astype(BF16))
            m_sc[:, cs] = m_new

    nfull = (i * TQ) // tk

    @pl.when(i == 0)
    def _():
        scores(0, s_c, smax_c)

    def prefetch():
        scores(0, s_c, smax_c, qn_ref)

    @pl.when(nfull == 0)
    def _():
        softmax_pv(0, s_c, smax_c, True)
        prefetch()

    @pl.when(nfull >= 1)
    def _():
        scores(1, s_b, smax_b)
        softmax_pv(0, s_c, smax_c, False)

    def pair(p, carry):
        j = 2 * p + 1
        scores(j + 1, s_a, smax_a)
        softmax_pv(j, s_b, smax_b, False)
        scores(j + 2, s_b, smax_b)
        softmax_pv(j + 1, s_a, smax_a, False)
        return carry

    lax.fori_loop(0, jnp.maximum(nfull - 1, 0) // 2, pair, 0)
    odd = (nfull % 2) == 1

    @pl.when(odd)
    def _():
        prefetch()
        softmax_pv(nfull, s_b, smax_b, True)

    @pl.when(jnp.logical_and(jnp.logical_not(odd), nfull >= 2))
    def _():
        scores(nfull, s_a, smax_a)
        softmax_pv(nfull - 1, s_b, smax_b, False)
        prefetch()
        softmax_pv(nfull, s_a, smax_a, True)

    inv = 1.0 / l_sc[...]
    ys = []
    for hh in range(MLA_HEADS):
        cs = slice(hh * TQ, (hh + 1) * TQ)
        ys.append(_dot(wuvt_ref[hh], (acc_sc[:, cs] * inv[:, cs]).astype(BF16)))
    yt = jnp.concatenate(ys, axis=0)
    yt = yt * lax.rsqrt(jnp.mean(yt * yt, axis=0, keepdims=True) + EPS)
    o_ref[...] = (yt.T * g_ref[...]).astype(BF16)


def _attn_prompt(qt, kf, kct, wuvt, g, *, tk, ct):
    nb, nq, _, ncols = qt.shape
    L = nq * TQ
    assert TQ & (TQ - 1) == 0 and L % tk == 0 and tk % TQ == 0 and ct % TQ == 0 and ncols % ct == 0
    return pl.pallas_call(
        functools.partial(_attn_body, tk=tk, ct=ct),
        grid=(nb, nq),
        in_specs=[pl.BlockSpec((None, None, QK_PAD, ncols), lambda b, i: (b, i, 0, 0)),
                  pl.BlockSpec((None, None, QK_PAD, ncols), lambda b, i: (b, jnp.minimum(i + 1, nq - 1), 0, 0)),
                  pl.BlockSpec((None, L, QK_PAD), lambda b, i: (b, 0, 0)),
                  pl.BlockSpec((None, L // tk, KV_RANK, tk), lambda b, i: (b, 0, 0, 0)),
                  pl.BlockSpec(wuvt.shape, lambda b, i: (0, 0, 0)),
                  pl.BlockSpec(g.shape, lambda b, i: (0, 0))],
        out_specs=pl.BlockSpec((None, TQ, MLA_INNER), lambda b, i: (b, i, 0)),
        out_shape=SDS((nb, L, MLA_INNER), BF16),
        scratch_shapes=[pltpu.VMEM((1, ncols), F32), pltpu.VMEM((1, ncols), F32),
                        pltpu.VMEM((KV_RANK, ncols), F32),
                        pltpu.VMEM((tk, ncols), F32), pltpu.VMEM((tk, ncols), F32), pltpu.VMEM((tk, ncols), F32),
                        pltpu.VMEM((1, ncols), F32), pltpu.VMEM((1, ncols), F32), pltpu.VMEM((1, ncols), F32)],
        compiler_params=_params(("arbitrary", "arbitrary")),
        name="attn_prompt",
    )(qt, qt, kf, kct, wuvt, g)


def _sattn_body(pt_ref, qa_ref, qr_ref, kn_ref, rn_ref, ckv_hbm, krt_hbm, o_ref, kbuf, rbuf, sem, *, npages, t_new):
    b = pl.program_id(0)
    nseq = pl.num_programs(0)

    def copies(seq_page, slot, p):
        off = p * PAGE_SIZE
        return (pltpu.make_async_copy(ckv_hbm.at[seq_page], kbuf.at[slot, p], sem.at[0, slot]),
                pltpu.make_async_copy(krt_hbm.at[seq_page], rbuf.at[slot, :, pl.ds(off, PAGE_SIZE)],
                                      sem.at[1, slot]))

    def start_fetch(seq, slot):
        for p in range(npages):
            for cp in copies(pt_ref[seq * npages + p], slot, p):
                cp.start()

    def wait_fetch(slot):
        pltpu.make_async_copy(ckv_hbm.at[pl.ds(0, npages)], kbuf.at[slot], sem.at[0, slot]).wait()
        pltpu.make_async_copy(rbuf.at[slot], rbuf.at[slot], sem.at[1, slot]).wait()

    @pl.when(b == 0)
    def _():
        start_fetch(0, 0)

    @pl.when(b + 1 < nseq)
    def _():
        start_fetch(b + 1, (b + 1) % 2)

    slot = b % 2
    wait_fetch(slot)
    pg = npages // SPLIT
    part = pg * PAGE_SIZE
    kps = [kbuf[slot, pl.ds(h * pg, pg)].reshape(part, KV_RANK).astype(BF16) for h in range(SPLIT)]
    rpt = rbuf[slot].astype(BF16)
    q = qa_ref[...]
    qr = qr_ref[...]
    kn = kn_ref[...]
    rn = rn_ref[...]
    s_main = [_dot_nt(q, kps[h]) for h in range(SPLIT)]
    s_rope = [_dot(qr, rpt[:, h * part:(h + 1) * part]) for h in range(SPLIT)]
    s_p = [a + b for a, b in zip(s_main, s_rope)]
    s_n = _dot_nt(q, kn) + _dot_nt(qr, rn)
    rows = t_new * MLA_HEADS
    tok = lax.broadcasted_iota(jnp.int32, (rows, t_new), 0) // MLA_HEADS
    col = lax.broadcasted_iota(jnp.int32, (rows, t_new), 1)
    s_n = jnp.where(col <= tok, s_n, -jnp.inf)
    m = jnp.max(s_n, axis=-1, keepdims=True)
    for s in s_p:
        m = jnp.maximum(m, jnp.max(s, axis=-1, keepdims=True))
    pn = jnp.exp(s_n - m)
    pp = [jnp.exp(s - m) for s in s_p]
    l = jnp.sum(pn, axis=-1, keepdims=True)
    for p in pp:
        l = l + jnp.sum(p, axis=-1, keepdims=True)
    inv = 1.0 / l
    parts = [_dot(pp[h].astype(BF16), kps[h]) for h in range(SPLIT)]
    o = _dot(pn.astype(BF16), kn) * inv
    for part_o in parts:
        o = o + part_o * inv
    o_ref[...] = o


def _sattn(page_table, qa, qr, kn, rn, cache_kv, cache_krt):
    nseq, npages = page_table.shape
    t_new = kn.shape[1]
    rows = t_new * MLA_HEADS
    grid_spec = pltpu.PrefetchScalarGridSpec(
        num_scalar_prefetch=1,
        grid=(nseq,),
        in_specs=[pl.BlockSpec((rows, KV_RANK), lambda b, pt: (b, 0)),
                  pl.BlockSpec((rows, QK_ROPE), lambda b, pt: (b, 0)),
                  pl.BlockSpec((None, t_new, KV_RANK), lambda b, pt: (b, 0, 0)),
                  pl.BlockSpec((None, t_new, QK_ROPE), lambda b, pt: (b, 0, 0)),
                  pl.BlockSpec(memory_space=pl.ANY),
                  pl.BlockSpec(memory_space=pl.ANY)],
        out_specs=pl.BlockSpec((rows, KV_RANK), lambda b, pt: (b, 0)),
        scratch_shapes=[pltpu.VMEM((2, npages, PAGE_SIZE, KV_RANK), F32),
                        pltpu.VMEM((2, QK_ROPE, npages * PAGE_SIZE), F32),
                        pltpu.SemaphoreType.DMA((2, 2))],
    )
    return pl.pallas_call(
        functools.partial(_sattn_body, npages=npages, t_new=t_new),
        grid_spec=grid_spec,
        out_shape=SDS((nseq * rows, KV_RANK), F32),
        compiler_params=_params(("arbitrary",)),
        name="attn_sample",
    )(page_table.reshape(-1), qa, qr, kn, rn, cache_kv, cache_krt)


def _apost_body(o_ref, wuv_ref, g_ref, y_ref):
    tm = o_ref.shape[0]
    o = jnp.zeros((tm, MLA_INNER), F32)
    for hh in range(MLA_HEADS):
        o = o + _dot(o_ref[:, hh * KV_RANK:(hh + 1) * KV_RANK].astype(BF16), wuv_ref[hh])
    y_ref[...] = _rms(o, g_ref[...]).astype(BF16)


def _apost(o_lat, wuv_pad, g):
    T = o_lat.shape[0]
    return pl.pallas_call(
        _apost_body,
        grid=(1,),
        in_specs=[pl.BlockSpec(o_lat.shape, lambda i: (0, 0)),
                  pl.BlockSpec(wuv_pad.shape, lambda i: (0, 0, 0)),
                  pl.BlockSpec(g.shape, lambda i: (0, 0))],
        out_specs=pl.BlockSpec((T, MLA_INNER), lambda i: (0, 0)),
        out_shape=SDS((T, MLA_INNER), BF16),
        compiler_params=_params(("arbitrary",)),
        name="attn_post",
    )(o_lat, wuv_pad, g)


def _mlp_body(x_ref, ys_ref, ya_ref, g1_ref, sh2_ref, sc2_ref, g2_ref, shf_ref, scf_ref,
              wout_ref, gmlp_ref, wup_ref, wdn_ref, gfin_ref, o_ref, x1_sc, h2_sc, acc_sc, *, final):
    j = pl.program_id(2)

    @pl.when(j == 0)
    def _():
        yy = jnp.concatenate([ys_ref[...], ya_ref[...]], axis=-1)
        x1 = x_ref[...] + g1_ref[...] * _dot(yy, wout_ref[...])
        x1_sc[...] = x1
        h2_sc[...] = (_rms(x1, gmlp_ref[...]) * (1.0 + sc2_ref[...]) + sh2_ref[...]).astype(BF16)
        acc_sc[...] = jnp.zeros(acc_sc.shape, F32)

    u = jnp.maximum(_dot(h2_sc[...], wup_ref[...]), 0.0)
    acc_sc[...] += _dot((u * u).astype(BF16), wdn_ref[...])

    @pl.when(j == pl.num_programs(2) - 1)
    def _():
        x2 = x1_sc[...] + g2_ref[...] * acc_sc[...]
        if final:
            x2 = _rms(x2, gfin_ref[...]) * (1.0 + scf_ref[...]) + shf_ref[...]
        o_ref[...] = x2


def _mlp(x3, ys, ya, mod3, wout, gmlp, wup, wdn, gfin, *, final, tm, tf):
    nb, L, d = x3.shape
    per_tok = mod3.shape[1] != 1
    tm = min(tm, L)
    nj = D_FF // tf

    def mod_spec(k):
        if per_tok:
            return pl.BlockSpec((None, tm, d), lambda b, i, j: (b, i, k))
        return pl.BlockSpec((None, 1, d), lambda b, i, j: (b, 0, k))

    def tok(width):
        return pl.BlockSpec((None, tm, width), lambda b, i, j: (b, i, 0))

    def const(a):
        return pl.BlockSpec(a.shape, lambda b, i, j: (0,) * a.ndim)

    return pl.pallas_call(
        functools.partial(_mlp_body, final=final),
        grid=(nb, L // tm, nj),
        in_specs=[tok(d), tok(SSD_INNER), tok(MLA_INNER)] + [mod_spec(k) for k in (2, 3, 4, 5, 6, 7)]
                 + [const(wout), const(gmlp),
                    pl.BlockSpec((d, tf), lambda b, i, j: (0, j)),
                    pl.BlockSpec((tf, d), lambda b, i, j: (j, 0)),
                    const(gfin)],
        out_specs=tok(d),
        out_shape=SDS((nb, L, d), F32),
        scratch_shapes=[pltpu.VMEM((tm, d), F32), pltpu.VMEM((tm, d), BF16), pltpu.VMEM((tm, d), F32)],
        compiler_params=_params(("arbitrary", "arbitrary", "arbitrary")),
        name="mlp",
    )(x3, ys, ya, mod3, mod3, mod3, mod3, mod3, mod3, wout, gmlp, wup, wdn, gfin)


def _rope_tables(pos):
    inv = 1.0 / (ROPE_THETA ** (np.arange(0, QK_ROPE, 2, dtype=np.float64) / QK_ROPE))
    ang = pos.astype(np.float64)[:, None] * inv[None, :]
    cos, sin = np.cos(ang).astype(np.float32), np.sin(ang).astype(np.float32)
    c32 = np.concatenate([cos, cos], axis=-1)
    s32 = np.concatenate([-sin, sin], axis=-1)
    tab1 = np.concatenate([c32, s32], axis=-1)
    tabq = np.concatenate([np.tile(c32, (1, MLA_HEADS)), np.tile(s32, (1, MLA_HEADS))], axis=-1)
    return tab1, tabq


def _swap_halves(w):
    half = w.shape[-1] // 2
    return jnp.concatenate([w[..., half:], w[..., :half]], axis=-1)


def kernel(x_prompt, x_sample, cache_kv_latent, cache_k_rope, state_conv, state_ssm, page_table,
           c_prompt, c_sample, w_ada, b_ada, norm_mix_g, w_in, conv_w, conv_b, dt_bias, a_log,
           d_skip, norm_ssd_g, q_norm_g, kv_norm_g, w_uq, w_uk, w_uv, norm_attn_g, w_out,
           norm_mlp_g, w_up, w_down, w_ada_final, b_ada_final, norm_final_g):
    depth = w_in.shape[0]
    b_p, seq, d = x_prompt.shape
    n_seq, t_new, _ = x_sample.shape
    n_tok_s = n_seq * t_new
    past_len = page_table.shape[1] * PAGE_SIZE

    c_all = jnp.concatenate([c_prompt, c_sample], axis=0)
    ada_fin = _ada(c_all, w_ada_final, b_ada_final)
    tab1_p, _ = _rope_tables(np.arange(seq))
    tab1_s, tabq_s = _rope_tables(past_len + np.arange(t_new))
    tab1_pt = jnp.asarray(np.ascontiguousarray(tab1_p.T))
    tab1_p = jnp.asarray(tab1_p)
    tabq_s = jnp.asarray(np.tile(tabq_s, (n_seq, 1)))
    tab1_s = jnp.asarray(np.tile(tab1_s, (n_seq, 1)))

    xp = x_prompt
    xs = x_sample.reshape(1, n_tok_s, d)
    outs_p, outs_s = [], []
    for l in range(depth):
        final = l == depth - 1
        wi = w_in[l]
        c1 = SSD_INNER
        c2 = c1 + CONV_DIM
        c3 = c2 + SSD_HEADS
        c4 = c3 + Q_RANK
        c5 = c4 + KV_RANK
        w_kr = wi[:, c5:]
        win = jnp.concatenate(
            [wi[:, :c2], wi[:, c3:c5], wi[:, c2:c3], jnp.zeros((d, MISC_KR - SSD_HEADS), F32), w_kr,
             _swap_halves(w_kr), jnp.zeros((d, LANES - MISC_KRSW - QK_ROPE), F32)], axis=1).astype(BF16)
        wq_h = w_uq[l].reshape(Q_RANK, MLA_HEADS, QK_NOPE + QK_ROPE)
        w_rope = wq_h[:, :, QK_NOPE:]
        wfold = _fold(jnp.transpose(wq_h, (1, 0, 2)), jnp.transpose(w_uk[l], (1, 0, 2)))
        wq = jnp.concatenate([wfold, w_rope.reshape(Q_RANK, -1).astype(BF16),
                              _swap_halves(w_rope).reshape(Q_RANK, -1).astype(BF16)], axis=1)
        wuv_pad = jnp.zeros((MLA_HEADS, KV_RANK, MLA_HEADS, V_DIM), F32)
        wuv_pad = wuv_pad.at[jnp.arange(MLA_HEADS), :, jnp.arange(MLA_HEADS), :].set(
            jnp.transpose(w_uv[l], (1, 0, 2)))
        wuv_pad = wuv_pad.reshape(MLA_HEADS, KV_RANK, MLA_INNER).astype(BF16)
        wuvt = jnp.transpose(w_uv[l], (1, 2, 0)).astype(BF16)
        wout = w_out[l].astype(BF16)
        wup = w_up[l].astype(BF16)
        wdn = w_down[l].astype(BF16)
        gmix = norm_mix_g[l].reshape(1, d)
        gmlp = norm_mlp_g[l].reshape(1, d)
        gfin = norm_final_g.reshape(1, d)
        qg = q_norm_g[l].reshape(1, Q_RANK)
        kvg = kv_norm_g[l].reshape(1, KV_RANK)
        gssd = norm_ssd_g[l].reshape(1, SSD_INNER)
        gattn = norm_attn_g[l].reshape(1, MLA_INNER)
        cw = conv_w[l]
        cb = conv_b[l].reshape(1, CONV_DIM)
        dtb = _row(dt_bias[l], LANES)
        alog = _row(a_log[l], LANES)
        dsk = jnp.repeat(d_skip[l].astype(F32), SSD_HEAD_DIM).reshape(1, SSD_INNER)

        ada = _ada(c_all, w_ada[l], b_ada[l])
        mod = jnp.concatenate([ada, ada_fin], axis=1)
        mod_p = mod[:b_p].reshape(b_p, 1, 8 * d)
        mod_s = jnp.repeat(mod[b_p:], t_new, axis=0).reshape(1, n_tok_s, 8 * d)

        tk = min(512, seq)
        z, xbc, dtr, ckv, kr, kf, kct, qt = _inproj(
            xp, mod_p, gmix, win, qg, kvg, wq.T, tab1_pt, tab1_p, q_transposed=True, tm=256, tk=tk)
        y_ssd, hfin = _ssd_prompt(xbc, z, dtr, cw, cb, dtb, alog, dsk, gssd)
        y_attn = _attn_prompt(qt, kf, kct, wuvt, gattn, tk=tk, ct=512)
        xp = _mlp(xp, y_ssd, y_attn, mod_p, wout, gmlp, wup, wdn, gfin, final=final, tm=512, tf=2048)
        tail = min(seq, CONV_WIDTH - 1)
        conv_tail = jnp.concatenate([jnp.zeros((b_p, CONV_WIDTH - 1 - tail, CONV_DIM), F32),
                                     xbc[:, seq - tail:]], axis=1)
        outs_p.append((ckv, kr, conv_tail, hfin.reshape(b_p, SSD_HEADS, SSD_HEAD_DIM, SSD_STATE)))

        z, xbc, dtr, ckv, kr, kc, krb, qa, qr = _inproj(
            xs, mod_s, gmix, win, qg, kvg, wq, tabq_s, tab1_s, q_transposed=False, tm=256)
        xbc3 = xbc.reshape(n_seq, t_new, CONV_DIM)
        xpad = jnp.concatenate([state_conv[l], xbc3], axis=1)
        xsh = jnp.stack([xpad[:, k:k + t_new].reshape(n_tok_s, CONV_DIM) for k in range(CONV_WIDTH)])
        ypre, eacs, xw, dec, bm, cm = _ssd_sample(xsh, dtr[0], cw, cb, dtb, alog, dsk, seg=t_new)

        def seqs(a):
            return a.reshape(n_seq, t_new, a.shape[-1])

        y_ssd, s_new = _sstate(seqs(cm), seqs(bm), seqs(ypre), seqs(eacs), seqs(xw), seqs(dec), seqs(z[0]),
                               state_ssm[l].reshape(n_seq, SSD_INNER, SSD_STATE), gssd, bs=8)
        o_lat = _sattn(page_table, qa.reshape(n_tok_s * MLA_HEADS, KV_RANK),
                       qr.reshape(n_tok_s * MLA_HEADS, QK_ROPE), seqs(kc[0]), seqs(krb[0]),
                       cache_kv_latent[l], jnp.swapaxes(cache_k_rope[l], -1, -2))
        y_attn = _apost(o_lat.reshape(n_tok_s, MLA_HEADS * KV_RANK), wuv_pad, gattn)
        xs = _mlp(xs, y_ssd.reshape(1, n_tok_s, SSD_INNER), y_attn.reshape(1, n_tok_s, MLA_INNER), mod_s,
                  wout, gmlp, wup, wdn, gfin, final=final, tm=n_tok_s, tf=2048)
        outs_s.append((seqs(ckv[0]), seqs(kr[0]), xpad[:, t_new:],
                       s_new.reshape(n_seq, SSD_HEADS, SSD_HEAD_DIM, SSD_STATE)))

    def stack(outs, k):
        return jnp.stack([o[k] for o in outs])

    return (xp, xs.reshape(n_seq, t_new, d),
            stack(outs_p, 0), stack(outs_p, 1), stack(outs_p, 2), stack(outs_p, 3),
            stack(outs_s, 0), stack(outs_s, 1), stack(outs_s, 2), stack(outs_s, 3))
```

```python
import functools
import math

import jax
import jax.numpy as jnp
import numpy as np
from jax import lax
from jax.experimental import pallas as pl
from jax.experimental.pallas import tpu as pltpu

F32 = jnp.float32
BF16 = jnp.bfloat16
SDS = jax.ShapeDtypeStruct

D_MODEL = 1024
SSD_HEADS = 8
SSD_HEAD_DIM = 64
SSD_INNER = SSD_HEADS * SSD_HEAD_DIM
SSD_GROUPS = 2
SSD_STATE = 128
CONV_WIDTH = 4
SSD_CHUNK = 128
CONV_DIM = SSD_INNER + 2 * SSD_GROUPS * SSD_STATE
MLA_HEADS = 8
QK_NOPE = 64
QK_ROPE = 32
V_DIM = 64
KV_RANK = 256
Q_RANK = 384
MLA_INNER = MLA_HEADS * V_DIM
ROPE_THETA = 10000.0
ATTN_SCALE = 1.0 / math.sqrt(QK_NOPE + QK_ROPE)
LOG2E = math.log2(math.e)
PAGE_SIZE = 128
D_FF = 4 * D_MODEL
EPS = 1e-6

LANES = 128
TQ = 128
QK_PAD = 384
SPLIT = 2
C_Z = 0
C_XBC = C_Z + SSD_INNER
C_QLAT = C_XBC + CONV_DIM
C_KVLAT = C_QLAT + Q_RANK
C_MISC = C_KVLAT + KV_RANK
W_IN_COLS = C_MISC + LANES
MISC_KR = 32
MISC_KRSW = 64
C_QROPE = MLA_HEADS * KV_RANK
C_QROPE_SW = C_QROPE + MLA_HEADS * QK_ROPE
WQ_COLS = C_QROPE_SW + MLA_HEADS * QK_ROPE

VMEM_LIMIT = 52 * 1024 * 1024


def _dot(a, b):
    return jnp.dot(a, b, preferred_element_type=F32)


def _dot_nt(a, b):
    return lax.dot_general(a, b, (((1,), (1,)), ((), ())), preferred_element_type=F32)


def _dot_tn(a, b):
    return lax.dot_general(a, b, (((0,), (0,)), ((), ())), preferred_element_type=F32)


def _silu(x):
    return x * jax.nn.sigmoid(x)


def _rms(x, g):
    return x * lax.rsqrt(jnp.mean(x * x, axis=-1, keepdims=True) + EPS) * g


def _split3_dot(mask_bf16, v):
    v1 = v.astype(BF16)
    r1 = v - v1.astype(F32)
    v2 = r1.astype(BF16)
    v3 = (r1 - v2.astype(F32)).astype(BF16)
    return _dot(mask_bf16, v1) + _dot(mask_bf16, v2) + _dot(mask_bf16, v3)


def _mod_spec(mod3, L, tm, d):
    rows = mod3.shape[1]
    if rows == 1:
        return 1, lambda k: pl.BlockSpec((None, 1, d), lambda b, i: (b, 0, k))
    rep = L // rows
    return rep, lambda k: pl.BlockSpec((None, tm // rep, d), lambda b, i: (b, i, k))


def _mod_rows(ref, rep):
    m = ref[...]
    if rep == 1:
        return m
    n = m.shape[0] * rep
    sel = (lax.broadcasted_iota(jnp.int32, (n, m.shape[0]), 0) // rep
           == lax.broadcasted_iota(jnp.int32, (n, m.shape[0]), 1))
    sel = jnp.where(sel, 1.0, 0.0).astype(BF16)
    hi = m.astype(BF16)
    lo = (m - hi.astype(F32)).astype(BF16)
    return _dot(sel, hi) + _dot(sel, lo)


def _params(sem, vmem=VMEM_LIMIT):
    return pltpu.CompilerParams(dimension_semantics=sem, vmem_limit_bytes=vmem)


def _ada_body(c_ref, w_ref, b_ref, o_ref):
    s = _silu(c_ref[...]).astype(BF16)
    o_ref[...] = _dot(s, w_ref[...].astype(BF16)) + b_ref[...]


def _ada(c, w, b):
    bsz, d = c.shape
    n = w.shape[1]
    tn = 1024
    return pl.pallas_call(
        _ada_body,
        grid=(n // tn,),
        in_specs=[pl.BlockSpec((bsz, d), lambda j: (0, 0)),
                  pl.BlockSpec((d, tn), lambda j: (0, j)),
                  pl.BlockSpec((1, tn), lambda j: (0, j))],
        out_specs=pl.BlockSpec((bsz, tn), lambda j: (0, j)),
        out_shape=SDS((bsz, n), F32),
        compiler_params=_params(("arbitrary",)),
        name="ada",
    )(c, w, b.reshape(1, n))


def _fold_body(wq_ref, wk_ref, o_ref):
    a = wq_ref[:, 0:QK_NOPE].astype(BF16)
    o_ref[...] = _dot_nt(a, wk_ref[...].astype(BF16)).astype(BF16)


def _fold(wq_h, wk_h):
    return pl.pallas_call(
        _fold_body,
        grid=(MLA_HEADS,),
        in_specs=[pl.BlockSpec((None, Q_RANK, QK_NOPE + QK_ROPE), lambda h: (h, 0, 0)),
                  pl.BlockSpec((None, KV_RANK, QK_NOPE), lambda h: (h, 0, 0))],
        out_specs=pl.BlockSpec((Q_RANK, KV_RANK), lambda h: (0, h)),
        out_shape=SDS((Q_RANK, MLA_HEADS * KV_RANK), BF16),
        compiler_params=_params(("arbitrary",)),
        name="fold",
    )(wq_h, wk_h)


def _inproj_body(*refs, q_transposed, rep):
    (x_ref, sh_ref, sc_ref, gmix_ref, win_ref, qg_ref, kvg_ref, wq_ref, tq_ref, tk_ref,
     z_ref, xbc_ref, dt_ref, ckv_ref, kr_ref) = refs[:15]
    h = _rms(x_ref[...], gmix_ref[...]) * (1.0 + _mod_rows(sc_ref, rep)) + _mod_rows(sh_ref, rep)
    proj = _dot(h.astype(BF16), win_ref[...])
    tm = proj.shape[0]
    z_ref[...] = proj[:, C_Z:C_XBC]
    xbc_ref[...] = proj[:, C_XBC:C_QLAT]
    q_scale = ATTN_SCALE * LOG2E if q_transposed else ATTN_SCALE
    qn = (_rms(proj[:, C_QLAT:C_KVLAT], qg_ref[...]) * q_scale).astype(BF16)
    ckv = _rms(proj[:, C_KVLAT:C_MISC], kvg_ref[...])
    ckv_ref[...] = ckv
    misc = proj[:, C_MISC:W_IN_COLS]
    lane = lax.broadcasted_iota(jnp.int32, misc.shape, 1)
    dt_ref[...] = jnp.where(lane < SSD_HEADS, misc, 0.0)
    tk = tk_ref[...]
    kr = (misc[:, MISC_KR:MISC_KR + QK_ROPE] * tk[:, :QK_ROPE]
          + misc[:, MISC_KRSW:MISC_KRSW + QK_ROPE] * tk[:, QK_ROPE:])
    kr_ref[...] = kr
    nr = MLA_HEADS * QK_ROPE
    if q_transposed:
        kf_ref, kct_ref, qt_ref = refs[15:]
        npad = QK_PAD - KV_RANK - QK_ROPE
        kf_ref[:, 0:KV_RANK] = ckv.astype(BF16)
        kct_ref[...] = ckv.T.astype(BF16)
        kf_ref[:, KV_RANK:QK_PAD] = jnp.concatenate([kr, jnp.zeros((tm, npad), F32)], axis=1).astype(BF16)
        qt = _dot_nt(wq_ref[...], qn)
        tq = tq_ref[...]
        cos_t = jnp.concatenate([tq[:QK_ROPE]] * MLA_HEADS, axis=0)
        sin_t = jnp.concatenate([tq[QK_ROPE:]] * MLA_HEADS, axis=0)
        rot = (qt[C_QROPE:C_QROPE_SW] * cos_t + qt[C_QROPE_SW:WQ_COLS] * sin_t).astype(BF16)
        zpad = jnp.zeros((npad, TQ), BF16)
        for c in range(tm // TQ):
            toks = slice(c * TQ, (c + 1) * TQ)
            for hh in range(MLA_HEADS):
                cols = slice(hh * TQ, (hh + 1) * TQ)
                qt_ref[c, 0:KV_RANK, cols] = qt[hh * KV_RANK:(hh + 1) * KV_RANK, toks].astype(BF16)
                qt_ref[c, KV_RANK:KV_RANK + QK_ROPE, cols] = rot[hh * QK_ROPE:(hh + 1) * QK_ROPE, toks]
                qt_ref[c, KV_RANK + QK_ROPE:QK_PAD, cols] = zpad
    else:
        kc_ref, krb_ref, qa_ref, qr_ref = refs[15:]
        kc_ref[...] = ckv.astype(BF16)
        krb_ref[...] = kr.astype(BF16)
        q = _dot(qn, wq_ref[...])
        tq = tq_ref[...]
        qa_ref[...] = q[:, :C_QROPE].astype(BF16)
        qr_ref[...] = (q[:, C_QROPE:C_QROPE_SW] * tq[:, :nr] + q[:, C_QROPE_SW:WQ_COLS] * tq[:, nr:]).astype(BF16)


def _inproj(x3, mod3, gmix, win, qg, kvg, wq, tabq, tabk, *, q_transposed, tm, tk=None):
    nb, L, d = x3.shape
    tm = min(tm, L)
    nt = L // tm
    rep, mod_spec = _mod_spec(mod3, L, tm, d)

    def tok(width):
        return pl.BlockSpec((None, tm, width), lambda b, i: (b, i, 0))

    def const(a):
        return pl.BlockSpec(a.shape, lambda b, i: (0,) * a.ndim, pipeline_mode=pl.Buffered(1))

    nr = MLA_HEADS * QK_ROPE
    out_specs = [tok(SSD_INNER), tok(CONV_DIM), tok(LANES), tok(KV_RANK), tok(QK_ROPE)]
    out_shape = [SDS((nb, L, SSD_INNER), F32), SDS((nb, L, CONV_DIM), F32), SDS((nb, L, LANES), F32),
                 SDS((nb, L, KV_RANK), F32), SDS((nb, L, QK_ROPE), F32)]
    if q_transposed:
        assert tm % TQ == 0 and tk % tm == 0
        r = tk // tm
        tabq_spec = pl.BlockSpec((2 * QK_ROPE, tm), lambda b, i: (0, i))
        out_specs += [tok(QK_PAD),
                      pl.BlockSpec((None, None, KV_RANK, tm), lambda b, i: (b, i // r, 0, i % r)),
                      pl.BlockSpec((None, tm // TQ, QK_PAD, MLA_HEADS * TQ), lambda b, i: (b, i, 0, 0))]
        out_shape += [SDS((nb, L, QK_PAD), BF16), SDS((nb, L // tk, KV_RANK, tk), BF16),
                      SDS((nb, L // TQ, QK_PAD, MLA_HEADS * TQ), BF16)]
    else:
        tabq_spec = pl.BlockSpec((tm, 2 * nr), lambda b, i: (i, 0))
        out_specs += [tok(KV_RANK), tok(QK_ROPE), tok(C_QROPE), tok(nr)]
        out_shape += [SDS((nb, L, KV_RANK), BF16), SDS((nb, L, QK_ROPE), BF16),
                      SDS((nb, L, C_QROPE), BF16), SDS((nb, L, nr), BF16)]
    return pl.pallas_call(
        functools.partial(_inproj_body, q_transposed=q_transposed, rep=rep),
        grid=(nb, nt),
        in_specs=[tok(d), mod_spec(0), mod_spec(1), const(gmix), const(win), const(qg), const(kvg), const(wq),
                  tabq_spec, pl.BlockSpec((tm, 2 * QK_ROPE), lambda b, i: (i, 0))],
        out_specs=out_specs,
        out_shape=out_shape,
        compiler_params=_params(("arbitrary", "arbitrary")),
        name="inproj_prompt" if q_transposed else "inproj_sample",
    )(x3, mod3, mod3, gmix, win, qg, kvg, wq, tabq, tabk)


def _ssd_body(*refs, Q, seg, carry):
    if carry:
        (xin_ref, z_ref, dt_ref, cw_ref, cb_ref, dtb_ref, alog_ref, dsk_ref, g_ref,
         y_ref, hfin_ref, xbuf, hT) = refs
    else:
        (xin_ref, dt_ref, cw_ref, cb_ref, dtb_ref, alog_ref, dsk_ref,
         ypre_ref, eacs_ref, xw_ref, dec_ref, bm_ref, cm_ref) = refs
    cw = cw_ref[...]
    acc = jnp.broadcast_to(cb_ref[...], (Q, CONV_DIM))
    if carry:
        c = pl.program_id(1)

        @pl.when(c == 0)
        def _():
            xbuf[0:8, :] = jnp.zeros((8, CONV_DIM), F32)
            hT[...] = jnp.zeros(hT.shape, F32)

        xbuf[8:8 + Q, :] = xin_ref[...]
        for k in range(CONV_WIDTH):
            acc = acc + cw[k:k + 1, :] * xbuf[pl.ds(8 - (CONV_WIDTH - 1) + k, Q), :]
        xbuf[0:8, :] = xbuf[Q:Q + 8, :]
    else:
        for k in range(CONV_WIDTH):
            acc = acc + cw[k:k + 1, :] * xin_ref[k]
    xc = _silu(acc)
    xs = xc[:, :SSD_INNER]
    gs = SSD_GROUPS * SSD_STATE
    bm = xc[:, SSD_INNER:SSD_INNER + gs]
    cm = xc[:, SSD_INNER + gs:]
    bm_b = bm.astype(BF16)
    cm_b = cm.astype(BF16)

    lane = lax.broadcasted_iota(jnp.int32, (Q, LANES), 1)
    v = dt_ref[...] + dtb_ref[...]
    dt = jnp.maximum(v, 0.0) + jnp.log1p(jnp.exp(-jnp.abs(v)))
    dt = jnp.where(lane < SSD_HEADS, dt, 0.0)
    dA = dt * (-jnp.exp(alog_ref[...]))
    ri = lax.broadcasted_iota(jnp.int32, (Q, Q), 0)
    ci = lax.broadcasted_iota(jnp.int32, (Q, Q), 1)
    if seg == Q:
        mask = ci <= ri
    else:
        same = (ri // seg) == (ci // seg)
        mask = jnp.logical_and(same, ci <= ri)
    acs = _split3_dot(jnp.where(mask, 1.0, 0.0).astype(BF16), dA)
    if seg == Q:
        acs_last = acs[Q - 1:Q, :]
    else:
        acs_last = _split3_dot(jnp.where(same, 1.0, 0.0).astype(BF16), dA)
    to_end = jnp.exp(acs_last - acs) * dt
    acsT = acs.T
    dtT = dt.T

    G = [_dot_nt(cm_b[:, g * SSD_STATE:(g + 1) * SSD_STATE], bm_b[:, g * SSD_STATE:(g + 1) * SSD_STATE])
         for g in range(SSD_GROUPS)]
    lane_lo = lane < SSD_HEAD_DIM
    heads_per_group = SSD_HEADS // SSD_GROUPS
    ypairs, epairs, xwpairs, decpairs = [], [], [], []
    for k in range(SSD_HEADS // 2):
        g = (2 * k) // heads_per_group
        xp = xs[:, k * LANES:(k + 1) * LANES]
        xhalf = (jnp.where(lane_lo, xp, 0.0).astype(BF16), jnp.where(lane_lo, 0.0, xp).astype(BF16))
        yk = jnp.zeros((Q, LANES), F32)
        for s in range(2):
            hh = 2 * k + s
            segm = acs[:, hh:hh + 1] - acsT[hh:hh + 1, :]
            m = G[g] * jnp.exp(jnp.where(mask, segm, -jnp.inf)) * dtT[hh:hh + 1, :]
            yk = yk + _dot(m.astype(BF16), xhalf[s])

        def pair(a):
            return jnp.where(lane_lo[:a.shape[0]], a[:, 2 * k:2 * k + 1], a[:, 2 * k + 1:2 * k + 2])

        e_p = jnp.exp(pair(acs))
        xw = xp * pair(to_end)
        dec = jnp.exp(pair(acs_last))
        if carry:
            h_prev = hT[k]
            yk = yk + _dot(cm_b[:, g * SSD_STATE:(g + 1) * SSD_STATE], h_prev.astype(BF16)) * e_p
            hT[k] = dec * h_prev + _dot_tn(bm_b[:, g * SSD_STATE:(g + 1) * SSD_STATE], xw.astype(BF16))
        else:
            epairs.append(e_p)
            xwpairs.append(xw)
            decpairs.append(dec)
        ypairs.append(yk)
    y = jnp.concatenate(ypairs, axis=1) + dsk_ref[...] * xs
    if carry:
        y = y * _silu(z_ref[...])
        y_ref[...] = _rms(y, g_ref[...]).astype(BF16)

        @pl.when(c == pl.num_programs(1) - 1)
        def _():
            for k in range(SSD_HEADS // 2):
                hfin_ref[k * LANES:(k + 1) * LANES, :] = hT[k].T
    else:
        ypre_ref[...] = y
        eacs_ref[...] = jnp.concatenate(epairs, axis=1)
        xw_ref[...] = jnp.concatenate(xwpairs, axis=1)
        dec_ref[...] = jnp.concatenate(decpairs, axis=1)
        bm_ref[...] = bm
        cm_ref[...] = cm


def _row(a, n):
    return jnp.pad(a.reshape(1, -1).astype(F32), ((0, 0), (0, n - a.size)))


def _ssd_prompt(xbc, z, dt, cw, cb, dtb, alog, dsk, g):
    nb, L, _ = xbc.shape
    Q = SSD_CHUNK if L % SSD_CHUNK == 0 else L
    nc = L // Q

    def tok(width):
        return pl.BlockSpec((None, Q, width), lambda b, c: (b, c, 0))

    def const(a):
        return pl.BlockSpec(a.shape, lambda b, c: (0,) * a.ndim)

    consts = (cw, cb, dtb, alog, dsk, g)
    return pl.pallas_call(
        functools.partial(_ssd_body, Q=Q, seg=Q, carry=True),
        grid=(nb, nc),
        in_specs=[tok(CONV_DIM), tok(SSD_INNER), tok(LANES)] + [const(a) for a in consts],
        out_specs=[tok(SSD_INNER), pl.BlockSpec((None, SSD_INNER, SSD_STATE), lambda b, c: (b, 0, 0))],
        out_shape=[SDS((nb, L, SSD_INNER), BF16), SDS((nb, SSD_INNER, SSD_STATE), F32)],
        scratch_shapes=[pltpu.VMEM((Q + 8, CONV_DIM), F32), pltpu.VMEM((SSD_HEADS // 2, SSD_STATE, LANES), F32)],
        compiler_params=_params(("arbitrary", "arbitrary")),
        name="ssd_prompt",
    )(xbc, z, dt, *consts)


def _ssd_sample(xsh, dt, cw, cb, dtb, alog, dsk, *, seg):
    _, T, _ = xsh.shape
    consts = (cw, cb, dtb, alog, dsk)

    def full(a):
        return pl.BlockSpec(a.shape, lambda i: (0,) * a.ndim)

    outs = [SDS((T, SSD_INNER), F32)] * 4 + [SDS((T, SSD_GROUPS * SSD_STATE), F32)] * 2
    return pl.pallas_call(
        functools.partial(_ssd_body, Q=T, seg=seg, carry=False),
        grid=(1,),
        in_specs=[full(xsh), full(dt)] + [full(a) for a in consts],
        out_specs=[full(o) for o in outs],
        out_shape=outs,
        compiler_params=_params(("arbitrary",)),
        name="ssd_sample",
    )(xsh, dt, *consts)


def _sstate_body(cm_ref, bm_ref, ypre_ref, eacs_ref, xw_ref, dec_ref, z_ref, s0_ref, g_ref, y_ref, sn_ref):
    s0 = s0_ref[...]
    s0b = s0.astype(BF16)
    cm = cm_ref[...].astype(BF16)
    bm = bm_ref[...].astype(BF16)
    rows = SSD_INNER // SSD_GROUPS
    yo = jnp.concatenate(
        [jnp.einsum("btn,bqn->btq", cm[:, :, g * SSD_STATE:(g + 1) * SSD_STATE],
                    s0b[:, g * rows:(g + 1) * rows, :], preferred_element_type=F32)
         for g in range(SSD_GROUPS)], axis=-1)
    y = (ypre_ref[...] + yo * eacs_ref[...]) * _silu(z_ref[...])
    y_ref[...] = _rms(y, g_ref[...]).astype(BF16)
    dec = dec_ref[...]
    hi = dec.astype(BF16)
    lo = (dec - hi.astype(F32)).astype(BF16)
    sel = jnp.where(lax.broadcasted_iota(jnp.int32, (dec.shape[0], dec.shape[1], SSD_STATE), 1) == 0,
                    1.0, 0.0).astype(BF16)
    dmat = (jnp.einsum("bjq,bjn->bqn", hi, sel, preferred_element_type=F32)
            + jnp.einsum("bjq,bjn->bqn", lo, sel, preferred_element_type=F32))
    xw = xw_ref[...].astype(BF16)
    upd = jnp.concatenate(
        [jnp.einsum("bjq,bjn->bqn", xw[:, :, g * rows:(g + 1) * rows],
                    bm[:, :, g * SSD_STATE:(g + 1) * SSD_STATE], preferred_element_type=F32)
         for g in range(SSD_GROUPS)], axis=1)
    sn_ref[...] = dmat * s0 + upd


def _sstate(cm, bm, ypre, eacs, xw, dec, z, s0, g, *, bs):
    nseq, t, _ = cm.shape
    bs = min(bs, nseq)

    def blk(a):
        return pl.BlockSpec((bs,) + a.shape[1:], lambda i: (i, 0, 0))

    ins = (cm, bm, ypre, eacs, xw, dec, z, s0)
    return pl.pallas_call(
        _sstate_body,
        grid=(nseq // bs,),
        in_specs=[blk(a) for a in ins] + [pl.BlockSpec(g.shape, lambda i: (0, 0))],
        out_specs=[blk(ypre), blk(s0)],
        out_shape=[SDS(ypre.shape, BF16), SDS(s0.shape, F32)],
        compiler_params=_params(("arbitrary",)),
        name="sstate",
    )(*ins, g)


def _attn_body(qt_ref, qn_ref, kf_ref, kct_ref, wuvt_ref, g_ref, o_ref, m_sc, l_sc, acc_sc,
               s_a, s_b, s_c, smax_a, smax_b, smax_c, *, tk, ct):
    i = pl.program_id(1)
    ncols = MLA_HEADS * TQ
    m_sc[...] = jnp.full(m_sc.shape, -jnp.inf, F32)
    l_sc[...] = jnp.zeros(l_sc.shape, F32)
    acc_sc[...] = jnp.zeros(acc_sc.shape, F32)
    tiles = [slice(n * ct, (n + 1) * ct) for n in range(ncols // ct)]

    def scores(j, s_ref, smax_ref, q_ref=qt_ref):
        kf = kf_ref[pl.ds(pl.multiple_of(j * tk, tk), tk), :]
        for cs in tiles:
            s = _dot(kf, q_ref[:, cs])
            s_ref[:, cs] = s
            smax_ref[:, cs] = jnp.max(s, axis=0, keepdims=True)

    def softmax_pv(j, s_ref, smax_ref, masked):
        kct = kct_ref[j]
        for cs in tiles:
            s = s_ref[:, cs]
            if masked:
                key = j * tk + lax.broadcasted_iota(jnp.int32, (tk, ct), 0)
                tok = i * TQ + (lax.broadcasted_iota(jnp.int32, (tk, ct), 1) & (TQ - 1))
                s = jnp.where(key <= tok, s, -jnp.inf)
                smax = jnp.max(s, axis=0, keepdims=True)
            else:
                smax = smax_ref[:, cs]
            m_prev = m_sc[:, cs]
            m_new = jnp.maximum(m_prev, smax)
            alpha = jnp.exp2(m_prev - m_new)
            p = jnp.exp2(s - m_new)
            l_sc[:, cs] = alpha * l_sc[:, cs] + jnp.sum(p, axis=0, keepdims=True)
            acc_sc[:, cs] = alpha * acc_sc[:, cs] + _dot(kct, p.astype(BF16))
            m_sc[:, cs] = m_new

    nfull = (i * TQ) // tk

    @pl.when(i == 0)
    def _():
        scores(0, s_c, smax_c)

    def prefetch():
        scores(0, s_c, smax_c, qn_ref)

    @pl.when(nfull == 0)
    def _():
        softmax_pv(0, s_c, smax_c, True)
        prefetch()

    @pl.when(nfull >= 1)
    def _():
        scores(1, s_b, smax_b)
        softmax_pv(0, s_c, smax_c, False)

    def pair(p, carry):
        j = 2 * p + 1
        scores(j + 1, s_a, smax_a)
        softmax_pv(j, s_b, smax_b, False)
        scores(j + 2, s_b, smax_b)
        softmax_pv(j + 1, s_a, smax_a, False)
        return carry

    lax.fori_loop(0, jnp.maximum(nfull - 1, 0) // 2, pair, 0)
    odd = (nfull % 2) == 1

    @pl.when(odd)
    def _():
        prefetch()
        softmax_pv(nfull, s_b, smax_b, True)

    @pl.when(jnp.logical_and(jnp.logical_not(odd), nfull >= 2))
    def _():
        scores(nfull, s_a, smax_a)
        softmax_pv(nfull - 1, s_b, smax_b, False)
        prefetch()
        softmax_pv(nfull, s_a, smax_a, True)

    inv = 1.0 / l_sc[...]
    ys = []
    for hh in range(MLA_HEADS):
        cs = slice(hh * TQ, (hh + 1) * TQ)
        ys.append(_dot(wuvt_ref[hh], (acc_sc[:, cs] * inv[:, cs]).astype(BF16)))
    yt = jnp.concatenate(ys, axis=0)
    yt = yt * lax.rsqrt(jnp.mean(yt * yt, axis=0, keepdims=True) + EPS)
    o_ref[...] = (yt.T * g_ref[...]).astype(BF16)


def _attn_prompt(qt, kf, kct, wuvt, g, *, tk, ct):
    nb, nq, _, ncols = qt.shape
    L = nq * TQ
    assert TQ & (TQ - 1) == 0 and L % tk == 0 and tk % TQ == 0 and ct % TQ == 0 and ncols % ct == 0
    return pl.pallas_call(
        functools.partial(_attn_body, tk=tk, ct=ct),
        grid=(nb, nq),
        in_specs=[pl.BlockSpec((None, None, QK_PAD, ncols), lambda b, i: (b, i, 0, 0)),
                  pl.BlockSpec((None, None, QK_PAD, ncols), lambda b, i: (b, jnp.minimum(i + 1, nq - 1), 0, 0)),
                  pl.BlockSpec((None, L, QK_PAD), lambda b, i: (b, 0, 0)),
                  pl.BlockSpec((None, L // tk, KV_RANK, tk), lambda b, i: (b, 0, 0, 0)),
                  pl.BlockSpec(wuvt.shape, lambda b, i: (0, 0, 0)),
                  pl.BlockSpec(g.shape, lambda b, i: (0, 0))],
        out_specs=pl.BlockSpec((None, TQ, MLA_INNER), lambda b, i: (b, i, 0)),
        out_shape=SDS((nb, L, MLA_INNER), BF16),
        scratch_shapes=[pltpu.VMEM((1, ncols), F32), pltpu.VMEM((1, ncols), F32),
                        pltpu.VMEM((KV_RANK, ncols), F32),
                        pltpu.VMEM((tk, ncols), F32), pltpu.VMEM((tk, ncols), F32), pltpu.VMEM((tk, ncols), F32),
                        pltpu.VMEM((1, ncols), F32), pltpu.VMEM((1, ncols), F32), pltpu.VMEM((1, ncols), F32)],
        compiler_params=_params(("arbitrary", "arbitrary")),
        name="attn_prompt",
    )(qt, qt, kf, kct, wuvt, g)


def _sattn_body(pt_ref, qa_ref, qr_ref, kn_ref, rn_ref, ckv_hbm, krt_hbm, o_ref, kbuf, rbuf, sem, *, npages, t_new):
    b = pl.program_id(0)
    nseq = pl.num_programs(0)

    def copies(seq_page, slot, p):
        off = p * PAGE_SIZE
        return (pltpu.make_async_copy(ckv_hbm.at[seq_page], kbuf.at[slot, p], sem.at[0, slot]),
                pltpu.make_async_copy(krt_hbm.at[seq_page], rbuf.at[slot, :, pl.ds(off, PAGE_SIZE)],
                                      sem.at[1, slot]))

    def start_fetch(seq, slot):
        for p in range(npages):
            for cp in copies(pt_ref[seq * npages + p], slot, p):
                cp.start()

    def wait_fetch(slot):
        pltpu.make_async_copy(ckv_hbm.at[pl.ds(0, npages)], kbuf.at[slot], sem.at[0, slot]).wait()
        pltpu.make_async_copy(rbuf.at[slot], rbuf.at[slot], sem.at[1, slot]).wait()

    @pl.when(b == 0)
    def _():
        start_fetch(0, 0)

    @pl.when(b + 1 < nseq)
    def _():
        start_fetch(b + 1, (b + 1) % 2)

    slot = b % 2
    wait_fetch(slot)
    pg = npages // SPLIT
    part = pg * PAGE_SIZE
    kps = [kbuf[slot, pl.ds(h * pg, pg)].reshape(part, KV_RANK).astype(BF16) for h in range(SPLIT)]
    rpt = rbuf[slot].astype(BF16)
    q = qa_ref[...]
    qr = qr_ref[...]
    kn = kn_ref[...]
    rn = rn_ref[...]
    s_main = [_dot_nt(q, kps[h]) for h in range(SPLIT)]
    s_rope = [_dot(qr, rpt[:, h * part:(h + 1) * part]) for h in range(SPLIT)]
    s_p = [a + b for a, b in zip(s_main, s_rope)]
    s_n = _dot_nt(q, kn) + _dot_nt(qr, rn)
    rows = t_new * MLA_HEADS
    tok = lax.broadcasted_iota(jnp.int32, (rows, t_new), 0) // MLA_HEADS
    col = lax.broadcasted_iota(jnp.int32, (rows, t_new), 1)
    s_n = jnp.where(col <= tok, s_n, -jnp.inf)
    m = jnp.max(s_n, axis=-1, keepdims=True)
    for s in s_p:
        m = jnp.maximum(m, jnp.max(s, axis=-1, keepdims=True))
    pn = jnp.exp(s_n - m)
    pp = [jnp.exp(s - m) for s in s_p]
    l = jnp.sum(pn, axis=-1, keepdims=True)
    for p in pp:
        l = l + jnp.sum(p, axis=-1, keepdims=True)
    inv = 1.0 / l
    parts = [_dot(pp[h].astype(BF16), kps[h]) for h in range(SPLIT)]
    o = _dot(pn.astype(BF16), kn) * inv
    for part_o in parts:
        o = o + part_o * inv
    o_ref[...] = o


def _sattn(page_table, qa, qr, kn, rn, cache_kv, cache_krt):
    nseq, npages = page_table.shape
    t_new = kn.shape[1]
    rows = t_new * MLA_HEADS
    grid_spec = pltpu.PrefetchScalarGridSpec(
        num_scalar_prefetch=1,
        grid=(nseq,),
        in_specs=[pl.BlockSpec((rows, KV_RANK), lambda b, pt: (b, 0)),
                  pl.BlockSpec((rows, QK_ROPE), lambda b, pt: (b, 0)),
                  pl.BlockSpec((None, t_new, KV_RANK), lambda b, pt: (b, 0, 0)),
                  pl.BlockSpec((None, t_new, QK_ROPE), lambda b, pt: (b, 0, 0)),
                  pl.BlockSpec(memory_space=pl.ANY),
                  pl.BlockSpec(memory_space=pl.ANY)],
        out_specs=pl.BlockSpec((rows, KV_RANK), lambda b, pt: (b, 0)),
        scratch_shapes=[pltpu.VMEM((2, npages, PAGE_SIZE, KV_RANK), F32),
                        pltpu.VMEM((2, QK_ROPE, npages * PAGE_SIZE), F32),
                        pltpu.SemaphoreType.DMA((2, 2))],
    )
    return pl.pallas_call(
        functools.partial(_sattn_body, npages=npages, t_new=t_new),
        grid_spec=grid_spec,
        out_shape=SDS((nseq * rows, KV_RANK), F32),
        compiler_params=_params(("arbitrary",)),
        name="attn_sample",
    )(page_table.reshape(-1), qa, qr, kn, rn, cache_kv, cache_krt)


def _apost_body(o_ref, wuv_ref, g_ref, y_ref):
    tm = o_ref.shape[0]
    o = jnp.zeros((tm, MLA_INNER), F32)
    for hh in range(MLA_HEADS):
        o = o + _dot(o_ref[:, hh * KV_RANK:(hh + 1) * KV_RANK].astype(BF16), wuv_ref[hh])
    y_ref[...] = _rms(o, g_ref[...]).astype(BF16)


def _apost(o_lat, wuv_pad, g):
    T = o_lat.shape[0]
    return pl.pallas_call(
        _apost_body,
        grid=(1,),
        in_specs=[pl.BlockSpec(o_lat.shape, lambda i: (0, 0)),
                  pl.BlockSpec(wuv_pad.shape, lambda i: (0, 0, 0)),
                  pl.BlockSpec(g.shape, lambda i: (0, 0))],
        out_specs=pl.BlockSpec((T, MLA_INNER), lambda i: (0, 0)),
        out_shape=SDS((T, MLA_INNER), BF16),
        compiler_params=_params(("arbitrary",)),
        name="attn_post",
    )(o_lat, wuv_pad, g)


def _mlp_body(x_ref, ys_ref, ya_ref, g1_ref, sh2_ref, sc2_ref, g2_ref, shf_ref, scf_ref,
              wout_ref, gmlp_ref, wup_ref, wdn_ref, gfin_ref, o_ref, *, final, tf, rep):
    yy = jnp.concatenate([ys_ref[...], ya_ref[...]], axis=-1)
    x1 = x_ref[...] + _mod_rows(g1_ref, rep) * _dot(yy, wout_ref[...])
    h2 = (_rms(x1, gmlp_ref[...]) * (1.0 + _mod_rows(sc2_ref, rep)) + _mod_rows(sh2_ref, rep)).astype(BF16)
    acc = jnp.zeros(x1.shape, F32)
    for c in range(D_FF // tf):
        u = jnp.maximum(_dot(h2, wup_ref[:, c * tf:(c + 1) * tf]), 0.0)
        acc = acc + _dot((u * u).astype(BF16), wdn_ref[c * tf:(c + 1) * tf, :])
    x2 = x1 + _mod_rows(g2_ref, rep) * acc
    if final:
        x2 = _rms(x2, gfin_ref[...]) * (1.0 + _mod_rows(scf_ref, rep)) + _mod_rows(shf_ref, rep)
    o_ref[...] = x2


def _mlp(x3, ys, ya, mod3, wout, gmlp, wup, wdn, gfin, *, final, tm, tf):
    nb, L, d = x3.shape
    tm = min(tm, L)
    rep, mod_spec = _mod_spec(mod3, L, tm, d)

    def tok(width):
        return pl.BlockSpec((None, tm, width), lambda b, i: (b, i, 0))

    def const(a):
        return pl.BlockSpec(a.shape, lambda b, i: (0,) * a.ndim, pipeline_mode=pl.Buffered(1))

    return pl.pallas_call(
        functools.partial(_mlp_body, final=final, tf=tf, rep=rep),
        grid=(nb, L // tm),
        in_specs=[tok(d), tok(SSD_INNER), tok(MLA_INNER)] + [mod_spec(k) for k in (2, 3, 4, 5, 6, 7)]
                 + [const(wout), const(gmlp), const(wup), const(wdn), const(gfin)],
        out_specs=tok(d),
        out_shape=SDS((nb, L, d), F32),
        compiler_params=_params(("arbitrary", "arbitrary")),
        name="mlp",
    )(x3, ys, ya, mod3, mod3, mod3, mod3, mod3, mod3, wout, gmlp, wup, wdn, gfin)


def _rope_tables(pos):
    inv = 1.0 / (ROPE_THETA ** (np.arange(0, QK_ROPE, 2, dtype=np.float64) / QK_ROPE))
    ang = pos.astype(np.float64)[:, None] * inv[None, :]
    cos, sin = np.cos(ang).astype(np.float32), np.sin(ang).astype(np.float32)
    c32 = np.concatenate([cos, cos], axis=-1)
    s32 = np.concatenate([-sin, sin], axis=-1)
    tab1 = np.concatenate([c32, s32], axis=-1)
    tabq = np.concatenate([np.tile(c32, (1, MLA_HEADS)), np.tile(s32, (1, MLA_HEADS))], axis=-1)
    return tab1, tabq


def _swap_halves(w):
    half = w.shape[-1] // 2
    return jnp.concatenate([w[..., half:], w[..., :half]], axis=-1)


def kernel(x_prompt, x_sample, cache_kv_latent, cache_k_rope, state_conv, state_ssm, page_table,
           c_prompt, c_sample, w_ada, b_ada, norm_mix_g, w_in, conv_w, conv_b, dt_bias, a_log,
           d_skip, norm_ssd_g, q_norm_g, kv_norm_g, w_uq, w_uk, w_uv, norm_attn_g, w_out,
           norm_mlp_g, w_up, w_down, w_ada_final, b_ada_final, norm_final_g):
    depth = w_in.shape[0]
    b_p, seq, d = x_prompt.shape
    n_seq, t_new, _ = x_sample.shape
    n_tok_s = n_seq * t_new
    past_len = page_table.shape[1] * PAGE_SIZE

    c_all = jnp.concatenate([c_prompt, c_sample], axis=0)
    ada_fin = _ada(c_all, w_ada_final, b_ada_final)
    tab1_p, _ = _rope_tables(np.arange(seq))
    tab1_s, tabq_s = _rope_tables(past_len + np.arange(t_new))
    tab1_pt = jnp.asarray(np.ascontiguousarray(tab1_p.T))
    tab1_p = jnp.asarray(tab1_p)
    tabq_s = jnp.asarray(np.tile(tabq_s, (n_seq, 1)))
    tab1_s = jnp.asarray(np.tile(tab1_s, (n_seq, 1)))

    xp = x_prompt
    xs = x_sample.reshape(1, n_tok_s, d)
    outs_p, outs_s = [], []
    for l in range(depth):
        final = l == depth - 1
        wi = w_in[l]
        c1 = SSD_INNER
        c2 = c1 + CONV_DIM
        c3 = c2 + SSD_HEADS
        c4 = c3 + Q_RANK
        c5 = c4 + KV_RANK
        w_kr = wi[:, c5:]
        win = jnp.concatenate(
            [wi[:, :c2], wi[:, c3:c5], wi[:, c2:c3], jnp.zeros((d, MISC_KR - SSD_HEADS), F32), w_kr,
             _swap_halves(w_kr), jnp.zeros((d, LANES - MISC_KRSW - QK_ROPE), F32)], axis=1).astype(BF16)
        wq_h = w_uq[l].reshape(Q_RANK, MLA_HEADS, QK_NOPE + QK_ROPE)
        w_rope = wq_h[:, :, QK_NOPE:]
        wfold = _fold(jnp.transpose(wq_h, (1, 0, 2)), jnp.transpose(w_uk[l], (1, 0, 2)))
        wq = jnp.concatenate([wfold, w_rope.reshape(Q_RANK, -1).astype(BF16),
                              _swap_halves(w_rope).reshape(Q_RANK, -1).astype(BF16)], axis=1)
        wuv_pad = jnp.zeros((MLA_HEADS, KV_RANK, MLA_HEADS, V_DIM), F32)
        wuv_pad = wuv_pad.at[jnp.arange(MLA_HEADS), :, jnp.arange(MLA_HEADS), :].set(
            jnp.transpose(w_uv[l], (1, 0, 2)))
        wuv_pad = wuv_pad.reshape(MLA_HEADS, KV_RANK, MLA_INNER).astype(BF16)
        wuvt = jnp.transpose(w_uv[l], (1, 2, 0)).astype(BF16)
        wout = w_out[l].astype(BF16)
        wup = w_up[l].astype(BF16)
        wdn = w_down[l].astype(BF16)
        gmix = norm_mix_g[l].reshape(1, d)
        gmlp = norm_mlp_g[l].reshape(1, d)
        gfin = norm_final_g.reshape(1, d)
        qg = q_norm_g[l].reshape(1, Q_RANK)
        kvg = kv_norm_g[l].reshape(1, KV_RANK)
        gssd = norm_ssd_g[l].reshape(1, SSD_INNER)
        gattn = norm_attn_g[l].reshape(1, MLA_INNER)
        cw = conv_w[l]
        cb = conv_b[l].reshape(1, CONV_DIM)
        dtb = _row(dt_bias[l], LANES)
        alog = _row(a_log[l], LANES)
        dsk = jnp.repeat(d_skip[l].astype(F32), SSD_HEAD_DIM).reshape(1, SSD_INNER)

        ada = _ada(c_all, w_ada[l], b_ada[l])
        mod = jnp.concatenate([ada, ada_fin], axis=1)
        mod_p = mod[:b_p].reshape(b_p, 1, 8 * d)
        mod_s = mod[b_p:].reshape(1, n_seq, 8 * d)

        tk = min(512, seq)
        z, xbc, dtr, ckv, kr, kf, kct, qt = _inproj(
            xp, mod_p, gmix, win, qg, kvg, wq.T, tab1_pt, tab1_p, q_transposed=True, tm=512, tk=tk)
        y_ssd, hfin = _ssd_prompt(xbc, z, dtr, cw, cb, dtb, alog, dsk, gssd)
        y_attn = _attn_prompt(qt, kf, kct, wuvt, gattn, tk=tk, ct=512)
        xp = _mlp(xp, y_ssd, y_attn, mod_p, wout, gmlp, wup, wdn, gfin, final=final, tm=512, tf=2048)
        tail = min(seq, CONV_WIDTH - 1)
        conv_tail = jnp.concatenate([jnp.zeros((b_p, CONV_WIDTH - 1 - tail, CONV_DIM), F32),
                                     xbc[:, seq - tail:]], axis=1)
        outs_p.append((ckv, kr, conv_tail, hfin.reshape(b_p, SSD_HEADS, SSD_HEAD_DIM, SSD_STATE)))

        z, xbc, dtr, ckv, kr, kc, krb, qa, qr = _inproj(
            xs, mod_s, gmix, win, qg, kvg, wq, tabq_s, tab1_s, q_transposed=False, tm=256)
        xbc3 = xbc.reshape(n_seq, t_new, CONV_DIM)
        xpad = jnp.concatenate([state_conv[l], xbc3], axis=1)
        xsh = jnp.stack([xpad[:, k:k + t_new].reshape(n_tok_s, CONV_DIM) for k in range(CONV_WIDTH)])
        ypre, eacs, xw, dec, bm, cm = _ssd_sample(xsh, dtr[0], cw, cb, dtb, alog, dsk, seg=t_new)

        def seqs(a):
            return a.reshape(n_seq, t_new, a.shape[-1])

        y_ssd, s_new = _sstate(seqs(cm), seqs(bm), seqs(ypre), seqs(eacs), seqs(xw), seqs(dec), seqs(z[0]),
                               state_ssm[l].reshape(n_seq, SSD_INNER, SSD_STATE), gssd, bs=8)
        o_lat = _sattn(page_table, qa.reshape(n_tok_s * MLA_HEADS, KV_RANK),
                       qr.reshape(n_tok_s * MLA_HEADS, QK_ROPE), seqs(kc[0]), seqs(krb[0]),
                       cache_kv_latent[l], jnp.swapaxes(cache_k_rope[l], -1, -2))
        y_attn = _apost(o_lat.reshape(n_tok_s, MLA_HEADS * KV_RANK), wuv_pad, gattn)
        xs = _mlp(xs, y_ssd.reshape(1, n_tok_s, SSD_INNER), y_attn.reshape(1, n_tok_s, MLA_INNER), mod_s,
                  wout, gmlp, wup, wdn, gfin, final=final, tm=n_tok_s, tf=2048)
        outs_s.append((seqs(ckv[0]), seqs(kr[0]), xpad[:, t_new:],
                       s_new.reshape(n_seq, SSD_HEADS, SSD_HEAD_DIM, SSD_STATE)))

    def stack(outs, k):
        return jnp.stack([o[k] for o in outs])

    return (xp, xs.reshape(n_seq, t_new, d),
            stack(outs_p, 0), stack(outs_p, 1), stack(outs_p, 2), stack(outs_p, 3),
            stack(outs_s, 0), stack(outs_s, 1), stack(outs_s, 2), stack(outs_s, 3))
```

```python
import functools
import math

import jax
import jax.numpy as jnp
import numpy as np
from jax import lax
from jax.experimental import pallas as pl
from jax.experimental.pallas import tpu as pltpu

F32 = jnp.float32
BF16 = jnp.bfloat16
SDS = jax.ShapeDtypeStruct

D_MODEL = 1024
SSD_HEADS = 8
SSD_HEAD_DIM = 64
SSD_INNER = SSD_HEADS * SSD_HEAD_DIM
SSD_GROUPS = 2
SSD_STATE = 128
CONV_WIDTH = 4
SSD_CHUNK = 128
CONV_DIM = SSD_INNER + 2 * SSD_GROUPS * SSD_STATE
MLA_HEADS = 8
QK_NOPE = 64
QK_ROPE = 32
V_DIM = 64
KV_RANK = 256
Q_RANK = 384
MLA_INNER = MLA_HEADS * V_DIM
ROPE_THETA = 10000.0
ATTN_SCALE = 1.0 / math.sqrt(QK_NOPE + QK_ROPE)
LOG2E = math.log2(math.e)
PAGE_SIZE = 128
D_FF = 4 * D_MODEL
EPS = 1e-6

LANES = 128
TQ = 128
PAIRS = MLA_HEADS // 2
PAIR_K = 256
SPLIT = 2
C_Z = 0
C_XBC = C_Z + SSD_INNER
C_QLAT = C_XBC + CONV_DIM
C_KVLAT = C_QLAT + Q_RANK
C_MISC = C_KVLAT + KV_RANK
W_IN_COLS = C_MISC + LANES
MISC_KR = 32
MISC_KRSW = 64
C_QROPE = MLA_HEADS * KV_RANK
C_QROPE_SW = C_QROPE + MLA_HEADS * QK_ROPE
WQ_COLS = C_QROPE_SW + MLA_HEADS * QK_ROPE

VMEM_LIMIT = 52 * 1024 * 1024


def _dot(a, b):
    return jnp.dot(a, b, preferred_element_type=F32)


def _dot_nt(a, b):
    return lax.dot_general(a, b, (((1,), (1,)), ((), ())), preferred_element_type=F32)


def _dot_tn(a, b):
    return lax.dot_general(a, b, (((0,), (0,)), ((), ())), preferred_element_type=F32)


def _silu(x):
    return x * jax.nn.sigmoid(x)


def _rms(x, g):
    return x * lax.rsqrt(jnp.mean(x * x, axis=-1, keepdims=True) + EPS) * g


def _split3_dot(mask_bf16, v):
    v1 = v.astype(BF16)
    r1 = v - v1.astype(F32)
    v2 = r1.astype(BF16)
    v3 = (r1 - v2.astype(F32)).astype(BF16)
    return _dot(mask_bf16, v1) + _dot(mask_bf16, v2) + _dot(mask_bf16, v3)


def _mod_spec(mod3, L, tm, d):
    rows = mod3.shape[1]
    if rows == 1:
        return 1, lambda k: pl.BlockSpec((None, 1, d), lambda b, i: (b, 0, k))
    rep = L // rows
    return rep, lambda k: pl.BlockSpec((None, tm // rep, d), lambda b, i: (b, i, k))


def _mod_rows(ref, rep):
    m = ref[...]
    if rep == 1:
        return m
    n = m.shape[0] * rep
    sel = (lax.broadcasted_iota(jnp.int32, (n, m.shape[0]), 0) // rep
           == lax.broadcasted_iota(jnp.int32, (n, m.shape[0]), 1))
    sel = jnp.where(sel, 1.0, 0.0).astype(BF16)
    hi = m.astype(BF16)
    lo = (m - hi.astype(F32)).astype(BF16)
    return _dot(sel, hi) + _dot(sel, lo)


def _params(sem, vmem=VMEM_LIMIT):
    return pltpu.CompilerParams(dimension_semantics=sem, vmem_limit_bytes=vmem)


def _ada_body(c_ref, w_ref, b_ref, o_ref):
    s = _silu(c_ref[...]).astype(BF16)
    o_ref[...] = _dot(s, w_ref[...].astype(BF16)) + b_ref[...]


def _ada(c, w, b):
    bsz, d = c.shape
    n = w.shape[1]
    tn = 1024
    return pl.pallas_call(
        _ada_body,
        grid=(n // tn,),
        in_specs=[pl.BlockSpec((bsz, d), lambda j: (0, 0)),
                  pl.BlockSpec((d, tn), lambda j: (0, j)),
                  pl.BlockSpec((1, tn), lambda j: (0, j))],
        out_specs=pl.BlockSpec((bsz, tn), lambda j: (0, j)),
        out_shape=SDS((bsz, n), F32),
        compiler_params=_params(("arbitrary",)),
        name="ada",
    )(c, w, b.reshape(1, n))


def _fold_body(wq_ref, wk_ref, o_ref):
    a = wq_ref[:, 0:QK_NOPE].astype(BF16)
    o_ref[...] = _dot_nt(a, wk_ref[...].astype(BF16)).astype(BF16)


def _fold(wq_h, wk_h):
    return pl.pallas_call(
        _fold_body,
        grid=(MLA_HEADS,),
        in_specs=[pl.BlockSpec((None, Q_RANK, QK_NOPE + QK_ROPE), lambda h: (h, 0, 0)),
                  pl.BlockSpec((None, KV_RANK, QK_NOPE), lambda h: (h, 0, 0))],
        out_specs=pl.BlockSpec((Q_RANK, KV_RANK), lambda h: (0, h)),
        out_shape=SDS((Q_RANK, MLA_HEADS * KV_RANK), BF16),
        compiler_params=_params(("arbitrary",)),
        name="fold",
    )(wq_h, wk_h)


def _inproj_body(*refs, q_transposed, rep):
    n_in = 12 if q_transposed else 10
    x_ref, sh_ref, sc_ref, gmix_ref, win_ref, qg_ref, kvg_ref, wq_ref, tq_ref, tk_ref = refs[:10]
    z_ref, xbc_ref, dt_ref, ckv_ref, kr_ref = refs[n_in:n_in + 5]
    h = _rms(x_ref[...], gmix_ref[...]) * (1.0 + _mod_rows(sc_ref, rep)) + _mod_rows(sh_ref, rep)
    proj = _dot(h.astype(BF16), win_ref[...])
    tm = proj.shape[0]
    z_ref[...] = proj[:, C_Z:C_XBC]
    xbc_ref[...] = proj[:, C_XBC:C_QLAT]
    q_scale = ATTN_SCALE * LOG2E if q_transposed else ATTN_SCALE
    qn = (_rms(proj[:, C_QLAT:C_KVLAT], qg_ref[...]) * q_scale).astype(BF16)
    ckv = _rms(proj[:, C_KVLAT:C_MISC], kvg_ref[...])
    ckv_ref[...] = ckv
    misc = proj[:, C_MISC:W_IN_COLS]
    lane = lax.broadcasted_iota(jnp.int32, misc.shape, 1)
    dt_ref[...] = jnp.where(lane < SSD_HEADS, misc, 0.0)
    tk = tk_ref[...]
    kr = (misc[:, MISC_KR:MISC_KR + QK_ROPE] * tk[:, :QK_ROPE]
          + misc[:, MISC_KRSW:MISC_KRSW + QK_ROPE] * tk[:, QK_ROPE:])
    kr_ref[...] = kr
    nr = MLA_HEADS * QK_ROPE
    if q_transposed:
        wk_ref, wvt_ref = refs[10:12]
        kp_ref, vt_ref, qt_ref = refs[n_in + 5:]
        ckv_b = ckv.astype(BF16)
        k_nope = _dot(ckv_b, wk_ref[...])
        kr_pad = jnp.concatenate([kr, jnp.zeros((tm, PAIR_K - 2 * QK_NOPE - QK_ROPE), F32)], axis=1).astype(BF16)
        for p in range(PAIRS):
            kp_ref[p, :, 0:2 * QK_NOPE] = k_nope[:, p * 2 * QK_NOPE:(p + 1) * 2 * QK_NOPE].astype(BF16)
            kp_ref[p, :, 2 * QK_NOPE:PAIR_K] = kr_pad
            vt_ref[p] = _dot_nt(wvt_ref[p], ckv_b).astype(BF16)
        qt = _dot_nt(wq_ref[...], qn)
        n0 = MLA_HEADS * QK_NOPE
        tq = tq_ref[...]
        cos_t = jnp.concatenate([tq[:QK_ROPE]] * MLA_HEADS, axis=0)
        sin_t = jnp.concatenate([tq[QK_ROPE:]] * MLA_HEADS, axis=0)
        rot = (qt[n0:n0 + nr] * cos_t + qt[n0 + nr:n0 + 2 * nr] * sin_t).astype(BF16)
        q_nope = qt[:n0].astype(BF16)
        zero = jnp.zeros((PAIR_K, TQ), BF16)
        for c in range(tm // TQ):
            toks = slice(c * TQ, (c + 1) * TQ)
            for p in range(PAIRS):
                for s in range(2):
                    hh = 2 * p + s
                    cols = slice(s * TQ, (s + 1) * TQ)
                    qt_ref[c, p, :, cols] = zero
                    qt_ref[c, p, s * QK_NOPE:(s + 1) * QK_NOPE, cols] = q_nope[hh * QK_NOPE:(hh + 1) * QK_NOPE, toks]
                    qt_ref[c, p, 2 * QK_NOPE:2 * QK_NOPE + QK_ROPE, cols] = rot[hh * QK_ROPE:(hh + 1) * QK_ROPE, toks]
    else:
        kc_ref, krb_ref, qa_ref, qr_ref = refs[n_in + 5:]
        kc_ref[...] = ckv.astype(BF16)
        krb_ref[...] = kr.astype(BF16)
        q = _dot(qn, wq_ref[...])
        tq = tq_ref[...]
        qa_ref[...] = q[:, :C_QROPE].astype(BF16)
        qr_ref[...] = (q[:, C_QROPE:C_QROPE_SW] * tq[:, :nr] + q[:, C_QROPE_SW:WQ_COLS] * tq[:, nr:]).astype(BF16)


def _inproj(x3, mod3, gmix, win, qg, kvg, wq, tabq, tabk, *, q_transposed, tm, tk=None, wk=None, wvt=None):
    nb, L, d = x3.shape
    tm = min(tm, L)
    nt = L // tm
    rep, mod_spec = _mod_spec(mod3, L, tm, d)

    def tok(width):
        return pl.BlockSpec((None, tm, width), lambda b, i: (b, i, 0))

    def const(a):
        return pl.BlockSpec(a.shape, lambda b, i: (0,) * a.ndim, pipeline_mode=pl.Buffered(1))

    nr = MLA_HEADS * QK_ROPE
    out_specs = [tok(SSD_INNER), tok(CONV_DIM), tok(LANES), tok(KV_RANK), tok(QK_ROPE)]
    out_shape = [SDS((nb, L, SSD_INNER), F32), SDS((nb, L, CONV_DIM), F32), SDS((nb, L, LANES), F32),
                 SDS((nb, L, KV_RANK), F32), SDS((nb, L, QK_ROPE), F32)]
    if q_transposed:
        assert tm % TQ == 0 and tk % tm == 0
        r = tk // tm
        tabq_spec = pl.BlockSpec((2 * QK_ROPE, tm), lambda b, i: (0, i))
        out_specs += [pl.BlockSpec((None, PAIRS, tm, PAIR_K), lambda b, i: (b, 0, i, 0)),
                      pl.BlockSpec((None, None, PAIRS, 2 * V_DIM, tm), lambda b, i: (b, i // r, 0, 0, i % r)),
                      pl.BlockSpec((None, tm // TQ, PAIRS, PAIR_K, 2 * TQ), lambda b, i: (b, i, 0, 0, 0))]
        out_shape += [SDS((nb, PAIRS, L, PAIR_K), BF16), SDS((nb, L // tk, PAIRS, 2 * V_DIM, tk), BF16),
                      SDS((nb, L // TQ, PAIRS, PAIR_K, 2 * TQ), BF16)]
        extra = [wk, wvt]
    else:
        extra = []
        tabq_spec = pl.BlockSpec((tm, 2 * nr), lambda b, i: (i, 0))
        out_specs += [tok(KV_RANK), tok(QK_ROPE), tok(C_QROPE), tok(nr)]
        out_shape += [SDS((nb, L, KV_RANK), BF16), SDS((nb, L, QK_ROPE), BF16),
                      SDS((nb, L, C_QROPE), BF16), SDS((nb, L, nr), BF16)]
    return pl.pallas_call(
        functools.partial(_inproj_body, q_transposed=q_transposed, rep=rep),
        grid=(nb, nt),
        in_specs=[tok(d), mod_spec(0), mod_spec(1), const(gmix), const(win), const(qg), const(kvg), const(wq),
                  tabq_spec, pl.BlockSpec((tm, 2 * QK_ROPE), lambda b, i: (i, 0))] + [const(a) for a in extra],
        out_specs=out_specs,
        out_shape=out_shape,
        compiler_params=_params(("arbitrary", "arbitrary")),
        name="inproj_prompt" if q_transposed else "inproj_sample",
    )(x3, mod3, mod3, gmix, win, qg, kvg, wq, tabq, tabk, *extra)


def _ssd_body(*refs, Q, seg, carry):
    if carry:
        (xin_ref, z_ref, dt_ref, cw_ref, cb_ref, dtb_ref, alog_ref, dsk_ref, g_ref,
         y_ref, hfin_ref, xbuf, hT) = refs
    else:
        (xin_ref, dt_ref, cw_ref, cb_ref, dtb_ref, alog_ref, dsk_ref,
         ypre_ref, eacs_ref, xw_ref, dec_ref, bm_ref, cm_ref) = refs
    cw = cw_ref[...]
    acc = jnp.broadcast_to(cb_ref[...], (Q, CONV_DIM))
    if carry:
        c = pl.program_id(1)

        @pl.when(c == 0)
        def _():
            xbuf[0:8, :] = jnp.zeros((8, CONV_DIM), F32)
            hT[...] = jnp.zeros(hT.shape, F32)

        xbuf[8:8 + Q, :] = xin_ref[...]
        for k in range(CONV_WIDTH):
            acc = acc + cw[k:k + 1, :] * xbuf[pl.ds(8 - (CONV_WIDTH - 1) + k, Q), :]
        xbuf[0:8, :] = xbuf[Q:Q + 8, :]
    else:
        for k in range(CONV_WIDTH):
            acc = acc + cw[k:k + 1, :] * xin_ref[k]
    xc = _silu(acc)
    xs = xc[:, :SSD_INNER]
    gs = SSD_GROUPS * SSD_STATE
    bm = xc[:, SSD_INNER:SSD_INNER + gs]
    cm = xc[:, SSD_INNER + gs:]
    bm_b = bm.astype(BF16)
    cm_b = cm.astype(BF16)

    lane = lax.broadcasted_iota(jnp.int32, (Q, LANES), 1)
    v = dt_ref[...] + dtb_ref[...]
    dt = jnp.maximum(v, 0.0) + jnp.log1p(jnp.exp(-jnp.abs(v)))
    dt = jnp.where(lane < SSD_HEADS, dt, 0.0)
    dA = dt * (-jnp.exp(alog_ref[...]))
    ri = lax.broadcasted_iota(jnp.int32, (Q, Q), 0)
    ci = lax.broadcasted_iota(jnp.int32, (Q, Q), 1)
    if seg == Q:
        mask = ci <= ri
    else:
        same = (ri // seg) == (ci // seg)
        mask = jnp.logical_and(same, ci <= ri)
    acs = _split3_dot(jnp.where(mask, 1.0, 0.0).astype(BF16), dA)
    if seg == Q:
        acs_last = acs[Q - 1:Q, :]
    else:
        acs_last = _split3_dot(jnp.where(same, 1.0, 0.0).astype(BF16), dA)
    to_end = jnp.exp(acs_last - acs) * dt
    acsT = acs.T
    dtT = dt.T

    G = [_dot_nt(cm_b[:, g * SSD_STATE:(g + 1) * SSD_STATE], bm_b[:, g * SSD_STATE:(g + 1) * SSD_STATE])
         for g in range(SSD_GROUPS)]
    lane_lo = lane < SSD_HEAD_DIM
    heads_per_group = SSD_HEADS // SSD_GROUPS
    ypairs, epairs, xwpairs, decpairs = [], [], [], []
    for k in range(SSD_HEADS // 2):
        g = (2 * k) // heads_per_group
        xp = xs[:, k * LANES:(k + 1) * LANES]
        xhalf = (jnp.where(lane_lo, xp, 0.0).astype(BF16), jnp.where(lane_lo, 0.0, xp).astype(BF16))
        yk = jnp.zeros((Q, LANES), F32)
        for s in range(2):
            hh = 2 * k + s
            segm = acs[:, hh:hh + 1] - acsT[hh:hh + 1, :]
            m = G[g] * jnp.exp(jnp.where(mask, segm, -jnp.inf)) * dtT[hh:hh + 1, :]
            yk = yk + _dot(m.astype(BF16), xhalf[s])

        def pair(a):
            return jnp.where(lane_lo[:a.shape[0]], a[:, 2 * k:2 * k + 1], a[:, 2 * k + 1:2 * k + 2])

        e_p = jnp.exp(pair(acs))
        xw = xp * pair(to_end)
        dec = jnp.exp(pair(acs_last))
        if carry:
            h_prev = hT[k]
            yk = yk + _dot(cm_b[:, g * SSD_STATE:(g + 1) * SSD_STATE], h_prev.astype(BF16)) * e_p
            hT[k] = dec * h_prev + _dot_tn(bm_b[:, g * SSD_STATE:(g + 1) * SSD_STATE], xw.astype(BF16))
        else:
            epairs.append(e_p)
            xwpairs.append(xw)
            decpairs.append(dec)
        ypairs.append(yk)
    y = jnp.concatenate(ypairs, axis=1) + dsk_ref[...] * xs
    if carry:
        y = y * _silu(z_ref[...])
        y_ref[...] = _rms(y, g_ref[...]).astype(BF16)

        @pl.when(c == pl.num_programs(1) - 1)
        def _():
            for k in range(SSD_HEADS // 2):
                hfin_ref[k * LANES:(k + 1) * LANES, :] = hT[k].T
    else:
        ypre_ref[...] = y
        eacs_ref[...] = jnp.concatenate(epairs, axis=1)
        xw_ref[...] = jnp.concatenate(xwpairs, axis=1)
        dec_ref[...] = jnp.concatenate(decpairs, axis=1)
        bm_ref[...] = bm
        cm_ref[...] = cm


def _row(a, n):
    return jnp.pad(a.reshape(1, -1).astype(F32), ((0, 0), (0, n - a.size)))


def _ssd_prompt(xbc, z, dt, cw, cb, dtb, alog, dsk, g):
    nb, L, _ = xbc.shape
    Q = SSD_CHUNK if L % SSD_CHUNK == 0 else L
    nc = L // Q

    def tok(width):
        return pl.BlockSpec((None, Q, width), lambda b, c: (b, c, 0))

    def const(a):
        return pl.BlockSpec(a.shape, lambda b, c: (0,) * a.ndim)

    consts = (cw, cb, dtb, alog, dsk, g)
    return pl.pallas_call(
        functools.partial(_ssd_body, Q=Q, seg=Q, carry=True),
        grid=(nb, nc),
        in_specs=[tok(CONV_DIM), tok(SSD_INNER), tok(LANES)] + [const(a) for a in consts],
        out_specs=[tok(SSD_INNER), pl.BlockSpec((None, SSD_INNER, SSD_STATE), lambda b, c: (b, 0, 0))],
        out_shape=[SDS((nb, L, SSD_INNER), BF16), SDS((nb, SSD_INNER, SSD_STATE), F32)],
        scratch_shapes=[pltpu.VMEM((Q + 8, CONV_DIM), F32), pltpu.VMEM((SSD_HEADS // 2, SSD_STATE, LANES), F32)],
        compiler_params=_params(("arbitrary", "arbitrary")),
        name="ssd_prompt",
    )(xbc, z, dt, *consts)


def _ssd_sample(xsh, dt, cw, cb, dtb, alog, dsk, *, seg):
    _, T, _ = xsh.shape
    consts = (cw, cb, dtb, alog, dsk)

    def full(a):
        return pl.BlockSpec(a.shape, lambda i: (0,) * a.ndim)

    outs = [SDS((T, SSD_INNER), F32)] * 4 + [SDS((T, SSD_GROUPS * SSD_STATE), F32)] * 2
    return pl.pallas_call(
        functools.partial(_ssd_body, Q=T, seg=seg, carry=False),
        grid=(1,),
        in_specs=[full(xsh), full(dt)] + [full(a) for a in consts],
        out_specs=[full(o) for o in outs],
        out_shape=outs,
        compiler_params=_params(("arbitrary",)),
        name="ssd_sample",
    )(xsh, dt, *consts)


def _sstate_body(cm_ref, bm_ref, ypre_ref, eacs_ref, xw_ref, dec_ref, z_ref, s0_ref, g_ref, y_ref, sn_ref):
    s0 = s0_ref[...]
    s0b = s0.astype(BF16)
    cm = cm_ref[...].astype(BF16)
    bm = bm_ref[...].astype(BF16)
    rows = SSD_INNER // SSD_GROUPS
    yo = jnp.concatenate(
        [jnp.einsum("btn,bqn->btq", cm[:, :, g * SSD_STATE:(g + 1) * SSD_STATE],
                    s0b[:, g * rows:(g + 1) * rows, :], preferred_element_type=F32)
         for g in range(SSD_GROUPS)], axis=-1)
    y = (ypre_ref[...] + yo * eacs_ref[...]) * _silu(z_ref[...])
    y_ref[...] = _rms(y, g_ref[...]).astype(BF16)
    dec = dec_ref[...]
    hi = dec.astype(BF16)
    lo = (dec - hi.astype(F32)).astype(BF16)
    sel = jnp.where(lax.broadcasted_iota(jnp.int32, (dec.shape[0], dec.shape[1], SSD_STATE), 1) == 0,
                    1.0, 0.0).astype(BF16)
    dmat = (jnp.einsum("bjq,bjn->bqn", hi, sel, preferred_element_type=F32)
            + jnp.einsum("bjq,bjn->bqn", lo, sel, preferred_element_type=F32))
    xw = xw_ref[...].astype(BF16)
    upd = jnp.concatenate(
        [jnp.einsum("bjq,bjn->bqn", xw[:, :, g * rows:(g + 1) * rows],
                    bm[:, :, g * SSD_STATE:(g + 1) * SSD_STATE], preferred_element_type=F32)
         for g in range(SSD_GROUPS)], axis=1)
    sn_ref[...] = dmat * s0 + upd


def _sstate(cm, bm, ypre, eacs, xw, dec, z, s0, g, *, bs):
    nseq, t, _ = cm.shape
    bs = min(bs, nseq)

    def blk(a):
        return pl.BlockSpec((bs,) + a.shape[1:], lambda i: (i, 0, 0))

    ins = (cm, bm, ypre, eacs, xw, dec, z, s0)
    return pl.pallas_call(
        _sstate_body,
        grid=(nseq // bs,),
        in_specs=[blk(a) for a in ins] + [pl.BlockSpec(g.shape, lambda i: (0, 0))],
        out_specs=[blk(ypre), blk(s0)],
        out_shape=[SDS(ypre.shape, BF16), SDS(s0.shape, F32)],
        compiler_params=_params(("arbitrary",)),
        name="sstate",
    )(*ins, g)


def _attn_body(qt_ref, qn_ref, kp_ref, vt_ref, g_ref, o_ref, m_sc, l_sc, acc_sc,
               s_a, s_b, s_c, smax_a, smax_b, smax_c, *, tk):
    i = pl.program_id(1)
    ncols = MLA_HEADS * TQ
    m_sc[...] = jnp.full(m_sc.shape, -jnp.inf, F32)
    l_sc[...] = jnp.zeros(l_sc.shape, F32)
    acc_sc[...] = jnp.zeros(acc_sc.shape, F32)
    ct = 2 * TQ
    tiles = [slice(p * ct, (p + 1) * ct) for p in range(PAIRS)]

    def scores(j, s_ref, smax_ref, q_ref=qt_ref):
        k0 = pl.multiple_of(j * tk, tk)
        for p, cs in enumerate(tiles):
            s = _dot(kp_ref[p, pl.ds(k0, tk), :], q_ref[p])
            s_ref[:, cs] = s
            smax_ref[:, cs] = jnp.max(s, axis=0, keepdims=True)

    def softmax_pv(j, s_ref, smax_ref, masked):
        for p, cs in enumerate(tiles):
            s = s_ref[:, cs]
            if masked:
                key = j * tk + lax.broadcasted_iota(jnp.int32, (tk, ct), 0)
                tok = i * TQ + (lax.broadcasted_iota(jnp.int32, (tk, ct), 1) & (TQ - 1))
                s = jnp.where(key <= tok, s, -jnp.inf)
                smax = jnp.max(s, axis=0, keepdims=True)
            else:
                smax = smax_ref[:, cs]
            m_prev = m_sc[:, cs]
            m_new = jnp.maximum(m_prev, smax)
            alpha = jnp.exp2(m_prev - m_new)
            e = jnp.exp2(s - m_new)
            l_sc[:, cs] = alpha * l_sc[:, cs] + jnp.sum(e, axis=0, keepdims=True)
            acc_sc[p] = alpha * acc_sc[p] + _dot(vt_ref[j, p], e.astype(BF16))
            m_sc[:, cs] = m_new

    nfull = (i * TQ) // tk

    @pl.when(i == 0)
    def _():
        scores(0, s_c, smax_c)

    def prefetch():
        scores(0, s_c, smax_c, qn_ref)

    @pl.when(nfull == 0)
    def _():
        softmax_pv(0, s_c, smax_c, True)
        prefetch()

    @pl.when(nfull >= 1)
    def _():
        scores(1, s_b, smax_b)
        softmax_pv(0, s_c, smax_c, False)

    def pair(p, carry):
        j = 2 * p + 1
        scores(j + 1, s_a, smax_a)
        softmax_pv(j, s_b, smax_b, False)
        scores(j + 2, s_b, smax_b)
        softmax_pv(j + 1, s_a, smax_a, False)
        return carry

    lax.fori_loop(0, jnp.maximum(nfull - 1, 0) // 2, pair, 0)
    odd = (nfull % 2) == 1

    @pl.when(odd)
    def _():
        prefetch()
        softmax_pv(nfull, s_b, smax_b, True)

    @pl.when(jnp.logical_and(jnp.logical_not(odd), nfull >= 2))
    def _():
        scores(nfull, s_a, smax_a)
        softmax_pv(nfull - 1, s_b, smax_b, False)
        prefetch()
        softmax_pv(nfull, s_a, smax_a, True)

    inv = 1.0 / l_sc[...]
    ys = []
    for hh in range(MLA_HEADS):
        p, s = divmod(hh, 2)
        ys.append(acc_sc[p, s * V_DIM:(s + 1) * V_DIM, s * TQ:(s + 1) * TQ]
                  * inv[:, hh * TQ:(hh + 1) * TQ])
    yt = jnp.concatenate(ys, axis=0)
    yt = yt * lax.rsqrt(jnp.mean(yt * yt, axis=0, keepdims=True) + EPS)
    o_ref[...] = (yt.T * g_ref[...]).astype(BF16)


def _attn_prompt(qt, kp, vt, g, *, tk):
    nb, nq = qt.shape[:2]
    L = nq * TQ
    ncols = MLA_HEADS * TQ
    assert TQ & (TQ - 1) == 0 and L % tk == 0 and tk % TQ == 0

    def q_spec(index):
        return pl.BlockSpec((None, None, PAIRS, PAIR_K, 2 * TQ), lambda b, i: (b, index(i), 0, 0, 0))

    return pl.pallas_call(
        functools.partial(_attn_body, tk=tk),
        grid=(nb, nq),
        in_specs=[q_spec(lambda i: i), q_spec(lambda i: jnp.minimum(i + 1, nq - 1)),
                  pl.BlockSpec((None, PAIRS, L, PAIR_K), lambda b, i: (b, 0, 0, 0), pipeline_mode=pl.Buffered(1)),
                  pl.BlockSpec((None, L // tk, PAIRS, 2 * V_DIM, tk), lambda b, i: (b, 0, 0, 0, 0),
                               pipeline_mode=pl.Buffered(1)),
                  pl.BlockSpec(g.shape, lambda b, i: (0, 0))],
        out_specs=pl.BlockSpec((None, TQ, MLA_INNER), lambda b, i: (b, i, 0)),
        out_shape=SDS((nb, L, MLA_INNER), BF16),
        scratch_shapes=[pltpu.VMEM((1, ncols), F32), pltpu.VMEM((1, ncols), F32),
                        pltpu.VMEM((PAIRS, 2 * V_DIM, 2 * TQ), F32),
                        pltpu.VMEM((tk, ncols), F32), pltpu.VMEM((tk, ncols), F32), pltpu.VMEM((tk, ncols), F32),
                        pltpu.VMEM((1, ncols), F32), pltpu.VMEM((1, ncols), F32), pltpu.VMEM((1, ncols), F32)],
        compiler_params=_params(("arbitrary", "arbitrary")),
        name="attn_prompt",
    )(qt, qt, kp, vt, g)


def _sattn_body(pt_ref, qa_ref, qr_ref, kn_ref, rn_ref, ckv_hbm, krt_hbm, o_ref, kbuf, rbuf, sem, *, npages, t_new):
    b = pl.program_id(0)
    nseq = pl.num_programs(0)

    def copies(seq_page, slot, p):
        off = p * PAGE_SIZE
        return (pltpu.make_async_copy(ckv_hbm.at[seq_page], kbuf.at[slot, p], sem.at[0, slot]),
                pltpu.make_async_copy(krt_hbm.at[seq_page], rbuf.at[slot, :, pl.ds(off, PAGE_SIZE)],
                                      sem.at[1, slot]))

    def start_fetch(seq, slot):
        for p in range(npages):
            for cp in copies(pt_ref[seq * npages + p], slot, p):
                cp.start()

    def wait_fetch(slot):
        pltpu.make_async_copy(ckv_hbm.at[pl.ds(0, npages)], kbuf.at[slot], sem.at[0, slot]).wait()
        pltpu.make_async_copy(rbuf.at[slot], rbuf.at[slot], sem.at[1, slot]).wait()

    @pl.when(b == 0)
    def _():
        start_fetch(0, 0)

    @pl.when(b + 1 < nseq)
    def _():
        start_fetch(b + 1, (b + 1) % 2)

    slot = b % 2
    wait_fetch(slot)
    pg = npages // SPLIT
    part = pg * PAGE_SIZE
    kps = [kbuf[slot, pl.ds(h * pg, pg)].reshape(part, KV_RANK).astype(BF16) for h in range(SPLIT)]
    rpt = rbuf[slot].astype(BF16)
    q = qa_ref[...]
    qr = qr_ref[...]
    kn = kn_ref[...]
    rn = rn_ref[...]
    s_main = [_dot_nt(q, kps[h]) for h in range(SPLIT)]
    s_rope = [_dot(qr, rpt[:, h * part:(h + 1) * part]) for h in range(SPLIT)]
    s_p = [a + b for a, b in zip(s_main, s_rope)]
    s_n = _dot_nt(q, kn) + _dot_nt(qr, rn)
    rows = t_new * MLA_HEADS
    tok = lax.broadcasted_iota(jnp.int32, (rows, t_new), 0) // MLA_HEADS
    col = lax.broadcasted_iota(jnp.int32, (rows, t_new), 1)
    s_n = jnp.where(col <= tok, s_n, -jnp.inf)
    m = jnp.max(s_n, axis=-1, keepdims=True)
    for s in s_p:
        m = jnp.maximum(m, jnp.max(s, axis=-1, keepdims=True))
    pn = jnp.exp(s_n - m)
    pp = [jnp.exp(s - m) for s in s_p]
    l = jnp.sum(pn, axis=-1, keepdims=True)
    for p in pp:
        l = l + jnp.sum(p, axis=-1, keepdims=True)
    inv = 1.0 / l
    parts = [_dot(pp[h].astype(BF16), kps[h]) for h in range(SPLIT)]
    o = _dot(pn.astype(BF16), kn) * inv
    for part_o in parts:
        o = o + part_o * inv
    o_ref[...] = o


def _sattn(page_table, qa, qr, kn, rn, cache_kv, cache_krt):
    nseq, npages = page_table.shape
    t_new = kn.shape[1]
    rows = t_new * MLA_HEADS
    grid_spec = pltpu.PrefetchScalarGridSpec(
        num_scalar_prefetch=1,
        grid=(nseq,),
        in_specs=[pl.BlockSpec((rows, KV_RANK), lambda b, pt: (b, 0)),
                  pl.BlockSpec((rows, QK_ROPE), lambda b, pt: (b, 0)),
                  pl.BlockSpec((None, t_new, KV_RANK), lambda b, pt: (b, 0, 0)),
                  pl.BlockSpec((None, t_new, QK_ROPE), lambda b, pt: (b, 0, 0)),
                  pl.BlockSpec(memory_space=pl.ANY),
                  pl.BlockSpec(memory_space=pl.ANY)],
        out_specs=pl.BlockSpec((rows, KV_RANK), lambda b, pt: (b, 0)),
        scratch_shapes=[pltpu.VMEM((2, npages, PAGE_SIZE, KV_RANK), F32),
                        pltpu.VMEM((2, QK_ROPE, npages * PAGE_SIZE), F32),
                        pltpu.SemaphoreType.DMA((2, 2))],
    )
    return pl.pallas_call(
        functools.partial(_sattn_body, npages=npages, t_new=t_new),
        grid_spec=grid_spec,
        out_shape=SDS((nseq * rows, KV_RANK), F32),
        compiler_params=_params(("arbitrary",)),
        name="attn_sample",
    )(page_table.reshape(-1), qa, qr, kn, rn, cache_kv, cache_krt)


def _apost_body(o_ref, wuv_ref, g_ref, y_ref):
    tm = o_ref.shape[0]
    o = jnp.zeros((tm, MLA_INNER), F32)
    for hh in range(MLA_HEADS):
        o = o + _dot(o_ref[:, hh * KV_RANK:(hh + 1) * KV_RANK].astype(BF16), wuv_ref[hh])
    y_ref[...] = _rms(o, g_ref[...]).astype(BF16)


def _apost(o_lat, wuv_pad, g):
    T = o_lat.shape[0]
    return pl.pallas_call(
        _apost_body,
        grid=(1,),
        in_specs=[pl.BlockSpec(o_lat.shape, lambda i: (0, 0)),
                  pl.BlockSpec(wuv_pad.shape, lambda i: (0, 0, 0)),
                  pl.BlockSpec(g.shape, lambda i: (0, 0))],
        out_specs=pl.BlockSpec((T, MLA_INNER), lambda i: (0, 0)),
        out_shape=SDS((T, MLA_INNER), BF16),
        compiler_params=_params(("arbitrary",)),
        name="attn_post",
    )(o_lat, wuv_pad, g)


def _mlp_body(x_ref, ys_ref, ya_ref, g1_ref, sh2_ref, sc2_ref, g2_ref, shf_ref, scf_ref,
              wout_ref, gmlp_ref, wup_ref, wdn_ref, gfin_ref, o_ref, *, final, tf, rep):
    yy = jnp.concatenate([ys_ref[...], ya_ref[...]], axis=-1)
    x1 = x_ref[...] + _mod_rows(g1_ref, rep) * _dot(yy, wout_ref[...])
    h2 = (_rms(x1, gmlp_ref[...]) * (1.0 + _mod_rows(sc2_ref, rep)) + _mod_rows(sh2_ref, rep)).astype(BF16)
    acc = jnp.zeros(x1.shape, F32)
    for c in range(D_FF // tf):
        u = jnp.maximum(_dot(h2, wup_ref[:, c * tf:(c + 1) * tf]), 0.0)
        acc = acc + _dot((u * u).astype(BF16), wdn_ref[c * tf:(c + 1) * tf, :])
    x2 = x1 + _mod_rows(g2_ref, rep) * acc
    if final:
        x2 = _rms(x2, gfin_ref[...]) * (1.0 + _mod_rows(scf_ref, rep)) + _mod_rows(shf_ref, rep)
    o_ref[...] = x2


def _mlp(x3, ys, ya, mod3, wout, gmlp, wup, wdn, gfin, *, final, tm, tf):
    nb, L, d = x3.shape
    tm = min(tm, L)
    rep, mod_spec = _mod_spec(mod3, L, tm, d)

    def tok(width):
        return pl.BlockSpec((None, tm, width), lambda b, i: (b, i, 0))

    def const(a):
        return pl.BlockSpec(a.shape, lambda b, i: (0,) * a.ndim, pipeline_mode=pl.Buffered(1))

    return pl.pallas_call(
        functools.partial(_mlp_body, final=final, tf=tf, rep=rep),
        grid=(nb, L // tm),
        in_specs=[tok(d), tok(SSD_INNER), tok(MLA_INNER)] + [mod_spec(k) for k in (2, 3, 4, 5, 6, 7)]
                 + [const(wout), const(gmlp), const(wup), const(wdn), const(gfin)],
        out_specs=tok(d),
        out_shape=SDS((nb, L, d), F32),
        compiler_params=_params(("arbitrary", "arbitrary")),
        name="mlp",
    )(x3, ys, ya, mod3, mod3, mod3, mod3, mod3, mod3, wout, gmlp, wup, wdn, gfin)


def _rope_tables(pos):
    inv = 1.0 / (ROPE_THETA ** (np.arange(0, QK_ROPE, 2, dtype=np.float64) / QK_ROPE))
    ang = pos.astype(np.float64)[:, None] * inv[None, :]
    cos, sin = np.cos(ang).astype(np.float32), np.sin(ang).astype(np.float32)
    c32 = np.concatenate([cos, cos], axis=-1)
    s32 = np.concatenate([-sin, sin], axis=-1)
    tab1 = np.concatenate([c32, s32], axis=-1)
    tabq = np.concatenate([np.tile(c32, (1, MLA_HEADS)), np.tile(s32, (1, MLA_HEADS))], axis=-1)
    return tab1, tabq


def _swap_halves(w):
    half = w.shape[-1] // 2
    return jnp.concatenate([w[..., half:], w[..., :half]], axis=-1)


def kernel(x_prompt, x_sample, cache_kv_latent, cache_k_rope, state_conv, state_ssm, page_table,
           c_prompt, c_sample, w_ada, b_ada, norm_mix_g, w_in, conv_w, conv_b, dt_bias, a_log,
           d_skip, norm_ssd_g, q_norm_g, kv_norm_g, w_uq, w_uk, w_uv, norm_attn_g, w_out,
           norm_mlp_g, w_up, w_down, w_ada_final, b_ada_final, norm_final_g):
    depth = w_in.shape[0]
    b_p, seq, d = x_prompt.shape
    n_seq, t_new, _ = x_sample.shape
    n_tok_s = n_seq * t_new
    past_len = page_table.shape[1] * PAGE_SIZE

    c_all = jnp.concatenate([c_prompt, c_sample], axis=0)
    ada_fin = _ada(c_all, w_ada_final, b_ada_final)
    tab1_p, _ = _rope_tables(np.arange(seq))
    tab1_s, tabq_s = _rope_tables(past_len + np.arange(t_new))
    tab1_pt = jnp.asarray(np.ascontiguousarray(tab1_p.T))
    tab1_p = jnp.asarray(tab1_p)
    tabq_s = jnp.asarray(np.tile(tabq_s, (n_seq, 1)))
    tab1_s = jnp.asarray(np.tile(tab1_s, (n_seq, 1)))

    xp = x_prompt
    xs = x_sample.reshape(1, n_tok_s, d)
    outs_p, outs_s = [], []
    for l in range(depth):
        final = l == depth - 1
        wi = w_in[l]
        c1 = SSD_INNER
        c2 = c1 + CONV_DIM
        c3 = c2 + SSD_HEADS
        c4 = c3 + Q_RANK
        c5 = c4 + KV_RANK
        w_kr = wi[:, c5:]
        win = jnp.concatenate(
            [wi[:, :c2], wi[:, c3:c5], wi[:, c2:c3], jnp.zeros((d, MISC_KR - SSD_HEADS), F32), w_kr,
             _swap_halves(w_kr), jnp.zeros((d, LANES - MISC_KRSW - QK_ROPE), F32)], axis=1).astype(BF16)
        wq_h = w_uq[l].reshape(Q_RANK, MLA_HEADS, QK_NOPE + QK_ROPE)
        w_rope = wq_h[:, :, QK_NOPE:]
        wfold = _fold(jnp.transpose(wq_h, (1, 0, 2)), jnp.transpose(w_uk[l], (1, 0, 2)))
        wq = jnp.concatenate([wfold, w_rope.reshape(Q_RANK, -1).astype(BF16),
                              _swap_halves(w_rope).reshape(Q_RANK, -1).astype(BF16)], axis=1)
        wuv_pad = jnp.zeros((MLA_HEADS, KV_RANK, MLA_HEADS, V_DIM), F32)
        wuv_pad = wuv_pad.at[jnp.arange(MLA_HEADS), :, jnp.arange(MLA_HEADS), :].set(
            jnp.transpose(w_uv[l], (1, 0, 2)))
        wuv_pad = wuv_pad.reshape(MLA_HEADS, KV_RANK, MLA_INNER).astype(BF16)
        wq_p = jnp.concatenate([wq_h[:, :, :QK_NOPE].reshape(Q_RANK, -1), w_rope.reshape(Q_RANK, -1),
                                _swap_halves(w_rope).reshape(Q_RANK, -1)], axis=1).T.astype(BF16)
        wk = w_uk[l].reshape(KV_RANK, MLA_HEADS * QK_NOPE).astype(BF16)
        wvt = jnp.transpose(w_uv[l], (1, 2, 0)).reshape(PAIRS, 2 * V_DIM, KV_RANK).astype(BF16)
        wout = w_out[l].astype(BF16)
        wup = w_up[l].astype(BF16)
        wdn = w_down[l].astype(BF16)
        gmix = norm_mix_g[l].reshape(1, d)
        gmlp = norm_mlp_g[l].reshape(1, d)
        gfin = norm_final_g.reshape(1, d)
        qg = q_norm_g[l].reshape(1, Q_RANK)
        kvg = kv_norm_g[l].reshape(1, KV_RANK)
        gssd = norm_ssd_g[l].reshape(1, SSD_INNER)
        gattn = norm_attn_g[l].reshape(1, MLA_INNER)
        cw = conv_w[l]
        cb = conv_b[l].reshape(1, CONV_DIM)
        dtb = _row(dt_bias[l], LANES)
        alog = _row(a_log[l], LANES)
        dsk = jnp.repeat(d_skip[l].astype(F32), SSD_HEAD_DIM).reshape(1, SSD_INNER)

        ada = _ada(c_all, w_ada[l], b_ada[l])
        mod = jnp.concatenate([ada, ada_fin], axis=1)
        mod_p = mod[:b_p].reshape(b_p, 1, 8 * d)
        mod_s = mod[b_p:].reshape(1, n_seq, 8 * d)

        tk = min(512, seq)
        z, xbc, dtr, ckv, kr, kp, vt, qt = _inproj(
            xp, mod_p, gmix, win, qg, kvg, wq_p, tab1_pt, tab1_p, q_transposed=True, tm=512, tk=tk,
            wk=wk, wvt=wvt)
        y_ssd, hfin = _ssd_prompt(xbc, z, dtr, cw, cb, dtb, alog, dsk, gssd)
        y_attn = _attn_prompt(qt, kp, vt, gattn, tk=tk)
        xp = _mlp(xp, y_ssd, y_attn, mod_p, wout, gmlp, wup, wdn, gfin, final=final, tm=512, tf=2048)
        tail = min(seq, CONV_WIDTH - 1)
        conv_tail = jnp.concatenate([jnp.zeros((b_p, CONV_WIDTH - 1 - tail, CONV_DIM), F32),
                                     xbc[:, seq - tail:]], axis=1)
        outs_p.append((ckv, kr, conv_tail, hfin.reshape(b_p, SSD_HEADS, SSD_HEAD_DIM, SSD_STATE)))

        z, xbc, dtr, ckv, kr, kc, krb, qa, qr = _inproj(
            xs, mod_s, gmix, win, qg, kvg, wq, tabq_s, tab1_s, q_transposed=False, tm=256)
        xbc3 = xbc.reshape(n_seq, t_new, CONV_DIM)
        xpad = jnp.concatenate([state_conv[l], xbc3], axis=1)
        xsh = jnp.stack([xpad[:, k:k + t_new].reshape(n_tok_s, CONV_DIM) for k in range(CONV_WIDTH)])
        ypre, eacs, xw, dec, bm, cm = _ssd_sample(xsh, dtr[0], cw, cb, dtb, alog, dsk, seg=t_new)

        def seqs(a):
            return a.reshape(n_seq, t_new, a.shape[-1])

        y_ssd, s_new = _sstate(seqs(cm), seqs(bm), seqs(ypre), seqs(eacs), seqs(xw), seqs(dec), seqs(z[0]),
                               state_ssm[l].reshape(n_seq, SSD_INNER, SSD_STATE), gssd, bs=8)
        o_lat = _sattn(page_table, qa.reshape(n_tok_s * MLA_HEADS, KV_RANK),
                       qr.reshape(n_tok_s * MLA_HEADS, QK_ROPE), seqs(kc[0]), seqs(krb[0]),
                       cache_kv_latent[l], jnp.swapaxes(cache_k_rope[l], -1, -2))
        y_attn = _apost(o_lat.reshape(n_tok_s, MLA_HEADS * KV_RANK), wuv_pad, gattn)
        xs = _mlp(xs, y_ssd.reshape(1, n_tok_s, SSD_INNER), y_attn.reshape(1, n_tok_s, MLA_INNER), mod_s,
                  wout, gmlp, wup, wdn, gfin, final=final, tm=n_tok_s, tf=2048)
        outs_s.append((seqs(ckv[0]), seqs(kr[0]), xpad[:, t_new:],
                       s_new.reshape(n_seq, SSD_HEADS, SSD_HEAD_DIM, SSD_STATE)))

    def stack(outs, k):
        return jnp.stack([o[k] for o in outs])

    return (xp, xs.reshape(n_seq, t_new, d),
            stack(outs_p, 0), stack(outs_p, 1), stack(outs_p, 2), stack(outs_p, 3),
            stack(outs_s, 0), stack(outs_s, 1), stack(outs_s, 2), stack(outs_s, 3))
```

```python
import functools
import math

import jax
import jax.numpy as jnp
import numpy as np
from jax import lax
from jax.experimental import pallas as pl
from jax.experimental.pallas import tpu as pltpu

F32 = jnp.float32
BF16 = jnp.bfloat16
SDS = jax.ShapeDtypeStruct

D_MODEL = 1024
SSD_HEADS = 8
SSD_HEAD_DIM = 64
SSD_INNER = SSD_HEADS * SSD_HEAD_DIM
SSD_GROUPS = 2
SSD_STATE = 128
CONV_WIDTH = 4
SSD_CHUNK = 128
CONV_DIM = SSD_INNER + 2 * SSD_GROUPS * SSD_STATE
MLA_HEADS = 8
QK_NOPE = 64
QK_ROPE = 32
V_DIM = 64
KV_RANK = 256
Q_RANK = 384
MLA_INNER = MLA_HEADS * V_DIM
ROPE_THETA = 10000.0
ATTN_SCALE = 1.0 / math.sqrt(QK_NOPE + QK_ROPE)
LOG2E = math.log2(math.e)
PAGE_SIZE = 128
D_FF = 4 * D_MODEL
EPS = 1e-6

LANES = 128
TQ = 128
PAIRS = MLA_HEADS // 2
PAIR_K = 256
PAIR_V = 2 * V_DIM + 16
SPLIT = 2
C_Z = 0
C_XBC = C_Z + SSD_INNER
C_QLAT = C_XBC + CONV_DIM
C_KVLAT = C_QLAT + Q_RANK
C_MISC = C_KVLAT + KV_RANK
W_IN_COLS = C_MISC + LANES
MISC_KR = 32
MISC_KRSW = 64
C_QROPE = MLA_HEADS * KV_RANK
C_QROPE_SW = C_QROPE + MLA_HEADS * QK_ROPE
WQ_COLS = C_QROPE_SW + MLA_HEADS * QK_ROPE

VMEM_LIMIT = 52 * 1024 * 1024


def _dot(a, b):
    return jnp.dot(a, b, preferred_element_type=F32)


def _dot_nt(a, b):
    return lax.dot_general(a, b, (((1,), (1,)), ((), ())), preferred_element_type=F32)


def _dot_tn(a, b):
    return lax.dot_general(a, b, (((0,), (0,)), ((), ())), preferred_element_type=F32)


def _silu(x):
    return x * jax.nn.sigmoid(x)


def _rms(x, g):
    return x * lax.rsqrt(jnp.mean(x * x, axis=-1, keepdims=True) + EPS) * g


def _split3_dot(mask_bf16, v):
    v1 = v.astype(BF16)
    r1 = v - v1.astype(F32)
    v2 = r1.astype(BF16)
    v3 = (r1 - v2.astype(F32)).astype(BF16)
    return _dot(mask_bf16, v1) + _dot(mask_bf16, v2) + _dot(mask_bf16, v3)


def _mod_spec(mod3, L, tm, d):
    rows = mod3.shape[1]
    if rows == 1:
        return 1, lambda k: pl.BlockSpec((None, 1, d), lambda b, i: (b, 0, k))
    rep = L // rows
    return rep, lambda k: pl.BlockSpec((None, tm // rep, d), lambda b, i: (b, i, k))


def _mod_rows(ref, rep):
    m = ref[...]
    if rep == 1:
        return m
    n = m.shape[0] * rep
    sel = (lax.broadcasted_iota(jnp.int32, (n, m.shape[0]), 0) // rep
           == lax.broadcasted_iota(jnp.int32, (n, m.shape[0]), 1))
    sel = jnp.where(sel, 1.0, 0.0).astype(BF16)
    hi = m.astype(BF16)
    lo = (m - hi.astype(F32)).astype(BF16)
    return _dot(sel, hi) + _dot(sel, lo)


def _params(sem, vmem=VMEM_LIMIT):
    return pltpu.CompilerParams(dimension_semantics=sem, vmem_limit_bytes=vmem)


def _ada_body(c_ref, w_ref, b_ref, o_ref):
    s = _silu(c_ref[...]).astype(BF16)
    o_ref[...] = _dot(s, w_ref[...].astype(BF16)) + b_ref[...]


def _ada(c, w, b):
    bsz, d = c.shape
    n = w.shape[1]
    tn = 1024
    return pl.pallas_call(
        _ada_body,
        grid=(n // tn,),
        in_specs=[pl.BlockSpec((bsz, d), lambda j: (0, 0)),
                  pl.BlockSpec((d, tn), lambda j: (0, j)),
                  pl.BlockSpec((1, tn), lambda j: (0, j))],
        out_specs=pl.BlockSpec((bsz, tn), lambda j: (0, j)),
        out_shape=SDS((bsz, n), F32),
        compiler_params=_params(("arbitrary",)),
        name="ada",
    )(c, w, b.reshape(1, n))


def _fold_body(wq_ref, wk_ref, o_ref):
    a = wq_ref[:, 0:QK_NOPE].astype(BF16)
    o_ref[...] = _dot_nt(a, wk_ref[...].astype(BF16)).astype(BF16)


def _fold(wq_h, wk_h):
    return pl.pallas_call(
        _fold_body,
        grid=(MLA_HEADS,),
        in_specs=[pl.BlockSpec((None, Q_RANK, QK_NOPE + QK_ROPE), lambda h: (h, 0, 0)),
                  pl.BlockSpec((None, KV_RANK, QK_NOPE), lambda h: (h, 0, 0))],
        out_specs=pl.BlockSpec((Q_RANK, KV_RANK), lambda h: (0, h)),
        out_shape=SDS((Q_RANK, MLA_HEADS * KV_RANK), BF16),
        compiler_params=_params(("arbitrary",)),
        name="fold",
    )(wq_h, wk_h)


def _inproj_body(*refs, q_transposed, rep):
    n_in = 12 if q_transposed else 10
    x_ref, sh_ref, sc_ref, gmix_ref, win_ref, qg_ref, kvg_ref, wq_ref, tq_ref, tk_ref = refs[:10]
    z_ref, xbc_ref, dt_ref, ckv_ref, kr_ref = refs[n_in:n_in + 5]
    h = _rms(x_ref[...], gmix_ref[...]) * (1.0 + _mod_rows(sc_ref, rep)) + _mod_rows(sh_ref, rep)
    proj = _dot(h.astype(BF16), win_ref[...])
    tm = proj.shape[0]
    z_ref[...] = proj[:, C_Z:C_XBC]
    xbc_ref[...] = proj[:, C_XBC:C_QLAT]
    q_scale = ATTN_SCALE * LOG2E if q_transposed else ATTN_SCALE
    qn = (_rms(proj[:, C_QLAT:C_KVLAT], qg_ref[...]) * q_scale).astype(BF16)
    ckv = _rms(proj[:, C_KVLAT:C_MISC], kvg_ref[...])
    ckv_ref[...] = ckv
    misc = proj[:, C_MISC:W_IN_COLS]
    lane = lax.broadcasted_iota(jnp.int32, misc.shape, 1)
    dt_ref[...] = jnp.where(lane < SSD_HEADS, misc, 0.0)
    tk = tk_ref[...]
    kr = (misc[:, MISC_KR:MISC_KR + QK_ROPE] * tk[:, :QK_ROPE]
          + misc[:, MISC_KRSW:MISC_KRSW + QK_ROPE] * tk[:, QK_ROPE:])
    kr_ref[...] = kr
    nr = MLA_HEADS * QK_ROPE
    if q_transposed:
        wk_ref, wvt_ref = refs[10:12]
        kp_ref, vt_ref, qt_ref = refs[n_in + 5:]
        ckv_b = ckv.astype(BF16)
        k_nope = _dot(ckv_b, wk_ref[...])
        kr_pad = jnp.concatenate([kr, jnp.zeros((tm, PAIR_K - 2 * QK_NOPE - QK_ROPE), F32)], axis=1).astype(BF16)
        pad_rows = PAIR_V - 2 * V_DIM
        ones_row = jnp.where(lax.broadcasted_iota(jnp.int32, (pad_rows, tm), 0) == 0, 1.0, 0.0).astype(BF16)
        for p in range(PAIRS):
            kp_ref[p, :, 0:2 * QK_NOPE] = k_nope[:, p * 2 * QK_NOPE:(p + 1) * 2 * QK_NOPE].astype(BF16)
            kp_ref[p, :, 2 * QK_NOPE:PAIR_K] = kr_pad
            vt_ref[p, 0:2 * V_DIM] = _dot_nt(wvt_ref[p], ckv_b).astype(BF16)
            vt_ref[p, 2 * V_DIM:PAIR_V] = ones_row
        qt = _dot_nt(wq_ref[...], qn)
        n0 = MLA_HEADS * QK_NOPE
        tq = tq_ref[...]
        cos_t = jnp.concatenate([tq[:QK_ROPE]] * MLA_HEADS, axis=0)
        sin_t = jnp.concatenate([tq[QK_ROPE:]] * MLA_HEADS, axis=0)
        rot = (qt[n0:n0 + nr] * cos_t + qt[n0 + nr:n0 + 2 * nr] * sin_t).astype(BF16)
        q_nope = qt[:n0].astype(BF16)
        zero = jnp.zeros((PAIR_K, TQ), BF16)
        for c in range(tm // TQ):
            toks = slice(c * TQ, (c + 1) * TQ)
            for p in range(PAIRS):
                for s in range(2):
                    hh = 2 * p + s
                    cols = slice(s * TQ, (s + 1) * TQ)
                    qt_ref[c, p, :, cols] = zero
                    qt_ref[c, p, s * QK_NOPE:(s + 1) * QK_NOPE, cols] = q_nope[hh * QK_NOPE:(hh + 1) * QK_NOPE, toks]
                    qt_ref[c, p, 2 * QK_NOPE:2 * QK_NOPE + QK_ROPE, cols] = rot[hh * QK_ROPE:(hh + 1) * QK_ROPE, toks]
    else:
        kc_ref, krb_ref, qa_ref, qr_ref = refs[n_in + 5:]
        kc_ref[...] = ckv.astype(BF16)
        krb_ref[...] = kr.astype(BF16)
        q = _dot(qn, wq_ref[...])
        tq = tq_ref[...]
        qa_ref[...] = q[:, :C_QROPE].astype(BF16)
        qr_ref[...] = (q[:, C_QROPE:C_QROPE_SW] * tq[:, :nr] + q[:, C_QROPE_SW:WQ_COLS] * tq[:, nr:]).astype(BF16)


def _inproj(x3, mod3, gmix, win, qg, kvg, wq, tabq, tabk, *, q_transposed, tm, tk=None, wk=None, wvt=None):
    nb, L, d = x3.shape
    tm = min(tm, L)
    nt = L // tm
    rep, mod_spec = _mod_spec(mod3, L, tm, d)

    def tok(width):
        return pl.BlockSpec((None, tm, width), lambda b, i: (b, i, 0))

    def const(a):
        return pl.BlockSpec(a.shape, lambda b, i: (0,) * a.ndim, pipeline_mode=pl.Buffered(1))

    nr = MLA_HEADS * QK_ROPE
    out_specs = [tok(SSD_INNER), tok(CONV_DIM), tok(LANES), tok(KV_RANK), tok(QK_ROPE)]
    out_shape = [SDS((nb, L, SSD_INNER), F32), SDS((nb, L, CONV_DIM), F32), SDS((nb, L, LANES), F32),
                 SDS((nb, L, KV_RANK), F32), SDS((nb, L, QK_ROPE), F32)]
    if q_transposed:
        assert tm % TQ == 0 and tk % tm == 0
        r = tk // tm
        tabq_spec = pl.BlockSpec((2 * QK_ROPE, tm), lambda b, i: (0, i))
        out_specs += [pl.BlockSpec((None, PAIRS, tm, PAIR_K), lambda b, i: (b, 0, i, 0)),
                      pl.BlockSpec((None, None, PAIRS, PAIR_V, tm), lambda b, i: (b, i // r, 0, 0, i % r)),
                      pl.BlockSpec((None, tm // TQ, PAIRS, PAIR_K, 2 * TQ), lambda b, i: (b, i, 0, 0, 0))]
        out_shape += [SDS((nb, PAIRS, L, PAIR_K), BF16), SDS((nb, L // tk, PAIRS, PAIR_V, tk), BF16),
                      SDS((nb, L // TQ, PAIRS, PAIR_K, 2 * TQ), BF16)]
        extra = [wk, wvt]
    else:
        extra = []
        tabq_spec = pl.BlockSpec((tm, 2 * nr), lambda b, i: (i, 0))
        out_specs += [tok(KV_RANK), tok(QK_ROPE), tok(C_QROPE), tok(nr)]
        out_shape += [SDS((nb, L, KV_RANK), BF16), SDS((nb, L, QK_ROPE), BF16),
                      SDS((nb, L, C_QROPE), BF16), SDS((nb, L, nr), BF16)]
    return pl.pallas_call(
        functools.partial(_inproj_body, q_transposed=q_transposed, rep=rep),
        grid=(nb, nt),
        in_specs=[tok(d), mod_spec(0), mod_spec(1), const(gmix), const(win), const(qg), const(kvg), const(wq),
                  tabq_spec, pl.BlockSpec((tm, 2 * QK_ROPE), lambda b, i: (i, 0))] + [const(a) for a in extra],
        out_specs=out_specs,
        out_shape=out_shape,
        compiler_params=_params(("arbitrary", "arbitrary")),
        name="inproj_prompt" if q_transposed else "inproj_sample",
    )(x3, mod3, mod3, gmix, win, qg, kvg, wq, tabq, tabk, *extra)


def _ssd_body(*refs, Q, seg, carry):
    if carry:
        (xin_ref, z_ref, dt_ref, cw_ref, cb_ref, dtb_ref, alog_ref, dsk_ref, g_ref,
         y_ref, hfin_ref, xbuf, hT) = refs
    else:
        (xin_ref, dt_ref, cw_ref, cb_ref, dtb_ref, alog_ref, dsk_ref,
         ypre_ref, eacs_ref, xw_ref, dec_ref, bm_ref, cm_ref) = refs
    cw = cw_ref[...]
    acc = jnp.broadcast_to(cb_ref[...], (Q, CONV_DIM))
    if carry:
        c = pl.program_id(1)

        @pl.when(c == 0)
        def _():
            xbuf[0:8, :] = jnp.zeros((8, CONV_DIM), F32)
            hT[...] = jnp.zeros(hT.shape, F32)

        xbuf[8:8 + Q, :] = xin_ref[...]
        for k in range(CONV_WIDTH):
            acc = acc + cw[k:k + 1, :] * xbuf[pl.ds(8 - (CONV_WIDTH - 1) + k, Q), :]
        xbuf[0:8, :] = xbuf[Q:Q + 8, :]
    else:
        for k in range(CONV_WIDTH):
            acc = acc + cw[k:k + 1, :] * xin_ref[k]
    xc = _silu(acc)
    xs = xc[:, :SSD_INNER]
    gs = SSD_GROUPS * SSD_STATE
    bm = xc[:, SSD_INNER:SSD_INNER + gs]
    cm = xc[:, SSD_INNER + gs:]
    bm_b = bm.astype(BF16)
    cm_b = cm.astype(BF16)

    lane = lax.broadcasted_iota(jnp.int32, (Q, LANES), 1)
    v = dt_ref[...] + dtb_ref[...]
    dt = jnp.maximum(v, 0.0) + jnp.log1p(jnp.exp(-jnp.abs(v)))
    dt = jnp.where(lane < SSD_HEADS, dt, 0.0)
    dA = dt * (-jnp.exp(alog_ref[...]))
    ri = lax.broadcasted_iota(jnp.int32, (Q, Q), 0)
    ci = lax.broadcasted_iota(jnp.int32, (Q, Q), 1)
    if seg == Q:
        mask = ci <= ri
    else:
        same = (ri // seg) == (ci // seg)
        mask = jnp.logical_and(same, ci <= ri)
    acs = _split3_dot(jnp.where(mask, 1.0, 0.0).astype(BF16), dA)
    if seg == Q:
        acs_last = acs[Q - 1:Q, :]
    else:
        acs_last = _split3_dot(jnp.where(same, 1.0, 0.0).astype(BF16), dA)
    to_end = jnp.exp(acs_last - acs) * dt
    acsT = acs.T
    dtT = dt.T

    G = [_dot_nt(cm_b[:, g * SSD_STATE:(g + 1) * SSD_STATE], bm_b[:, g * SSD_STATE:(g + 1) * SSD_STATE])
         for g in range(SSD_GROUPS)]
    lane_lo = lane < SSD_HEAD_DIM
    heads_per_group = SSD_HEADS // SSD_GROUPS
    ypairs, epairs, xwpairs, decpairs = [], [], [], []
    for k in range(SSD_HEADS // 2):
        g = (2 * k) // heads_per_group
        xp = xs[:, k * LANES:(k + 1) * LANES]
        xhalf = (jnp.where(lane_lo, xp, 0.0).astype(BF16), jnp.where(lane_lo, 0.0, xp).astype(BF16))
        yk = jnp.zeros((Q, LANES), F32)
        for s in range(2):
            hh = 2 * k + s
            segm = acs[:, hh:hh + 1] - acsT[hh:hh + 1, :]
            m = G[g] * jnp.exp(jnp.where(mask, segm, -jnp.inf)) * dtT[hh:hh + 1, :]
            yk = yk + _dot(m.astype(BF16), xhalf[s])

        def pair(a):
            return jnp.where(lane_lo[:a.shape[0]], a[:, 2 * k:2 * k + 1], a[:, 2 * k + 1:2 * k + 2])

        e_p = jnp.exp(pair(acs))
        xw = xp * pair(to_end)
        dec = jnp.exp(pair(acs_last))
        if carry:
            h_prev = hT[k]
            yk = yk + _dot(cm_b[:, g * SSD_STATE:(g + 1) * SSD_STATE], h_prev.astype(BF16)) * e_p
            hT[k] = dec * h_prev + _dot_tn(bm_b[:, g * SSD_STATE:(g + 1) * SSD_STATE], xw.astype(BF16))
        else:
            epairs.append(e_p)
            xwpairs.append(xw)
            decpairs.append(dec)
        ypairs.append(yk)
    y = jnp.concatenate(ypairs, axis=1) + dsk_ref[...] * xs
    if carry:
        y = y * _silu(z_ref[...])
        y_ref[...] = _rms(y, g_ref[...]).astype(BF16)

        @pl.when(c == pl.num_programs(1) - 1)
        def _():
            for k in range(SSD_HEADS // 2):
                hfin_ref[k * LANES:(k + 1) * LANES, :] = hT[k].T
    else:
        ypre_ref[...] = y
        eacs_ref[...] = jnp.concatenate(epairs, axis=1)
        xw_ref[...] = jnp.concatenate(xwpairs, axis=1)
        dec_ref[...] = jnp.concatenate(decpairs, axis=1)
        bm_ref[...] = bm
        cm_ref[...] = cm


def _row(a, n):
    return jnp.pad(a.reshape(1, -1).astype(F32), ((0, 0), (0, n - a.size)))


def _ssd_prompt(xbc, z, dt, cw, cb, dtb, alog, dsk, g):
    nb, L, _ = xbc.shape
    Q = SSD_CHUNK if L % SSD_CHUNK == 0 else L
    nc = L // Q

    def tok(width):
        return pl.BlockSpec((None, Q, width), lambda b, c: (b, c, 0))

    def const(a):
        return pl.BlockSpec(a.shape, lambda b, c: (0,) * a.ndim)

    consts = (cw, cb, dtb, alog, dsk, g)
    return pl.pallas_call(
        functools.partial(_ssd_body, Q=Q, seg=Q, carry=True),
        grid=(nb, nc),
        in_specs=[tok(CONV_DIM), tok(SSD_INNER), tok(LANES)] + [const(a) for a in consts],
        out_specs=[tok(SSD_INNER), pl.BlockSpec((None, SSD_INNER, SSD_STATE), lambda b, c: (b, 0, 0))],
        out_shape=[SDS((nb, L, SSD_INNER), BF16), SDS((nb, SSD_INNER, SSD_STATE), F32)],
        scratch_shapes=[pltpu.VMEM((Q + 8, CONV_DIM), F32), pltpu.VMEM((SSD_HEADS // 2, SSD_STATE, LANES), F32)],
        compiler_params=_params(("arbitrary", "arbitrary")),
        name="ssd_prompt",
    )(xbc, z, dt, *consts)


def _ssd_sample(xsh, dt, cw, cb, dtb, alog, dsk, *, seg):
    _, T, _ = xsh.shape
    consts = (cw, cb, dtb, alog, dsk)

    def full(a):
        return pl.BlockSpec(a.shape, lambda i: (0,) * a.ndim)

    outs = [SDS((T, SSD_INNER), F32)] * 4 + [SDS((T, SSD_GROUPS * SSD_STATE), F32)] * 2
    return pl.pallas_call(
        functools.partial(_ssd_body, Q=T, seg=seg, carry=False),
        grid=(1,),
        in_specs=[full(xsh), full(dt)] + [full(a) for a in consts],
        out_specs=[full(o) for o in outs],
        out_shape=outs,
        compiler_params=_params(("arbitrary",)),
        name="ssd_sample",
    )(xsh, dt, *consts)


def _sstate_body(cm_ref, bm_ref, ypre_ref, eacs_ref, xw_ref, dec_ref, z_ref, s0_ref, g_ref, y_ref, sn_ref):
    s0 = s0_ref[...]
    s0b = s0.astype(BF16)
    cm = cm_ref[...].astype(BF16)
    bm = bm_ref[...].astype(BF16)
    rows = SSD_INNER // SSD_GROUPS
    yo = jnp.concatenate(
        [jnp.einsum("btn,bqn->btq", cm[:, :, g * SSD_STATE:(g + 1) * SSD_STATE],
                    s0b[:, g * rows:(g + 1) * rows, :], preferred_element_type=F32)
         for g in range(SSD_GROUPS)], axis=-1)
    y = (ypre_ref[...] + yo * eacs_ref[...]) * _silu(z_ref[...])
    y_ref[...] = _rms(y, g_ref[...]).astype(BF16)
    dec = dec_ref[...]
    hi = dec.astype(BF16)
    lo = (dec - hi.astype(F32)).astype(BF16)
    sel = jnp.where(lax.broadcasted_iota(jnp.int32, (dec.shape[0], dec.shape[1], SSD_STATE), 1) == 0,
                    1.0, 0.0).astype(BF16)
    dmat = (jnp.einsum("bjq,bjn->bqn", hi, sel, preferred_element_type=F32)
            + jnp.einsum("bjq,bjn->bqn", lo, sel, preferred_element_type=F32))
    xw = xw_ref[...].astype(BF16)
    upd = jnp.concatenate(
        [jnp.einsum("bjq,bjn->bqn", xw[:, :, g * rows:(g + 1) * rows],
                    bm[:, :, g * SSD_STATE:(g + 1) * SSD_STATE], preferred_element_type=F32)
         for g in range(SSD_GROUPS)], axis=1)
    sn_ref[...] = dmat * s0 + upd


def _sstate(cm, bm, ypre, eacs, xw, dec, z, s0, g, *, bs):
    nseq, t, _ = cm.shape
    bs = min(bs, nseq)

    def blk(a):
        return pl.BlockSpec((bs,) + a.shape[1:], lambda i: (i, 0, 0))

    ins = (cm, bm, ypre, eacs, xw, dec, z, s0)
    return pl.pallas_call(
        _sstate_body,
        grid=(nseq // bs,),
        in_specs=[blk(a) for a in ins] + [pl.BlockSpec(g.shape, lambda i: (0, 0))],
        out_specs=[blk(ypre), blk(s0)],
        out_shape=[SDS(ypre.shape, BF16), SDS(s0.shape, F32)],
        compiler_params=_params(("arbitrary",)),
        name="sstate",
    )(*ins, g)


def _attn_body(qt_ref, qn_ref, kp_ref, vt_ref, g_ref, o_ref, m_sc, acc_sc,
               s_a, s_b, s_c, smax_a, smax_b, smax_c, *, tk):
    i = pl.program_id(1)
    ncols = MLA_HEADS * TQ
    m_sc[...] = jnp.full(m_sc.shape, -jnp.inf, F32)
    acc_sc[...] = jnp.zeros(acc_sc.shape, F32)
    ct = 2 * TQ
    tiles = [slice(p * ct, (p + 1) * ct) for p in range(PAIRS)]

    def scores(j, s_ref, smax_ref, q_ref=qt_ref):
        k0 = pl.multiple_of(j * tk, tk)
        for p, cs in enumerate(tiles):
            s = _dot(kp_ref[p, pl.ds(k0, tk), :], q_ref[p])
            s_ref[:, cs] = s
            smax_ref[:, cs] = jnp.max(s, axis=0, keepdims=True)

    def softmax_pv(j, s_ref, smax_ref, masked):
        for p, cs in enumerate(tiles):
            s = s_ref[:, cs]
            if masked:
                key = j * tk + lax.broadcasted_iota(jnp.int32, (tk, ct), 0)
                tok = i * TQ + (lax.broadcasted_iota(jnp.int32, (tk, ct), 1) & (TQ - 1))
                s = jnp.where(key <= tok, s, -jnp.inf)
                smax = jnp.max(s, axis=0, keepdims=True)
            else:
                smax = smax_ref[:, cs]
            m_prev = m_sc[:, cs]
            m_new = jnp.maximum(m_prev, smax)
            alpha = jnp.exp2(m_prev - m_new)
            e = jnp.exp2(s - m_new)
            acc_sc[p] = alpha * acc_sc[p] + _dot(vt_ref[j, p], e.astype(BF16))
            m_sc[:, cs] = m_new

    nfull = (i * TQ) // tk

    @pl.when(i == 0)
    def _():
        scores(0, s_c, smax_c)

    def prefetch():
        scores(0, s_c, smax_c, qn_ref)

    @pl.when(nfull == 0)
    def _():
        softmax_pv(0, s_c, smax_c, True)
        prefetch()

    @pl.when(nfull >= 1)
    def _():
        scores(1, s_b, smax_b)
        softmax_pv(0, s_c, smax_c, False)

    def pair(p, carry):
        j = 2 * p + 1
        scores(j + 1, s_a, smax_a)
        softmax_pv(j, s_b, smax_b, False)
        scores(j + 2, s_b, smax_b)
        softmax_pv(j + 1, s_a, smax_a, False)
        return carry

    lax.fori_loop(0, jnp.maximum(nfull - 1, 0) // 2, pair, 0)
    odd = (nfull % 2) == 1

    @pl.when(odd)
    def _():
        prefetch()
        softmax_pv(nfull, s_b, smax_b, True)

    @pl.when(jnp.logical_and(jnp.logical_not(odd), nfull >= 2))
    def _():
        scores(nfull, s_a, smax_a)
        softmax_pv(nfull - 1, s_b, smax_b, False)
        prefetch()
        softmax_pv(nfull, s_a, smax_a, True)

    ys = []
    for hh in range(MLA_HEADS):
        p, s = divmod(hh, 2)
        cols = slice(s * TQ, (s + 1) * TQ)
        denom = acc_sc[p, 2 * V_DIM:2 * V_DIM + 1, cols]
        ys.append(acc_sc[p, s * V_DIM:(s + 1) * V_DIM, cols] * (1.0 / denom))
    yt = jnp.concatenate(ys, axis=0)
    yt = yt * lax.rsqrt(jnp.mean(yt * yt, axis=0, keepdims=True) + EPS)
    o_ref[...] = (yt.T * g_ref[...]).astype(BF16)


def _attn_prompt(qt, kp, vt, g, *, tk):
    nb, nq = qt.shape[:2]
    L = nq * TQ
    ncols = MLA_HEADS * TQ
    assert TQ & (TQ - 1) == 0 and L % tk == 0 and tk % TQ == 0

    def q_spec(index):
        return pl.BlockSpec((None, None, PAIRS, PAIR_K, 2 * TQ), lambda b, i: (b, index(i), 0, 0, 0))

    return pl.pallas_call(
        functools.partial(_attn_body, tk=tk),
        grid=(nb, nq),
        in_specs=[q_spec(lambda i: i), q_spec(lambda i: jnp.minimum(i + 1, nq - 1)),
                  pl.BlockSpec((None, PAIRS, L, PAIR_K), lambda b, i: (b, 0, 0, 0), pipeline_mode=pl.Buffered(1)),
                  pl.BlockSpec((None, L // tk, PAIRS, PAIR_V, tk), lambda b, i: (b, 0, 0, 0, 0),
                               pipeline_mode=pl.Buffered(1)),
                  pl.BlockSpec(g.shape, lambda b, i: (0, 0))],
        out_specs=pl.BlockSpec((None, TQ, MLA_INNER), lambda b, i: (b, i, 0)),
        out_shape=SDS((nb, L, MLA_INNER), BF16),
        scratch_shapes=[pltpu.VMEM((1, ncols), F32),
                        pltpu.VMEM((PAIRS, PAIR_V, 2 * TQ), F32),
                        pltpu.VMEM((tk, ncols), F32), pltpu.VMEM((tk, ncols), F32), pltpu.VMEM((tk, ncols), F32),
                        pltpu.VMEM((1, ncols), F32), pltpu.VMEM((1, ncols), F32), pltpu.VMEM((1, ncols), F32)],
        compiler_params=_params(("arbitrary", "arbitrary")),
        name="attn_prompt",
    )(qt, qt, kp, vt, g)


def _sattn_body(pt_ref, qa_ref, qr_ref, kn_ref, rn_ref, ckv_hbm, krt_hbm, o_ref, kbuf, rbuf, sem, *, npages, t_new):
    b = pl.program_id(0)
    nseq = pl.num_programs(0)

    def copies(seq_page, slot, p):
        off = p * PAGE_SIZE
        return (pltpu.make_async_copy(ckv_hbm.at[seq_page], kbuf.at[slot, p], sem.at[0, slot]),
                pltpu.make_async_copy(krt_hbm.at[seq_page], rbuf.at[slot, :, pl.ds(off, PAGE_SIZE)],
                                      sem.at[1, slot]))

    def start_fetch(seq, slot):
        for p in range(npages):
            for cp in copies(pt_ref[seq * npages + p], slot, p):
                cp.start()

    def wait_fetch(slot):
        pltpu.make_async_copy(ckv_hbm.at[pl.ds(0, npages)], kbuf.at[slot], sem.at[0, slot]).wait()
        pltpu.make_async_copy(rbuf.at[slot], rbuf.at[slot], sem.at[1, slot]).wait()

    @pl.when(b == 0)
    def _():
        start_fetch(0, 0)

    @pl.when(b + 1 < nseq)
    def _():
        start_fetch(b + 1, (b + 1) % 2)

    slot = b % 2
    wait_fetch(slot)
    pg = npages // SPLIT
    part = pg * PAGE_SIZE
    kps = [kbuf[slot, pl.ds(h * pg, pg)].reshape(part, KV_RANK).astype(BF16) for h in range(SPLIT)]
    rpt = rbuf[slot].astype(BF16)
    q = qa_ref[...]
    qr = qr_ref[...]
    kn = kn_ref[...]
    rn = rn_ref[...]
    s_main = [_dot_nt(q, kps[h]) for h in range(SPLIT)]
    s_rope = [_dot(qr, rpt[:, h * part:(h + 1) * part]) for h in range(SPLIT)]
    s_p = [a + b for a, b in zip(s_main, s_rope)]
    s_n = _dot_nt(q, kn) + _dot_nt(qr, rn)
    rows = t_new * MLA_HEADS
    tok = lax.broadcasted_iota(jnp.int32, (rows, t_new), 0) // MLA_HEADS
    col = lax.broadcasted_iota(jnp.int32, (rows, t_new), 1)
    s_n = jnp.where(col <= tok, s_n, -jnp.inf)
    m = jnp.max(s_n, axis=-1, keepdims=True)
    for s in s_p:
        m = jnp.maximum(m, jnp.max(s, axis=-1, keepdims=True))
    pn = jnp.exp(s_n - m)
    pp = [jnp.exp(s - m) for s in s_p]
    l = jnp.sum(pn, axis=-1, keepdims=True)
    for p in pp:
        l = l + jnp.sum(p, axis=-1, keepdims=True)
    inv = 1.0 / l
    parts = [_dot(pp[h].astype(BF16), kps[h]) for h in range(SPLIT)]
    o = _dot(pn.astype(BF16), kn) * inv
    for part_o in parts:
        o = o + part_o * inv
    o_ref[...] = o


def _sattn(page_table, qa, qr, kn, rn, cache_kv, cache_krt):
    nseq, npages = page_table.shape
    t_new = kn.shape[1]
    rows = t_new * MLA_HEADS
    grid_spec = pltpu.PrefetchScalarGridSpec(
        num_scalar_prefetch=1,
        grid=(nseq,),
        in_specs=[pl.BlockSpec((rows, KV_RANK), lambda b, pt: (b, 0)),
                  pl.BlockSpec((rows, QK_ROPE), lambda b, pt: (b, 0)),
                  pl.BlockSpec((None, t_new, KV_RANK), lambda b, pt: (b, 0, 0)),
                  pl.BlockSpec((None, t_new, QK_ROPE), lambda b, pt: (b, 0, 0)),
                  pl.BlockSpec(memory_space=pl.ANY),
                  pl.BlockSpec(memory_space=pl.ANY)],
        out_specs=pl.BlockSpec((rows, KV_RANK), lambda b, pt: (b, 0)),
        scratch_shapes=[pltpu.VMEM((2, npages, PAGE_SIZE, KV_RANK), F32),
                        pltpu.VMEM((2, QK_ROPE, npages * PAGE_SIZE), F32),
                        pltpu.SemaphoreType.DMA((2, 2))],
    )
    return pl.pallas_call(
        functools.partial(_sattn_body, npages=npages, t_new=t_new),
        grid_spec=grid_spec,
        out_shape=SDS((nseq * rows, KV_RANK), F32),
        compiler_params=_params(("arbitrary",)),
        name="attn_sample",
    )(page_table.reshape(-1), qa, qr, kn, rn, cache_kv, cache_krt)


def _apost_body(o_ref, wuv_ref, g_ref, y_ref):
    tm = o_ref.shape[0]
    o = jnp.zeros((tm, MLA_INNER), F32)
    for hh in range(MLA_HEADS):
        o = o + _dot(o_ref[:, hh * KV_RANK:(hh + 1) * KV_RANK].astype(BF16), wuv_ref[hh])
    y_ref[...] = _rms(o, g_ref[...]).astype(BF16)


def _apost(o_lat, wuv_pad, g):
    T = o_lat.shape[0]
    return pl.pallas_call(
        _apost_body,
        grid=(1,),
        in_specs=[pl.BlockSpec(o_lat.shape, lambda i: (0, 0)),
                  pl.BlockSpec(wuv_pad.shape, lambda i: (0, 0, 0)),
                  pl.BlockSpec(g.shape, lambda i: (0, 0))],
        out_specs=pl.BlockSpec((T, MLA_INNER), lambda i: (0, 0)),
        out_shape=SDS((T, MLA_INNER), BF16),
        compiler_params=_params(("arbitrary",)),
        name="attn_post",
    )(o_lat, wuv_pad, g)


def _mlp_body(x_ref, ys_ref, ya_ref, g1_ref, sh2_ref, sc2_ref, g2_ref, shf_ref, scf_ref,
              wout_ref, gmlp_ref, wup_ref, wdn_ref, gfin_ref, o_ref, *, final, tf, rep):
    yy = jnp.concatenate([ys_ref[...], ya_ref[...]], axis=-1)
    x1 = x_ref[...] + _mod_rows(g1_ref, rep) * _dot(yy, wout_ref[...])
    h2 = (_rms(x1, gmlp_ref[...]) * (1.0 + _mod_rows(sc2_ref, rep)) + _mod_rows(sh2_ref, rep)).astype(BF16)
    acc = jnp.zeros(x1.shape, F32)
    for c in range(D_FF // tf):
        u = jnp.maximum(_dot(h2, wup_ref[:, c * tf:(c + 1) * tf]), 0.0)
        acc = acc + _dot((u * u).astype(BF16), wdn_ref[c * tf:(c + 1) * tf, :])
    x2 = x1 + _mod_rows(g2_ref, rep) * acc
    if final:
        x2 = _rms(x2, gfin_ref[...]) * (1.0 + _mod_rows(scf_ref, rep)) + _mod_rows(shf_ref, rep)
    o_ref[...] = x2


def _mlp(x3, ys, ya, mod3, wout, gmlp, wup, wdn, gfin, *, final, tm, tf):
    nb, L, d = x3.shape
    tm = min(tm, L)
    rep, mod_spec = _mod_spec(mod3, L, tm, d)

    def tok(width):
        return pl.BlockSpec((None, tm, width), lambda b, i: (b, i, 0))

    def const(a):
        return pl.BlockSpec(a.shape, lambda b, i: (0,) * a.ndim, pipeline_mode=pl.Buffered(1))

    return pl.pallas_call(
        functools.partial(_mlp_body, final=final, tf=tf, rep=rep),
        grid=(nb, L // tm),
        in_specs=[tok(d), tok(SSD_INNER), tok(MLA_INNER)] + [mod_spec(k) for k in (2, 3, 4, 5, 6, 7)]
                 + [const(wout), const(gmlp), const(wup), const(wdn), const(gfin)],
        out_specs=tok(d),
        out_shape=SDS((nb, L, d), F32),
        compiler_params=_params(("arbitrary", "arbitrary")),
        name="mlp",
    )(x3, ys, ya, mod3, mod3, mod3, mod3, mod3, mod3, wout, gmlp, wup, wdn, gfin)


def _rope_tables(pos):
    inv = 1.0 / (ROPE_THETA ** (np.arange(0, QK_ROPE, 2, dtype=np.float64) / QK_ROPE))
    ang = pos.astype(np.float64)[:, None] * inv[None, :]
    cos, sin = np.cos(ang).astype(np.float32), np.sin(ang).astype(np.float32)
    c32 = np.concatenate([cos, cos], axis=-1)
    s32 = np.concatenate([-sin, sin], axis=-1)
    tab1 = np.concatenate([c32, s32], axis=-1)
    tabq = np.concatenate([np.tile(c32, (1, MLA_HEADS)), np.tile(s32, (1, MLA_HEADS))], axis=-1)
    return tab1, tabq


def _swap_halves(w):
    half = w.shape[-1] // 2
    return jnp.concatenate([w[..., half:], w[..., :half]], axis=-1)


def kernel(x_prompt, x_sample, cache_kv_latent, cache_k_rope, state_conv, state_ssm, page_table,
           c_prompt, c_sample, w_ada, b_ada, norm_mix_g, w_in, conv_w, conv_b, dt_bias, a_log,
           d_skip, norm_ssd_g, q_norm_g, kv_norm_g, w_uq, w_uk, w_uv, norm_attn_g, w_out,
           norm_mlp_g, w_up, w_down, w_ada_final, b_ada_final, norm_final_g):
    depth = w_in.shape[0]
    b_p, seq, d = x_prompt.shape
    n_seq, t_new, _ = x_sample.shape
    n_tok_s = n_seq * t_new
    past_len = page_table.shape[1] * PAGE_SIZE

    c_all = jnp.concatenate([c_prompt, c_sample], axis=0)
    ada_fin = _ada(c_all, w_ada_final, b_ada_final)
    tab1_p, _ = _rope_tables(np.arange(seq))
    tab1_s, tabq_s = _rope_tables(past_len + np.arange(t_new))
    tab1_pt = jnp.asarray(np.ascontiguousarray(tab1_p.T))
    tab1_p = jnp.asarray(tab1_p)
    tabq_s = jnp.asarray(np.tile(tabq_s, (n_seq, 1)))
    tab1_s = jnp.asarray(np.tile(tab1_s, (n_seq, 1)))

    xp = x_prompt
    xs = x_sample.reshape(1, n_tok_s, d)
    outs_p, outs_s = [], []
    for l in range(depth):
        final = l == depth - 1
        wi = w_in[l]
        c1 = SSD_INNER
        c2 = c1 + CONV_DIM
        c3 = c2 + SSD_HEADS
        c4 = c3 + Q_RANK
        c5 = c4 + KV_RANK
        w_kr = wi[:, c5:]
        win = jnp.concatenate(
            [wi[:, :c2], wi[:, c3:c5], wi[:, c2:c3], jnp.zeros((d, MISC_KR - SSD_HEADS), F32), w_kr,
             _swap_halves(w_kr), jnp.zeros((d, LANES - MISC_KRSW - QK_ROPE), F32)], axis=1).astype(BF16)
        wq_h = w_uq[l].reshape(Q_RANK, MLA_HEADS, QK_NOPE + QK_ROPE)
        w_rope = wq_h[:, :, QK_NOPE:]
        wfold = _fold(jnp.transpose(wq_h, (1, 0, 2)), jnp.transpose(w_uk[l], (1, 0, 2)))
        wq = jnp.concatenate([wfold, w_rope.reshape(Q_RANK, -1).astype(BF16),
                              _swap_halves(w_rope).reshape(Q_RANK, -1).astype(BF16)], axis=1)
        wuv_pad = jnp.zeros((MLA_HEADS, KV_RANK, MLA_HEADS, V_DIM), F32)
        wuv_pad = wuv_pad.at[jnp.arange(MLA_HEADS), :, jnp.arange(MLA_HEADS), :].set(
            jnp.transpose(w_uv[l], (1, 0, 2)))
        wuv_pad = wuv_pad.reshape(MLA_HEADS, KV_RANK, MLA_INNER).astype(BF16)
        wq_p = jnp.concatenate([wq_h[:, :, :QK_NOPE].reshape(Q_RANK, -1), w_rope.reshape(Q_RANK, -1),
                                _swap_halves(w_rope).reshape(Q_RANK, -1)], axis=1).T.astype(BF16)
        wk = w_uk[l].reshape(KV_RANK, MLA_HEADS * QK_NOPE).astype(BF16)
        wvt = jnp.transpose(w_uv[l], (1, 2, 0)).reshape(PAIRS, 2 * V_DIM, KV_RANK).astype(BF16)
        wout = w_out[l].astype(BF16)
        wup = w_up[l].astype(BF16)
        wdn = w_down[l].astype(BF16)
        gmix = norm_mix_g[l].reshape(1, d)
        gmlp = norm_mlp_g[l].reshape(1, d)
        gfin = norm_final_g.reshape(1, d)
        qg = q_norm_g[l].reshape(1, Q_RANK)
        kvg = kv_norm_g[l].reshape(1, KV_RANK)
        gssd = norm_ssd_g[l].reshape(1, SSD_INNER)
        gattn = norm_attn_g[l].reshape(1, MLA_INNER)
        cw = conv_w[l]
        cb = conv_b[l].reshape(1, CONV_DIM)
        dtb = _row(dt_bias[l], LANES)
        alog = _row(a_log[l], LANES)
        dsk = jnp.repeat(d_skip[l].astype(F32), SSD_HEAD_DIM).reshape(1, SSD_INNER)

        ada = _ada(c_all, w_ada[l], b_ada[l])
        mod = jnp.concatenate([ada, ada_fin], axis=1)
        mod_p = mod[:b_p].reshape(b_p, 1, 8 * d)
        mod_s = mod[b_p:].reshape(1, n_seq, 8 * d)

        tk = min(512, seq)
        z, xbc, dtr, ckv, kr, kp, vt, qt = _inproj(
            xp, mod_p, gmix, win, qg, kvg, wq_p, tab1_pt, tab1_p, q_transposed=True, tm=512, tk=tk,
            wk=wk, wvt=wvt)
        y_ssd, hfin = _ssd_prompt(xbc, z, dtr, cw, cb, dtb, alog, dsk, gssd)
        y_attn = _attn_prompt(qt, kp, vt, gattn, tk=tk)
        xp = _mlp(xp, y_ssd, y_attn, mod_p, wout, gmlp, wup, wdn, gfin, final=final, tm=512, tf=2048)
        tail = min(seq, CONV_WIDTH - 1)
        conv_tail = jnp.concatenate([jnp.zeros((b_p, CONV_WIDTH - 1 - tail, CONV_DIM), F32),
                                     xbc[:, seq - tail:]], axis=1)
        outs_p.append((ckv, kr, conv_tail, hfin.reshape(b_p, SSD_HEADS, SSD_HEAD_DIM, SSD_STATE)))

        z, xbc, dtr, ckv, kr, kc, krb, qa, qr = _inproj(
            xs, mod_s, gmix, win, qg, kvg, wq, tabq_s, tab1_s, q_transposed=False, tm=256)
        xbc3 = xbc.reshape(n_seq, t_new, CONV_DIM)
        xpad = jnp.concatenate([state_conv[l], xbc3], axis=1)
        xsh = jnp.stack([xpad[:, k:k + t_new].reshape(n_tok_s, CONV_DIM) for k in range(CONV_WIDTH)])
        ypre, eacs, xw, dec, bm, cm = _ssd_sample(xsh, dtr[0], cw, cb, dtb, alog, dsk, seg=t_new)

        def seqs(a):
            return a.reshape(n_seq, t_new, a.shape[-1])

        y_ssd, s_new = _sstate(seqs(cm), seqs(bm), seqs(ypre), seqs(eacs), seqs(xw), seqs(dec), seqs(z[0]),
                               state_ssm[l].reshape(n_seq, SSD_INNER, SSD_STATE), gssd, bs=8)
        o_lat = _sattn(page_table, qa.reshape(n_tok_s * MLA_HEADS, KV_RANK),
                       qr.reshape(n_tok_s * MLA_HEADS, QK_ROPE), seqs(kc[0]), seqs(krb[0]),
                       cache_kv_latent[l], jnp.swapaxes(cache_k_rope[l], -1, -2))
        y_attn = _apost(o_lat.reshape(n_tok_s, MLA_HEADS * KV_RANK), wuv_pad, gattn)
        xs = _mlp(xs, y_ssd.reshape(1, n_tok_s, SSD_INNER), y_attn.reshape(1, n_tok_s, MLA_INNER), mod_s,
                  wout, gmlp, wup, wdn, gfin, final=final, tm=n_tok_s, tf=2048)
        outs_s.append((seqs(ckv[0]), seqs(kr[0]), xpad[:, t_new:],
                       s_new.reshape(n_seq, SSD_HEADS, SSD_HEAD_DIM, SSD_STATE)))

    def stack(outs, k):
        return jnp.stack([o[k] for o in outs])

    return (xp, xs.reshape(n_seq, t_new, d),
            stack(outs_p, 0), stack(outs_p, 1), stack(outs_p, 2), stack(outs_p, 3),
            stack(outs_s, 0), stack(outs_s, 1), stack(outs_s, 2), stack(outs_s, 3))
```

```python
import functools
import math

import jax
import jax.numpy as jnp
import numpy as np
from jax import lax
from jax.experimental import pallas as pl
from jax.experimental.pallas import tpu as pltpu

F32 = jnp.float32
BF16 = jnp.bfloat16
SDS = jax.ShapeDtypeStruct

D_MODEL = 1024
SSD_HEADS = 8
SSD_HEAD_DIM = 64
SSD_INNER = SSD_HEADS * SSD_HEAD_DIM
SSD_GROUPS = 2
SSD_STATE = 128
CONV_WIDTH = 4
SSD_CHUNK = 128
CONV_DIM = SSD_INNER + 2 * SSD_GROUPS * SSD_STATE
MLA_HEADS = 8
QK_NOPE = 64
QK_ROPE = 32
V_DIM = 64
KV_RANK = 256
Q_RANK = 384
MLA_INNER = MLA_HEADS * V_DIM
ROPE_THETA = 10000.0
ATTN_SCALE = 1.0 / math.sqrt(QK_NOPE + QK_ROPE)
LOG2E = math.log2(math.e)
PAGE_SIZE = 128
D_FF = 4 * D_MODEL
EPS = 1e-6

LANES = 128
TQ = 128
PAIRS = MLA_HEADS // 2
PAIR_K = 256
PAIR_V = 2 * V_DIM + 16
SPLIT = 2
FUSED_SPLIT = 8
C_Z = 0
C_XBC = C_Z + SSD_INNER
C_QLAT = C_XBC + CONV_DIM
C_KVLAT = C_QLAT + Q_RANK
C_MISC = C_KVLAT + KV_RANK
W_IN_COLS = C_MISC + LANES
MISC_KR = 32
MISC_KRSW = 64
C_QROPE = MLA_HEADS * KV_RANK
C_QROPE_SW = C_QROPE + MLA_HEADS * QK_ROPE
WQ_COLS = C_QROPE_SW + MLA_HEADS * QK_ROPE

VMEM_LIMIT = 52 * 1024 * 1024


def _dot(a, b):
    return jnp.dot(a, b, preferred_element_type=F32)


def _dot_nt(a, b):
    return lax.dot_general(a, b, (((1,), (1,)), ((), ())), preferred_element_type=F32)


def _dot_tn(a, b):
    return lax.dot_general(a, b, (((0,), (0,)), ((), ())), preferred_element_type=F32)


def _silu(x):
    return x * jax.nn.sigmoid(x)


def _rms(x, g):
    return x * lax.rsqrt(jnp.mean(x * x, axis=-1, keepdims=True) + EPS) * g


def _split3_dot(mask_bf16, v):
    v1 = v.astype(BF16)
    r1 = v - v1.astype(F32)
    v2 = r1.astype(BF16)
    v3 = (r1 - v2.astype(F32)).astype(BF16)
    return _dot(mask_bf16, v1) + _dot(mask_bf16, v2) + _dot(mask_bf16, v3)


def _mod_spec(mod3, L, tm, d):
    rows = mod3.shape[1]
    if rows == 1:
        return 1, lambda k: pl.BlockSpec((None, 1, d), lambda b, i: (b, 0, k))
    rep = L // rows
    return rep, lambda k: pl.BlockSpec((None, tm // rep, d), lambda b, i: (b, i, k))


def _mod_rows(ref, rep):
    m = ref[...]
    if rep == 1:
        return m
    n = m.shape[0] * rep
    sel = (lax.broadcasted_iota(jnp.int32, (n, m.shape[0]), 0) // rep
           == lax.broadcasted_iota(jnp.int32, (n, m.shape[0]), 1))
    sel = jnp.where(sel, 1.0, 0.0).astype(BF16)
    hi = m.astype(BF16)
    lo = (m - hi.astype(F32)).astype(BF16)
    return _dot(sel, hi) + _dot(sel, lo)


def _params(sem, vmem=VMEM_LIMIT):
    return pltpu.CompilerParams(dimension_semantics=sem, vmem_limit_bytes=vmem)


def _ada_body(c_ref, w_ref, b_ref, o_ref):
    s = _silu(c_ref[...]).astype(BF16)
    o_ref[...] = _dot(s, w_ref[...].astype(BF16)) + b_ref[...]


def _ada(c, w, b):
    bsz, d = c.shape
    n = w.shape[1]
    tn = 1024
    return pl.pallas_call(
        _ada_body,
        grid=(n // tn,),
        in_specs=[pl.BlockSpec((bsz, d), lambda j: (0, 0)),
                  pl.BlockSpec((d, tn), lambda j: (0, j)),
                  pl.BlockSpec((1, tn), lambda j: (0, j))],
        out_specs=pl.BlockSpec((bsz, tn), lambda j: (0, j)),
        out_shape=SDS((bsz, n), F32),
        compiler_params=_params(("arbitrary",)),
        name="ada",
    )(c, w, b.reshape(1, n))


def _fold_body(wq_ref, wk_ref, o_ref):
    a = wq_ref[:, 0:QK_NOPE].astype(BF16)
    o_ref[...] = _dot_nt(a, wk_ref[...].astype(BF16)).astype(BF16)


def _fold(wq_h, wk_h):
    return pl.pallas_call(
        _fold_body,
        grid=(MLA_HEADS,),
        in_specs=[pl.BlockSpec((None, Q_RANK, QK_NOPE + QK_ROPE), lambda h: (h, 0, 0)),
                  pl.BlockSpec((None, KV_RANK, QK_NOPE), lambda h: (h, 0, 0))],
        out_specs=pl.BlockSpec((Q_RANK, KV_RANK), lambda h: (0, h)),
        out_shape=SDS((Q_RANK, MLA_HEADS * KV_RANK), BF16),
        compiler_params=_params(("arbitrary",)),
        name="fold",
    )(wq_h, wk_h)


def _inproj_body(*refs, q_transposed, rep):
    n_in = 12 if q_transposed else 10
    x_ref, sh_ref, sc_ref, gmix_ref, win_ref, qg_ref, kvg_ref, wq_ref, tq_ref, tk_ref = refs[:10]
    z_ref, xbc_ref, dt_ref, ckv_ref, kr_ref = refs[n_in:n_in + 5]
    h = _rms(x_ref[...], gmix_ref[...]) * (1.0 + _mod_rows(sc_ref, rep)) + _mod_rows(sh_ref, rep)
    proj = _dot(h.astype(BF16), win_ref[...])
    tm = proj.shape[0]
    z_ref[...] = proj[:, C_Z:C_XBC]
    xbc_ref[...] = proj[:, C_XBC:C_QLAT]
    q_scale = ATTN_SCALE * LOG2E if q_transposed else ATTN_SCALE
    qn = (_rms(proj[:, C_QLAT:C_KVLAT], qg_ref[...]) * q_scale).astype(BF16)
    ckv = _rms(proj[:, C_KVLAT:C_MISC], kvg_ref[...])
    ckv_ref[...] = ckv
    misc = proj[:, C_MISC:W_IN_COLS]
    lane = lax.broadcasted_iota(jnp.int32, misc.shape, 1)
    dt_ref[...] = jnp.where(lane < SSD_HEADS, misc, 0.0)
    tk = tk_ref[...]
    kr = (misc[:, MISC_KR:MISC_KR + QK_ROPE] * tk[:, :QK_ROPE]
          + misc[:, MISC_KRSW:MISC_KRSW + QK_ROPE] * tk[:, QK_ROPE:])
    kr_ref[...] = kr
    nr = MLA_HEADS * QK_ROPE
    if q_transposed:
        wk_ref, wvt_ref = refs[10:12]
        kp_ref, vt_ref, qt_ref = refs[n_in + 5:]
        ckv_b = ckv.astype(BF16)
        k_nope = _dot(ckv_b, wk_ref[...])
        kr_pad = jnp.concatenate([kr, jnp.zeros((tm, PAIR_K - 2 * QK_NOPE - QK_ROPE), F32)], axis=1).astype(BF16)
        pad_rows = PAIR_V - 2 * V_DIM
        ones_row = jnp.where(lax.broadcasted_iota(jnp.int32, (pad_rows, tm), 0) == 0, 1.0, 0.0).astype(BF16)
        for p in range(PAIRS):
            kp_ref[p, :, 0:2 * QK_NOPE] = k_nope[:, p * 2 * QK_NOPE:(p + 1) * 2 * QK_NOPE].astype(BF16)
            kp_ref[p, :, 2 * QK_NOPE:PAIR_K] = kr_pad
            vt_ref[p, 0:2 * V_DIM] = _dot_nt(wvt_ref[p], ckv_b).astype(BF16)
            vt_ref[p, 2 * V_DIM:PAIR_V] = ones_row
        qt = _dot_nt(wq_ref[...], qn)
        n0 = MLA_HEADS * QK_NOPE
        tq = tq_ref[...]
        cos_t = jnp.concatenate([tq[:QK_ROPE]] * MLA_HEADS, axis=0)
        sin_t = jnp.concatenate([tq[QK_ROPE:]] * MLA_HEADS, axis=0)
        rot = (qt[n0:n0 + nr] * cos_t + qt[n0 + nr:n0 + 2 * nr] * sin_t).astype(BF16)
        q_nope = qt[:n0].astype(BF16)
        zero = jnp.zeros((PAIR_K, TQ), BF16)
        for c in range(tm // TQ):
            toks = slice(c * TQ, (c + 1) * TQ)
            for p in range(PAIRS):
                for s in range(2):
                    hh = 2 * p + s
                    cols = slice(s * TQ, (s + 1) * TQ)
                    qt_ref[c, p, :, cols] = zero
                    qt_ref[c, p, s * QK_NOPE:(s + 1) * QK_NOPE, cols] = q_nope[hh * QK_NOPE:(hh + 1) * QK_NOPE, toks]
                    qt_ref[c, p, 2 * QK_NOPE:2 * QK_NOPE + QK_ROPE, cols] = rot[hh * QK_ROPE:(hh + 1) * QK_ROPE, toks]
    else:
        kc_ref, krb_ref, qa_ref, qr_ref = refs[n_in + 5:]
        kc_ref[...] = ckv.astype(BF16)
        krb_ref[...] = kr.astype(BF16)
        q = _dot(qn, wq_ref[...])
        tq = tq_ref[...]
        qa_ref[...] = q[:, :C_QROPE].astype(BF16)
        qr_ref[...] = (q[:, C_QROPE:C_QROPE_SW] * tq[:, :nr] + q[:, C_QROPE_SW:WQ_COLS] * tq[:, nr:]).astype(BF16)


def _inproj(x3, mod3, gmix, win, qg, kvg, wq, tabq, tabk, *, q_transposed, tm, tk=None, wk=None, wvt=None):
    nb, L, d = x3.shape
    tm = min(tm, L)
    nt = L // tm
    rep, mod_spec = _mod_spec(mod3, L, tm, d)

    def tok(width):
        return pl.BlockSpec((None, tm, width), lambda b, i: (b, i, 0))

    def const(a):
        return pl.BlockSpec(a.shape, lambda b, i: (0,) * a.ndim, pipeline_mode=pl.Buffered(1))

    nr = MLA_HEADS * QK_ROPE
    out_specs = [tok(SSD_INNER), tok(CONV_DIM), tok(LANES), tok(KV_RANK), tok(QK_ROPE)]
    out_shape = [SDS((nb, L, SSD_INNER), F32), SDS((nb, L, CONV_DIM), F32), SDS((nb, L, LANES), F32),
                 SDS((nb, L, KV_RANK), F32), SDS((nb, L, QK_ROPE), F32)]
    if q_transposed:
        assert tm % TQ == 0 and tk % tm == 0
        r = tk // tm
        tabq_spec = pl.BlockSpec((2 * QK_ROPE, tm), lambda b, i: (0, i))
        out_specs += [pl.BlockSpec((None, PAIRS, tm, PAIR_K), lambda b, i: (b, 0, i, 0)),
                      pl.BlockSpec((None, None, PAIRS, PAIR_V, tm), lambda b, i: (b, i // r, 0, 0, i % r)),
                      pl.BlockSpec((None, tm // TQ, PAIRS, PAIR_K, 2 * TQ), lambda b, i: (b, i, 0, 0, 0))]
        out_shape += [SDS((nb, PAIRS, L, PAIR_K), BF16), SDS((nb, L // tk, PAIRS, PAIR_V, tk), BF16),
                      SDS((nb, L // TQ, PAIRS, PAIR_K, 2 * TQ), BF16)]
        extra = [wk, wvt]
    else:
        extra = []
        tabq_spec = pl.BlockSpec((tm, 2 * nr), lambda b, i: (i, 0))
        out_specs += [tok(KV_RANK), tok(QK_ROPE), tok(C_QROPE), tok(nr)]
        out_shape += [SDS((nb, L, KV_RANK), BF16), SDS((nb, L, QK_ROPE), BF16),
                      SDS((nb, L, C_QROPE), BF16), SDS((nb, L, nr), BF16)]
    return pl.pallas_call(
        functools.partial(_inproj_body, q_transposed=q_transposed, rep=rep),
        grid=(nb, nt),
        in_specs=[tok(d), mod_spec(0), mod_spec(1), const(gmix), const(win), const(qg), const(kvg), const(wq),
                  tabq_spec, pl.BlockSpec((tm, 2 * QK_ROPE), lambda b, i: (i, 0))] + [const(a) for a in extra],
        out_specs=out_specs,
        out_shape=out_shape,
        compiler_params=_params(("arbitrary", "arbitrary")),
        name="inproj_prompt" if q_transposed else "inproj_sample",
    )(x3, mod3, mod3, gmix, win, qg, kvg, wq, tabq, tabk, *extra)


def _ssd_reset(c, xbuf, hT):
    @pl.when(c == 0)
    def _():
        xbuf[0:8, :] = jnp.zeros((8, CONV_DIM), F32)
        hT[...] = jnp.zeros(hT.shape, F32)


def _ssd_final(c, n_chunks, hfin_ref, hT):
    @pl.when(c == n_chunks - 1)
    def _():
        for k in range(SSD_HEADS // 2):
            hfin_ref[k * LANES:(k + 1) * LANES, :] = hT[k].T


def _ssd_body(*refs, **kw):
    for _ in _ssd_steps(*refs, **kw):
        pass


def _ssd_steps(*refs, Q, seg, carry, chunk_pos=None):
    if carry:
        (xin_ref, z_ref, dt_ref, cw_ref, cb_ref, dtb_ref, alog_ref, dsk_ref, g_ref,
         y_ref, hfin_ref, xbuf, hT) = refs
    else:
        (xin_ref, dt_ref, cw_ref, cb_ref, dtb_ref, alog_ref, dsk_ref,
         ypre_ref, eacs_ref, xw_ref, dec_ref, bm_ref, cm_ref) = refs
    cw = cw_ref[...]
    acc = jnp.broadcast_to(cb_ref[...], (Q, CONV_DIM))
    if carry:
        c, n_chunks = (pl.program_id(1), pl.num_programs(1)) if chunk_pos is None else chunk_pos
        if chunk_pos is None:
            _ssd_reset(c, xbuf, hT)
        xbuf[8:8 + Q, :] = xin_ref[...]
        for k in range(CONV_WIDTH):
            acc = acc + cw[k:k + 1, :] * xbuf[pl.ds(8 - (CONV_WIDTH - 1) + k, Q), :]
        xbuf[0:8, :] = xbuf[Q:Q + 8, :]
    else:
        for k in range(CONV_WIDTH):
            acc = acc + cw[k:k + 1, :] * xin_ref[k]
    xc = _silu(acc)
    xs = xc[:, :SSD_INNER]
    gs = SSD_GROUPS * SSD_STATE
    bm = xc[:, SSD_INNER:SSD_INNER + gs]
    cm = xc[:, SSD_INNER + gs:]
    bm_b = bm.astype(BF16)
    cm_b = cm.astype(BF16)
    yield

    lane = lax.broadcasted_iota(jnp.int32, (Q, LANES), 1)
    v = dt_ref[...] + dtb_ref[...]
    dt = jnp.maximum(v, 0.0) + jnp.log1p(jnp.exp(-jnp.abs(v)))
    dt = jnp.where(lane < SSD_HEADS, dt, 0.0)
    dA = dt * (-jnp.exp(alog_ref[...]))
    ri = lax.broadcasted_iota(jnp.int32, (Q, Q), 0)
    ci = lax.broadcasted_iota(jnp.int32, (Q, Q), 1)
    if seg == Q:
        mask = ci <= ri
    else:
        same = (ri // seg) == (ci // seg)
        mask = jnp.logical_and(same, ci <= ri)
    acs = _split3_dot(jnp.where(mask, 1.0, 0.0).astype(BF16), dA)
    if seg == Q:
        acs_last = acs[Q - 1:Q, :]
    else:
        acs_last = _split3_dot(jnp.where(same, 1.0, 0.0).astype(BF16), dA)
    to_end = jnp.exp(acs_last - acs) * dt
    acsT = acs.T
    dtT = dt.T
    yield

    G = [_dot_nt(cm_b[:, g * SSD_STATE:(g + 1) * SSD_STATE], bm_b[:, g * SSD_STATE:(g + 1) * SSD_STATE])
         for g in range(SSD_GROUPS)]
    lane_lo = lane < SSD_HEAD_DIM
    heads_per_group = SSD_HEADS // SSD_GROUPS
    ypairs, epairs, xwpairs, decpairs = [], [], [], []
    for k in range(SSD_HEADS // 2):
        g = (2 * k) // heads_per_group
        xp = xs[:, k * LANES:(k + 1) * LANES]
        xhalf = (jnp.where(lane_lo, xp, 0.0).astype(BF16), jnp.where(lane_lo, 0.0, xp).astype(BF16))
        yk = jnp.zeros((Q, LANES), F32)
        for s in range(2):
            hh = 2 * k + s
            segm = acs[:, hh:hh + 1] - acsT[hh:hh + 1, :]
            m = G[g] * jnp.exp(jnp.where(mask, segm, -jnp.inf)) * dtT[hh:hh + 1, :]
            yk = yk + _dot(m.astype(BF16), xhalf[s])

        def pair(a):
            return jnp.where(lane_lo[:a.shape[0]], a[:, 2 * k:2 * k + 1], a[:, 2 * k + 1:2 * k + 2])

        e_p = jnp.exp(pair(acs))
        xw = xp * pair(to_end)
        dec = jnp.exp(pair(acs_last))
        if carry:
            h_prev = hT[k]
            yk = yk + _dot(cm_b[:, g * SSD_STATE:(g + 1) * SSD_STATE], h_prev.astype(BF16)) * e_p
            hT[k] = dec * h_prev + _dot_tn(bm_b[:, g * SSD_STATE:(g + 1) * SSD_STATE], xw.astype(BF16))
        else:
            epairs.append(e_p)
            xwpairs.append(xw)
            decpairs.append(dec)
        ypairs.append(yk)
        yield
    y = jnp.concatenate(ypairs, axis=1) + dsk_ref[...] * xs
    if carry:
        y = y * _silu(z_ref[...])
        y_ref[...] = _rms(y, g_ref[...]).astype(BF16)

        if chunk_pos is None:
            _ssd_final(c, n_chunks, hfin_ref, hT)
    else:
        ypre_ref[...] = y
        eacs_ref[...] = jnp.concatenate(epairs, axis=1)
        xw_ref[...] = jnp.concatenate(xwpairs, axis=1)
        dec_ref[...] = jnp.concatenate(decpairs, axis=1)
        bm_ref[...] = bm
        cm_ref[...] = cm


def _row(a, n):
    return jnp.pad(a.reshape(1, -1).astype(F32), ((0, 0), (0, n - a.size)))


def _ssd_prompt(xbc, z, dt, cw, cb, dtb, alog, dsk, g):
    nb, L, _ = xbc.shape
    Q = SSD_CHUNK if L % SSD_CHUNK == 0 else L
    nc = L // Q

    def tok(width):
        return pl.BlockSpec((None, Q, width), lambda b, c: (b, c, 0))

    def const(a):
        return pl.BlockSpec(a.shape, lambda b, c: (0,) * a.ndim)

    consts = (cw, cb, dtb, alog, dsk, g)
    return pl.pallas_call(
        functools.partial(_ssd_body, Q=Q, seg=Q, carry=True),
        grid=(nb, nc),
        in_specs=[tok(CONV_DIM), tok(SSD_INNER), tok(LANES)] + [const(a) for a in consts],
        out_specs=[tok(SSD_INNER), pl.BlockSpec((None, SSD_INNER, SSD_STATE), lambda b, c: (b, 0, 0))],
        out_shape=[SDS((nb, L, SSD_INNER), BF16), SDS((nb, SSD_INNER, SSD_STATE), F32)],
        scratch_shapes=[pltpu.VMEM((Q + 8, CONV_DIM), F32), pltpu.VMEM((SSD_HEADS // 2, SSD_STATE, LANES), F32)],
        compiler_params=_params(("arbitrary", "arbitrary")),
        name="ssd_prompt",
    )(xbc, z, dt, *consts)


def _ssd_sample(xsh, dt, cw, cb, dtb, alog, dsk, *, seg):
    _, T, _ = xsh.shape
    consts = (cw, cb, dtb, alog, dsk)

    def full(a):
        return pl.BlockSpec(a.shape, lambda i: (0,) * a.ndim)

    outs = [SDS((T, SSD_INNER), F32)] * 4 + [SDS((T, SSD_GROUPS * SSD_STATE), F32)] * 2
    return pl.pallas_call(
        functools.partial(_ssd_body, Q=T, seg=seg, carry=False),
        grid=(1,),
        in_specs=[full(xsh), full(dt)] + [full(a) for a in consts],
        out_specs=[full(o) for o in outs],
        out_shape=outs,
        compiler_params=_params(("arbitrary",)),
        name="ssd_sample",
    )(xsh, dt, *consts)


def _sstate_body(cm_ref, bm_ref, ypre_ref, eacs_ref, xw_ref, dec_ref, z_ref, s0_ref, g_ref, y_ref, sn_ref):
    s0 = s0_ref[...]
    s0b = s0.astype(BF16)
    cm = cm_ref[...].astype(BF16)
    bm = bm_ref[...].astype(BF16)
    rows = SSD_INNER // SSD_GROUPS
    yo = jnp.concatenate(
        [jnp.einsum("btn,bqn->btq", cm[:, :, g * SSD_STATE:(g + 1) * SSD_STATE],
                    s0b[:, g * rows:(g + 1) * rows, :], preferred_element_type=F32)
         for g in range(SSD_GROUPS)], axis=-1)
    y = (ypre_ref[...] + yo * eacs_ref[...]) * _silu(z_ref[...])
    y_ref[...] = _rms(y, g_ref[...]).astype(BF16)
    dec = dec_ref[...]
    hi = dec.astype(BF16)
    lo = (dec - hi.astype(F32)).astype(BF16)
    sel = jnp.where(lax.broadcasted_iota(jnp.int32, (dec.shape[0], dec.shape[1], SSD_STATE), 1) == 0,
                    1.0, 0.0).astype(BF16)
    dmat = (jnp.einsum("bjq,bjn->bqn", hi, sel, preferred_element_type=F32)
            + jnp.einsum("bjq,bjn->bqn", lo, sel, preferred_element_type=F32))
    xw = xw_ref[...].astype(BF16)
    upd = jnp.concatenate(
        [jnp.einsum("bjq,bjn->bqn", xw[:, :, g * rows:(g + 1) * rows],
                    bm[:, :, g * SSD_STATE:(g + 1) * SSD_STATE], preferred_element_type=F32)
         for g in range(SSD_GROUPS)], axis=1)
    sn_ref[...] = dmat * s0 + upd


def _sstate(cm, bm, ypre, eacs, xw, dec, z, s0, g, *, bs):
    nseq, t, _ = cm.shape
    bs = min(bs, nseq)

    def blk(a):
        return pl.BlockSpec((bs,) + a.shape[1:], lambda i: (i, 0, 0))

    ins = (cm, bm, ypre, eacs, xw, dec, z, s0)
    return pl.pallas_call(
        _sstate_body,
        grid=(nseq // bs,),
        in_specs=[blk(a) for a in ins] + [pl.BlockSpec(g.shape, lambda i: (0, 0))],
        out_specs=[blk(ypre), blk(s0)],
        out_shape=[SDS(ypre.shape, BF16), SDS(s0.shape, F32)],
        compiler_params=_params(("arbitrary",)),
        name="sstate",
    )(*ins, g)


def _attn_body(qt_ref, qn_ref, kp_ref, vt_ref, g_ref, o_ref, m_sc, acc_sc,
               s_a, s_b, s_c, smax_a, smax_b, smax_c, *, tk):
    i = pl.program_id(1)
    ncols = MLA_HEADS * TQ
    m_sc[...] = jnp.full(m_sc.shape, -jnp.inf, F32)
    acc_sc[...] = jnp.zeros(acc_sc.shape, F32)
    ct = 2 * TQ
    tiles = [slice(p * ct, (p + 1) * ct) for p in range(PAIRS)]

    def scores(j, s_ref, smax_ref, q_ref=qt_ref):
        k0 = pl.multiple_of(j * tk, tk)
        for p, cs in enumerate(tiles):
            s = _dot(kp_ref[p, pl.ds(k0, tk), :], q_ref[p])
            s_ref[:, cs] = s
            smax_ref[:, cs] = jnp.max(s, axis=0, keepdims=True)

    def softmax_pv(j, s_ref, smax_ref, masked):
        for p, cs in enumerate(tiles):
            s = s_ref[:, cs]
            if masked:
                key = j * tk + lax.broadcasted_iota(jnp.int32, (tk, ct), 0)
                tok = i * TQ + (lax.broadcasted_iota(jnp.int32, (tk, ct), 1) & (TQ - 1))
                s = jnp.where(key <= tok, s, -jnp.inf)
                smax = jnp.max(s, axis=0, keepdims=True)
            else:
                smax = smax_ref[:, cs]
            m_prev = m_sc[:, cs]
            m_new = jnp.maximum(m_prev, smax)
            alpha = jnp.exp2(m_prev - m_new)
            e = jnp.exp2(s - m_new)
            acc_sc[p] = alpha * acc_sc[p] + _dot(vt_ref[j, p], e.astype(BF16))
            m_sc[:, cs] = m_new

    nfull = (i * TQ) // tk

    @pl.when(i == 0)
    def _():
        scores(0, s_c, smax_c)

    def prefetch():
        scores(0, s_c, smax_c, qn_ref)

    @pl.when(nfull == 0)
    def _():
        softmax_pv(0, s_c, smax_c, True)
        prefetch()

    @pl.when(nfull >= 1)
    def _():
        scores(1, s_b, smax_b)
        softmax_pv(0, s_c, smax_c, False)

    def pair(p, carry):
        j = 2 * p + 1
        scores(j + 1, s_a, smax_a)
        softmax_pv(j, s_b, smax_b, False)
        scores(j + 2, s_b, smax_b)
        softmax_pv(j + 1, s_a, smax_a, False)
        return carry

    lax.fori_loop(0, jnp.maximum(nfull - 1, 0) // 2, pair, 0)
    odd = (nfull % 2) == 1

    @pl.when(odd)
    def _():
        prefetch()
        softmax_pv(nfull, s_b, smax_b, True)

    @pl.when(jnp.logical_and(jnp.logical_not(odd), nfull >= 2))
    def _():
        scores(nfull, s_a, smax_a)
        softmax_pv(nfull - 1, s_b, smax_b, False)
        prefetch()
        softmax_pv(nfull, s_a, smax_a, True)

    ys = []
    for hh in range(MLA_HEADS):
        p, s = divmod(hh, 2)
        cols = slice(s * TQ, (s + 1) * TQ)
        denom = acc_sc[p, 2 * V_DIM:2 * V_DIM + 1, cols]
        ys.append(acc_sc[p, s * V_DIM:(s + 1) * V_DIM, cols] * (1.0 / denom))
    yt = jnp.concatenate(ys, axis=0)
    yt = yt * lax.rsqrt(jnp.mean(yt * yt, axis=0, keepdims=True) + EPS)
    o_ref[...] = (yt.T * g_ref[...]).astype(BF16)


def _attn_prompt(qt, kp, vt, g, *, tk):
    nb, nq = qt.shape[:2]
    L = nq * TQ
    ncols = MLA_HEADS * TQ
    assert TQ & (TQ - 1) == 0 and L % tk == 0 and tk % TQ == 0

    def q_spec(index):
        return pl.BlockSpec((None, None, PAIRS, PAIR_K, 2 * TQ), lambda b, i: (b, index(i), 0, 0, 0))

    return pl.pallas_call(
        functools.partial(_attn_body, tk=tk),
        grid=(nb, nq),
        in_specs=[q_spec(lambda i: i), q_spec(lambda i: jnp.minimum(i + 1, nq - 1)),
                  pl.BlockSpec((None, PAIRS, L, PAIR_K), lambda b, i: (b, 0, 0, 0), pipeline_mode=pl.Buffered(1)),
                  pl.BlockSpec((None, L // tk, PAIRS, PAIR_V, tk), lambda b, i: (b, 0, 0, 0, 0),
                               pipeline_mode=pl.Buffered(1)),
                  pl.BlockSpec(g.shape, lambda b, i: (0, 0))],
        out_specs=pl.BlockSpec((None, TQ, MLA_INNER), lambda b, i: (b, i, 0)),
        out_shape=SDS((nb, L, MLA_INNER), BF16),
        scratch_shapes=[pltpu.VMEM((1, ncols), F32),
                        pltpu.VMEM((PAIRS, PAIR_V, 2 * TQ), F32),
                        pltpu.VMEM((tk, ncols), F32), pltpu.VMEM((tk, ncols), F32), pltpu.VMEM((tk, ncols), F32),
                        pltpu.VMEM((1, ncols), F32), pltpu.VMEM((1, ncols), F32), pltpu.VMEM((1, ncols), F32)],
        compiler_params=_params(("arbitrary", "arbitrary")),
        name="attn_prompt",
    )(qt, qt, kp, vt, g)


def _sattn_body(pt_ref, qa_ref, qr_ref, kn_ref, rn_ref, ckv_hbm, krt_hbm, o_ref, kbuf, rbuf, sem, *, npages, t_new):
    slot = _sattn_fetch(pt_ref, ckv_hbm, krt_hbm, kbuf, rbuf, sem, npages=npages)
    for _ in _sattn_steps(slot, qa_ref, qr_ref, kn_ref, rn_ref, o_ref, kbuf, rbuf, npages=npages, t_new=t_new,
                          split=SPLIT):
        pass


def _sattn_fetch(pt_ref, ckv_hbm, krt_hbm, kbuf, rbuf, sem, *, npages):
    b = pl.program_id(0)
    nseq = pl.num_programs(0)

    def copies(seq_page, slot, p):
        off = p * PAGE_SIZE
        return (pltpu.make_async_copy(ckv_hbm.at[seq_page], kbuf.at[slot, p], sem.at[0, slot]),
                pltpu.make_async_copy(krt_hbm.at[seq_page], rbuf.at[slot, :, pl.ds(off, PAGE_SIZE)],
                                      sem.at[1, slot]))

    def start_fetch(seq, slot):
        for p in range(npages):
            for cp in copies(pt_ref[seq * npages + p], slot, p):
                cp.start()

    def wait_fetch(slot):
        pltpu.make_async_copy(ckv_hbm.at[pl.ds(0, npages)], kbuf.at[slot], sem.at[0, slot]).wait()
        pltpu.make_async_copy(rbuf.at[slot], rbuf.at[slot], sem.at[1, slot]).wait()

    @pl.when(b == 0)
    def _():
        start_fetch(0, 0)

    @pl.when(b + 1 < nseq)
    def _():
        start_fetch(b + 1, (b + 1) % 2)

    slot = b % 2
    wait_fetch(slot)
    return slot


def _sattn_steps(slot, qa_ref, qr_ref, kn_ref, rn_ref, o_ref, kbuf, rbuf, *, npages, t_new, split):
    split = math.gcd(split, npages)
    grp = math.gcd(SPLIT, split)
    pg = npages // split
    part = pg * PAGE_SIZE
    q = qa_ref[...]
    qr = qr_ref[...]
    kn = kn_ref[...]
    rn = rn_ref[...]
    kps, s_p = [], []
    for h0 in range(0, split, grp):
        hs = range(h0, h0 + grp)
        kps += [kbuf[slot, pl.ds(h * pg, pg)].reshape(part, KV_RANK).astype(BF16) for h in hs]
        s_main = [_dot_nt(q, kps[h]) for h in hs]
        s_rope = [_dot(qr, rbuf[slot, :, h * part:(h + 1) * part].astype(BF16)) for h in hs]
        s_p += [a + b for a, b in zip(s_main, s_rope)]
        yield
    s_n = _dot_nt(q, kn) + _dot_nt(qr, rn)
    rows = t_new * MLA_HEADS
    tok = lax.broadcasted_iota(jnp.int32, (rows, t_new), 0) // MLA_HEADS
    col = lax.broadcasted_iota(jnp.int32, (rows, t_new), 1)
    s_n = jnp.where(col <= tok, s_n, -jnp.inf)
    m = jnp.max(s_n, axis=-1, keepdims=True)
    for s in s_p:
        m = jnp.maximum(m, jnp.max(s, axis=-1, keepdims=True))
    pn = jnp.exp(s_n - m)
    pp = [jnp.exp(s - m) for s in s_p]
    l = jnp.sum(pn, axis=-1, keepdims=True)
    for p in pp:
        l = l + jnp.sum(p, axis=-1, keepdims=True)
    inv = 1.0 / l
    o = _dot(pn.astype(BF16), kn) * inv
    yield
    for h0 in range(0, split, grp):
        parts = [_dot(pp[h].astype(BF16), kps[h]) for h in range(h0, h0 + grp)]
        for part_o in parts:
            o = o + part_o * inv
        yield
    o_ref[...] = o


def _sattn(page_table, qa, qr, kn, rn, cache_kv, cache_krt):
    nseq, npages = page_table.shape
    t_new = kn.shape[1]
    rows = t_new * MLA_HEADS
    grid_spec = pltpu.PrefetchScalarGridSpec(
        num_scalar_prefetch=1,
        grid=(nseq,),
        in_specs=[pl.BlockSpec((rows, KV_RANK), lambda b, pt: (b, 0)),
                  pl.BlockSpec((rows, QK_ROPE), lambda b, pt: (b, 0)),
                  pl.BlockSpec((None, t_new, KV_RANK), lambda b, pt: (b, 0, 0)),
                  pl.BlockSpec((None, t_new, QK_ROPE), lambda b, pt: (b, 0, 0)),
                  pl.BlockSpec(memory_space=pl.ANY),
                  pl.BlockSpec(memory_space=pl.ANY)],
        out_specs=pl.BlockSpec((rows, KV_RANK), lambda b, pt: (b, 0)),
        scratch_shapes=[pltpu.VMEM((2, npages, PAGE_SIZE, KV_RANK), F32),
                        pltpu.VMEM((2, QK_ROPE, npages * PAGE_SIZE), F32),
                        pltpu.SemaphoreType.DMA((2, 2))],
    )
    return pl.pallas_call(
        functools.partial(_sattn_body, npages=npages, t_new=t_new),
        grid_spec=grid_spec,
        out_shape=SDS((nseq * rows, KV_RANK), F32),
        compiler_params=_params(("arbitrary",)),
        name="attn_sample",
    )(page_table.reshape(-1), qa, qr, kn, rn, cache_kv, cache_krt)


def _ssd_sattn_body(pt_ref, qa_ref, qr_ref, kn_ref, rn_ref, ckv_hbm, krt_hbm,
                    xin_ref, z_ref, dt_ref, cw_ref, cb_ref, dtb_ref, alog_ref, dsk_ref, g_ref,
                    o_ref, y_ref, hfin_ref, kbuf, rbuf, sem, xbuf, hT, *, npages, t_new, Q, n_chunks):
    c = pl.program_id(0) % n_chunks
    _ssd_reset(c, xbuf, hT)
    slot = _sattn_fetch(pt_ref, ckv_hbm, krt_hbm, kbuf, rbuf, sem, npages=npages)
    ssd = _ssd_steps(xin_ref, z_ref, dt_ref, cw_ref, cb_ref, dtb_ref, alog_ref, dsk_ref, g_ref, y_ref, hfin_ref,
                     xbuf, hT, Q=Q, seg=Q, carry=True, chunk_pos=(c, n_chunks))
    att = _sattn_steps(slot, qa_ref, qr_ref, kn_ref, rn_ref, o_ref, kbuf, rbuf, npages=npages, t_new=t_new,
                       split=FUSED_SPLIT)
    done = object()
    live = [att, ssd]
    while live:
        live = [gen for gen in live if next(gen, done) is not done]
    _ssd_final(c, n_chunks, hfin_ref, hT)


def _ssd_sattn(page_table, qa, qr, kn, rn, cache_kv, cache_krt, xbc, z, dt, cw, cb, dtb, alog, dsk, g):
    nseq, npages = page_table.shape
    t_new = kn.shape[1]
    rows = t_new * MLA_HEADS
    nb, L, _ = xbc.shape
    Q = SSD_CHUNK if L % SSD_CHUNK == 0 else L
    nc = L // Q
    assert nseq == nb * nc
    consts = (cw, cb, dtb, alog, dsk, g)

    def tok(width):
        return pl.BlockSpec((None, Q, width), lambda s, pt: (s // nc, s % nc, 0))

    def const(a):
        return pl.BlockSpec(a.shape, lambda s, pt: (0,) * a.ndim)

    grid_spec = pltpu.PrefetchScalarGridSpec(
        num_scalar_prefetch=1,
        grid=(nseq,),
        in_specs=[pl.BlockSpec((rows, KV_RANK), lambda s, pt: (s, 0)),
                  pl.BlockSpec((rows, QK_ROPE), lambda s, pt: (s, 0)),
                  pl.BlockSpec((None, t_new, KV_RANK), lambda s, pt: (s, 0, 0)),
                  pl.BlockSpec((None, t_new, QK_ROPE), lambda s, pt: (s, 0, 0)),
                  pl.BlockSpec(memory_space=pl.ANY),
                  pl.BlockSpec(memory_space=pl.ANY),
                  tok(CONV_DIM), tok(SSD_INNER), tok(LANES)] + [const(a) for a in consts],
        out_specs=[pl.BlockSpec((rows, KV_RANK), lambda s, pt: (s, 0)),
                   tok(SSD_INNER),
                   pl.BlockSpec((None, SSD_INNER, SSD_STATE), lambda s, pt: (s // nc, 0, 0))],
        scratch_shapes=[pltpu.VMEM((2, npages, PAGE_SIZE, KV_RANK), F32),
                        pltpu.VMEM((2, QK_ROPE, npages * PAGE_SIZE), F32),
                        pltpu.SemaphoreType.DMA((2, 2)),
                        pltpu.VMEM((Q + 8, CONV_DIM), F32),
                        pltpu.VMEM((SSD_HEADS // 2, SSD_STATE, LANES), F32)],
    )
    return pl.pallas_call(
        functools.partial(_ssd_sattn_body, npages=npages, t_new=t_new, Q=Q, n_chunks=nc),
        grid_spec=grid_spec,
        out_shape=[SDS((nseq * rows, KV_RANK), F32), SDS((nb, L, SSD_INNER), BF16),
                   SDS((nb, SSD_INNER, SSD_STATE), F32)],
        compiler_params=_params(("arbitrary",)),
        name="ssd_prompt_attn_sample",
    )(page_table.reshape(-1), qa, qr, kn, rn, cache_kv, cache_krt, xbc, z, dt, *consts)


def _apost_body(o_ref, wuv_ref, g_ref, y_ref):
    tm = o_ref.shape[0]
    o = jnp.zeros((tm, MLA_INNER), F32)
    for hh in range(MLA_HEADS):
        o = o + _dot(o_ref[:, hh * KV_RANK:(hh + 1) * KV_RANK].astype(BF16), wuv_ref[hh])
    y_ref[...] = _rms(o, g_ref[...]).astype(BF16)


def _apost(o_lat, wuv_pad, g):
    T = o_lat.shape[0]
    return pl.pallas_call(
        _apost_body,
        grid=(1,),
        in_specs=[pl.BlockSpec(o_lat.shape, lambda i: (0, 0)),
                  pl.BlockSpec(wuv_pad.shape, lambda i: (0, 0, 0)),
                  pl.BlockSpec(g.shape, lambda i: (0, 0))],
        out_specs=pl.BlockSpec((T, MLA_INNER), lambda i: (0, 0)),
        out_shape=SDS((T, MLA_INNER), BF16),
        compiler_params=_params(("arbitrary",)),
        name="attn_post",
    )(o_lat, wuv_pad, g)


def _mlp_body(x_ref, ys_ref, ya_ref, g1_ref, sh2_ref, sc2_ref, g2_ref, shf_ref, scf_ref,
              wout_ref, gmlp_ref, wup_ref, wdn_ref, gfin_ref, o_ref, *, final, tf, rep):
    yy = jnp.concatenate([ys_ref[...], ya_ref[...]], axis=-1)
    x1 = x_ref[...] + _mod_rows(g1_ref, rep) * _dot(yy, wout_ref[...])
    h2 = (_rms(x1, gmlp_ref[...]) * (1.0 + _mod_rows(sc2_ref, rep)) + _mod_rows(sh2_ref, rep)).astype(BF16)
    acc = jnp.zeros(x1.shape, F32)
    for c in range(D_FF // tf):
        u = jnp.maximum(_dot(h2, wup_ref[:, c * tf:(c + 1) * tf]), 0.0)
        acc = acc + _dot((u * u).astype(BF16), wdn_ref[c * tf:(c + 1) * tf, :])
    x2 = x1 + _mod_rows(g2_ref, rep) * acc
    if final:
        x2 = _rms(x2, gfin_ref[...]) * (1.0 + _mod_rows(scf_ref, rep)) + _mod_rows(shf_ref, rep)
    o_ref[...] = x2


def _mlp(x3, ys, ya, mod3, wout, gmlp, wup, wdn, gfin, *, final, tm, tf):
    nb, L, d = x3.shape
    tm = min(tm, L)
    rep, mod_spec = _mod_spec(mod3, L, tm, d)

    def tok(width):
        return pl.BlockSpec((None, tm, width), lambda b, i: (b, i, 0))

    def const(a):
        return pl.BlockSpec(a.shape, lambda b, i: (0,) * a.ndim, pipeline_mode=pl.Buffered(1))

    return pl.pallas_call(
        functools.partial(_mlp_body, final=final, tf=tf, rep=rep),
        grid=(nb, L // tm),
        in_specs=[tok(d), tok(SSD_INNER), tok(MLA_INNER)] + [mod_spec(k) for k in (2, 3, 4, 5, 6, 7)]
                 + [const(wout), const(gmlp), const(wup), const(wdn), const(gfin)],
        out_specs=tok(d),
        out_shape=SDS((nb, L, d), F32),
        compiler_params=_params(("arbitrary", "arbitrary")),
        name="mlp",
    )(x3, ys, ya, mod3, mod3, mod3, mod3, mod3, mod3, wout, gmlp, wup, wdn, gfin)


def _rope_tables(pos):
    inv = 1.0 / (ROPE_THETA ** (np.arange(0, QK_ROPE, 2, dtype=np.float64) / QK_ROPE))
    ang = pos.astype(np.float64)[:, None] * inv[None, :]
    cos, sin = np.cos(ang).astype(np.float32), np.sin(ang).astype(np.float32)
    c32 = np.concatenate([cos, cos], axis=-1)
    s32 = np.concatenate([-sin, sin], axis=-1)
    tab1 = np.concatenate([c32, s32], axis=-1)
    tabq = np.concatenate([np.tile(c32, (1, MLA_HEADS)), np.tile(s32, (1, MLA_HEADS))], axis=-1)
    return tab1, tabq


def _swap_halves(w):
    half = w.shape[-1] // 2
    return jnp.concatenate([w[..., half:], w[..., :half]], axis=-1)


def kernel(x_prompt, x_sample, cache_kv_latent, cache_k_rope, state_conv, state_ssm, page_table,
           c_prompt, c_sample, w_ada, b_ada, norm_mix_g, w_in, conv_w, conv_b, dt_bias, a_log,
           d_skip, norm_ssd_g, q_norm_g, kv_norm_g, w_uq, w_uk, w_uv, norm_attn_g, w_out,
           norm_mlp_g, w_up, w_down, w_ada_final, b_ada_final, norm_final_g):
    depth = w_in.shape[0]
    b_p, seq, d = x_prompt.shape
    n_seq, t_new, _ = x_sample.shape
    n_tok_s = n_seq * t_new
    past_len = page_table.shape[1] * PAGE_SIZE

    c_all = jnp.concatenate([c_prompt, c_sample], axis=0)
    ada_fin = _ada(c_all, w_ada_final, b_ada_final)
    tab1_p, _ = _rope_tables(np.arange(seq))
    tab1_s, tabq_s = _rope_tables(past_len + np.arange(t_new))
    tab1_pt = jnp.asarray(np.ascontiguousarray(tab1_p.T))
    tab1_p = jnp.asarray(tab1_p)
    tabq_s = jnp.asarray(np.tile(tabq_s, (n_seq, 1)))
    tab1_s = jnp.asarray(np.tile(tab1_s, (n_seq, 1)))

    xp = x_prompt
    xs = x_sample.reshape(1, n_tok_s, d)
    outs_p, outs_s = [], []
    for l in range(depth):
        final = l == depth - 1
        wi = w_in[l]
        c1 = SSD_INNER
        c2 = c1 + CONV_DIM
        c3 = c2 + SSD_HEADS
        c4 = c3 + Q_RANK
        c5 = c4 + KV_RANK
        w_kr = wi[:, c5:]
        win = jnp.concatenate(
            [wi[:, :c2], wi[:, c3:c5], wi[:, c2:c3], jnp.zeros((d, MISC_KR - SSD_HEADS), F32), w_kr,
             _swap_halves(w_kr), jnp.zeros((d, LANES - MISC_KRSW - QK_ROPE), F32)], axis=1).astype(BF16)
        wq_h = w_uq[l].reshape(Q_RANK, MLA_HEADS, QK_NOPE + QK_ROPE)
        w_rope = wq_h[:, :, QK_NOPE:]
        wfold = _fold(jnp.transpose(wq_h, (1, 0, 2)), jnp.transpose(w_uk[l], (1, 0, 2)))
        wq = jnp.concatenate([wfold, w_rope.reshape(Q_RANK, -1).astype(BF16),
                              _swap_halves(w_rope).reshape(Q_RANK, -1).astype(BF16)], axis=1)
        wuv_pad = jnp.zeros((MLA_HEADS, KV_RANK, MLA_HEADS, V_DIM), F32)
        wuv_pad = wuv_pad.at[jnp.arange(MLA_HEADS), :, jnp.arange(MLA_HEADS), :].set(
            jnp.transpose(w_uv[l], (1, 0, 2)))
        wuv_pad = wuv_pad.reshape(MLA_HEADS, KV_RANK, MLA_INNER).astype(BF16)
        wq_p = jnp.concatenate([wq_h[:, :, :QK_NOPE].reshape(Q_RANK, -1), w_rope.reshape(Q_RANK, -1),
                                _swap_halves(w_rope).reshape(Q_RANK, -1)], axis=1).T.astype(BF16)
        wk = w_uk[l].reshape(KV_RANK, MLA_HEADS * QK_NOPE).astype(BF16)
        wvt = jnp.transpose(w_uv[l], (1, 2, 0)).reshape(PAIRS, 2 * V_DIM, KV_RANK).astype(BF16)
        wout = w_out[l].astype(BF16)
        wup = w_up[l].astype(BF16)
        wdn = w_down[l].astype(BF16)
        gmix = norm_mix_g[l].reshape(1, d)
        gmlp = norm_mlp_g[l].reshape(1, d)
        gfin = norm_final_g.reshape(1, d)
        qg = q_norm_g[l].reshape(1, Q_RANK)
        kvg = kv_norm_g[l].reshape(1, KV_RANK)
        gssd = norm_ssd_g[l].reshape(1, SSD_INNER)
        gattn = norm_attn_g[l].reshape(1, MLA_INNER)
        cw = conv_w[l]
        cb = conv_b[l].reshape(1, CONV_DIM)
        dtb = _row(dt_bias[l], LANES)
        alog = _row(a_log[l], LANES)
        dsk = jnp.repeat(d_skip[l].astype(F32), SSD_HEAD_DIM).reshape(1, SSD_INNER)

        ada = _ada(c_all, w_ada[l], b_ada[l])
        mod = jnp.concatenate([ada, ada_fin], axis=1)
        mod_p = mod[:b_p].reshape(b_p, 1, 8 * d)
        mod_s = mod[b_p:].reshape(1, n_seq, 8 * d)

        def seqs(a):
            return a.reshape(n_seq, t_new, a.shape[-1])

        tk = min(512, seq)
        z_p, xbc_p, dtr_p, ckv_p, kr_p, kp, vt, qt = _inproj(
            xp, mod_p, gmix, win, qg, kvg, wq_p, tab1_pt, tab1_p, q_transposed=True, tm=512, tk=tk,
            wk=wk, wvt=wvt)
        z_s, xbc_s, dtr_s, ckv_s, kr_s, kc, krb, qa, qr = _inproj(
            xs, mod_s, gmix, win, qg, kvg, wq, tabq_s, tab1_s, q_transposed=False, tm=256)
        xpad = jnp.concatenate([state_conv[l], xbc_s.reshape(n_seq, t_new, CONV_DIM)], axis=1)
        xsh = jnp.stack([xpad[:, k:k + t_new].reshape(n_tok_s, CONV_DIM) for k in range(CONV_WIDTH)])
        ypre, eacs, xw, dec, bm, cm = _ssd_sample(xsh, dtr_s[0], cw, cb, dtb, alog, dsk, seg=t_new)
        y_ssd_s, s_new = _sstate(seqs(cm), seqs(bm), seqs(ypre), seqs(eacs), seqs(xw), seqs(dec), seqs(z_s[0]),
                                 state_ssm[l].reshape(n_seq, SSD_INNER, SSD_STATE), gssd, bs=8)

        sattn_args = (page_table, qa.reshape(n_tok_s * MLA_HEADS, KV_RANK),
                      qr.reshape(n_tok_s * MLA_HEADS, QK_ROPE), seqs(kc[0]), seqs(krb[0]),
                      cache_kv_latent[l], jnp.swapaxes(cache_k_rope[l], -1, -2))
        ssd_args = (xbc_p, z_p, dtr_p, cw, cb, dtb, alog, dsk, gssd)
        chunk = SSD_CHUNK if seq % SSD_CHUNK == 0 else seq
        if n_seq == b_p * (seq // chunk):
            o_lat, y_ssd_p, hfin = _ssd_sattn(*sattn_args, *ssd_args)
        else:
            y_ssd_p, hfin = _ssd_prompt(*ssd_args)
            o_lat = _sattn(*sattn_args)

        y_attn_p = _attn_prompt(qt, kp, vt, gattn, tk=tk)
        xp = _mlp(xp, y_ssd_p, y_attn_p, mod_p, wout, gmlp, wup, wdn, gfin, final=final, tm=512, tf=2048)
        tail = min(seq, CONV_WIDTH - 1)
        conv_tail = jnp.concatenate([jnp.zeros((b_p, CONV_WIDTH - 1 - tail, CONV_DIM), F32),
                                     xbc_p[:, seq - tail:]], axis=1)
        outs_p.append((ckv_p, kr_p, conv_tail, hfin.reshape(b_p, SSD_HEADS, SSD_HEAD_DIM, SSD_STATE)))

        y_attn_s = _apost(o_lat.reshape(n_tok_s, MLA_HEADS * KV_RANK), wuv_pad, gattn)
        xs = _mlp(xs, y_ssd_s.reshape(1, n_tok_s, SSD_INNER), y_attn_s.reshape(1, n_tok_s, MLA_INNER), mod_s,
                  wout, gmlp, wup, wdn, gfin, final=final, tm=n_tok_s, tf=2048)
        outs_s.append((seqs(ckv_s[0]), seqs(kr_s[0]), xpad[:, t_new:],
                       s_new.reshape(n_seq, SSD_HEADS, SSD_HEAD_DIM, SSD_STATE)))

    def stack(outs, k):
        return jnp.stack([o[k] for o in outs])

    return (xp, xs.reshape(n_seq, t_new, d),
            stack(outs_p, 0), stack(outs_p, 1), stack(outs_p, 2), stack(outs_p, 3),
            stack(outs_s, 0), stack(outs_s, 1), stack(outs_s, 2), stack(outs_s, 3))
```

```python
import functools
import math

import jax
import jax.numpy as jnp
import numpy as np
from jax import lax
from jax.experimental import pallas as pl
from jax.experimental.pallas import tpu as pltpu

F32 = jnp.float32
BF16 = jnp.bfloat16
SDS = jax.ShapeDtypeStruct

D_MODEL = 1024
SSD_HEADS = 8
SSD_HEAD_DIM = 64
SSD_INNER = SSD_HEADS * SSD_HEAD_DIM
SSD_GROUPS = 2
SSD_STATE = 128
CONV_WIDTH = 4
SSD_CHUNK = 128
CONV_DIM = SSD_INNER + 2 * SSD_GROUPS * SSD_STATE
MLA_HEADS = 8
QK_NOPE = 64
QK_ROPE = 32
V_DIM = 64
KV_RANK = 256
Q_RANK = 384
MLA_INNER = MLA_HEADS * V_DIM
ROPE_THETA = 10000.0
ATTN_SCALE = 1.0 / math.sqrt(QK_NOPE + QK_ROPE)
LOG2E = math.log2(math.e)
PAGE_SIZE = 128
D_FF = 4 * D_MODEL
EPS = 1e-6

LANES = 128
TQ = 128
PAIRS = MLA_HEADS // 2
PAIR_K = 256
PAIR_V = 2 * V_DIM + 16
SPLIT = 2
FUSED_SPLIT = 32
CONV_LANES = 128
C_Z = 0
C_XBC = C_Z + SSD_INNER
C_QLAT = C_XBC + CONV_DIM
C_KVLAT = C_QLAT + Q_RANK
C_MISC = C_KVLAT + KV_RANK
W_IN_COLS = C_MISC + LANES
MISC_KR = 32
MISC_KRSW = 64
C_QROPE = MLA_HEADS * KV_RANK
C_QROPE_SW = C_QROPE + MLA_HEADS * QK_ROPE
WQ_COLS = C_QROPE_SW + MLA_HEADS * QK_ROPE

VMEM_LIMIT = 52 * 1024 * 1024


def _dot(a, b):
    return jnp.dot(a, b, preferred_element_type=F32)


def _dot_nt(a, b):
    return lax.dot_general(a, b, (((1,), (1,)), ((), ())), preferred_element_type=F32)


def _dot_tn(a, b):
    return lax.dot_general(a, b, (((0,), (0,)), ((), ())), preferred_element_type=F32)


def _silu(x):
    return x * jax.nn.sigmoid(x)


def _rms(x, g):
    return x * lax.rsqrt(jnp.mean(x * x, axis=-1, keepdims=True) + EPS) * g


def _split3_dot(mask_bf16, v):
    v1 = v.astype(BF16)
    r1 = v - v1.astype(F32)
    v2 = r1.astype(BF16)
    v3 = (r1 - v2.astype(F32)).astype(BF16)
    return _dot(mask_bf16, v1) + _dot(mask_bf16, v2) + _dot(mask_bf16, v3)


def _mod_spec(mod3, L, tm, d):
    rows = mod3.shape[1]
    if rows == 1:
        return 1, lambda k: pl.BlockSpec((None, 1, d), lambda b, i: (b, 0, k))
    rep = L // rows
    return rep, lambda k: pl.BlockSpec((None, tm // rep, d), lambda b, i: (b, i, k))


def _mod_rows(ref, rep):
    m = ref[...]
    if rep == 1:
        return m
    n = m.shape[0] * rep
    sel = (lax.broadcasted_iota(jnp.int32, (n, m.shape[0]), 0) // rep
           == lax.broadcasted_iota(jnp.int32, (n, m.shape[0]), 1))
    sel = jnp.where(sel, 1.0, 0.0).astype(BF16)
    hi = m.astype(BF16)
    lo = (m - hi.astype(F32)).astype(BF16)
    return _dot(sel, hi) + _dot(sel, lo)


def _params(sem, vmem=VMEM_LIMIT):
    return pltpu.CompilerParams(dimension_semantics=sem, vmem_limit_bytes=vmem)


def _ada_body(c_ref, w_ref, b_ref, o_ref):
    s = _silu(c_ref[...]).astype(BF16)
    o_ref[...] = _dot(s, w_ref[...].astype(BF16)) + b_ref[...]


def _ada(c, w, b):
    bsz, d = c.shape
    n = w.shape[1]
    tn = 1024
    return pl.pallas_call(
        _ada_body,
        grid=(n // tn,),
        in_specs=[pl.BlockSpec((bsz, d), lambda j: (0, 0)),
                  pl.BlockSpec((d, tn), lambda j: (0, j)),
                  pl.BlockSpec((1, tn), lambda j: (0, j))],
        out_specs=pl.BlockSpec((bsz, tn), lambda j: (0, j)),
        out_shape=SDS((bsz, n), F32),
        compiler_params=_params(("arbitrary",)),
        name="ada",
    )(c, w, b.reshape(1, n))


def _fold_body(wq_ref, wk_ref, o_ref):
    a = wq_ref[:, 0:QK_NOPE].astype(BF16)
    o_ref[...] = _dot_nt(a, wk_ref[...].astype(BF16)).astype(BF16)


def _fold(wq_h, wk_h):
    return pl.pallas_call(
        _fold_body,
        grid=(MLA_HEADS,),
        in_specs=[pl.BlockSpec((None, Q_RANK, QK_NOPE + QK_ROPE), lambda h: (h, 0, 0)),
                  pl.BlockSpec((None, KV_RANK, QK_NOPE), lambda h: (h, 0, 0))],
        out_specs=pl.BlockSpec((Q_RANK, KV_RANK), lambda h: (0, h)),
        out_shape=SDS((Q_RANK, MLA_HEADS * KV_RANK), BF16),
        compiler_params=_params(("arbitrary",)),
        name="fold",
    )(wq_h, wk_h)


def _inproj_body(*refs, q_transposed, rep):
    n_in = 12 if q_transposed else 10
    x_ref, sh_ref, sc_ref, gmix_ref, win_ref, qg_ref, kvg_ref, wq_ref, tq_ref, tk_ref = refs[:10]
    z_ref, xbc_ref, dt_ref, ckv_ref, kr_ref = refs[n_in:n_in + 5]
    h = _rms(x_ref[...], gmix_ref[...]) * (1.0 + _mod_rows(sc_ref, rep)) + _mod_rows(sh_ref, rep)
    proj = _dot(h.astype(BF16), win_ref[...])
    tm = proj.shape[0]
    z_ref[...] = proj[:, C_Z:C_XBC]
    xbc_ref[...] = proj[:, C_XBC:C_QLAT]
    q_scale = ATTN_SCALE * LOG2E if q_transposed else ATTN_SCALE
    qn = (_rms(proj[:, C_QLAT:C_KVLAT], qg_ref[...]) * q_scale).astype(BF16)
    ckv = _rms(proj[:, C_KVLAT:C_MISC], kvg_ref[...])
    ckv_ref[...] = ckv
    misc = proj[:, C_MISC:W_IN_COLS]
    lane = lax.broadcasted_iota(jnp.int32, misc.shape, 1)
    dt_ref[...] = jnp.where(lane < SSD_HEADS, misc, 0.0)
    tk = tk_ref[...]
    kr = (misc[:, MISC_KR:MISC_KR + QK_ROPE] * tk[:, :QK_ROPE]
          + misc[:, MISC_KRSW:MISC_KRSW + QK_ROPE] * tk[:, QK_ROPE:])
    kr_ref[...] = kr
    nr = MLA_HEADS * QK_ROPE
    if q_transposed:
        wk_ref, wvt_ref = refs[10:12]
        kp_ref, vt_ref, qt_ref = refs[n_in + 5:]
        ckv_b = ckv.astype(BF16)
        k_nope = _dot(ckv_b, wk_ref[...])
        kr_pad = jnp.concatenate([kr, jnp.zeros((tm, PAIR_K - 2 * QK_NOPE - QK_ROPE), F32)], axis=1).astype(BF16)
        pad_rows = PAIR_V - 2 * V_DIM
        ones_row = jnp.where(lax.broadcasted_iota(jnp.int32, (pad_rows, tm), 0) == 0, 1.0, 0.0).astype(BF16)
        for p in range(PAIRS):
            kp_ref[p, :, 0:2 * QK_NOPE] = k_nope[:, p * 2 * QK_NOPE:(p + 1) * 2 * QK_NOPE].astype(BF16)
            kp_ref[p, :, 2 * QK_NOPE:PAIR_K] = kr_pad
            vt_ref[p, 0:2 * V_DIM] = _dot_nt(wvt_ref[p], ckv_b).astype(BF16)
            vt_ref[p, 2 * V_DIM:PAIR_V] = ones_row
        qt = _dot_nt(wq_ref[...], qn)
        n0 = MLA_HEADS * QK_NOPE
        tq = tq_ref[...]
        cos_t = jnp.concatenate([tq[:QK_ROPE]] * MLA_HEADS, axis=0)
        sin_t = jnp.concatenate([tq[QK_ROPE:]] * MLA_HEADS, axis=0)
        rot = (qt[n0:n0 + nr] * cos_t + qt[n0 + nr:n0 + 2 * nr] * sin_t).astype(BF16)
        q_nope = qt[:n0].astype(BF16)
        zero = jnp.zeros((PAIR_K, TQ), BF16)
        for c in range(tm // TQ):
            toks = slice(c * TQ, (c + 1) * TQ)
            for p in range(PAIRS):
                for s in range(2):
                    hh = 2 * p + s
                    cols = slice(s * TQ, (s + 1) * TQ)
                    qt_ref[c, p, :, cols] = zero
                    qt_ref[c, p, s * QK_NOPE:(s + 1) * QK_NOPE, cols] = q_nope[hh * QK_NOPE:(hh + 1) * QK_NOPE, toks]
                    qt_ref[c, p, 2 * QK_NOPE:2 * QK_NOPE + QK_ROPE, cols] = rot[hh * QK_ROPE:(hh + 1) * QK_ROPE, toks]
    else:
        kc_ref, krb_ref, qa_ref, qr_ref = refs[n_in + 5:]
        kc_ref[...] = ckv.astype(BF16)
        krb_ref[...] = kr.astype(BF16)
        q = _dot(qn, wq_ref[...])
        tq = tq_ref[...]
        qa_ref[...] = q[:, :C_QROPE].astype(BF16)
        qr_ref[...] = (q[:, C_QROPE:C_QROPE_SW] * tq[:, :nr] + q[:, C_QROPE_SW:WQ_COLS] * tq[:, nr:]).astype(BF16)


def _inproj(x3, mod3, gmix, win, qg, kvg, wq, tabq, tabk, *, q_transposed, tm, tk=None, wk=None, wvt=None):
    nb, L, d = x3.shape
    tm = min(tm, L)
    nt = L // tm
    rep, mod_spec = _mod_spec(mod3, L, tm, d)

    def tok(width):
        return pl.BlockSpec((None, tm, width), lambda b, i: (b, i, 0))

    def const(a):
        return pl.BlockSpec(a.shape, lambda b, i: (0,) * a.ndim, pipeline_mode=pl.Buffered(1))

    nr = MLA_HEADS * QK_ROPE
    out_specs = [tok(SSD_INNER), tok(CONV_DIM), tok(LANES), tok(KV_RANK), tok(QK_ROPE)]
    out_shape = [SDS((nb, L, SSD_INNER), F32), SDS((nb, L, CONV_DIM), F32), SDS((nb, L, LANES), F32),
                 SDS((nb, L, KV_RANK), F32), SDS((nb, L, QK_ROPE), F32)]
    if q_transposed:
        assert tm % TQ == 0 and tk % tm == 0
        r = tk // tm
        tabq_spec = pl.BlockSpec((2 * QK_ROPE, tm), lambda b, i: (0, i))
        out_specs += [pl.BlockSpec((None, PAIRS, tm, PAIR_K), lambda b, i: (b, 0, i, 0)),
                      pl.BlockSpec((None, None, PAIRS, PAIR_V, tm), lambda b, i: (b, i // r, 0, 0, i % r)),
                      pl.BlockSpec((None, tm // TQ, PAIRS, PAIR_K, 2 * TQ), lambda b, i: (b, i, 0, 0, 0))]
        out_shape += [SDS((nb, PAIRS, L, PAIR_K), BF16), SDS((nb, L // tk, PAIRS, PAIR_V, tk), BF16),
                      SDS((nb, L // TQ, PAIRS, PAIR_K, 2 * TQ), BF16)]
        extra = [wk, wvt]
    else:
        extra = []
        tabq_spec = pl.BlockSpec((tm, 2 * nr), lambda b, i: (i, 0))
        out_specs += [tok(KV_RANK), tok(QK_ROPE), tok(C_QROPE), tok(nr)]
        out_shape += [SDS((nb, L, KV_RANK), BF16), SDS((nb, L, QK_ROPE), BF16),
                      SDS((nb, L, C_QROPE), BF16), SDS((nb, L, nr), BF16)]
    return pl.pallas_call(
        functools.partial(_inproj_body, q_transposed=q_transposed, rep=rep),
        grid=(nb, nt),
        in_specs=[tok(d), mod_spec(0), mod_spec(1), const(gmix), const(win), const(qg), const(kvg), const(wq),
                  tabq_spec, pl.BlockSpec((tm, 2 * QK_ROPE), lambda b, i: (i, 0))] + [const(a) for a in extra],
        out_specs=out_specs,
        out_shape=out_shape,
        compiler_params=_params(("arbitrary", "arbitrary")),
        name="inproj_prompt" if q_transposed else "inproj_sample",
    )(x3, mod3, mod3, gmix, win, qg, kvg, wq, tabq, tabk, *extra)


def _ssd_reset(c, xbuf, hT):
    @pl.when(c == 0)
    def _():
        xbuf[0:8, :] = jnp.zeros((8, CONV_DIM), F32)
        hT[...] = jnp.zeros(hT.shape, F32)


def _ssd_final(c, n_chunks, hfin_ref, hT):
    @pl.when(c == n_chunks - 1)
    def _():
        for k in range(SSD_HEADS // 2):
            hfin_ref[k * LANES:(k + 1) * LANES, :] = hT[k].T


def _ssd_body(*refs, **kw):
    for _ in _ssd_steps(*refs, **kw):
        pass


def _ssd_steps(*refs, Q, seg, carry, chunk_pos=None):
    if carry:
        (xin_ref, z_ref, dt_ref, cw_ref, cb_ref, dtb_ref, alog_ref, dsk_ref, g_ref,
         y_ref, hfin_ref, xbuf, hT) = refs
    else:
        (xin_ref, dt_ref, cw_ref, cb_ref, dtb_ref, alog_ref, dsk_ref,
         ypre_ref, eacs_ref, xw_ref, dec_ref, bm_ref, cm_ref) = refs
    cw = cw_ref[...]
    cb = cb_ref[...]
    if carry:
        c, n_chunks = (pl.program_id(1), pl.num_programs(1)) if chunk_pos is None else chunk_pos
        if chunk_pos is None:
            _ssd_reset(c, xbuf, hT)
        xbuf[8:8 + Q, :] = xin_ref[...]
    xc_parts = []
    for lb in range(CONV_DIM // CONV_LANES):
        cols = slice(lb * CONV_LANES, (lb + 1) * CONV_LANES)
        acc = jnp.broadcast_to(cb[:, cols], (Q, CONV_LANES))
        for k in range(CONV_WIDTH):
            if carry:
                tap = xbuf[pl.ds(8 - (CONV_WIDTH - 1) + k, Q), cols]
            else:
                tap = xin_ref[k, :, cols]
            acc = acc + cw[k:k + 1, cols] * tap
        xc_parts.append(_silu(acc))
        yield
    if carry:
        xbuf[0:8, :] = xbuf[Q:Q + 8, :]
    xc = jnp.concatenate(xc_parts, axis=1)
    xs = xc[:, :SSD_INNER]
    gs = SSD_GROUPS * SSD_STATE
    bm = xc[:, SSD_INNER:SSD_INNER + gs]
    cm = xc[:, SSD_INNER + gs:]
    bm_b = bm.astype(BF16)
    cm_b = cm.astype(BF16)

    lane = lax.broadcasted_iota(jnp.int32, (Q, LANES), 1)
    v = dt_ref[...] + dtb_ref[...]
    dt = jnp.maximum(v, 0.0) + jnp.log1p(jnp.exp(-jnp.abs(v)))
    dt = jnp.where(lane < SSD_HEADS, dt, 0.0)
    dA = dt * (-jnp.exp(alog_ref[...]))
    ri = lax.broadcasted_iota(jnp.int32, (Q, Q), 0)
    ci = lax.broadcasted_iota(jnp.int32, (Q, Q), 1)
    if seg == Q:
        mask = ci <= ri
    else:
        same = (ri // seg) == (ci // seg)
        mask = jnp.logical_and(same, ci <= ri)
    acs = _split3_dot(jnp.where(mask, 1.0, 0.0).astype(BF16), dA)
    if seg == Q:
        acs_last = acs[Q - 1:Q, :]
    else:
        acs_last = _split3_dot(jnp.where(same, 1.0, 0.0).astype(BF16), dA)
    to_end = jnp.exp(acs_last - acs) * dt
    acsT = acs.T
    dtT = dt.T
    yield

    G = [_dot_nt(cm_b[:, g * SSD_STATE:(g + 1) * SSD_STATE], bm_b[:, g * SSD_STATE:(g + 1) * SSD_STATE])
         for g in range(SSD_GROUPS)]
    lane_lo = lane < SSD_HEAD_DIM
    heads_per_group = SSD_HEADS // SSD_GROUPS
    ypairs, epairs, xwpairs, decpairs = [], [], [], []
    for k in range(SSD_HEADS // 2):
        g = (2 * k) // heads_per_group
        xp = xs[:, k * LANES:(k + 1) * LANES]
        xhalf = (jnp.where(lane_lo, xp, 0.0).astype(BF16), jnp.where(lane_lo, 0.0, xp).astype(BF16))
        yk = jnp.zeros((Q, LANES), F32)
        for s in range(2):
            hh = 2 * k + s
            segm = acs[:, hh:hh + 1] - acsT[hh:hh + 1, :]
            m = G[g] * jnp.exp(jnp.where(mask, segm, -jnp.inf)) * dtT[hh:hh + 1, :]
            yk = yk + _dot(m.astype(BF16), xhalf[s])
            yield

        def pair(a):
            return jnp.where(lane_lo[:a.shape[0]], a[:, 2 * k:2 * k + 1], a[:, 2 * k + 1:2 * k + 2])

        e_p = jnp.exp(pair(acs))
        xw = xp * pair(to_end)
        dec = jnp.exp(pair(acs_last))
        if carry:
            h_prev = hT[k]
            yk = yk + _dot(cm_b[:, g * SSD_STATE:(g + 1) * SSD_STATE], h_prev.astype(BF16)) * e_p
            hT[k] = dec * h_prev + _dot_tn(bm_b[:, g * SSD_STATE:(g + 1) * SSD_STATE], xw.astype(BF16))
        else:
            epairs.append(e_p)
            xwpairs.append(xw)
            decpairs.append(dec)
        ypairs.append(yk)
        yield
    y = jnp.concatenate(ypairs, axis=1) + dsk_ref[...] * xs
    if carry:
        y = y * _silu(z_ref[...])
        y_ref[...] = _rms(y, g_ref[...]).astype(BF16)

        if chunk_pos is None:
            _ssd_final(c, n_chunks, hfin_ref, hT)
    else:
        ypre_ref[...] = y
        eacs_ref[...] = jnp.concatenate(epairs, axis=1)
        xw_ref[...] = jnp.concatenate(xwpairs, axis=1)
        dec_ref[...] = jnp.concatenate(decpairs, axis=1)
        bm_ref[...] = bm
        cm_ref[...] = cm


def _row(a, n):
    return jnp.pad(a.reshape(1, -1).astype(F32), ((0, 0), (0, n - a.size)))


def _ssd_prompt(xbc, z, dt, cw, cb, dtb, alog, dsk, g):
    nb, L, _ = xbc.shape
    Q = SSD_CHUNK if L % SSD_CHUNK == 0 else L
    nc = L // Q

    def tok(width):
        return pl.BlockSpec((None, Q, width), lambda b, c: (b, c, 0))

    def const(a):
        return pl.BlockSpec(a.shape, lambda b, c: (0,) * a.ndim)

    consts = (cw, cb, dtb, alog, dsk, g)
    return pl.pallas_call(
        functools.partial(_ssd_body, Q=Q, seg=Q, carry=True),
        grid=(nb, nc),
        in_specs=[tok(CONV_DIM), tok(SSD_INNER), tok(LANES)] + [const(a) for a in consts],
        out_specs=[tok(SSD_INNER), pl.BlockSpec((None, SSD_INNER, SSD_STATE), lambda b, c: (b, 0, 0))],
        out_shape=[SDS((nb, L, SSD_INNER), BF16), SDS((nb, SSD_INNER, SSD_STATE), F32)],
        scratch_shapes=[pltpu.VMEM((Q + 8, CONV_DIM), F32), pltpu.VMEM((SSD_HEADS // 2, SSD_STATE, LANES), F32)],
        compiler_params=_params(("arbitrary", "arbitrary")),
        name="ssd_prompt",
    )(xbc, z, dt, *consts)


def _ssd_sample(xsh, dt, cw, cb, dtb, alog, dsk, *, seg):
    _, T, _ = xsh.shape
    consts = (cw, cb, dtb, alog, dsk)

    def full(a):
        return pl.BlockSpec(a.shape, lambda i: (0,) * a.ndim)

    outs = [SDS((T, SSD_INNER), F32)] * 4 + [SDS((T, SSD_GROUPS * SSD_STATE), F32)] * 2
    return pl.pallas_call(
        functools.partial(_ssd_body, Q=T, seg=seg, carry=False),
        grid=(1,),
        in_specs=[full(xsh), full(dt)] + [full(a) for a in consts],
        out_specs=[full(o) for o in outs],
        out_shape=outs,
        compiler_params=_params(("arbitrary",)),
        name="ssd_sample",
    )(xsh, dt, *consts)


def _sstate_body(cm_ref, bm_ref, ypre_ref, eacs_ref, xw_ref, dec_ref, z_ref, s0_ref, g_ref, y_ref, sn_ref):
    s0 = s0_ref[...]
    s0b = s0.astype(BF16)
    cm = cm_ref[...].astype(BF16)
    bm = bm_ref[...].astype(BF16)
    rows = SSD_INNER // SSD_GROUPS
    yo = jnp.concatenate(
        [jnp.einsum("btn,bqn->btq", cm[:, :, g * SSD_STATE:(g + 1) * SSD_STATE],
                    s0b[:, g * rows:(g + 1) * rows, :], preferred_element_type=F32)
         for g in range(SSD_GROUPS)], axis=-1)
    y = (ypre_ref[...] + yo * eacs_ref[...]) * _silu(z_ref[...])
    y_ref[...] = _rms(y, g_ref[...]).astype(BF16)
    dec = dec_ref[...]
    hi = dec.astype(BF16)
    lo = (dec - hi.astype(F32)).astype(BF16)
    sel = jnp.where(lax.broadcasted_iota(jnp.int32, (dec.shape[0], dec.shape[1], SSD_STATE), 1) == 0,
                    1.0, 0.0).astype(BF16)
    dmat = (jnp.einsum("bjq,bjn->bqn", hi, sel, preferred_element_type=F32)
            + jnp.einsum("bjq,bjn->bqn", lo, sel, preferred_element_type=F32))
    xw = xw_ref[...].astype(BF16)
    upd = jnp.concatenate(
        [jnp.einsum("bjq,bjn->bqn", xw[:, :, g * rows:(g + 1) * rows],
                    bm[:, :, g * SSD_STATE:(g + 1) * SSD_STATE], preferred_element_type=F32)
         for g in range(SSD_GROUPS)], axis=1)
    sn_ref[...] = dmat * s0 + upd


def _sstate(cm, bm, ypre, eacs, xw, dec, z, s0, g, *, bs):
    nseq, t, _ = cm.shape
    bs = min(bs, nseq)

    def blk(a):
        return pl.BlockSpec((bs,) + a.shape[1:], lambda i: (i, 0, 0))

    ins = (cm, bm, ypre, eacs, xw, dec, z, s0)
    return pl.pallas_call(
        _sstate_body,
        grid=(nseq // bs,),
        in_specs=[blk(a) for a in ins] + [pl.BlockSpec(g.shape, lambda i: (0, 0))],
        out_specs=[blk(ypre), blk(s0)],
        out_shape=[SDS(ypre.shape, BF16), SDS(s0.shape, F32)],
        compiler_params=_params(("arbitrary",)),
        name="sstate",
    )(*ins, g)


def _attn_body(qt_ref, qn_ref, kp_ref, vt_ref, g_ref, o_ref, m_sc, acc_sc,
               s_a, s_b, s_c, smax_a, smax_b, smax_c, *, tk):
    i = pl.program_id(1)
    ncols = MLA_HEADS * TQ
    m_sc[...] = jnp.full(m_sc.shape, -jnp.inf, F32)
    acc_sc[...] = jnp.zeros(acc_sc.shape, F32)
    ct = 2 * TQ
    tiles = [slice(p * ct, (p + 1) * ct) for p in range(PAIRS)]

    def scores(j, s_ref, smax_ref, q_ref=qt_ref):
        k0 = pl.multiple_of(j * tk, tk)
        for p, cs in enumerate(tiles):
            s = _dot(kp_ref[p, pl.ds(k0, tk), :], q_ref[p])
            s_ref[:, cs] = s
            smax_ref[:, cs] = jnp.max(s, axis=0, keepdims=True)

    def softmax_pv(j, s_ref, smax_ref, masked):
        for p, cs in enumerate(tiles):
            s = s_ref[:, cs]
            if masked:
                key = j * tk + lax.broadcasted_iota(jnp.int32, (tk, ct), 0)
                tok = i * TQ + (lax.broadcasted_iota(jnp.int32, (tk, ct), 1) & (TQ - 1))
                s = jnp.where(key <= tok, s, -jnp.inf)
                smax = jnp.max(s, axis=0, keepdims=True)
            else:
                smax = smax_ref[:, cs]
            m_prev = m_sc[:, cs]
            m_new = jnp.maximum(m_prev, smax)
            alpha = jnp.exp2(m_prev - m_new)
            e = jnp.exp2(s - m_new)
            acc_sc[p] = alpha * acc_sc[p] + _dot(vt_ref[j, p], e.astype(BF16))
            m_sc[:, cs] = m_new

    nfull = (i * TQ) // tk

    @pl.when(i == 0)
    def _():
        scores(0, s_c, smax_c)

    def prefetch():
        scores(0, s_c, smax_c, qn_ref)

    @pl.when(nfull == 0)
    def _():
        softmax_pv(0, s_c, smax_c, True)
        prefetch()

    @pl.when(nfull >= 1)
    def _():
        scores(1, s_b, smax_b)
        softmax_pv(0, s_c, smax_c, False)

    def pair(p, carry):
        j = 2 * p + 1
        scores(j + 1, s_a, smax_a)
        softmax_pv(j, s_b, smax_b, False)
        scores(j + 2, s_b, smax_b)
        softmax_pv(j + 1, s_a, smax_a, False)
        return carry

    lax.fori_loop(0, jnp.maximum(nfull - 1, 0) // 2, pair, 0)
    odd = (nfull % 2) == 1

    @pl.when(odd)
    def _():
        prefetch()
        softmax_pv(nfull, s_b, smax_b, True)

    @pl.when(jnp.logical_and(jnp.logical_not(odd), nfull >= 2))
    def _():
        scores(nfull, s_a, smax_a)
        softmax_pv(nfull - 1, s_b, smax_b, False)
        prefetch()
        softmax_pv(nfull, s_a, smax_a, True)

    ys = []
    for hh in range(MLA_HEADS):
        p, s = divmod(hh, 2)
        cols = slice(s * TQ, (s + 1) * TQ)
        denom = acc_sc[p, 2 * V_DIM:2 * V_DIM + 1, cols]
        ys.append(acc_sc[p, s * V_DIM:(s + 1) * V_DIM, cols] * (1.0 / denom))
    yt = jnp.concatenate(ys, axis=0)
    yt = yt * lax.rsqrt(jnp.mean(yt * yt, axis=0, keepdims=True) + EPS)
    o_ref[...] = (yt.T * g_ref[...]).astype(BF16)


def _attn_prompt(qt, kp, vt, g, *, tk):
    nb, nq = qt.shape[:2]
    L = nq * TQ
    ncols = MLA_HEADS * TQ
    assert TQ & (TQ - 1) == 0 and L % tk == 0 and tk % TQ == 0

    def q_spec(index):
        return pl.BlockSpec((None, None, PAIRS, PAIR_K, 2 * TQ), lambda b, i: (b, index(i), 0, 0, 0))

    return pl.pallas_call(
        functools.partial(_attn_body, tk=tk),
        grid=(nb, nq),
        in_specs=[q_spec(lambda i: i), q_spec(lambda i: jnp.minimum(i + 1, nq - 1)),
                  pl.BlockSpec((None, PAIRS, L, PAIR_K), lambda b, i: (b, 0, 0, 0), pipeline_mode=pl.Buffered(1)),
                  pl.BlockSpec((None, L // tk, PAIRS, PAIR_V, tk), lambda b, i: (b, 0, 0, 0, 0),
                               pipeline_mode=pl.Buffered(1)),
                  pl.BlockSpec(g.shape, lambda b, i: (0, 0))],
        out_specs=pl.BlockSpec((None, TQ, MLA_INNER), lambda b, i: (b, i, 0)),
        out_shape=SDS((nb, L, MLA_INNER), BF16),
        scratch_shapes=[pltpu.VMEM((1, ncols), F32),
                        pltpu.VMEM((PAIRS, PAIR_V, 2 * TQ), F32),
                        pltpu.VMEM((tk, ncols), F32), pltpu.VMEM((tk, ncols), F32), pltpu.VMEM((tk, ncols), F32),
                        pltpu.VMEM((1, ncols), F32), pltpu.VMEM((1, ncols), F32), pltpu.VMEM((1, ncols), F32)],
        compiler_params=_params(("arbitrary", "arbitrary")),
        name="attn_prompt",
    )(qt, qt, kp, vt, g)


def _sattn_body(pt_ref, qa_ref, qr_ref, kn_ref, rn_ref, ckv_hbm, krt_hbm, o_ref, kbuf, rbuf, sem, *, npages, t_new):
    slot = _sattn_fetch(pt_ref, ckv_hbm, krt_hbm, kbuf, rbuf, sem, npages=npages)
    for _ in _sattn_steps(slot, qa_ref, qr_ref, kn_ref, rn_ref, o_ref, kbuf, rbuf, npages=npages, t_new=t_new,
                          split=SPLIT):
        pass


def _sattn_fetch(pt_ref, ckv_hbm, krt_hbm, kbuf, rbuf, sem, *, npages):
    b = pl.program_id(0)
    nseq = pl.num_programs(0)

    def copies(seq_page, slot, p):
        off = p * PAGE_SIZE
        return (pltpu.make_async_copy(ckv_hbm.at[seq_page], kbuf.at[slot, p], sem.at[0, slot]),
                pltpu.make_async_copy(krt_hbm.at[seq_page], rbuf.at[slot, :, pl.ds(off, PAGE_SIZE)],
                                      sem.at[1, slot]))

    def start_fetch(seq, slot):
        for p in range(npages):
            for cp in copies(pt_ref[seq * npages + p], slot, p):
                cp.start()

    def wait_fetch(slot):
        pltpu.make_async_copy(ckv_hbm.at[pl.ds(0, npages)], kbuf.at[slot], sem.at[0, slot]).wait()
        pltpu.make_async_copy(rbuf.at[slot], rbuf.at[slot], sem.at[1, slot]).wait()

    @pl.when(b == 0)
    def _():
        start_fetch(0, 0)

    @pl.when(b + 1 < nseq)
    def _():
        start_fetch(b + 1, (b + 1) % 2)

    slot = b % 2
    wait_fetch(slot)
    return slot


def _sattn_steps(slot, qa_ref, qr_ref, kn_ref, rn_ref, o_ref, kbuf, rbuf, *, npages, t_new, split):
    split = math.gcd(split, npages)
    grp = math.gcd(SPLIT, split)
    pg = npages // split
    part = pg * PAGE_SIZE
    q = qa_ref[...]
    qr = qr_ref[...]
    kn = kn_ref[...]
    rn = rn_ref[...]
    kps, s_p = [], []
    for h0 in range(0, split, grp):
        hs = range(h0, h0 + grp)
        kps += [kbuf[slot, pl.ds(h * pg, pg)].reshape(part, KV_RANK).astype(BF16) for h in hs]
        s_main = [_dot_nt(q, kps[h]) for h in hs]
        s_rope = [_dot(qr, rbuf[slot, :, h * part:(h + 1) * part].astype(BF16)) for h in hs]
        s_p += [a + b for a, b in zip(s_main, s_rope)]
        yield
    s_n = _dot_nt(q, kn) + _dot_nt(qr, rn)
    rows = t_new * MLA_HEADS
    tok = lax.broadcasted_iota(jnp.int32, (rows, t_new), 0) // MLA_HEADS
    col = lax.broadcasted_iota(jnp.int32, (rows, t_new), 1)
    s_n = jnp.where(col <= tok, s_n, -jnp.inf)
    m = jnp.max(s_n, axis=-1, keepdims=True)
    for s in s_p:
        m = jnp.maximum(m, jnp.max(s, axis=-1, keepdims=True))
    pn = jnp.exp(s_n - m)
    pp = [jnp.exp(s - m) for s in s_p]
    l = jnp.sum(pn, axis=-1, keepdims=True)
    for p in pp:
        l = l + jnp.sum(p, axis=-1, keepdims=True)
    inv = 1.0 / l
    o = _dot(pn.astype(BF16), kn) * inv
    yield
    for h0 in range(0, split, grp):
        parts = [_dot(pp[h].astype(BF16), kps[h]) for h in range(h0, h0 + grp)]
        for part_o in parts:
            o = o + part_o * inv
        yield
    o_ref[...] = o


def _sattn(page_table, qa, qr, kn, rn, cache_kv, cache_krt):
    nseq, npages = page_table.shape
    t_new = kn.shape[1]
    rows = t_new * MLA_HEADS
    grid_spec = pltpu.PrefetchScalarGridSpec(
        num_scalar_prefetch=1,
        grid=(nseq,),
        in_specs=[pl.BlockSpec((rows, KV_RANK), lambda b, pt: (b, 0)),
                  pl.BlockSpec((rows, QK_ROPE), lambda b, pt: (b, 0)),
                  pl.BlockSpec((None, t_new, KV_RANK), lambda b, pt: (b, 0, 0)),
                  pl.BlockSpec((None, t_new, QK_ROPE), lambda b, pt: (b, 0, 0)),
                  pl.BlockSpec(memory_space=pl.ANY),
                  pl.BlockSpec(memory_space=pl.ANY)],
        out_specs=pl.BlockSpec((rows, KV_RANK), lambda b, pt: (b, 0)),
        scratch_shapes=[pltpu.VMEM((2, npages, PAGE_SIZE, KV_RANK), F32),
                        pltpu.VMEM((2, QK_ROPE, npages * PAGE_SIZE), F32),
                        pltpu.SemaphoreType.DMA((2, 2))],
    )
    return pl.pallas_call(
        functools.partial(_sattn_body, npages=npages, t_new=t_new),
        grid_spec=grid_spec,
        out_shape=SDS((nseq * rows, KV_RANK), F32),
        compiler_params=_params(("arbitrary",)),
        name="attn_sample",
    )(page_table.reshape(-1), qa, qr, kn, rn, cache_kv, cache_krt)


def _ssd_sattn_body(pt_ref, qa_ref, qr_ref, kn_ref, rn_ref, ckv_hbm, krt_hbm,
                    xin_ref, z_ref, dt_ref, cw_ref, cb_ref, dtb_ref, alog_ref, dsk_ref, g_ref,
                    o_ref, y_ref, hfin_ref, kbuf, rbuf, sem, xbuf, hT, *, npages, t_new, Q, n_chunks):
    c = pl.program_id(0) % n_chunks
    _ssd_reset(c, xbuf, hT)
    slot = _sattn_fetch(pt_ref, ckv_hbm, krt_hbm, kbuf, rbuf, sem, npages=npages)
    ssd = _ssd_steps(xin_ref, z_ref, dt_ref, cw_ref, cb_ref, dtb_ref, alog_ref, dsk_ref, g_ref, y_ref, hfin_ref,
                     xbuf, hT, Q=Q, seg=Q, carry=True, chunk_pos=(c, n_chunks))
    att = _sattn_steps(slot, qa_ref, qr_ref, kn_ref, rn_ref, o_ref, kbuf, rbuf, npages=npages, t_new=t_new,
                       split=FUSED_SPLIT)
    done = object()
    live = [att, ssd]
    while live:
        live = [gen for gen in live if next(gen, done) is not done]
    _ssd_final(c, n_chunks, hfin_ref, hT)


def _ssd_sattn(page_table, qa, qr, kn, rn, cache_kv, cache_krt, xbc, z, dt, cw, cb, dtb, alog, dsk, g):
    nseq, npages = page_table.shape
    t_new = kn.shape[1]
    rows = t_new * MLA_HEADS
    nb, L, _ = xbc.shape
    Q = SSD_CHUNK if L % SSD_CHUNK == 0 else L
    nc = L // Q
    assert nseq == nb * nc
    consts = (cw, cb, dtb, alog, dsk, g)

    def tok(width):
        return pl.BlockSpec((None, Q, width), lambda s, pt: (s // nc, s % nc, 0))

    def const(a):
        return pl.BlockSpec(a.shape, lambda s, pt: (0,) * a.ndim)

    grid_spec = pltpu.PrefetchScalarGridSpec(
        num_scalar_prefetch=1,
        grid=(nseq,),
        in_specs=[pl.BlockSpec((rows, KV_RANK), lambda s, pt: (s, 0)),
                  pl.BlockSpec((rows, QK_ROPE), lambda s, pt: (s, 0)),
                  pl.BlockSpec((None, t_new, KV_RANK), lambda s, pt: (s, 0, 0)),
                  pl.BlockSpec((None, t_new, QK_ROPE), lambda s, pt: (s, 0, 0)),
                  pl.BlockSpec(memory_space=pl.ANY),
                  pl.BlockSpec(memory_space=pl.ANY),
                  tok(CONV_DIM), tok(SSD_INNER), tok(LANES)] + [const(a) for a in consts],
        out_specs=[pl.BlockSpec((rows, KV_RANK), lambda s, pt: (s, 0)),
                   tok(SSD_INNER),
                   pl.BlockSpec((None, SSD_INNER, SSD_STATE), lambda s, pt: (s // nc, 0, 0))],
        scratch_shapes=[pltpu.VMEM((2, npages, PAGE_SIZE, KV_RANK), F32),
                        pltpu.VMEM((2, QK_ROPE, npages * PAGE_SIZE), F32),
                        pltpu.SemaphoreType.DMA((2, 2)),
                        pltpu.VMEM((Q + 8, CONV_DIM), F32),
                        pltpu.VMEM((SSD_HEADS // 2, SSD_STATE, LANES), F32)],
    )
    return pl.pallas_call(
        functools.partial(_ssd_sattn_body, npages=npages, t_new=t_new, Q=Q, n_chunks=nc),
        grid_spec=grid_spec,
        out_shape=[SDS((nseq * rows, KV_RANK), F32), SDS((nb, L, SSD_INNER), BF16),
                   SDS((nb, SSD_INNER, SSD_STATE), F32)],
        compiler_params=_params(("arbitrary",)),
        name="ssd_prompt_attn_sample",
    )(page_table.reshape(-1), qa, qr, kn, rn, cache_kv, cache_krt, xbc, z, dt, *consts)


def _apost_body(o_ref, wuv_ref, g_ref, y_ref):
    tm = o_ref.shape[0]
    o = jnp.zeros((tm, MLA_INNER), F32)
    for hh in range(MLA_HEADS):
        o = o + _dot(o_ref[:, hh * KV_RANK:(hh + 1) * KV_RANK].astype(BF16), wuv_ref[hh])
    y_ref[...] = _rms(o, g_ref[...]).astype(BF16)


def _apost(o_lat, wuv_pad, g):
    T = o_lat.shape[0]
    return pl.pallas_call(
        _apost_body,
        grid=(1,),
        in_specs=[pl.BlockSpec(o_lat.shape, lambda i: (0, 0)),
                  pl.BlockSpec(wuv_pad.shape, lambda i: (0, 0, 0)),
                  pl.BlockSpec(g.shape, lambda i: (0, 0))],
        out_specs=pl.BlockSpec((T, MLA_INNER), lambda i: (0, 0)),
        out_shape=SDS((T, MLA_INNER), BF16),
        compiler_params=_params(("arbitrary",)),
        name="attn_post",
    )(o_lat, wuv_pad, g)


def _mlp_body(x_ref, ys_ref, ya_ref, g1_ref, sh2_ref, sc2_ref, g2_ref, shf_ref, scf_ref,
              wout_ref, gmlp_ref, wup_ref, wdn_ref, gfin_ref, o_ref, *, final, tf, rep):
    yy = jnp.concatenate([ys_ref[...], ya_ref[...]], axis=-1)
    x1 = x_ref[...] + _mod_rows(g1_ref, rep) * _dot(yy, wout_ref[...])
    h2 = (_rms(x1, gmlp_ref[...]) * (1.0 + _mod_rows(sc2_ref, rep)) + _mod_rows(sh2_ref, rep)).astype(BF16)
    acc = jnp.zeros(x1.shape, F32)
    for c in range(D_FF // tf):
        u = jnp.maximum(_dot(h2, wup_ref[:, c * tf:(c + 1) * tf]), 0.0)
        acc = acc + _dot((u * u).astype(BF16), wdn_ref[c * tf:(c + 1) * tf, :])
    x2 = x1 + _mod_rows(g2_ref, rep) * acc
    if final:
        x2 = _rms(x2, gfin_ref[...]) * (1.0 + _mod_rows(scf_ref, rep)) + _mod_rows(shf_ref, rep)
    o_ref[...] = x2


def _mlp(x3, ys, ya, mod3, wout, gmlp, wup, wdn, gfin, *, final, tm, tf):
    nb, L, d = x3.shape
    tm = min(tm, L)
    rep, mod_spec = _mod_spec(mod3, L, tm, d)

    def tok(width):
        return pl.BlockSpec((None, tm, width), lambda b, i: (b, i, 0))

    def const(a):
        return pl.BlockSpec(a.shape, lambda b, i: (0,) * a.ndim, pipeline_mode=pl.Buffered(1))

    return pl.pallas_call(
        functools.partial(_mlp_body, final=final, tf=tf, rep=rep),
        grid=(nb, L // tm),
        in_specs=[tok(d), tok(SSD_INNER), tok(MLA_INNER)] + [mod_spec(k) for k in (2, 3, 4, 5, 6, 7)]
                 + [const(wout), const(gmlp), const(wup), const(wdn), const(gfin)],
        out_specs=tok(d),
        out_shape=SDS((nb, L, d), F32),
        compiler_params=_params(("arbitrary", "arbitrary")),
        name="mlp",
    )(x3, ys, ya, mod3, mod3, mod3, mod3, mod3, mod3, wout, gmlp, wup, wdn, gfin)


def _rope_tables(pos):
    inv = 1.0 / (ROPE_THETA ** (np.arange(0, QK_ROPE, 2, dtype=np.float64) / QK_ROPE))
    ang = pos.astype(np.float64)[:, None] * inv[None, :]
    cos, sin = np.cos(ang).astype(np.float32), np.sin(ang).astype(np.float32)
    c32 = np.concatenate([cos, cos], axis=-1)
    s32 = np.concatenate([-sin, sin], axis=-1)
    tab1 = np.concatenate([c32, s32], axis=-1)
    tabq = np.concatenate([np.tile(c32, (1, MLA_HEADS)), np.tile(s32, (1, MLA_HEADS))], axis=-1)
    return tab1, tabq


def _swap_halves(w):
    half = w.shape[-1] // 2
    return jnp.concatenate([w[..., half:], w[..., :half]], axis=-1)


def kernel(x_prompt, x_sample, cache_kv_latent, cache_k_rope, state_conv, state_ssm, page_table,
           c_prompt, c_sample, w_ada, b_ada, norm_mix_g, w_in, conv_w, conv_b, dt_bias, a_log,
           d_skip, norm_ssd_g, q_norm_g, kv_norm_g, w_uq, w_uk, w_uv, norm_attn_g, w_out,
           norm_mlp_g, w_up, w_down, w_ada_final, b_ada_final, norm_final_g):
    depth = w_in.shape[0]
    b_p, seq, d = x_prompt.shape
    n_seq, t_new, _ = x_sample.shape
    n_tok_s = n_seq * t_new
    past_len = page_table.shape[1] * PAGE_SIZE

    c_all = jnp.concatenate([c_prompt, c_sample], axis=0)
    ada_fin = _ada(c_all, w_ada_final, b_ada_final)
    tab1_p, _ = _rope_tables(np.arange(seq))
    tab1_s, tabq_s = _rope_tables(past_len + np.arange(t_new))
    tab1_pt = jnp.asarray(np.ascontiguousarray(tab1_p.T))
    tab1_p = jnp.asarray(tab1_p)
    tabq_s = jnp.asarray(np.tile(tabq_s, (n_seq, 1)))
    tab1_s = jnp.asarray(np.tile(tab1_s, (n_seq, 1)))

    xp = x_prompt
    xs = x_sample.reshape(1, n_tok_s, d)
    outs_p, outs_s = [], []
    for l in range(depth):
        final = l == depth - 1
        wi = w_in[l]
        c1 = SSD_INNER
        c2 = c1 + CONV_DIM
        c3 = c2 + SSD_HEADS
        c4 = c3 + Q_RANK
        c5 = c4 + KV_RANK
        w_kr = wi[:, c5:]
        win = jnp.concatenate(
            [wi[:, :c2], wi[:, c3:c5], wi[:, c2:c3], jnp.zeros((d, MISC_KR - SSD_HEADS), F32), w_kr,
             _swap_halves(w_kr), jnp.zeros((d, LANES - MISC_KRSW - QK_ROPE), F32)], axis=1).astype(BF16)
        wq_h = w_uq[l].reshape(Q_RANK, MLA_HEADS, QK_NOPE + QK_ROPE)
        w_rope = wq_h[:, :, QK_NOPE:]
        wfold = _fold(jnp.transpose(wq_h, (1, 0, 2)), jnp.transpose(w_uk[l], (1, 0, 2)))
        wq = jnp.concatenate([wfold, w_rope.reshape(Q_RANK, -1).astype(BF16),
                              _swap_halves(w_rope).reshape(Q_RANK, -1).astype(BF16)], axis=1)
        wuv_pad = jnp.zeros((MLA_HEADS, KV_RANK, MLA_HEADS, V_DIM), F32)
        wuv_pad = wuv_pad.at[jnp.arange(MLA_HEADS), :, jnp.arange(MLA_HEADS), :].set(
            jnp.transpose(w_uv[l], (1, 0, 2)))
        wuv_pad = wuv_pad.reshape(MLA_HEADS, KV_RANK, MLA_INNER).astype(BF16)
        wq_p = jnp.concatenate([wq_h[:, :, :QK_NOPE].reshape(Q_RANK, -1), w_rope.reshape(Q_RANK, -1),
                                _swap_halves(w_rope).reshape(Q_RANK, -1)], axis=1).T.astype(BF16)
        wk = w_uk[l].reshape(KV_RANK, MLA_HEADS * QK_NOPE).astype(BF16)
        wvt = jnp.transpose(w_uv[l], (1, 2, 0)).reshape(PAIRS, 2 * V_DIM, KV_RANK).astype(BF16)
        wout = w_out[l].astype(BF16)
        wup = w_up[l].astype(BF16)
        wdn = w_down[l].astype(BF16)
        gmix = norm_mix_g[l].reshape(1, d)
        gmlp = norm_mlp_g[l].reshape(1, d)
        gfin = norm_final_g.reshape(1, d)
        qg = q_norm_g[l].reshape(1, Q_RANK)
        kvg = kv_norm_g[l].reshape(1, KV_RANK)
        gssd = norm_ssd_g[l].reshape(1, SSD_INNER)
        gattn = norm_attn_g[l].reshape(1, MLA_INNER)
        cw = conv_w[l]
        cb = conv_b[l].reshape(1, CONV_DIM)
        dtb = _row(dt_bias[l], LANES)
        alog = _row(a_log[l], LANES)
        dsk = jnp.repeat(d_skip[l].astype(F32), SSD_HEAD_DIM).reshape(1, SSD_INNER)

        ada = _ada(c_all, w_ada[l], b_ada[l])
        mod = jnp.concatenate([ada, ada_fin], axis=1)
        mod_p = mod[:b_p].reshape(b_p, 1, 8 * d)
        mod_s = mod[b_p:].reshape(1, n_seq, 8 * d)

        def seqs(a):
            return a.reshape(n_seq, t_new, a.shape[-1])

        tk = min(512, seq)
        z_p, xbc_p, dtr_p, ckv_p, kr_p, kp, vt, qt = _inproj(
            xp, mod_p, gmix, win, qg, kvg, wq_p, tab1_pt, tab1_p, q_transposed=True, tm=512, tk=tk,
            wk=wk, wvt=wvt)
        z_s, xbc_s, dtr_s, ckv_s, kr_s, kc, krb, qa, qr = _inproj(
            xs, mod_s, gmix, win, qg, kvg, wq, tabq_s, tab1_s, q_transposed=False, tm=256)
        xpad = jnp.concatenate([state_conv[l], xbc_s.reshape(n_seq, t_new, CONV_DIM)], axis=1)
        xsh = jnp.stack([xpad[:, k:k + t_new].reshape(n_tok_s, CONV_DIM) for k in range(CONV_WIDTH)])
        ypre, eacs, xw, dec, bm, cm = _ssd_sample(xsh, dtr_s[0], cw, cb, dtb, alog, dsk, seg=t_new)
        y_ssd_s, s_new = _sstate(seqs(cm), seqs(bm), seqs(ypre), seqs(eacs), seqs(xw), seqs(dec), seqs(z_s[0]),
                                 state_ssm[l].reshape(n_seq, SSD_INNER, SSD_STATE), gssd, bs=8)

        sattn_args = (page_table, qa.reshape(n_tok_s * MLA_HEADS, KV_RANK),
                      qr.reshape(n_tok_s * MLA_HEADS, QK_ROPE), seqs(kc[0]), seqs(krb[0]),
                      cache_kv_latent[l], jnp.swapaxes(cache_k_rope[l], -1, -2))
        ssd_args = (xbc_p, z_p, dtr_p, cw, cb, dtb, alog, dsk, gssd)
        chunk = SSD_CHUNK if seq % SSD_CHUNK == 0 else seq
        if n_seq == b_p * (seq // chunk):
            o_lat, y_ssd_p, hfin = _ssd_sattn(*sattn_args, *ssd_args)
        else:
            y_ssd_p, hfin = _ssd_prompt(*ssd_args)
            o_lat = _sattn(*sattn_args)

        y_attn_p = _attn_prompt(qt, kp, vt, gattn, tk=tk)
        xp = _mlp(xp, y_ssd_p, y_attn_p, mod_p, wout, gmlp, wup, wdn, gfin, final=final, tm=512, tf=2048)
        tail = min(seq, CONV_WIDTH - 1)
        conv_tail = jnp.concatenate([jnp.zeros((b_p, CONV_WIDTH - 1 - tail, CONV_DIM), F32),
                                     xbc_p[:, seq - tail:]], axis=1)
        outs_p.append((ckv_p, kr_p, conv_tail, hfin.reshape(b_p, SSD_HEADS, SSD_HEAD_DIM, SSD_STATE)))

        y_attn_s = _apost(o_lat.reshape(n_tok_s, MLA_HEADS * KV_RANK), wuv_pad, gattn)
        xs = _mlp(xs, y_ssd_s.reshape(1, n_tok_s, SSD_INNER), y_attn_s.reshape(1, n_tok_s, MLA_INNER), mod_s,
                  wout, gmlp, wup, wdn, gfin, final=final, tm=n_tok_s, tf=2048)
        outs_s.append((seqs(ckv_s[0]), seqs(kr_s[0]), xpad[:, t_new:],
                       s_new.reshape(n_seq, SSD_HEADS, SSD_HEAD_DIM, SSD_STATE)))

    def stack(outs, k):
        return jnp.stack([o[k] for o in outs])

    return (xp, xs.reshape(n_seq, t_new, d),
            stack(outs_p, 0), stack(outs_p, 1), stack(outs_p, 2), stack(outs_p, 3),
            stack(outs_s, 0), stack(outs_s, 1), stack(outs_s, 2), stack(outs_s, 3))
```

```python
import functools
import math

import jax
import jax.numpy as jnp
import numpy as np
from jax import lax
from jax.experimental import pallas as pl
from jax.experimental.pallas import tpu as pltpu

F32 = jnp.float32
BF16 = jnp.bfloat16
SDS = jax.ShapeDtypeStruct

D_MODEL = 1024
SSD_HEADS = 8
SSD_HEAD_DIM = 64
SSD_INNER = SSD_HEADS * SSD_HEAD_DIM
SSD_GROUPS = 2
SSD_STATE = 128
CONV_WIDTH = 4
SSD_CHUNK = 128
CONV_DIM = SSD_INNER + 2 * SSD_GROUPS * SSD_STATE
MLA_HEADS = 8
QK_NOPE = 64
QK_ROPE = 32
V_DIM = 64
KV_RANK = 256
Q_RANK = 384
MLA_INNER = MLA_HEADS * V_DIM
ROPE_THETA = 10000.0
ATTN_SCALE = 1.0 / math.sqrt(QK_NOPE + QK_ROPE)
LOG2E = math.log2(math.e)
PAGE_SIZE = 128
D_FF = 4 * D_MODEL
EPS = 1e-6

LANES = 128
TQ = 128
PAIRS = MLA_HEADS // 2
PAIR_K = 256
PAIR_V = 2 * V_DIM + 16
SPLIT = 2
FUSED_SPLIT = 32
CONV_LANES = 128
C_Z = 0
C_XBC = C_Z + SSD_INNER
C_QLAT = C_XBC + CONV_DIM
C_KVLAT = C_QLAT + Q_RANK
C_MISC = C_KVLAT + KV_RANK
W_IN_COLS = C_MISC + LANES
MISC_KR = 32
MISC_KRSW = 64
C_QROPE = MLA_HEADS * KV_RANK
C_QROPE_SW = C_QROPE + MLA_HEADS * QK_ROPE
WQ_COLS = C_QROPE_SW + MLA_HEADS * QK_ROPE

VMEM_LIMIT = 52 * 1024 * 1024


def _dot(a, b):
    return jnp.dot(a, b, preferred_element_type=F32)


def _dot_nt(a, b):
    return lax.dot_general(a, b, (((1,), (1,)), ((), ())), preferred_element_type=F32)


def _dot_tn(a, b):
    return lax.dot_general(a, b, (((0,), (0,)), ((), ())), preferred_element_type=F32)


def _silu(x):
    return x * jax.nn.sigmoid(x)


def _rms(x, g):
    return x * lax.rsqrt(jnp.mean(x * x, axis=-1, keepdims=True) + EPS) * g


def _split3_dot(mask_bf16, v):
    v1 = v.astype(BF16)
    r1 = v - v1.astype(F32)
    v2 = r1.astype(BF16)
    v3 = (r1 - v2.astype(F32)).astype(BF16)
    return _dot(mask_bf16, v1) + _dot(mask_bf16, v2) + _dot(mask_bf16, v3)


def _mod_spec(mod3, L, tm, d):
    rows = mod3.shape[1]
    if rows == 1:
        return 1, lambda k: pl.BlockSpec((None, 1, d), lambda b, i: (b, 0, k))
    rep = L // rows
    return rep, lambda k: pl.BlockSpec((None, tm // rep, d), lambda b, i: (b, i, k))


def _mod_rows(ref, rep):
    m = ref[...]
    if rep == 1:
        return m
    n = m.shape[0] * rep
    sel = (lax.broadcasted_iota(jnp.int32, (n, m.shape[0]), 0) // rep
           == lax.broadcasted_iota(jnp.int32, (n, m.shape[0]), 1))
    sel = jnp.where(sel, 1.0, 0.0).astype(BF16)
    hi = m.astype(BF16)
    lo = (m - hi.astype(F32)).astype(BF16)
    return _dot(sel, hi) + _dot(sel, lo)


def _params(sem, vmem=VMEM_LIMIT):
    return pltpu.CompilerParams(dimension_semantics=sem, vmem_limit_bytes=vmem)


def _ada_body(c_ref, w_ref, b_ref, o_ref):
    s = _silu(c_ref[...]).astype(BF16)
    o_ref[...] = _dot(s, w_ref[...].astype(BF16)) + b_ref[...]


def _ada(c, w, b):
    bsz, d = c.shape
    n = w.shape[1]
    tn = 1024
    return pl.pallas_call(
        _ada_body,
        grid=(n // tn,),
        in_specs=[pl.BlockSpec((bsz, d), lambda j: (0, 0)),
                  pl.BlockSpec((d, tn), lambda j: (0, j)),
                  pl.BlockSpec((1, tn), lambda j: (0, j))],
        out_specs=pl.BlockSpec((bsz, tn), lambda j: (0, j)),
        out_shape=SDS((bsz, n), F32),
        compiler_params=_params(("arbitrary",)),
        name="ada",
    )(c, w, b.reshape(1, n))


def _fold_body(wq_ref, wk_ref, o_ref):
    a = wq_ref[:, 0:QK_NOPE].astype(BF16)
    o_ref[...] = _dot_nt(a, wk_ref[...].astype(BF16)).astype(BF16)


def _fold(wq_h, wk_h):
    return pl.pallas_call(
        _fold_body,
        grid=(MLA_HEADS,),
        in_specs=[pl.BlockSpec((None, Q_RANK, QK_NOPE + QK_ROPE), lambda h: (h, 0, 0)),
                  pl.BlockSpec((None, KV_RANK, QK_NOPE), lambda h: (h, 0, 0))],
        out_specs=pl.BlockSpec((Q_RANK, KV_RANK), lambda h: (0, h)),
        out_shape=SDS((Q_RANK, MLA_HEADS * KV_RANK), BF16),
        compiler_params=_params(("arbitrary",)),
        name="fold",
    )(wq_h, wk_h)


def _inproj_body(*refs, q_transposed, rep):
    tm = refs[0].shape[0]
    halves = 2 if (q_transposed and rep == 1 and tm % (2 * TQ) == 0) else 1
    rows = tm // halves
    gens = [_inproj_steps(refs, slice(k * rows, (k + 1) * rows), q_transposed, rep) for k in range(halves)]
    done = object()
    while gens:
        gens = [g for g in gens if next(g, done) is not done]


def _inproj_steps(refs, rs, q_transposed, rep):
    n_in = 12 if q_transposed else 10
    x_ref, sh_ref, sc_ref, gmix_ref, win_ref, qg_ref, kvg_ref, wq_ref, tq_ref, tk_ref = refs[:10]
    z_ref, xbc_ref, dt_ref, ckv_ref, kr_ref = refs[n_in:n_in + 5]
    sc, sh = _mod_rows(sc_ref, rep), _mod_rows(sh_ref, rep)
    if rep > 1:
        sc, sh = sc[rs], sh[rs]
    h = _rms(x_ref[rs, :], gmix_ref[...]) * (1.0 + sc) + sh
    proj = _dot(h.astype(BF16), win_ref[...])
    yield
    tm = proj.shape[0]
    z_ref[rs, :] = proj[:, C_Z:C_XBC]
    xbc_ref[rs, :] = proj[:, C_XBC:C_QLAT]
    q_scale = ATTN_SCALE * LOG2E if q_transposed else ATTN_SCALE
    qn = (_rms(proj[:, C_QLAT:C_KVLAT], qg_ref[...]) * q_scale).astype(BF16)
    ckv = _rms(proj[:, C_KVLAT:C_MISC], kvg_ref[...])
    ckv_ref[rs, :] = ckv
    misc = proj[:, C_MISC:W_IN_COLS]
    lane = lax.broadcasted_iota(jnp.int32, misc.shape, 1)
    dt_ref[rs, :] = jnp.where(lane < SSD_HEADS, misc, 0.0)
    tk = tk_ref[rs, :]
    kr = (misc[:, MISC_KR:MISC_KR + QK_ROPE] * tk[:, :QK_ROPE]
          + misc[:, MISC_KRSW:MISC_KRSW + QK_ROPE] * tk[:, QK_ROPE:])
    kr_ref[rs, :] = kr
    nr = MLA_HEADS * QK_ROPE
    if q_transposed:
        wk_ref, wvt_ref = refs[10:12]
        kp_ref, vt_ref, qt_ref = refs[n_in + 5:]
        ckv_b = ckv.astype(BF16)
        k_nope = _dot(ckv_b, wk_ref[...])
        kr_pad = jnp.concatenate([kr, jnp.zeros((tm, PAIR_K - 2 * QK_NOPE - QK_ROPE), F32)], axis=1).astype(BF16)
        pad_rows = PAIR_V - 2 * V_DIM
        ones_row = jnp.where(lax.broadcasted_iota(jnp.int32, (pad_rows, tm), 0) == 0, 1.0, 0.0).astype(BF16)
        for p in range(PAIRS):
            kp_ref[p, rs, 0:2 * QK_NOPE] = k_nope[:, p * 2 * QK_NOPE:(p + 1) * 2 * QK_NOPE].astype(BF16)
            kp_ref[p, rs, 2 * QK_NOPE:PAIR_K] = kr_pad
            vt_ref[p, 0:2 * V_DIM, rs] = _dot_nt(wvt_ref[p], ckv_b).astype(BF16)
            vt_ref[p, 2 * V_DIM:PAIR_V, rs] = ones_row
        yield
        qt = _dot_nt(wq_ref[...], qn)
        yield
        n0 = MLA_HEADS * QK_NOPE
        tq = tq_ref[:, rs]
        cos_t = jnp.concatenate([tq[:QK_ROPE]] * MLA_HEADS, axis=0)
        sin_t = jnp.concatenate([tq[QK_ROPE:]] * MLA_HEADS, axis=0)
        rot = (qt[n0:n0 + nr] * cos_t + qt[n0 + nr:n0 + 2 * nr] * sin_t).astype(BF16)
        q_nope = qt[:n0].astype(BF16)
        zero = jnp.zeros((PAIR_K, TQ), BF16)
        c0 = rs.start // TQ
        for c in range(tm // TQ):
            toks = slice(c * TQ, (c + 1) * TQ)
            for p in range(PAIRS):
                for s in range(2):
                    hh = 2 * p + s
                    cols = slice(s * TQ, (s + 1) * TQ)
                    qt_ref[c0 + c, p, :, cols] = zero
                    qt_ref[c0 + c, p, s * QK_NOPE:(s + 1) * QK_NOPE, cols] = (
                        q_nope[hh * QK_NOPE:(hh + 1) * QK_NOPE, toks])
                    qt_ref[c0 + c, p, 2 * QK_NOPE:2 * QK_NOPE + QK_ROPE, cols] = (
                        rot[hh * QK_ROPE:(hh + 1) * QK_ROPE, toks])
    else:
        kc_ref, krb_ref, qa_ref, qr_ref = refs[n_in + 5:]
        kc_ref[rs, :] = ckv.astype(BF16)
        krb_ref[rs, :] = kr.astype(BF16)
        q = _dot(qn, wq_ref[...])
        tq = tq_ref[rs, :]
        qa_ref[rs, :] = q[:, :C_QROPE].astype(BF16)
        qr_ref[rs, :] = (q[:, C_QROPE:C_QROPE_SW] * tq[:, :nr] + q[:, C_QROPE_SW:WQ_COLS] * tq[:, nr:]).astype(BF16)


def _inproj(x3, mod3, gmix, win, qg, kvg, wq, tabq, tabk, *, q_transposed, tm, tk=None, wk=None, wvt=None):
    nb, L, d = x3.shape
    tm = min(tm, L)
    nt = L // tm
    rep, mod_spec = _mod_spec(mod3, L, tm, d)

    def tok(width):
        return pl.BlockSpec((None, tm, width), lambda b, i: (b, i, 0))

    def const(a):
        return pl.BlockSpec(a.shape, lambda b, i: (0,) * a.ndim, pipeline_mode=pl.Buffered(1))

    nr = MLA_HEADS * QK_ROPE
    out_specs = [tok(SSD_INNER), tok(CONV_DIM), tok(LANES), tok(KV_RANK), tok(QK_ROPE)]
    out_shape = [SDS((nb, L, SSD_INNER), F32), SDS((nb, L, CONV_DIM), F32), SDS((nb, L, LANES), F32),
                 SDS((nb, L, KV_RANK), F32), SDS((nb, L, QK_ROPE), F32)]
    if q_transposed:
        assert tm % TQ == 0 and tk % tm == 0
        r = tk // tm
        tabq_spec = pl.BlockSpec((2 * QK_ROPE, tm), lambda b, i: (0, i))
        out_specs += [pl.BlockSpec((None, PAIRS, tm, PAIR_K), lambda b, i: (b, 0, i, 0)),
                      pl.BlockSpec((None, None, PAIRS, PAIR_V, tm), lambda b, i: (b, i // r, 0, 0, i % r)),
                      pl.BlockSpec((None, tm // TQ, PAIRS, PAIR_K, 2 * TQ), lambda b, i: (b, i, 0, 0, 0))]
        out_shape += [SDS((nb, PAIRS, L, PAIR_K), BF16), SDS((nb, L // tk, PAIRS, PAIR_V, tk), BF16),
                      SDS((nb, L // TQ, PAIRS, PAIR_K, 2 * TQ), BF16)]
        extra = [wk, wvt]
    else:
        extra = []
        tabq_spec = pl.BlockSpec((tm, 2 * nr), lambda b, i: (i, 0))
        out_specs += [tok(KV_RANK), tok(QK_ROPE), tok(C_QROPE), tok(nr)]
        out_shape += [SDS((nb, L, KV_RANK), BF16), SDS((nb, L, QK_ROPE), BF16),
                      SDS((nb, L, C_QROPE), BF16), SDS((nb, L, nr), BF16)]
    return pl.pallas_call(
        functools.partial(_inproj_body, q_transposed=q_transposed, rep=rep),
        grid=(nb, nt),
        in_specs=[tok(d), mod_spec(0), mod_spec(1), const(gmix), const(win), const(qg), const(kvg), const(wq),
                  tabq_spec, pl.BlockSpec((tm, 2 * QK_ROPE), lambda b, i: (i, 0))] + [const(a) for a in extra],
        out_specs=out_specs,
        out_shape=out_shape,
        compiler_params=_params(("arbitrary", "arbitrary")),
        name="inproj_prompt" if q_transposed else "inproj_sample",
    )(x3, mod3, mod3, gmix, win, qg, kvg, wq, tabq, tabk, *extra)


def _ssd_reset(c, xbuf, hT):
    @pl.when(c == 0)
    def _():
        xbuf[0:8, :] = jnp.zeros((8, CONV_DIM), F32)
        hT[...] = jnp.zeros(hT.shape, F32)


def _ssd_final(c, n_chunks, hfin_ref, hT):
    @pl.when(c == n_chunks - 1)
    def _():
        for k in range(SSD_HEADS // 2):
            hfin_ref[k * LANES:(k + 1) * LANES, :] = hT[k].T


def _ssd_body(*refs, **kw):
    for _ in _ssd_steps(*refs, **kw):
        pass


def _ssd_steps(*refs, Q, seg, carry, chunk_pos=None):
    if carry:
        (xin_ref, z_ref, dt_ref, cw_ref, cb_ref, dtb_ref, alog_ref, dsk_ref, g_ref,
         y_ref, hfin_ref, xbuf, hT) = refs
    else:
        (xin_ref, dt_ref, cw_ref, cb_ref, dtb_ref, alog_ref, dsk_ref,
         ypre_ref, eacs_ref, xw_ref, dec_ref, bm_ref, cm_ref) = refs
    cw = cw_ref[...]
    cb = cb_ref[...]
    if carry:
        c, n_chunks = (pl.program_id(1), pl.num_programs(1)) if chunk_pos is None else chunk_pos
        if chunk_pos is None:
            _ssd_reset(c, xbuf, hT)
        xbuf[8:8 + Q, :] = xin_ref[...]
    xc_parts = []
    for lb in range(CONV_DIM // CONV_LANES):
        cols = slice(lb * CONV_LANES, (lb + 1) * CONV_LANES)
        acc = jnp.broadcast_to(cb[:, cols], (Q, CONV_LANES))
        for k in range(CONV_WIDTH):
            if carry:
                tap = xbuf[pl.ds(8 - (CONV_WIDTH - 1) + k, Q), cols]
            else:
                tap = xin_ref[k, :, cols]
            acc = acc + cw[k:k + 1, cols] * tap
        xc_parts.append(_silu(acc))
        yield
    if carry:
        xbuf[0:8, :] = xbuf[Q:Q + 8, :]
    xc = jnp.concatenate(xc_parts, axis=1)
    xs = xc[:, :SSD_INNER]
    gs = SSD_GROUPS * SSD_STATE
    bm = xc[:, SSD_INNER:SSD_INNER + gs]
    cm = xc[:, SSD_INNER + gs:]
    bm_b = bm.astype(BF16)
    cm_b = cm.astype(BF16)

    lane = lax.broadcasted_iota(jnp.int32, (Q, LANES), 1)
    v = dt_ref[...] + dtb_ref[...]
    dt = jnp.maximum(v, 0.0) + jnp.log1p(jnp.exp(-jnp.abs(v)))
    dt = jnp.where(lane < SSD_HEADS, dt, 0.0)
    dA = dt * (-jnp.exp(alog_ref[...]))
    ri = lax.broadcasted_iota(jnp.int32, (Q, Q), 0)
    ci = lax.broadcasted_iota(jnp.int32, (Q, Q), 1)
    if seg == Q:
        mask = ci <= ri
    else:
        same = (ri // seg) == (ci // seg)
        mask = jnp.logical_and(same, ci <= ri)
    acs = _split3_dot(jnp.where(mask, 1.0, 0.0).astype(BF16), dA)
    if seg == Q:
        acs_last = acs[Q - 1:Q, :]
    else:
        acs_last = _split3_dot(jnp.where(same, 1.0, 0.0).astype(BF16), dA)
    to_end = jnp.exp(acs_last - acs) * dt
    acsT = acs.T
    dtT = dt.T
    yield

    G = [_dot_nt(cm_b[:, g * SSD_STATE:(g + 1) * SSD_STATE], bm_b[:, g * SSD_STATE:(g + 1) * SSD_STATE])
         for g in range(SSD_GROUPS)]
    lane_lo = lane < SSD_HEAD_DIM
    heads_per_group = SSD_HEADS // SSD_GROUPS
    ypairs, epairs, xwpairs, decpairs = [], [], [], []
    for k in range(SSD_HEADS // 2):
        g = (2 * k) // heads_per_group
        xp = xs[:, k * LANES:(k + 1) * LANES]
        xhalf = (jnp.where(lane_lo, xp, 0.0).astype(BF16), jnp.where(lane_lo, 0.0, xp).astype(BF16))
        yk = jnp.zeros((Q, LANES), F32)
        for s in range(2):
            hh = 2 * k + s
            segm = acs[:, hh:hh + 1] - acsT[hh:hh + 1, :]
            m = G[g] * jnp.exp(jnp.where(mask, segm, -jnp.inf)) * dtT[hh:hh + 1, :]
            yk = yk + _dot(m.astype(BF16), xhalf[s])
            yield

        def pair(a):
            return jnp.where(lane_lo[:a.shape[0]], a[:, 2 * k:2 * k + 1], a[:, 2 * k + 1:2 * k + 2])

        e_p = jnp.exp(pair(acs))
        xw = xp * pair(to_end)
        dec = jnp.exp(pair(acs_last))
        if carry:
            h_prev = hT[k]
            yk = yk + _dot(cm_b[:, g * SSD_STATE:(g + 1) * SSD_STATE], h_prev.astype(BF16)) * e_p
            hT[k] = dec * h_prev + _dot_tn(bm_b[:, g * SSD_STATE:(g + 1) * SSD_STATE], xw.astype(BF16))
        else:
            epairs.append(e_p)
            xwpairs.append(xw)
            decpairs.append(dec)
        ypairs.append(yk)
        yield
    y = jnp.concatenate(ypairs, axis=1) + dsk_ref[...] * xs
    if carry:
        y = y * _silu(z_ref[...])
        y_ref[...] = _rms(y, g_ref[...]).astype(BF16)

        if chunk_pos is None:
            _ssd_final(c, n_chunks, hfin_ref, hT)
    else:
        ypre_ref[...] = y
        eacs_ref[...] = jnp.concatenate(epairs, axis=1)
        xw_ref[...] = jnp.concatenate(xwpairs, axis=1)
        dec_ref[...] = jnp.concatenate(decpairs, axis=1)
        bm_ref[...] = bm
        cm_ref[...] = cm


def _row(a, n):
    return jnp.pad(a.reshape(1, -1).astype(F32), ((0, 0), (0, n - a.size)))


def _ssd_prompt(xbc, z, dt, cw, cb, dtb, alog, dsk, g):
    nb, L, _ = xbc.shape
    Q = SSD_CHUNK if L % SSD_CHUNK == 0 else L
    nc = L // Q

    def tok(width):
        return pl.BlockSpec((None, Q, width), lambda b, c: (b, c, 0))

    def const(a):
        return pl.BlockSpec(a.shape, lambda b, c: (0,) * a.ndim)

    consts = (cw, cb, dtb, alog, dsk, g)
    return pl.pallas_call(
        functools.partial(_ssd_body, Q=Q, seg=Q, carry=True),
        grid=(nb, nc),
        in_specs=[tok(CONV_DIM), tok(SSD_INNER), tok(LANES)] + [const(a) for a in consts],
        out_specs=[tok(SSD_INNER), pl.BlockSpec((None, SSD_INNER, SSD_STATE), lambda b, c: (b, 0, 0))],
        out_shape=[SDS((nb, L, SSD_INNER), BF16), SDS((nb, SSD_INNER, SSD_STATE), F32)],
        scratch_shapes=[pltpu.VMEM((Q + 8, CONV_DIM), F32), pltpu.VMEM((SSD_HEADS // 2, SSD_STATE, LANES), F32)],
        compiler_params=_params(("arbitrary", "arbitrary")),
        name="ssd_prompt",
    )(xbc, z, dt, *consts)


def _ssd_sample(xsh, dt, cw, cb, dtb, alog, dsk, *, seg):
    _, T, _ = xsh.shape
    consts = (cw, cb, dtb, alog, dsk)

    def full(a):
        return pl.BlockSpec(a.shape, lambda i: (0,) * a.ndim)

    outs = [SDS((T, SSD_INNER), F32)] * 4 + [SDS((T, SSD_GROUPS * SSD_STATE), F32)] * 2
    return pl.pallas_call(
        functools.partial(_ssd_body, Q=T, seg=seg, carry=False),
        grid=(1,),
        in_specs=[full(xsh), full(dt)] + [full(a) for a in consts],
        out_specs=[full(o) for o in outs],
        out_shape=outs,
        compiler_params=_params(("arbitrary",)),
        name="ssd_sample",
    )(xsh, dt, *consts)


def _sstate_body(cm_ref, bm_ref, ypre_ref, eacs_ref, xw_ref, dec_ref, z_ref, s0_ref, g_ref, y_ref, sn_ref):
    s0 = s0_ref[...]
    s0b = s0.astype(BF16)
    cm = cm_ref[...].astype(BF16)
    bm = bm_ref[...].astype(BF16)
    rows = SSD_INNER // SSD_GROUPS
    yo = jnp.concatenate(
        [jnp.einsum("btn,bqn->btq", cm[:, :, g * SSD_STATE:(g + 1) * SSD_STATE],
                    s0b[:, g * rows:(g + 1) * rows, :], preferred_element_type=F32)
         for g in range(SSD_GROUPS)], axis=-1)
    y = (ypre_ref[...] + yo * eacs_ref[...]) * _silu(z_ref[...])
    y_ref[...] = _rms(y, g_ref[...]).astype(BF16)
    dec = dec_ref[...]
    hi = dec.astype(BF16)
    lo = (dec - hi.astype(F32)).astype(BF16)
    sel = jnp.where(lax.broadcasted_iota(jnp.int32, (dec.shape[0], dec.shape[1], SSD_STATE), 1) == 0,
                    1.0, 0.0).astype(BF16)
    dmat = (jnp.einsum("bjq,bjn->bqn", hi, sel, preferred_element_type=F32)
            + jnp.einsum("bjq,bjn->bqn", lo, sel, preferred_element_type=F32))
    xw = xw_ref[...].astype(BF16)
    upd = jnp.concatenate(
        [jnp.einsum("bjq,bjn->bqn", xw[:, :, g * rows:(g + 1) * rows],
                    bm[:, :, g * SSD_STATE:(g + 1) * SSD_STATE], preferred_element_type=F32)
         for g in range(SSD_GROUPS)], axis=1)
    sn_ref[...] = dmat * s0 + upd


def _sstate(cm, bm, ypre, eacs, xw, dec, z, s0, g, *, bs):
    nseq, t, _ = cm.shape
    bs = min(bs, nseq)

    def blk(a):
        return pl.BlockSpec((bs,) + a.shape[1:], lambda i: (i, 0, 0))

    ins = (cm, bm, ypre, eacs, xw, dec, z, s0)
    return pl.pallas_call(
        _sstate_body,
        grid=(nseq // bs,),
        in_specs=[blk(a) for a in ins] + [pl.BlockSpec(g.shape, lambda i: (0, 0))],
        out_specs=[blk(ypre), blk(s0)],
        out_shape=[SDS(ypre.shape, BF16), SDS(s0.shape, F32)],
        compiler_params=_params(("arbitrary",)),
        name="sstate",
    )(*ins, g)


def _attn_body(qt_ref, qn_ref, kp_ref, vt_ref, g_ref, o_ref, m_sc, acc_sc,
               s_a, s_b, s_c, smax_a, smax_b, smax_c, *, tk):
    i = pl.program_id(1)
    ncols = MLA_HEADS * TQ
    m_sc[...] = jnp.full(m_sc.shape, -jnp.inf, F32)
    acc_sc[...] = jnp.zeros(acc_sc.shape, F32)
    ct = 2 * TQ
    tiles = [slice(p * ct, (p + 1) * ct) for p in range(PAIRS)]

    def scores(j, s_ref, smax_ref, q_ref=qt_ref):
        k0 = pl.multiple_of(j * tk, tk)
        for p, cs in enumerate(tiles):
            s = _dot(kp_ref[p, pl.ds(k0, tk), :], q_ref[p])
            s_ref[:, cs] = s
            smax_ref[:, cs] = jnp.max(s, axis=0, keepdims=True)

    def softmax_pv(j, s_ref, smax_ref, masked):
        for p, cs in enumerate(tiles):
            s = s_ref[:, cs]
            if masked:
                key = j * tk + lax.broadcasted_iota(jnp.int32, (tk, ct), 0)
                tok = i * TQ + (lax.broadcasted_iota(jnp.int32, (tk, ct), 1) & (TQ - 1))
                s = jnp.where(key <= tok, s, -jnp.inf)
                smax = jnp.max(s, axis=0, keepdims=True)
            else:
                smax = smax_ref[:, cs]
            m_prev = m_sc[:, cs]
            m_new = jnp.maximum(m_prev, smax)
            alpha = jnp.exp2(m_prev - m_new)
            e = jnp.exp2(s - m_new)
            acc_sc[p] = alpha * acc_sc[p] + _dot(vt_ref[j, p], e.astype(BF16))
            m_sc[:, cs] = m_new

    nfull = (i * TQ) // tk

    @pl.when(i == 0)
    def _():
        scores(0, s_c, smax_c)

    def prefetch():
        scores(0, s_c, smax_c, qn_ref)

    @pl.when(nfull == 0)
    def _():
        softmax_pv(0, s_c, smax_c, True)
        prefetch()

    @pl.when(nfull >= 1)
    def _():
        scores(1, s_b, smax_b)
        softmax_pv(0, s_c, smax_c, False)

    def pair(p, carry):
        j = 2 * p + 1
        scores(j + 1, s_a, smax_a)
        softmax_pv(j, s_b, smax_b, False)
        scores(j + 2, s_b, smax_b)
        softmax_pv(j + 1, s_a, smax_a, False)
        return carry

    lax.fori_loop(0, jnp.maximum(nfull - 1, 0) // 2, pair, 0)
    odd = (nfull % 2) == 1

    @pl.when(odd)
    def _():
        prefetch()
        softmax_pv(nfull, s_b, smax_b, True)

    @pl.when(jnp.logical_and(jnp.logical_not(odd), nfull >= 2))
    def _():
        scores(nfull, s_a, smax_a)
        softmax_pv(nfull - 1, s_b, smax_b, False)
        prefetch()
        softmax_pv(nfull, s_a, smax_a, True)

    ys = []
    for hh in range(MLA_HEADS):
        p, s = divmod(hh, 2)
        cols = slice(s * TQ, (s + 1) * TQ)
        denom = acc_sc[p, 2 * V_DIM:2 * V_DIM + 1, cols]
        ys.append(acc_sc[p, s * V_DIM:(s + 1) * V_DIM, cols] * (1.0 / denom))
    yt = jnp.concatenate(ys, axis=0)
    yt = yt * lax.rsqrt(jnp.mean(yt * yt, axis=0, keepdims=True) + EPS)
    o_ref[...] = (yt.T * g_ref[...]).astype(BF16)


def _attn_prompt(qt, kp, vt, g, *, tk):
    nb, nq = qt.shape[:2]
    L = nq * TQ
    ncols = MLA_HEADS * TQ
    assert TQ & (TQ - 1) == 0 and L % tk == 0 and tk % TQ == 0

    def q_spec(index):
        return pl.BlockSpec((None, None, PAIRS, PAIR_K, 2 * TQ), lambda b, i: (b, index(i), 0, 0, 0))

    return pl.pallas_call(
        functools.partial(_attn_body, tk=tk),
        grid=(nb, nq),
        in_specs=[q_spec(lambda i: i), q_spec(lambda i: jnp.minimum(i + 1, nq - 1)),
                  pl.BlockSpec((None, PAIRS, L, PAIR_K), lambda b, i: (b, 0, 0, 0), pipeline_mode=pl.Buffered(1)),
                  pl.BlockSpec((None, L // tk, PAIRS, PAIR_V, tk), lambda b, i: (b, 0, 0, 0, 0),
                               pipeline_mode=pl.Buffered(1)),
                  pl.BlockSpec(g.shape, lambda b, i: (0, 0))],
        out_specs=pl.BlockSpec((None, TQ, MLA_INNER), lambda b, i: (b, i, 0)),
        out_shape=SDS((nb, L, MLA_INNER), BF16),
        scratch_shapes=[pltpu.VMEM((1, ncols), F32),
                        pltpu.VMEM((PAIRS, PAIR_V, 2 * TQ), F32),
                        pltpu.VMEM((tk, ncols), F32), pltpu.VMEM((tk, ncols), F32), pltpu.VMEM((tk, ncols), F32),
                        pltpu.VMEM((1, ncols), F32), pltpu.VMEM((1, ncols), F32), pltpu.VMEM((1, ncols), F32)],
        compiler_params=_params(("arbitrary", "arbitrary")),
        name="attn_prompt",
    )(qt, qt, kp, vt, g)


def _sattn_body(pt_ref, qa_ref, qr_ref, kn_ref, rn_ref, ckv_hbm, krt_hbm, o_ref, kbuf, rbuf, sem, *, npages, t_new):
    slot = _sattn_fetch(pt_ref, ckv_hbm, krt_hbm, kbuf, rbuf, sem, npages=npages)
    for _ in _sattn_steps(slot, qa_ref, qr_ref, kn_ref, rn_ref, o_ref, kbuf, rbuf, npages=npages, t_new=t_new,
                          split=SPLIT):
        pass


def _sattn_fetch(pt_ref, ckv_hbm, krt_hbm, kbuf, rbuf, sem, *, npages):
    b = pl.program_id(0)
    nseq = pl.num_programs(0)

    def copies(seq_page, slot, p):
        off = p * PAGE_SIZE
        return (pltpu.make_async_copy(ckv_hbm.at[seq_page], kbuf.at[slot, p], sem.at[0, slot]),
                pltpu.make_async_copy(krt_hbm.at[seq_page], rbuf.at[slot, :, pl.ds(off, PAGE_SIZE)],
                                      sem.at[1, slot]))

    def start_fetch(seq, slot):
        for p in range(npages):
            for cp in copies(pt_ref[seq * npages + p], slot, p):
                cp.start()

    def wait_fetch(slot):
        pltpu.make_async_copy(ckv_hbm.at[pl.ds(0, npages)], kbuf.at[slot], sem.at[0, slot]).wait()
        pltpu.make_async_copy(rbuf.at[slot], rbuf.at[slot], sem.at[1, slot]).wait()

    @pl.when(b == 0)
    def _():
        start_fetch(0, 0)

    @pl.when(b + 1 < nseq)
    def _():
        start_fetch(b + 1, (b + 1) % 2)

    slot = b % 2
    wait_fetch(slot)
    return slot


def _sattn_steps(slot, qa_ref, qr_ref, kn_ref, rn_ref, o_ref, kbuf, rbuf, *, npages, t_new, split):
    split = math.gcd(split, npages)
    grp = math.gcd(SPLIT, split)
    pg = npages // split
    part = pg * PAGE_SIZE
    q = qa_ref[...]
    qr = qr_ref[...]
    kn = kn_ref[...]
    rn = rn_ref[...]
    kps, s_p = [], []
    for h0 in range(0, split, grp):
        hs = range(h0, h0 + grp)
        kps += [kbuf[slot, pl.ds(h * pg, pg)].reshape(part, KV_RANK).astype(BF16) for h in hs]
        s_main = [_dot_nt(q, kps[h]) for h in hs]
        s_rope = [_dot(qr, rbuf[slot, :, h * part:(h + 1) * part].astype(BF16)) for h in hs]
        s_p += [a + b for a, b in zip(s_main, s_rope)]
        yield
    s_n = _dot_nt(q, kn) + _dot_nt(qr, rn)
    rows = t_new * MLA_HEADS
    tok = lax.broadcasted_iota(jnp.int32, (rows, t_new), 0) // MLA_HEADS
    col = lax.broadcasted_iota(jnp.int32, (rows, t_new), 1)
    s_n = jnp.where(col <= tok, s_n, -jnp.inf)
    m = jnp.max(s_n, axis=-1, keepdims=True)
    for s in s_p:
        m = jnp.maximum(m, jnp.max(s, axis=-1, keepdims=True))
    pn = jnp.exp(s_n - m)
    pp = [jnp.exp(s - m) for s in s_p]
    l = jnp.sum(pn, axis=-1, keepdims=True)
    for p in pp:
        l = l + jnp.sum(p, axis=-1, keepdims=True)
    inv = 1.0 / l
    o = _dot(pn.astype(BF16), kn) * inv
    yield
    for h0 in range(0, split, grp):
        parts = [_dot(pp[h].astype(BF16), kps[h]) for h in range(h0, h0 + grp)]
        for part_o in parts:
            o = o + part_o * inv
        yield
    o_ref[...] = o


def _sattn(page_table, qa, qr, kn, rn, cache_kv, cache_krt):
    nseq, npages = page_table.shape
    t_new = kn.shape[1]
    rows = t_new * MLA_HEADS
    grid_spec = pltpu.PrefetchScalarGridSpec(
        num_scalar_prefetch=1,
        grid=(nseq,),
        in_specs=[pl.BlockSpec((rows, KV_RANK), lambda b, pt: (b, 0)),
                  pl.BlockSpec((rows, QK_ROPE), lambda b, pt: (b, 0)),
                  pl.BlockSpec((None, t_new, KV_RANK), lambda b, pt: (b, 0, 0)),
                  pl.BlockSpec((None, t_new, QK_ROPE), lambda b, pt: (b, 0, 0)),
                  pl.BlockSpec(memory_space=pl.ANY),
                  pl.BlockSpec(memory_space=pl.ANY)],
        out_specs=pl.BlockSpec((rows, KV_RANK), lambda b, pt: (b, 0)),
        scratch_shapes=[pltpu.VMEM((2, npages, PAGE_SIZE, KV_RANK), F32),
                        pltpu.VMEM((2, QK_ROPE, npages * PAGE_SIZE), F32),
                        pltpu.SemaphoreType.DMA((2, 2))],
    )
    return pl.pallas_call(
        functools.partial(_sattn_body, npages=npages, t_new=t_new),
        grid_spec=grid_spec,
        out_shape=SDS((nseq * rows, KV_RANK), F32),
        compiler_params=_params(("arbitrary",)),
        name="attn_sample",
    )(page_table.reshape(-1), qa, qr, kn, rn, cache_kv, cache_krt)


def _ssd_sattn_body(pt_ref, qa_ref, qr_ref, kn_ref, rn_ref, ckv_hbm, krt_hbm,
                    xin_ref, z_ref, dt_ref, cw_ref, cb_ref, dtb_ref, alog_ref, dsk_ref, g_ref,
                    o_ref, y_ref, hfin_ref, kbuf, rbuf, sem, xbuf, hT, *, npages, t_new, Q, n_chunks):
    c = pl.program_id(0) % n_chunks
    _ssd_reset(c, xbuf, hT)
    slot = _sattn_fetch(pt_ref, ckv_hbm, krt_hbm, kbuf, rbuf, sem, npages=npages)
    ssd = _ssd_steps(xin_ref, z_ref, dt_ref, cw_ref, cb_ref, dtb_ref, alog_ref, dsk_ref, g_ref, y_ref, hfin_ref,
                     xbuf, hT, Q=Q, seg=Q, carry=True, chunk_pos=(c, n_chunks))
    att = _sattn_steps(slot, qa_ref, qr_ref, kn_ref, rn_ref, o_ref, kbuf, rbuf, npages=npages, t_new=t_new,
                       split=FUSED_SPLIT)
    done = object()
    live = [att, ssd]
    while live:
        live = [gen for gen in live if next(gen, done) is not done]
    _ssd_final(c, n_chunks, hfin_ref, hT)


def _ssd_sattn(page_table, qa, qr, kn, rn, cache_kv, cache_krt, xbc, z, dt, cw, cb, dtb, alog, dsk, g):
    nseq, npages = page_table.shape
    t_new = kn.shape[1]
    rows = t_new * MLA_HEADS
    nb, L, _ = xbc.shape
    Q = SSD_CHUNK if L % SSD_CHUNK == 0 else L
    nc = L // Q
    assert nseq == nb * nc
    consts = (cw, cb, dtb, alog, dsk, g)

    def tok(width):
        return pl.BlockSpec((None, Q, width), lambda s, pt: (s // nc, s % nc, 0))

    def const(a):
        return pl.BlockSpec(a.shape, lambda s, pt: (0,) * a.ndim)

    grid_spec = pltpu.PrefetchScalarGridSpec(
        num_scalar_prefetch=1,
        grid=(nseq,),
        in_specs=[pl.BlockSpec((rows, KV_RANK), lambda s, pt: (s, 0)),
                  pl.BlockSpec((rows, QK_ROPE), lambda s, pt: (s, 0)),
                  pl.BlockSpec((None, t_new, KV_RANK), lambda s, pt: (s, 0, 0)),
                  pl.BlockSpec((None, t_new, QK_ROPE), lambda s, pt: (s, 0, 0)),
                  pl.BlockSpec(memory_space=pl.ANY),
                  pl.BlockSpec(memory_space=pl.ANY),
                  tok(CONV_DIM), tok(SSD_INNER), tok(LANES)] + [const(a) for a in consts],
        out_specs=[pl.BlockSpec((rows, KV_RANK), lambda s, pt: (s, 0)),
                   tok(SSD_INNER),
                   pl.BlockSpec((None, SSD_INNER, SSD_STATE), lambda s, pt: (s // nc, 0, 0))],
        scratch_shapes=[pltpu.VMEM((2, npages, PAGE_SIZE, KV_RANK), F32),
                        pltpu.VMEM((2, QK_ROPE, npages * PAGE_SIZE), F32),
                        pltpu.SemaphoreType.DMA((2, 2)),
                        pltpu.VMEM((Q + 8, CONV_DIM), F32),
                        pltpu.VMEM((SSD_HEADS // 2, SSD_STATE, LANES), F32)],
    )
    return pl.pallas_call(
        functools.partial(_ssd_sattn_body, npages=npages, t_new=t_new, Q=Q, n_chunks=nc),
        grid_spec=grid_spec,
        out_shape=[SDS((nseq * rows, KV_RANK), F32), SDS((nb, L, SSD_INNER), BF16),
                   SDS((nb, SSD_INNER, SSD_STATE), F32)],
        compiler_params=_params(("arbitrary",)),
        name="ssd_prompt_attn_sample",
    )(page_table.reshape(-1), qa, qr, kn, rn, cache_kv, cache_krt, xbc, z, dt, *consts)


def _apost_body(o_ref, wuv_ref, g_ref, y_ref):
    tm = o_ref.shape[0]
    o = jnp.zeros((tm, MLA_INNER), F32)
    for hh in range(MLA_HEADS):
        o = o + _dot(o_ref[:, hh * KV_RANK:(hh + 1) * KV_RANK].astype(BF16), wuv_ref[hh])
    y_ref[...] = _rms(o, g_ref[...]).astype(BF16)


def _apost(o_lat, wuv_pad, g):
    T = o_lat.shape[0]
    return pl.pallas_call(
        _apost_body,
        grid=(1,),
        in_specs=[pl.BlockSpec(o_lat.shape, lambda i: (0, 0)),
                  pl.BlockSpec(wuv_pad.shape, lambda i: (0, 0, 0)),
                  pl.BlockSpec(g.shape, lambda i: (0, 0))],
        out_specs=pl.BlockSpec((T, MLA_INNER), lambda i: (0, 0)),
        out_shape=SDS((T, MLA_INNER), BF16),
        compiler_params=_params(("arbitrary",)),
        name="attn_post",
    )(o_lat, wuv_pad, g)


def _mlp_body(x_ref, ys_ref, ya_ref, g1_ref, sh2_ref, sc2_ref, g2_ref, shf_ref, scf_ref,
              wout_ref, gmlp_ref, wup_ref, wdn_ref, gfin_ref, o_ref, *, final, tf, rep):
    yy = jnp.concatenate([ys_ref[...], ya_ref[...]], axis=-1)
    x1 = x_ref[...] + _mod_rows(g1_ref, rep) * _dot(yy, wout_ref[...])
    h2 = (_rms(x1, gmlp_ref[...]) * (1.0 + _mod_rows(sc2_ref, rep)) + _mod_rows(sh2_ref, rep)).astype(BF16)
    acc = jnp.zeros(x1.shape, F32)
    for c in range(D_FF // tf):
        u = jnp.maximum(_dot(h2, wup_ref[:, c * tf:(c + 1) * tf]), 0.0)
        acc = acc + _dot((u * u).astype(BF16), wdn_ref[c * tf:(c + 1) * tf, :])
    x2 = x1 + _mod_rows(g2_ref, rep) * acc
    if final:
        x2 = _rms(x2, gfin_ref[...]) * (1.0 + _mod_rows(scf_ref, rep)) + _mod_rows(shf_ref, rep)
    o_ref[...] = x2


def _mlp(x3, ys, ya, mod3, wout, gmlp, wup, wdn, gfin, *, final, tm, tf):
    nb, L, d = x3.shape
    tm = min(tm, L)
    rep, mod_spec = _mod_spec(mod3, L, tm, d)

    def tok(width):
        return pl.BlockSpec((None, tm, width), lambda b, i: (b, i, 0))

    def const(a):
        return pl.BlockSpec(a.shape, lambda b, i: (0,) * a.ndim, pipeline_mode=pl.Buffered(1))

    return pl.pallas_call(
        functools.partial(_mlp_body, final=final, tf=tf, rep=rep),
        grid=(nb, L // tm),
        in_specs=[tok(d), tok(SSD_INNER), tok(MLA_INNER)] + [mod_spec(k) for k in (2, 3, 4, 5, 6, 7)]
                 + [const(wout), const(gmlp), const(wup), const(wdn), const(gfin)],
        out_specs=tok(d),
        out_shape=SDS((nb, L, d), F32),
        compiler_params=_params(("arbitrary", "arbitrary")),
        name="mlp",
    )(x3, ys, ya, mod3, mod3, mod3, mod3, mod3, mod3, wout, gmlp, wup, wdn, gfin)


def _rope_tables(pos):
    inv = 1.0 / (ROPE_THETA ** (np.arange(0, QK_ROPE, 2, dtype=np.float64) / QK_ROPE))
    ang = pos.astype(np.float64)[:, None] * inv[None, :]
    cos, sin = np.cos(ang).astype(np.float32), np.sin(ang).astype(np.float32)
    c32 = np.concatenate([cos, cos], axis=-1)
    s32 = np.concatenate([-sin, sin], axis=-1)
    tab1 = np.concatenate([c32, s32], axis=-1)
    tabq = np.concatenate([np.tile(c32, (1, MLA_HEADS)), np.tile(s32, (1, MLA_HEADS))], axis=-1)
    return tab1, tabq


def _swap_halves(w):
    half = w.shape[-1] // 2
    return jnp.concatenate([w[..., half:], w[..., :half]], axis=-1)


def kernel(x_prompt, x_sample, cache_kv_latent, cache_k_rope, state_conv, state_ssm, page_table,
           c_prompt, c_sample, w_ada, b_ada, norm_mix_g, w_in, conv_w, conv_b, dt_bias, a_log,
           d_skip, norm_ssd_g, q_norm_g, kv_norm_g, w_uq, w_uk, w_uv, norm_attn_g, w_out,
           norm_mlp_g, w_up, w_down, w_ada_final, b_ada_final, norm_final_g):
    depth = w_in.shape[0]
    b_p, seq, d = x_prompt.shape
    n_seq, t_new, _ = x_sample.shape
    n_tok_s = n_seq * t_new
    past_len = page_table.shape[1] * PAGE_SIZE

    c_all = jnp.concatenate([c_prompt, c_sample], axis=0)
    ada_fin = _ada(c_all, w_ada_final, b_ada_final)
    tab1_p, _ = _rope_tables(np.arange(seq))
    tab1_s, tabq_s = _rope_tables(past_len + np.arange(t_new))
    tab1_pt = jnp.asarray(np.ascontiguousarray(tab1_p.T))
    tab1_p = jnp.asarray(tab1_p)
    tabq_s = jnp.asarray(np.tile(tabq_s, (n_seq, 1)))
    tab1_s = jnp.asarray(np.tile(tab1_s, (n_seq, 1)))

    xp = x_prompt
    xs = x_sample.reshape(1, n_tok_s, d)
    outs_p, outs_s = [], []
    for l in range(depth):
        final = l == depth - 1
        wi = w_in[l]
        c1 = SSD_INNER
        c2 = c1 + CONV_DIM
        c3 = c2 + SSD_HEADS
        c4 = c3 + Q_RANK
        c5 = c4 + KV_RANK
        w_kr = wi[:, c5:]
        win = jnp.concatenate(
            [wi[:, :c2], wi[:, c3:c5], wi[:, c2:c3], jnp.zeros((d, MISC_KR - SSD_HEADS), F32), w_kr,
             _swap_halves(w_kr), jnp.zeros((d, LANES - MISC_KRSW - QK_ROPE), F32)], axis=1).astype(BF16)
        wq_h = w_uq[l].reshape(Q_RANK, MLA_HEADS, QK_NOPE + QK_ROPE)
        w_rope = wq_h[:, :, QK_NOPE:]
        wfold = _fold(jnp.transpose(wq_h, (1, 0, 2)), jnp.transpose(w_uk[l], (1, 0, 2)))
        wq = jnp.concatenate([wfold, w_rope.reshape(Q_RANK, -1).astype(BF16),
                              _swap_halves(w_rope).reshape(Q_RANK, -1).astype(BF16)], axis=1)
        wuv_pad = jnp.zeros((MLA_HEADS, KV_RANK, MLA_HEADS, V_DIM), F32)
        wuv_pad = wuv_pad.at[jnp.arange(MLA_HEADS), :, jnp.arange(MLA_HEADS), :].set(
            jnp.transpose(w_uv[l], (1, 0, 2)))
        wuv_pad = wuv_pad.reshape(MLA_HEADS, KV_RANK, MLA_INNER).astype(BF16)
        wq_p = jnp.concatenate([wq_h[:, :, :QK_NOPE].reshape(Q_RANK, -1), w_rope.reshape(Q_RANK, -1),
                                _swap_halves(w_rope).reshape(Q_RANK, -1)], axis=1).T.astype(BF16)
        wk = w_uk[l].reshape(KV_RANK, MLA_HEADS * QK_NOPE).astype(BF16)
        wvt = jnp.transpose(w_uv[l], (1, 2, 0)).reshape(PAIRS, 2 * V_DIM, KV_RANK).astype(BF16)
        wout = w_out[l].astype(BF16)
        wup = w_up[l].astype(BF16)
        wdn = w_down[l].astype(BF16)
        gmix = norm_mix_g[l].reshape(1, d)
        gmlp = norm_mlp_g[l].reshape(1, d)
        gfin = norm_final_g.reshape(1, d)
        qg = q_norm_g[l].reshape(1, Q_RANK)
        kvg = kv_norm_g[l].reshape(1, KV_RANK)
        gssd = norm_ssd_g[l].reshape(1, SSD_INNER)
        gattn = norm_attn_g[l].reshape(1, MLA_INNER)
        cw = conv_w[l]
        cb = conv_b[l].reshape(1, CONV_DIM)
        dtb = _row(dt_bias[l], LANES)
        alog = _row(a_log[l], LANES)
        dsk = jnp.repeat(d_skip[l].astype(F32), SSD_HEAD_DIM).reshape(1, SSD_INNER)

        ada = _ada(c_all, w_ada[l], b_ada[l])
        mod = jnp.concatenate([ada, ada_fin], axis=1)
        mod_p = mod[:b_p].reshape(b_p, 1, 8 * d)
        mod_s = mod[b_p:].reshape(1, n_seq, 8 * d)

        def seqs(a):
            return a.reshape(n_seq, t_new, a.shape[-1])

        tk = min(512, seq)
        z_p, xbc_p, dtr_p, ckv_p, kr_p, kp, vt, qt = _inproj(
            xp, mod_p, gmix, win, qg, kvg, wq_p, tab1_pt, tab1_p, q_transposed=True, tm=512, tk=tk,
            wk=wk, wvt=wvt)
        z_s, xbc_s, dtr_s, ckv_s, kr_s, kc, krb, qa, qr = _inproj(
            xs, mod_s, gmix, win, qg, kvg, wq, tabq_s, tab1_s, q_transposed=False, tm=256)
        xpad = jnp.concatenate([state_conv[l], xbc_s.reshape(n_seq, t_new, CONV_DIM)], axis=1)
        xsh = jnp.stack([xpad[:, k:k + t_new].reshape(n_tok_s, CONV_DIM) for k in range(CONV_WIDTH)])
        ypre, eacs, xw, dec, bm, cm = _ssd_sample(xsh, dtr_s[0], cw, cb, dtb, alog, dsk, seg=t_new)
        y_ssd_s, s_new = _sstate(seqs(cm), seqs(bm), seqs(ypre), seqs(eacs), seqs(xw), seqs(dec), seqs(z_s[0]),
                                 state_ssm[l].reshape(n_seq, SSD_INNER, SSD_STATE), gssd, bs=8)

        sattn_args = (page_table, qa.reshape(n_tok_s * MLA_HEADS, KV_RANK),
                      qr.reshape(n_tok_s * MLA_HEADS, QK_ROPE), seqs(kc[0]), seqs(krb[0]),
                      cache_kv_latent[l], jnp.swapaxes(cache_k_rope[l], -1, -2))
        ssd_args = (xbc_p, z_p, dtr_p, cw, cb, dtb, alog, dsk, gssd)
        chunk = SSD_CHUNK if seq % SSD_CHUNK == 0 else seq
        if n_seq == b_p * (seq // chunk):
            o_lat, y_ssd_p, hfin = _ssd_sattn(*sattn_args, *ssd_args)
        else:
            y_ssd_p, hfin = _ssd_prompt(*ssd_args)
            o_lat = _sattn(*sattn_args)

        y_attn_p = _attn_prompt(qt, kp, vt, gattn, tk=tk)
        xp = _mlp(xp, y_ssd_p, y_attn_p, mod_p, wout, gmlp, wup, wdn, gfin, final=final, tm=512, tf=2048)
        tail = min(seq, CONV_WIDTH - 1)
        conv_tail = jnp.concatenate([jnp.zeros((b_p, CONV_WIDTH - 1 - tail, CONV_DIM), F32),
                                     xbc_p[:, seq - tail:]], axis=1)
        outs_p.append((ckv_p, kr_p, conv_tail, hfin.reshape(b_p, SSD_HEADS, SSD_HEAD_DIM, SSD_STATE)))

        y_attn_s = _apost(o_lat.reshape(n_tok_s, MLA_HEADS * KV_RANK), wuv_pad, gattn)
        xs = _mlp(xs, y_ssd_s.reshape(1, n_tok_s, SSD_INNER), y_attn_s.reshape(1, n_tok_s, MLA_INNER), mod_s,
                  wout, gmlp, wup, wdn, gfin, final=final, tm=n_tok_s, tf=2048)
        outs_s.append((seqs(ckv_s[0]), seqs(kr_s[0]), xpad[:, t_new:],
                       s_new.reshape(n_seq, SSD_HEADS, SSD_HEAD_DIM, SSD_STATE)))

    def stack(outs, k):
        return jnp.stack([o[k] for o in outs])

    return (xp, xs.reshape(n_seq, t_new, d),
            stack(outs_p, 0), stack(outs_p, 1), stack(outs_p, 2), stack(outs_p, 3),
            stack(outs_s, 0), stack(outs_s, 1), stack(outs_s, 2), stack(outs_s, 3))
```

```python
import functools
import math

import jax
import jax.numpy as jnp
import numpy as np
from jax import lax
from jax.experimental import pallas as pl
from jax.experimental.pallas import tpu as pltpu

F32 = jnp.float32
BF16 = jnp.bfloat16
SDS = jax.ShapeDtypeStruct

D_MODEL = 1024
SSD_HEADS = 8
SSD_HEAD_DIM = 64
SSD_INNER = SSD_HEADS * SSD_HEAD_DIM
SSD_GROUPS = 2
SSD_STATE = 128
CONV_WIDTH = 4
SSD_CHUNK = 128
CONV_DIM = SSD_INNER + 2 * SSD_GROUPS * SSD_STATE
MLA_HEADS = 8
QK_NOPE = 64
QK_ROPE = 32
V_DIM = 64
KV_RANK = 256
Q_RANK = 384
MLA_INNER = MLA_HEADS * V_DIM
ROPE_THETA = 10000.0
ATTN_SCALE = 1.0 / math.sqrt(QK_NOPE + QK_ROPE)
LOG2E = math.log2(math.e)
PAGE_SIZE = 128
D_FF = 4 * D_MODEL
EPS = 1e-6

LANES = 128
TQ = 128
PAIRS = MLA_HEADS // 2
PAIR_K = 256
PAIR_V = 2 * V_DIM + 16
SPLIT = 2
FUSED_SPLIT = 32
CONV_LANES = 128
C_Z = 0
C_XBC = C_Z + SSD_INNER
C_QLAT = C_XBC + CONV_DIM
C_KVLAT = C_QLAT + Q_RANK
C_MISC = C_KVLAT + KV_RANK
W_IN_COLS = C_MISC + LANES
MISC_KR = 32
MISC_KRSW = 64
C_QROPE = MLA_HEADS * KV_RANK
C_QROPE_SW = C_QROPE + MLA_HEADS * QK_ROPE
WQ_COLS = C_QROPE_SW + MLA_HEADS * QK_ROPE

VMEM_LIMIT = 52 * 1024 * 1024

TM_INPROJ_PROMPT = 512
TM_INPROJ_SAMPLE = 256
TM_MLP = 512
TF_MLP = 2048
TK_ATTN = 512
BS_STATE = 8


def _dot(a, b):
    return jnp.dot(a, b, preferred_element_type=F32)


def _dot_nt(a, b):
    return lax.dot_general(a, b, (((1,), (1,)), ((), ())), preferred_element_type=F32)


def _dot_tn(a, b):
    return lax.dot_general(a, b, (((0,), (0,)), ((), ())), preferred_element_type=F32)


def _silu(x):
    return x * jax.nn.sigmoid(x)


def _rms(x, g):
    return x * lax.rsqrt(jnp.mean(x * x, axis=-1, keepdims=True) + EPS) * g


def _split3_dot(mask_bf16, v):
    v1 = v.astype(BF16)
    r1 = v - v1.astype(F32)
    v2 = r1.astype(BF16)
    v3 = (r1 - v2.astype(F32)).astype(BF16)
    return _dot(mask_bf16, v1) + _dot(mask_bf16, v2) + _dot(mask_bf16, v3)


def _mod_spec(mod3, L, tm, d):
    rows = mod3.shape[1]
    if rows == 1:
        return 1, lambda k: pl.BlockSpec((None, 1, d), lambda b, i: (b, 0, k))
    rep = L // rows
    return rep, lambda k: pl.BlockSpec((None, tm // rep, d), lambda b, i: (b, i, k))


def _mod_rows(ref, rep):
    m = ref[...]
    if rep == 1:
        return m
    n = m.shape[0] * rep
    sel = (lax.broadcasted_iota(jnp.int32, (n, m.shape[0]), 0) // rep
           == lax.broadcasted_iota(jnp.int32, (n, m.shape[0]), 1))
    sel = jnp.where(sel, 1.0, 0.0).astype(BF16)
    hi = m.astype(BF16)
    lo = (m - hi.astype(F32)).astype(BF16)
    return _dot(sel, hi) + _dot(sel, lo)


def _params(sem, vmem=VMEM_LIMIT):
    return pltpu.CompilerParams(dimension_semantics=sem, vmem_limit_bytes=vmem)


def _ada_body(c_ref, w_ref, b_ref, o_ref):
    s = _silu(c_ref[...]).astype(BF16)
    o_ref[...] = _dot(s, w_ref[...].astype(BF16)) + b_ref[...]


def _ada(c, w, b):
    bsz, d = c.shape
    n = w.shape[1]
    tn = 1024
    return pl.pallas_call(
        _ada_body,
        grid=(n // tn,),
        in_specs=[pl.BlockSpec((bsz, d), lambda j: (0, 0)),
                  pl.BlockSpec((d, tn), lambda j: (0, j)),
                  pl.BlockSpec((1, tn), lambda j: (0, j))],
        out_specs=pl.BlockSpec((bsz, tn), lambda j: (0, j)),
        out_shape=SDS((bsz, n), F32),
        compiler_params=_params(("arbitrary",)),
        name="ada",
    )(c, w, b.reshape(1, n))


def _fold_body(wq_ref, wk_ref, o_ref):
    a = wq_ref[:, 0:QK_NOPE].astype(BF16)
    o_ref[...] = _dot_nt(a, wk_ref[...].astype(BF16)).astype(BF16)


def _fold(wq_h, wk_h):
    return pl.pallas_call(
        _fold_body,
        grid=(MLA_HEADS,),
        in_specs=[pl.BlockSpec((None, Q_RANK, QK_NOPE + QK_ROPE), lambda h: (h, 0, 0)),
                  pl.BlockSpec((None, KV_RANK, QK_NOPE), lambda h: (h, 0, 0))],
        out_specs=pl.BlockSpec((Q_RANK, KV_RANK), lambda h: (0, h)),
        out_shape=SDS((Q_RANK, MLA_HEADS * KV_RANK), BF16),
        compiler_params=_params(("arbitrary",)),
        name="fold",
    )(wq_h, wk_h)


def _inproj_body(*refs, q_transposed, rep):
    tm = refs[0].shape[0]
    halves = 2 if (q_transposed and rep == 1 and tm % (2 * TQ) == 0) else 1
    rows = tm // halves
    gens = [_inproj_steps(refs, slice(k * rows, (k + 1) * rows), q_transposed, rep) for k in range(halves)]
    done = object()
    while gens:
        gens = [g for g in gens if next(g, done) is not done]


def _inproj_steps(refs, rs, q_transposed, rep):
    n_in = 12 if q_transposed else 10
    x_ref, sh_ref, sc_ref, gmix_ref, win_ref, qg_ref, kvg_ref, wq_ref, tq_ref, tk_ref = refs[:10]
    z_ref, xbc_ref, dt_ref, ckv_ref, kr_ref = refs[n_in:n_in + 5]
    sc, sh = _mod_rows(sc_ref, rep), _mod_rows(sh_ref, rep)
    if rep > 1:
        sc, sh = sc[rs], sh[rs]
    h = _rms(x_ref[rs, :], gmix_ref[...]) * (1.0 + sc) + sh
    proj = _dot(h.astype(BF16), win_ref[...])
    yield
    tm = proj.shape[0]
    z_ref[rs, :] = proj[:, C_Z:C_XBC]
    xbc_ref[rs, :] = proj[:, C_XBC:C_QLAT]
    q_scale = ATTN_SCALE * LOG2E if q_transposed else ATTN_SCALE
    qn = (_rms(proj[:, C_QLAT:C_KVLAT], qg_ref[...]) * q_scale).astype(BF16)
    ckv = _rms(proj[:, C_KVLAT:C_MISC], kvg_ref[...])
    ckv_ref[rs, :] = ckv
    misc = proj[:, C_MISC:W_IN_COLS]
    lane = lax.broadcasted_iota(jnp.int32, misc.shape, 1)
    dt_ref[rs, :] = jnp.where(lane < SSD_HEADS, misc, 0.0)
    tk = tk_ref[rs, :]
    kr = (misc[:, MISC_KR:MISC_KR + QK_ROPE] * tk[:, :QK_ROPE]
          + misc[:, MISC_KRSW:MISC_KRSW + QK_ROPE] * tk[:, QK_ROPE:])
    kr_ref[rs, :] = kr
    nr = MLA_HEADS * QK_ROPE
    if q_transposed:
        wk_ref, wvt_ref = refs[10:12]
        kp_ref, vt_ref, qt_ref = refs[n_in + 5:]
        ckv_b = ckv.astype(BF16)
        k_nope = _dot(ckv_b, wk_ref[...])
        kr_pad = jnp.concatenate([kr, jnp.zeros((tm, PAIR_K - 2 * QK_NOPE - QK_ROPE), F32)], axis=1).astype(BF16)
        pad_rows = PAIR_V - 2 * V_DIM
        ones_row = jnp.where(lax.broadcasted_iota(jnp.int32, (pad_rows, tm), 0) == 0, 1.0, 0.0).astype(BF16)
        for p in range(PAIRS):
            kp_ref[p, rs, 0:2 * QK_NOPE] = k_nope[:, p * 2 * QK_NOPE:(p + 1) * 2 * QK_NOPE].astype(BF16)
            kp_ref[p, rs, 2 * QK_NOPE:PAIR_K] = kr_pad
            vt_ref[p, 0:2 * V_DIM, rs] = _dot_nt(wvt_ref[p], ckv_b).astype(BF16)
            vt_ref[p, 2 * V_DIM:PAIR_V, rs] = ones_row
        yield
        qt = _dot_nt(wq_ref[...], qn)
        yield
        n0 = MLA_HEADS * QK_NOPE
        tq = tq_ref[:, rs]
        cos_t = jnp.concatenate([tq[:QK_ROPE]] * MLA_HEADS, axis=0)
        sin_t = jnp.concatenate([tq[QK_ROPE:]] * MLA_HEADS, axis=0)
        rot = (qt[n0:n0 + nr] * cos_t + qt[n0 + nr:n0 + 2 * nr] * sin_t).astype(BF16)
        q_nope = qt[:n0].astype(BF16)
        zero = jnp.zeros((PAIR_K, TQ), BF16)
        c0 = rs.start // TQ
        for c in range(tm // TQ):
            toks = slice(c * TQ, (c + 1) * TQ)
            for p in range(PAIRS):
                for s in range(2):
                    hh = 2 * p + s
                    cols = slice(s * TQ, (s + 1) * TQ)
                    qt_ref[c0 + c, p, :, cols] = zero
                    qt_ref[c0 + c, p, s * QK_NOPE:(s + 1) * QK_NOPE, cols] = (
                        q_nope[hh * QK_NOPE:(hh + 1) * QK_NOPE, toks])
                    qt_ref[c0 + c, p, 2 * QK_NOPE:2 * QK_NOPE + QK_ROPE, cols] = (
                        rot[hh * QK_ROPE:(hh + 1) * QK_ROPE, toks])
    else:
        kc_ref, krb_ref, qa_ref, qr_ref = refs[n_in + 5:]
        kc_ref[rs, :] = ckv.astype(BF16)
        krb_ref[rs, :] = kr.astype(BF16)
        q = _dot(qn, wq_ref[...])
        tq = tq_ref[rs, :]
        qa_ref[rs, :] = q[:, :C_QROPE].astype(BF16)
        qr_ref[rs, :] = (q[:, C_QROPE:C_QROPE_SW] * tq[:, :nr] + q[:, C_QROPE_SW:WQ_COLS] * tq[:, nr:]).astype(BF16)


def _inproj(x3, mod3, gmix, win, qg, kvg, wq, tabq, tabk, *, q_transposed, tm, tk=None, wk=None, wvt=None):
    nb, L, d = x3.shape
    tm = min(tm, L)
    nt = L // tm
    rep, mod_spec = _mod_spec(mod3, L, tm, d)

    def tok(width):
        return pl.BlockSpec((None, tm, width), lambda b, i: (b, i, 0))

    def const(a):
        return pl.BlockSpec(a.shape, lambda b, i: (0,) * a.ndim, pipeline_mode=pl.Buffered(1))

    nr = MLA_HEADS * QK_ROPE
    out_specs = [tok(SSD_INNER), tok(CONV_DIM), tok(LANES), tok(KV_RANK), tok(QK_ROPE)]
    out_shape = [SDS((nb, L, SSD_INNER), F32), SDS((nb, L, CONV_DIM), F32), SDS((nb, L, LANES), F32),
                 SDS((nb, L, KV_RANK), F32), SDS((nb, L, QK_ROPE), F32)]
    if q_transposed:
        assert tm % TQ == 0 and tk % tm == 0
        r = tk // tm
        tabq_spec = pl.BlockSpec((2 * QK_ROPE, tm), lambda b, i: (0, i))
        out_specs += [pl.BlockSpec((None, PAIRS, tm, PAIR_K), lambda b, i: (b, 0, i, 0)),
                      pl.BlockSpec((None, None, PAIRS, PAIR_V, tm), lambda b, i: (b, i // r, 0, 0, i % r)),
                      pl.BlockSpec((None, tm // TQ, PAIRS, PAIR_K, 2 * TQ), lambda b, i: (b, i, 0, 0, 0))]
        out_shape += [SDS((nb, PAIRS, L, PAIR_K), BF16), SDS((nb, L // tk, PAIRS, PAIR_V, tk), BF16),
                      SDS((nb, L // TQ, PAIRS, PAIR_K, 2 * TQ), BF16)]
        extra = [wk, wvt]
    else:
        extra = []
        tabq_spec = pl.BlockSpec((tm, 2 * nr), lambda b, i: (i, 0))
        out_specs += [tok(KV_RANK), tok(QK_ROPE), tok(C_QROPE), tok(nr)]
        out_shape += [SDS((nb, L, KV_RANK), BF16), SDS((nb, L, QK_ROPE), BF16),
                      SDS((nb, L, C_QROPE), BF16), SDS((nb, L, nr), BF16)]
    return pl.pallas_call(
        functools.partial(_inproj_body, q_transposed=q_transposed, rep=rep),
        grid=(nb, nt),
        in_specs=[tok(d), mod_spec(0), mod_spec(1), const(gmix), const(win), const(qg), const(kvg), const(wq),
                  tabq_spec, pl.BlockSpec((tm, 2 * QK_ROPE), lambda b, i: (i, 0))] + [const(a) for a in extra],
        out_specs=out_specs,
        out_shape=out_shape,
        compiler_params=_params(("arbitrary", "arbitrary")),
        name="inproj_prompt" if q_transposed else "inproj_sample",
    )(x3, mod3, mod3, gmix, win, qg, kvg, wq, tabq, tabk, *extra)


def _ssd_reset(c, xbuf, hT):
    @pl.when(c == 0)
    def _():
        xbuf[0:8, :] = jnp.zeros((8, CONV_DIM), F32)
        hT[...] = jnp.zeros(hT.shape, F32)


def _ssd_final(c, n_chunks, hfin_ref, hT):
    @pl.when(c == n_chunks - 1)
    def _():
        for k in range(SSD_HEADS // 2):
            hfin_ref[k * LANES:(k + 1) * LANES, :] = hT[k].T


def _ssd_body(*refs, **kw):
    for _ in _ssd_steps(*refs, **kw):
        pass


def _ssd_steps(*refs, Q, seg, carry, chunk_pos=None):
    if carry:
        (xin_ref, z_ref, dt_ref, cw_ref, cb_ref, dtb_ref, alog_ref, dsk_ref, g_ref,
         y_ref, hfin_ref, xbuf, hT) = refs
    else:
        (xin_ref, dt_ref, cw_ref, cb_ref, dtb_ref, alog_ref, dsk_ref,
         ypre_ref, eacs_ref, xw_ref, dec_ref, bm_ref, cm_ref) = refs
    cw = cw_ref[...]
    cb = cb_ref[...]
    if carry:
        c, n_chunks = (pl.program_id(1), pl.num_programs(1)) if chunk_pos is None else chunk_pos
        if chunk_pos is None:
            _ssd_reset(c, xbuf, hT)
        xbuf[8:8 + Q, :] = xin_ref[...]
    xc_parts = []
    for lb in range(CONV_DIM // CONV_LANES):
        cols = slice(lb * CONV_LANES, (lb + 1) * CONV_LANES)
        acc = jnp.broadcast_to(cb[:, cols], (Q, CONV_LANES))
        for k in range(CONV_WIDTH):
            if carry:
                tap = xbuf[pl.ds(8 - (CONV_WIDTH - 1) + k, Q), cols]
            else:
                tap = xin_ref[k, :, cols]
            acc = acc + cw[k:k + 1, cols] * tap
        xc_parts.append(_silu(acc))
        yield
    if carry:
        xbuf[0:8, :] = xbuf[Q:Q + 8, :]
    xc = jnp.concatenate(xc_parts, axis=1)
    xs = xc[:, :SSD_INNER]
    gs = SSD_GROUPS * SSD_STATE
    bm = xc[:, SSD_INNER:SSD_INNER + gs]
    cm = xc[:, SSD_INNER + gs:]
    bm_b = bm.astype(BF16)
    cm_b = cm.astype(BF16)

    lane = lax.broadcasted_iota(jnp.int32, (Q, LANES), 1)
    v = dt_ref[...] + dtb_ref[...]
    dt = jnp.maximum(v, 0.0) + jnp.log1p(jnp.exp(-jnp.abs(v)))
    dt = jnp.where(lane < SSD_HEADS, dt, 0.0)
    dA = dt * (-jnp.exp(alog_ref[...]))
    ri = lax.broadcasted_iota(jnp.int32, (Q, Q), 0)
    ci = lax.broadcasted_iota(jnp.int32, (Q, Q), 1)
    if seg == Q:
        mask = ci <= ri
    else:
        same = (ri // seg) == (ci // seg)
        mask = jnp.logical_and(same, ci <= ri)
    acs = _split3_dot(jnp.where(mask, 1.0, 0.0).astype(BF16), dA)
    if seg == Q:
        acs_last = acs[Q - 1:Q, :]
    else:
        acs_last = _split3_dot(jnp.where(same, 1.0, 0.0).astype(BF16), dA)
    to_end = jnp.exp(acs_last - acs) * dt
    acsT = acs.T
    dtT = dt.T
    yield

    G = [_dot_nt(cm_b[:, g * SSD_STATE:(g + 1) * SSD_STATE], bm_b[:, g * SSD_STATE:(g + 1) * SSD_STATE])
         for g in range(SSD_GROUPS)]
    lane_lo = lane < SSD_HEAD_DIM
    heads_per_group = SSD_HEADS // SSD_GROUPS
    ypairs, epairs, xwpairs, decpairs = [], [], [], []
    for k in range(SSD_HEADS // 2):
        g = (2 * k) // heads_per_group
        xp = xs[:, k * LANES:(k + 1) * LANES]
        xhalf = (jnp.where(lane_lo, xp, 0.0).astype(BF16), jnp.where(lane_lo, 0.0, xp).astype(BF16))
        yk = jnp.zeros((Q, LANES), F32)
        for s in range(2):
            hh = 2 * k + s
            segm = acs[:, hh:hh + 1] - acsT[hh:hh + 1, :]
            m = G[g] * jnp.exp(jnp.where(mask, segm, -jnp.inf)) * dtT[hh:hh + 1, :]
            yk = yk + _dot(m.astype(BF16), xhalf[s])
            yield

        def pair(a):
            return jnp.where(lane_lo[:a.shape[0]], a[:, 2 * k:2 * k + 1], a[:, 2 * k + 1:2 * k + 2])

        e_p = jnp.exp(pair(acs))
        xw = xp * pair(to_end)
        dec = jnp.exp(pair(acs_last))
        if carry:
            h_prev = hT[k]
            yk = yk + _dot(cm_b[:, g * SSD_STATE:(g + 1) * SSD_STATE], h_prev.astype(BF16)) * e_p
            hT[k] = dec * h_prev + _dot_tn(bm_b[:, g * SSD_STATE:(g + 1) * SSD_STATE], xw.astype(BF16))
        else:
            epairs.append(e_p)
            xwpairs.append(xw)
            decpairs.append(dec)
        ypairs.append(yk)
        yield
    y = jnp.concatenate(ypairs, axis=1) + dsk_ref[...] * xs
    if carry:
        y = y * _silu(z_ref[...])
        y_ref[...] = _rms(y, g_ref[...]).astype(BF16)

        if chunk_pos is None:
            _ssd_final(c, n_chunks, hfin_ref, hT)
    else:
        ypre_ref[...] = y
        eacs_ref[...] = jnp.concatenate(epairs, axis=1)
        xw_ref[...] = jnp.concatenate(xwpairs, axis=1)
        dec_ref[...] = jnp.concatenate(decpairs, axis=1)
        bm_ref[...] = bm
        cm_ref[...] = cm


def _row(a, n):
    return jnp.pad(a.reshape(1, -1).astype(F32), ((0, 0), (0, n - a.size)))


def _ssd_prompt(xbc, z, dt, cw, cb, dtb, alog, dsk, g):
    nb, L, _ = xbc.shape
    Q = SSD_CHUNK if L % SSD_CHUNK == 0 else L
    nc = L // Q

    def tok(width):
        return pl.BlockSpec((None, Q, width), lambda b, c: (b, c, 0))

    def const(a):
        return pl.BlockSpec(a.shape, lambda b, c: (0,) * a.ndim)

    consts = (cw, cb, dtb, alog, dsk, g)
    return pl.pallas_call(
        functools.partial(_ssd_body, Q=Q, seg=Q, carry=True),
        grid=(nb, nc),
        in_specs=[tok(CONV_DIM), tok(SSD_INNER), tok(LANES)] + [const(a) for a in consts],
        out_specs=[tok(SSD_INNER), pl.BlockSpec((None, SSD_INNER, SSD_STATE), lambda b, c: (b, 0, 0))],
        out_shape=[SDS((nb, L, SSD_INNER), BF16), SDS((nb, SSD_INNER, SSD_STATE), F32)],
        scratch_shapes=[pltpu.VMEM((Q + 8, CONV_DIM), F32), pltpu.VMEM((SSD_HEADS // 2, SSD_STATE, LANES), F32)],
        compiler_params=_params(("arbitrary", "arbitrary")),
        name="ssd_prompt",
    )(xbc, z, dt, *consts)


def _ssd_sample(xsh, dt, cw, cb, dtb, alog, dsk, *, seg):
    _, T, _ = xsh.shape
    consts = (cw, cb, dtb, alog, dsk)

    def full(a):
        return pl.BlockSpec(a.shape, lambda i: (0,) * a.ndim)

    outs = [SDS((T, SSD_INNER), F32)] * 4 + [SDS((T, SSD_GROUPS * SSD_STATE), F32)] * 2
    return pl.pallas_call(
        functools.partial(_ssd_body, Q=T, seg=seg, carry=False),
        grid=(1,),
        in_specs=[full(xsh), full(dt)] + [full(a) for a in consts],
        out_specs=[full(o) for o in outs],
        out_shape=outs,
        compiler_params=_params(("arbitrary",)),
        name="ssd_sample",
    )(xsh, dt, *consts)


def _sstate_body(cm_ref, bm_ref, ypre_ref, eacs_ref, xw_ref, dec_ref, z_ref, s0_ref, g_ref, y_ref, sn_ref):
    s0 = s0_ref[...]
    s0b = s0.astype(BF16)
    cm = cm_ref[...].astype(BF16)
    bm = bm_ref[...].astype(BF16)
    rows = SSD_INNER // SSD_GROUPS
    yo = jnp.concatenate(
        [jnp.einsum("btn,bqn->btq", cm[:, :, g * SSD_STATE:(g + 1) * SSD_STATE],
                    s0b[:, g * rows:(g + 1) * rows, :], preferred_element_type=F32)
         for g in range(SSD_GROUPS)], axis=-1)
    y = (ypre_ref[...] + yo * eacs_ref[...]) * _silu(z_ref[...])
    y_ref[...] = _rms(y, g_ref[...]).astype(BF16)
    dec = dec_ref[...]
    hi = dec.astype(BF16)
    lo = (dec - hi.astype(F32)).astype(BF16)
    sel = jnp.where(lax.broadcasted_iota(jnp.int32, (dec.shape[0], dec.shape[1], SSD_STATE), 1) == 0,
                    1.0, 0.0).astype(BF16)
    dmat = (jnp.einsum("bjq,bjn->bqn", hi, sel, preferred_element_type=F32)
            + jnp.einsum("bjq,bjn->bqn", lo, sel, preferred_element_type=F32))
    xw = xw_ref[...].astype(BF16)
    upd = jnp.concatenate(
        [jnp.einsum("bjq,bjn->bqn", xw[:, :, g * rows:(g + 1) * rows],
                    bm[:, :, g * SSD_STATE:(g + 1) * SSD_STATE], preferred_element_type=F32)
         for g in range(SSD_GROUPS)], axis=1)
    sn_ref[...] = dmat * s0 + upd


def _sstate(cm, bm, ypre, eacs, xw, dec, z, s0, g, *, bs):
    nseq, t, _ = cm.shape
    bs = min(bs, nseq)

    def blk(a):
        return pl.BlockSpec((bs,) + a.shape[1:], lambda i: (i, 0, 0))

    ins = (cm, bm, ypre, eacs, xw, dec, z, s0)
    return pl.pallas_call(
        _sstate_body,
        grid=(nseq // bs,),
        in_specs=[blk(a) for a in ins] + [pl.BlockSpec(g.shape, lambda i: (0, 0))],
        out_specs=[blk(ypre), blk(s0)],
        out_shape=[SDS(ypre.shape, BF16), SDS(s0.shape, F32)],
        compiler_params=_params(("arbitrary",)),
        name="sstate",
    )(*ins, g)


def _attn_body(qt_ref, qn_ref, kp_ref, vt_ref, g_ref, o_ref, m_sc, acc_sc,
               s_a, s_b, s_c, smax_a, smax_b, smax_c, *, tk):
    i = pl.program_id(1)
    ncols = MLA_HEADS * TQ
    m_sc[...] = jnp.full(m_sc.shape, -jnp.inf, F32)
    acc_sc[...] = jnp.zeros(acc_sc.shape, F32)
    ct = 2 * TQ
    tiles = [slice(p * ct, (p + 1) * ct) for p in range(PAIRS)]

    def scores(j, s_ref, smax_ref, q_ref=qt_ref):
        k0 = pl.multiple_of(j * tk, tk)
        for p, cs in enumerate(tiles):
            s = _dot(kp_ref[p, pl.ds(k0, tk), :], q_ref[p])
            s_ref[:, cs] = s
            smax_ref[:, cs] = jnp.max(s, axis=0, keepdims=True)

    def softmax_pv(j, s_ref, smax_ref, masked):
        for p, cs in enumerate(tiles):
            s = s_ref[:, cs]
            if masked:
                key = j * tk + lax.broadcasted_iota(jnp.int32, (tk, ct), 0)
                tok = i * TQ + (lax.broadcasted_iota(jnp.int32, (tk, ct), 1) & (TQ - 1))
                s = jnp.where(key <= tok, s, -jnp.inf)
                smax = jnp.max(s, axis=0, keepdims=True)
            else:
                smax = smax_ref[:, cs]
            m_prev = m_sc[:, cs]
            m_new = jnp.maximum(m_prev, smax)
            alpha = jnp.exp2(m_prev - m_new)
            e = jnp.exp2(s - m_new)
            acc_sc[p] = alpha * acc_sc[p] + _dot(vt_ref[j, p], e.astype(BF16))
            m_sc[:, cs] = m_new

    nfull = (i * TQ) // tk

    @pl.when(i == 0)
    def _():
        scores(0, s_c, smax_c)

    def prefetch():
        scores(0, s_c, smax_c, qn_ref)

    @pl.when(nfull == 0)
    def _():
        softmax_pv(0, s_c, smax_c, True)
        prefetch()

    @pl.when(nfull >= 1)
    def _():
        scores(1, s_b, smax_b)
        softmax_pv(0, s_c, smax_c, False)

    def pair(p, carry):
        j = 2 * p + 1
        scores(j + 1, s_a, smax_a)
        softmax_pv(j, s_b, smax_b, False)
        scores(j + 2, s_b, smax_b)
        softmax_pv(j + 1, s_a, smax_a, False)
        return carry

    lax.fori_loop(0, jnp.maximum(nfull - 1, 0) // 2, pair, 0)
    odd = (nfull % 2) == 1

    @pl.when(odd)
    def _():
        prefetch()
        softmax_pv(nfull, s_b, smax_b, True)

    @pl.when(jnp.logical_and(jnp.logical_not(odd), nfull >= 2))
    def _():
        scores(nfull, s_a, smax_a)
        softmax_pv(nfull - 1, s_b, smax_b, False)
        prefetch()
        softmax_pv(nfull, s_a, smax_a, True)

    ys = []
    for hh in range(MLA_HEADS):
        p, s = divmod(hh, 2)
        cols = slice(s * TQ, (s + 1) * TQ)
        denom = acc_sc[p, 2 * V_DIM:2 * V_DIM + 1, cols]
        ys.append(acc_sc[p, s * V_DIM:(s + 1) * V_DIM, cols] * (1.0 / denom))
    yt = jnp.concatenate(ys, axis=0)
    yt = yt * lax.rsqrt(jnp.mean(yt * yt, axis=0, keepdims=True) + EPS)
    o_ref[...] = (yt.T * g_ref[...]).astype(BF16)


def _attn_prompt(qt, kp, vt, g, *, tk):
    nb, nq = qt.shape[:2]
    L = nq * TQ
    ncols = MLA_HEADS * TQ
    assert TQ & (TQ - 1) == 0 and L % tk == 0 and tk % TQ == 0

    def q_spec(index):
        return pl.BlockSpec((None, None, PAIRS, PAIR_K, 2 * TQ), lambda b, i: (b, index(i), 0, 0, 0))

    return pl.pallas_call(
        functools.partial(_attn_body, tk=tk),
        grid=(nb, nq),
        in_specs=[q_spec(lambda i: i), q_spec(lambda i: jnp.minimum(i + 1, nq - 1)),
                  pl.BlockSpec((None, PAIRS, L, PAIR_K), lambda b, i: (b, 0, 0, 0), pipeline_mode=pl.Buffered(1)),
                  pl.BlockSpec((None, L // tk, PAIRS, PAIR_V, tk), lambda b, i: (b, 0, 0, 0, 0),
                               pipeline_mode=pl.Buffered(1)),
                  pl.BlockSpec(g.shape, lambda b, i: (0, 0))],
        out_specs=pl.BlockSpec((None, TQ, MLA_INNER), lambda b, i: (b, i, 0)),
        out_shape=SDS((nb, L, MLA_INNER), BF16),
        scratch_shapes=[pltpu.VMEM((1, ncols), F32),
                        pltpu.VMEM((PAIRS, PAIR_V, 2 * TQ), F32),
                        pltpu.VMEM((tk, ncols), F32), pltpu.VMEM((tk, ncols), F32), pltpu.VMEM((tk, ncols), F32),
                        pltpu.VMEM((1, ncols), F32), pltpu.VMEM((1, ncols), F32), pltpu.VMEM((1, ncols), F32)],
        compiler_params=_params(("arbitrary", "arbitrary")),
        name="attn_prompt",
    )(qt, qt, kp, vt, g)


def _sattn_body(pt_ref, qa_ref, qr_ref, kn_ref, rn_ref, ckv_hbm, krt_hbm, o_ref, kbuf, rbuf, sem, *, npages, t_new):
    slot = _sattn_fetch(pt_ref, ckv_hbm, krt_hbm, kbuf, rbuf, sem, npages=npages)
    for _ in _sattn_steps(slot, qa_ref, qr_ref, kn_ref, rn_ref, o_ref, kbuf, rbuf, npages=npages, t_new=t_new,
                          split=SPLIT):
        pass


def _sattn_fetch(pt_ref, ckv_hbm, krt_hbm, kbuf, rbuf, sem, *, npages):
    b = pl.program_id(0)
    nseq = pl.num_programs(0)

    def copies(seq_page, slot, p):
        off = p * PAGE_SIZE
        return (pltpu.make_async_copy(ckv_hbm.at[seq_page], kbuf.at[slot, p], sem.at[0, slot]),
                pltpu.make_async_copy(krt_hbm.at[seq_page], rbuf.at[slot, :, pl.ds(off, PAGE_SIZE)],
                                      sem.at[1, slot]))

    def start_fetch(seq, slot):
        for p in range(npages):
            for cp in copies(pt_ref[seq * npages + p], slot, p):
                cp.start()

    def wait_fetch(slot):
        pltpu.make_async_copy(ckv_hbm.at[pl.ds(0, npages)], kbuf.at[slot], sem.at[0, slot]).wait()
        pltpu.make_async_copy(rbuf.at[slot], rbuf.at[slot], sem.at[1, slot]).wait()

    @pl.when(b == 0)
    def _():
        start_fetch(0, 0)

    @pl.when(b + 1 < nseq)
    def _():
        start_fetch(b + 1, (b + 1) % 2)

    slot = b % 2
    wait_fetch(slot)
    return slot


def _sattn_steps(slot, qa_ref, qr_ref, kn_ref, rn_ref, o_ref, kbuf, rbuf, *, npages, t_new, split):
    split = math.gcd(split, npages)
    grp = math.gcd(SPLIT, split)
    pg = npages // split
    part = pg * PAGE_SIZE
    q = qa_ref[...]
    qr = qr_ref[...]
    kn = kn_ref[...]
    rn = rn_ref[...]
    kps, s_p = [], []
    for h0 in range(0, split, grp):
        hs = range(h0, h0 + grp)
        kps += [kbuf[slot, pl.ds(h * pg, pg)].reshape(part, KV_RANK).astype(BF16) for h in hs]
        s_main = [_dot_nt(q, kps[h]) for h in hs]
        s_rope = [_dot(qr, rbuf[slot, :, h * part:(h + 1) * part].astype(BF16)) for h in hs]
        s_p += [a + b for a, b in zip(s_main, s_rope)]
        yield
    s_n = _dot_nt(q, kn) + _dot_nt(qr, rn)
    rows = t_new * MLA_HEADS
    tok = lax.broadcasted_iota(jnp.int32, (rows, t_new), 0) // MLA_HEADS
    col = lax.broadcasted_iota(jnp.int32, (rows, t_new), 1)
    s_n = jnp.where(col <= tok, s_n, -jnp.inf)
    m = jnp.max(s_n, axis=-1, keepdims=True)
    for s in s_p:
        m = jnp.maximum(m, jnp.max(s, axis=-1, keepdims=True))
    pn = jnp.exp(s_n - m)
    pp = [jnp.exp(s - m) for s in s_p]
    l = jnp.sum(pn, axis=-1, keepdims=True)
    for p in pp:
        l = l + jnp.sum(p, axis=-1, keepdims=True)
    inv = 1.0 / l
    o = _dot(pn.astype(BF16), kn) * inv
    yield
    for h0 in range(0, split, grp):
        parts = [_dot(pp[h].astype(BF16), kps[h]) for h in range(h0, h0 + grp)]
        for part_o in parts:
            o = o + part_o * inv
        yield
    o_ref[...] = o


def _sattn(page_table, qa, qr, kn, rn, cache_kv, cache_krt):
    nseq, npages = page_table.shape
    t_new = kn.shape[1]
    rows = t_new * MLA_HEADS
    grid_spec = pltpu.PrefetchScalarGridSpec(
        num_scalar_prefetch=1,
        grid=(nseq,),
        in_specs=[pl.BlockSpec((rows, KV_RANK), lambda b, pt: (b, 0)),
                  pl.BlockSpec((rows, QK_ROPE), lambda b, pt: (b, 0)),
                  pl.BlockSpec((None, t_new, KV_RANK), lambda b, pt: (b, 0, 0)),
                  pl.BlockSpec((None, t_new, QK_ROPE), lambda b, pt: (b, 0, 0)),
                  pl.BlockSpec(memory_space=pl.ANY),
                  pl.BlockSpec(memory_space=pl.ANY)],
        out_specs=pl.BlockSpec((rows, KV_RANK), lambda b, pt: (b, 0)),
        scratch_shapes=[pltpu.VMEM((2, npages, PAGE_SIZE, KV_RANK), F32),
                        pltpu.VMEM((2, QK_ROPE, npages * PAGE_SIZE), F32),
                        pltpu.SemaphoreType.DMA((2, 2))],
    )
    return pl.pallas_call(
        functools.partial(_sattn_body, npages=npages, t_new=t_new),
        grid_spec=grid_spec,
        out_shape=SDS((nseq * rows, KV_RANK), F32),
        compiler_params=_params(("arbitrary",)),
        name="attn_sample",
    )(page_table.reshape(-1), qa, qr, kn, rn, cache_kv, cache_krt)


def _ssd_sattn_body(pt_ref, qa_ref, qr_ref, kn_ref, rn_ref, ckv_hbm, krt_hbm,
                    xin_ref, z_ref, dt_ref, cw_ref, cb_ref, dtb_ref, alog_ref, dsk_ref, g_ref,
                    o_ref, y_ref, hfin_ref, kbuf, rbuf, sem, xbuf, hT, *, npages, t_new, Q, n_chunks):
    c = pl.program_id(0) % n_chunks
    _ssd_reset(c, xbuf, hT)
    slot = _sattn_fetch(pt_ref, ckv_hbm, krt_hbm, kbuf, rbuf, sem, npages=npages)
    ssd = _ssd_steps(xin_ref, z_ref, dt_ref, cw_ref, cb_ref, dtb_ref, alog_ref, dsk_ref, g_ref, y_ref, hfin_ref,
                     xbuf, hT, Q=Q, seg=Q, carry=True, chunk_pos=(c, n_chunks))
    att = _sattn_steps(slot, qa_ref, qr_ref, kn_ref, rn_ref, o_ref, kbuf, rbuf, npages=npages, t_new=t_new,
                       split=FUSED_SPLIT)
    done = object()
    live = [att, ssd]
    while live:
        live = [gen for gen in live if next(gen, done) is not done]
    _ssd_final(c, n_chunks, hfin_ref, hT)


def _ssd_sattn(page_table, qa, qr, kn, rn, cache_kv, cache_krt, xbc, z, dt, cw, cb, dtb, alog, dsk, g):
    nseq, npages = page_table.shape
    t_new = kn.shape[1]
    rows = t_new * MLA_HEADS
    nb, L, _ = xbc.shape
    Q = SSD_CHUNK if L % SSD_CHUNK == 0 else L
    nc = L // Q
    assert nseq == nb * nc
    consts = (cw, cb, dtb, alog, dsk, g)

    def tok(width):
        return pl.BlockSpec((None, Q, width), lambda s, pt: (s // nc, s % nc, 0))

    def const(a):
        return pl.BlockSpec(a.shape, lambda s, pt: (0,) * a.ndim)

    grid_spec = pltpu.PrefetchScalarGridSpec(
        num_scalar_prefetch=1,
        grid=(nseq,),
        in_specs=[pl.BlockSpec((rows, KV_RANK), lambda s, pt: (s, 0)),
                  pl.BlockSpec((rows, QK_ROPE), lambda s, pt: (s, 0)),
                  pl.BlockSpec((None, t_new, KV_RANK), lambda s, pt: (s, 0, 0)),
                  pl.BlockSpec((None, t_new, QK_ROPE), lambda s, pt: (s, 0, 0)),
                  pl.BlockSpec(memory_space=pl.ANY),
                  pl.BlockSpec(memory_space=pl.ANY),
                  tok(CONV_DIM), tok(SSD_INNER), tok(LANES)] + [const(a) for a in consts],
        out_specs=[pl.BlockSpec((rows, KV_RANK), lambda s, pt: (s, 0)),
                   tok(SSD_INNER),
                   pl.BlockSpec((None, SSD_INNER, SSD_STATE), lambda s, pt: (s // nc, 0, 0))],
        scratch_shapes=[pltpu.VMEM((2, npages, PAGE_SIZE, KV_RANK), F32),
                        pltpu.VMEM((2, QK_ROPE, npages * PAGE_SIZE), F32),
                        pltpu.SemaphoreType.DMA((2, 2)),
                        pltpu.VMEM((Q + 8, CONV_DIM), F32),
                        pltpu.VMEM((SSD_HEADS // 2, SSD_STATE, LANES), F32)],
    )
    return pl.pallas_call(
        functools.partial(_ssd_sattn_body, npages=npages, t_new=t_new, Q=Q, n_chunks=nc),
        grid_spec=grid_spec,
        out_shape=[SDS((nseq * rows, KV_RANK), F32), SDS((nb, L, SSD_INNER), BF16),
                   SDS((nb, SSD_INNER, SSD_STATE), F32)],
        compiler_params=_params(("arbitrary",)),
        name="ssd_prompt_attn_sample",
    )(page_table.reshape(-1), qa, qr, kn, rn, cache_kv, cache_krt, xbc, z, dt, *consts)


def _apost_body(o_ref, wuv_ref, g_ref, y_ref):
    tm = o_ref.shape[0]
    o = jnp.zeros((tm, MLA_INNER), F32)
    for hh in range(MLA_HEADS):
        o = o + _dot(o_ref[:, hh * KV_RANK:(hh + 1) * KV_RANK].astype(BF16), wuv_ref[hh])
    y_ref[...] = _rms(o, g_ref[...]).astype(BF16)


def _apost(o_lat, wuv_pad, g):
    T = o_lat.shape[0]
    return pl.pallas_call(
        _apost_body,
        grid=(1,),
        in_specs=[pl.BlockSpec(o_lat.shape, lambda i: (0, 0)),
                  pl.BlockSpec(wuv_pad.shape, lambda i: (0, 0, 0)),
                  pl.BlockSpec(g.shape, lambda i: (0, 0))],
        out_specs=pl.BlockSpec((T, MLA_INNER), lambda i: (0, 0)),
        out_shape=SDS((T, MLA_INNER), BF16),
        compiler_params=_params(("arbitrary",)),
        name="attn_post",
    )(o_lat, wuv_pad, g)


def _mlp_body(x_ref, ys_ref, ya_ref, g1_ref, sh2_ref, sc2_ref, g2_ref, shf_ref, scf_ref,
              wout_ref, gmlp_ref, wup_ref, wdn_ref, gfin_ref, o_ref, *, final, tf, rep):
    yy = jnp.concatenate([ys_ref[...], ya_ref[...]], axis=-1)
    x1 = x_ref[...] + _mod_rows(g1_ref, rep) * _dot(yy, wout_ref[...])
    h2 = (_rms(x1, gmlp_ref[...]) * (1.0 + _mod_rows(sc2_ref, rep)) + _mod_rows(sh2_ref, rep)).astype(BF16)
    acc = jnp.zeros(x1.shape, F32)
    for c in range(D_FF // tf):
        u = jnp.maximum(_dot(h2, wup_ref[:, c * tf:(c + 1) * tf]), 0.0)
        acc = acc + _dot((u * u).astype(BF16), wdn_ref[c * tf:(c + 1) * tf, :])
    x2 = x1 + _mod_rows(g2_ref, rep) * acc
    if final:
        x2 = _rms(x2, gfin_ref[...]) * (1.0 + _mod_rows(scf_ref, rep)) + _mod_rows(shf_ref, rep)
    o_ref[...] = x2


def _mlp(x3, ys, ya, mod3, wout, gmlp, wup, wdn, gfin, *, final, tm, tf):
    nb, L, d = x3.shape
    tm = min(tm, L)
    rep, mod_spec = _mod_spec(mod3, L, tm, d)

    def tok(width):
        return pl.BlockSpec((None, tm, width), lambda b, i: (b, i, 0))

    def const(a):
        return pl.BlockSpec(a.shape, lambda b, i: (0,) * a.ndim, pipeline_mode=pl.Buffered(1))

    return pl.pallas_call(
        functools.partial(_mlp_body, final=final, tf=tf, rep=rep),
        grid=(nb, L // tm),
        in_specs=[tok(d), tok(SSD_INNER), tok(MLA_INNER)] + [mod_spec(k) for k in (2, 3, 4, 5, 6, 7)]
                 + [const(wout), const(gmlp), const(wup), const(wdn), const(gfin)],
        out_specs=tok(d),
        out_shape=SDS((nb, L, d), F32),
        compiler_params=_params(("arbitrary", "arbitrary")),
        name="mlp",
    )(x3, ys, ya, mod3, mod3, mod3, mod3, mod3, mod3, wout, gmlp, wup, wdn, gfin)


def _rope_tables(pos):
    inv = 1.0 / (ROPE_THETA ** (np.arange(0, QK_ROPE, 2, dtype=np.float64) / QK_ROPE))
    ang = pos.astype(np.float64)[:, None] * inv[None, :]
    cos, sin = np.cos(ang).astype(np.float32), np.sin(ang).astype(np.float32)
    c32 = np.concatenate([cos, cos], axis=-1)
    s32 = np.concatenate([-sin, sin], axis=-1)
    tab1 = np.concatenate([c32, s32], axis=-1)
    tabq = np.concatenate([np.tile(c32, (1, MLA_HEADS)), np.tile(s32, (1, MLA_HEADS))], axis=-1)
    return tab1, tabq


def _swap_halves(w):
    half = w.shape[-1] // 2
    return jnp.concatenate([w[..., half:], w[..., :half]], axis=-1)


def kernel(x_prompt, x_sample, cache_kv_latent, cache_k_rope, state_conv, state_ssm, page_table,
           c_prompt, c_sample, w_ada, b_ada, norm_mix_g, w_in, conv_w, conv_b, dt_bias, a_log,
           d_skip, norm_ssd_g, q_norm_g, kv_norm_g, w_uq, w_uk, w_uv, norm_attn_g, w_out,
           norm_mlp_g, w_up, w_down, w_ada_final, b_ada_final, norm_final_g):
    depth = w_in.shape[0]
    b_p, seq, d = x_prompt.shape
    n_seq, t_new, _ = x_sample.shape
    n_tok_s = n_seq * t_new
    past_len = page_table.shape[1] * PAGE_SIZE

    c_all = jnp.concatenate([c_prompt, c_sample], axis=0)
    ada_fin = _ada(c_all, w_ada_final, b_ada_final)
    tab1_p, _ = _rope_tables(np.arange(seq))
    tab1_s, tabq_s = _rope_tables(past_len + np.arange(t_new))
    tab1_pt = jnp.asarray(np.ascontiguousarray(tab1_p.T))
    tab1_p = jnp.asarray(tab1_p)
    tabq_s = jnp.asarray(np.tile(tabq_s, (n_seq, 1)))
    tab1_s = jnp.asarray(np.tile(tab1_s, (n_seq, 1)))

    xp = x_prompt
    xs = x_sample.reshape(1, n_tok_s, d)
    outs_p, outs_s = [], []
    for l in range(depth):
        final = l == depth - 1
        wi = w_in[l]
        c1 = SSD_INNER
        c2 = c1 + CONV_DIM
        c3 = c2 + SSD_HEADS
        c4 = c3 + Q_RANK
        c5 = c4 + KV_RANK
        w_kr = wi[:, c5:]
        win = jnp.concatenate(
            [wi[:, :c2], wi[:, c3:c5], wi[:, c2:c3], jnp.zeros((d, MISC_KR - SSD_HEADS), F32), w_kr,
             _swap_halves(w_kr), jnp.zeros((d, LANES - MISC_KRSW - QK_ROPE), F32)], axis=1).astype(BF16)
        wq_h = w_uq[l].reshape(Q_RANK, MLA_HEADS, QK_NOPE + QK_ROPE)
        w_rope = wq_h[:, :, QK_NOPE:]
        wfold = _fold(jnp.transpose(wq_h, (1, 0, 2)), jnp.transpose(w_uk[l], (1, 0, 2)))
        wq = jnp.concatenate([wfold, w_rope.reshape(Q_RANK, -1).astype(BF16),
                              _swap_halves(w_rope).reshape(Q_RANK, -1).astype(BF16)], axis=1)
        wuv_pad = jnp.zeros((MLA_HEADS, KV_RANK, MLA_HEADS, V_DIM), F32)
        wuv_pad = wuv_pad.at[jnp.arange(MLA_HEADS), :, jnp.arange(MLA_HEADS), :].set(
            jnp.transpose(w_uv[l], (1, 0, 2)))
        wuv_pad = wuv_pad.reshape(MLA_HEADS, KV_RANK, MLA_INNER).astype(BF16)
        wq_p = jnp.concatenate([wq_h[:, :, :QK_NOPE].reshape(Q_RANK, -1), w_rope.reshape(Q_RANK, -1),
                                _swap_halves(w_rope).reshape(Q_RANK, -1)], axis=1).T.astype(BF16)
        wk = w_uk[l].reshape(KV_RANK, MLA_HEADS * QK_NOPE).astype(BF16)
        wvt = jnp.transpose(w_uv[l], (1, 2, 0)).reshape(PAIRS, 2 * V_DIM, KV_RANK).astype(BF16)
        wout = w_out[l].astype(BF16)
        wup = w_up[l].astype(BF16)
        wdn = w_down[l].astype(BF16)
        gmix = norm_mix_g[l].reshape(1, d)
        gmlp = norm_mlp_g[l].reshape(1, d)
        gfin = norm_final_g.reshape(1, d)
        qg = q_norm_g[l].reshape(1, Q_RANK)
        kvg = kv_norm_g[l].reshape(1, KV_RANK)
        gssd = norm_ssd_g[l].reshape(1, SSD_INNER)
        gattn = norm_attn_g[l].reshape(1, MLA_INNER)
        cw = conv_w[l]
        cb = conv_b[l].reshape(1, CONV_DIM)
        dtb = _row(dt_bias[l], LANES)
        alog = _row(a_log[l], LANES)
        dsk = jnp.repeat(d_skip[l].astype(F32), SSD_HEAD_DIM).reshape(1, SSD_INNER)

        ada = _ada(c_all, w_ada[l], b_ada[l])
        mod = jnp.concatenate([ada, ada_fin], axis=1)
        mod_p = mod[:b_p].reshape(b_p, 1, 8 * d)
        mod_s = mod[b_p:].reshape(1, n_seq, 8 * d)

        def seqs(a):
            return a.reshape(n_seq, t_new, a.shape[-1])

        tk = min(TK_ATTN, seq)
        z_p, xbc_p, dtr_p, ckv_p, kr_p, kp, vt, qt = _inproj(
            xp, mod_p, gmix, win, qg, kvg, wq_p, tab1_pt, tab1_p, q_transposed=True, tm=TM_INPROJ_PROMPT, tk=tk,
            wk=wk, wvt=wvt)
        z_s, xbc_s, dtr_s, ckv_s, kr_s, kc, krb, qa, qr = _inproj(
            xs, mod_s, gmix, win, qg, kvg, wq, tabq_s, tab1_s, q_transposed=False, tm=TM_INPROJ_SAMPLE)
        xpad = jnp.concatenate([state_conv[l], xbc_s.reshape(n_seq, t_new, CONV_DIM)], axis=1)
        xsh = jnp.stack([xpad[:, k:k + t_new].reshape(n_tok_s, CONV_DIM) for k in range(CONV_WIDTH)])
        ypre, eacs, xw, dec, bm, cm = _ssd_sample(xsh, dtr_s[0], cw, cb, dtb, alog, dsk, seg=t_new)
        y_ssd_s, s_new = _sstate(seqs(cm), seqs(bm), seqs(ypre), seqs(eacs), seqs(xw), seqs(dec), seqs(z_s[0]),
                                 state_ssm[l].reshape(n_seq, SSD_INNER, SSD_STATE), gssd, bs=BS_STATE)

        sattn_args = (page_table, qa.reshape(n_tok_s * MLA_HEADS, KV_RANK),
                      qr.reshape(n_tok_s * MLA_HEADS, QK_ROPE), seqs(kc[0]), seqs(krb[0]),
                      cache_kv_latent[l], jnp.swapaxes(cache_k_rope[l], -1, -2))
        ssd_args = (xbc_p, z_p, dtr_p, cw, cb, dtb, alog, dsk, gssd)
        chunk = SSD_CHUNK if seq % SSD_CHUNK == 0 else seq
        if n_seq == b_p * (seq // chunk):
            o_lat, y_ssd_p, hfin = _ssd_sattn(*sattn_args, *ssd_args)
        else:
            y_ssd_p, hfin = _ssd_prompt(*ssd_args)
            o_lat = _sattn(*sattn_args)

        y_attn_p = _attn_prompt(qt, kp, vt, gattn, tk=tk)
        xp = _mlp(xp, y_ssd_p, y_attn_p, mod_p, wout, gmlp, wup, wdn, gfin, final=final, tm=TM_MLP, tf=TF_MLP)
        tail = min(seq, CONV_WIDTH - 1)
        conv_tail = jnp.concatenate([jnp.zeros((b_p, CONV_WIDTH - 1 - tail, CONV_DIM), F32),
                                     xbc_p[:, seq - tail:]], axis=1)
        outs_p.append((ckv_p, kr_p, conv_tail, hfin.reshape(b_p, SSD_HEADS, SSD_HEAD_DIM, SSD_STATE)))

        y_attn_s = _apost(o_lat.reshape(n_tok_s, MLA_HEADS * KV_RANK), wuv_pad, gattn)
        xs = _mlp(xs, y_ssd_s.reshape(1, n_tok_s, SSD_INNER), y_attn_s.reshape(1, n_tok_s, MLA_INNER), mod_s,
                  wout, gmlp, wup, wdn, gfin, final=final, tm=TM_MLP, tf=TF_MLP)
        outs_s.append((seqs(ckv_s[0]), seqs(kr_s[0]), xpad[:, t_new:],
                       s_new.reshape(n_seq, SSD_HEADS, SSD_HEAD_DIM, SSD_STATE)))

    def stack(outs, k):
        return jnp.stack([o[k] for o in outs])

    return (xp, xs.reshape(n_seq, t_new, d),
            stack(outs_p, 0), stack(outs_p, 1), stack(outs_p, 2), stack(outs_p, 3),
            stack(outs_s, 0), stack(outs_s, 1), stack(outs_s, 2), stack(outs_s, 3))
```

```python
import functools
import math

import jax
import jax.numpy as jnp
import numpy as np
from jax import lax
from jax.experimental import pallas as pl
from jax.experimental.pallas import tpu as pltpu

F32 = jnp.float32
BF16 = jnp.bfloat16
SDS = jax.ShapeDtypeStruct

D_MODEL = 1024
SSD_HEADS = 8
SSD_HEAD_DIM = 64
SSD_INNER = SSD_HEADS * SSD_HEAD_DIM
SSD_GROUPS = 2
SSD_STATE = 128
CONV_WIDTH = 4
SSD_CHUNK = 128
CONV_DIM = SSD_INNER + 2 * SSD_GROUPS * SSD_STATE
MLA_HEADS = 8
QK_NOPE = 64
QK_ROPE = 32
V_DIM = 64
KV_RANK = 256
Q_RANK = 384
MLA_INNER = MLA_HEADS * V_DIM
ROPE_THETA = 10000.0
ATTN_SCALE = 1.0 / math.sqrt(QK_NOPE + QK_ROPE)
LOG2E = math.log2(math.e)
PAGE_SIZE = 128
D_FF = 4 * D_MODEL
EPS = 1e-6

LANES = 128
TQ = 128
PAIRS = MLA_HEADS // 2
PAIR_K = 256
PAIR_V = 2 * V_DIM + 16
SPLIT = 2
FUSED_SPLIT = 32
CONV_LANES = 128
C_Z = 0
C_XBC = C_Z + SSD_INNER
C_QLAT = C_XBC + CONV_DIM
C_KVLAT = C_QLAT + Q_RANK
C_MISC = C_KVLAT + KV_RANK
W_IN_COLS = C_MISC + LANES
MISC_KR = 32
MISC_KRSW = 64
C_QROPE = MLA_HEADS * KV_RANK
C_QROPE_SW = C_QROPE + MLA_HEADS * QK_ROPE
WQ_COLS = C_QROPE_SW + MLA_HEADS * QK_ROPE

VMEM_LIMIT = 52 * 1024 * 1024

TM_INPROJ_PROMPT = 512
TM_INPROJ_SAMPLE = 256
TM_MLP = 512
TF_MLP = 2048
TK_ATTN = 512
BS_STATE = 8


def _dot(a, b):
    return jnp.dot(a, b, preferred_element_type=F32)


def _dot_nt(a, b):
    return lax.dot_general(a, b, (((1,), (1,)), ((), ())), preferred_element_type=F32)


def _dot_tn(a, b):
    return lax.dot_general(a, b, (((0,), (0,)), ((), ())), preferred_element_type=F32)


def _silu(x):
    return x * jax.nn.sigmoid(x)


def _rms(x, g):
    return x * lax.rsqrt(jnp.mean(x * x, axis=-1, keepdims=True) + EPS) * g


def _split3_dot(mask_bf16, v):
    v1 = v.astype(BF16)
    r1 = v - v1.astype(F32)
    v2 = r1.astype(BF16)
    v3 = (r1 - v2.astype(F32)).astype(BF16)
    return _dot(mask_bf16, v1) + _dot(mask_bf16, v2) + _dot(mask_bf16, v3)


def _mod_spec(mod3, L, tm, d):
    rows = mod3.shape[1]
    if rows == 1:
        return 1, lambda k: pl.BlockSpec((None, 1, d), lambda b, i: (b, 0, k))
    rep = L // rows
    return rep, lambda k: pl.BlockSpec((None, tm // rep, d), lambda b, i: (b, i, k))


def _mod_rows(ref, rep):
    m = ref[...]
    if rep == 1:
        return m
    n = m.shape[0] * rep
    sel = (lax.broadcasted_iota(jnp.int32, (n, m.shape[0]), 0) // rep
           == lax.broadcasted_iota(jnp.int32, (n, m.shape[0]), 1))
    sel = jnp.where(sel, 1.0, 0.0).astype(BF16)
    hi = m.astype(BF16)
    lo = (m - hi.astype(F32)).astype(BF16)
    return _dot(sel, hi) + _dot(sel, lo)


def _params(sem, vmem=VMEM_LIMIT):
    return pltpu.CompilerParams(dimension_semantics=sem, vmem_limit_bytes=vmem)


def _ada_body(c_ref, w_ref, b_ref, o_ref):
    s = _silu(c_ref[...]).astype(BF16)
    o_ref[...] = _dot(s, w_ref[...].astype(BF16)) + b_ref[...]


def _ada(c, w, b):
    bsz, d = c.shape
    n = w.shape[1]
    tn = 1024
    return pl.pallas_call(
        _ada_body,
        grid=(n // tn,),
        in_specs=[pl.BlockSpec((bsz, d), lambda j: (0, 0)),
                  pl.BlockSpec((d, tn), lambda j: (0, j)),
                  pl.BlockSpec((1, tn), lambda j: (0, j))],
        out_specs=pl.BlockSpec((bsz, tn), lambda j: (0, j)),
        out_shape=SDS((bsz, n), F32),
        compiler_params=_params(("arbitrary",)),
        name="ada",
    )(c, w, b.reshape(1, n))


def _fold_body(wq_ref, wk_ref, o_ref):
    a = wq_ref[:, 0:QK_NOPE].astype(BF16)
    o_ref[...] = _dot_nt(a, wk_ref[...].astype(BF16)).astype(BF16)


def _fold(wq_h, wk_h):
    return pl.pallas_call(
        _fold_body,
        grid=(MLA_HEADS,),
        in_specs=[pl.BlockSpec((None, Q_RANK, QK_NOPE + QK_ROPE), lambda h: (h, 0, 0)),
                  pl.BlockSpec((None, KV_RANK, QK_NOPE), lambda h: (h, 0, 0))],
        out_specs=pl.BlockSpec((Q_RANK, KV_RANK), lambda h: (0, h)),
        out_shape=SDS((Q_RANK, MLA_HEADS * KV_RANK), BF16),
        compiler_params=_params(("arbitrary",)),
        name="fold",
    )(wq_h, wk_h)


def _inproj_body(*refs, q_transposed, rep):
    tm = refs[0].shape[0]
    halves = 2 if (q_transposed and rep == 1 and tm % (2 * TQ) == 0) else 1
    rows = tm // halves
    gens = [_inproj_steps(refs, slice(k * rows, (k + 1) * rows), q_transposed, rep) for k in range(halves)]
    done = object()
    while gens:
        gens = [g for g in gens if next(g, done) is not done]


def _inproj_steps(refs, rs, q_transposed, rep):
    n_in = 12 if q_transposed else 10
    x_ref, sh_ref, sc_ref, gmix_ref, win_ref, qg_ref, kvg_ref, wq_ref, tq_ref, tk_ref = refs[:10]
    z_ref, xbc_ref, dt_ref, ckv_ref, kr_ref = refs[n_in:n_in + 5]
    sc, sh = _mod_rows(sc_ref, rep), _mod_rows(sh_ref, rep)
    if rep > 1:
        sc, sh = sc[rs], sh[rs]
    h = _rms(x_ref[rs, :], gmix_ref[...]) * (1.0 + sc) + sh
    proj = _dot(h.astype(BF16), win_ref[...])
    yield
    tm = proj.shape[0]
    z_ref[rs, :] = proj[:, C_Z:C_XBC]
    xbc_ref[rs, :] = proj[:, C_XBC:C_QLAT]
    q_scale = ATTN_SCALE * LOG2E if q_transposed else ATTN_SCALE
    qn = (_rms(proj[:, C_QLAT:C_KVLAT], qg_ref[...]) * q_scale).astype(BF16)
    ckv = _rms(proj[:, C_KVLAT:C_MISC], kvg_ref[...])
    ckv_ref[rs, :] = ckv
    misc = proj[:, C_MISC:W_IN_COLS]
    lane = lax.broadcasted_iota(jnp.int32, misc.shape, 1)
    dt_ref[rs, :] = jnp.where(lane < SSD_HEADS, misc, 0.0)
    tk = tk_ref[rs, :]
    kr = (misc[:, MISC_KR:MISC_KR + QK_ROPE] * tk[:, :QK_ROPE]
          + misc[:, MISC_KRSW:MISC_KRSW + QK_ROPE] * tk[:, QK_ROPE:])
    kr_ref[rs, :] = kr
    nr = MLA_HEADS * QK_ROPE
    if q_transposed:
        wk_ref, wvt_ref = refs[10:12]
        kp_ref, vt_ref, qt_ref = refs[n_in + 5:]
        ckv_b = ckv.astype(BF16)
        k_nope = _dot(ckv_b, wk_ref[...])
        kr_pad = jnp.concatenate([kr, jnp.zeros((tm, PAIR_K - 2 * QK_NOPE - QK_ROPE), F32)], axis=1).astype(BF16)
        pad_rows = PAIR_V - 2 * V_DIM
        ones_row = jnp.where(lax.broadcasted_iota(jnp.int32, (pad_rows, tm), 0) == 0, 1.0, 0.0).astype(BF16)
        for p in range(PAIRS):
            kp_ref[p, rs, 0:2 * QK_NOPE] = k_nope[:, p * 2 * QK_NOPE:(p + 1) * 2 * QK_NOPE].astype(BF16)
            kp_ref[p, rs, 2 * QK_NOPE:PAIR_K] = kr_pad
            vt_ref[p, 0:2 * V_DIM, rs] = _dot_nt(wvt_ref[p], ckv_b).astype(BF16)
            vt_ref[p, 2 * V_DIM:PAIR_V, rs] = ones_row
        yield
        qt = _dot_nt(wq_ref[...], qn)
        yield
        n0 = MLA_HEADS * QK_NOPE
        tq = tq_ref[:, rs]
        cos_t = jnp.concatenate([tq[:QK_ROPE]] * MLA_HEADS, axis=0)
        sin_t = jnp.concatenate([tq[QK_ROPE:]] * MLA_HEADS, axis=0)
        rot = (qt[n0:n0 + nr] * cos_t + qt[n0 + nr:n0 + 2 * nr] * sin_t).astype(BF16)
        q_nope = qt[:n0].astype(BF16)
        zero = jnp.zeros((PAIR_K, TQ), BF16)
        c0 = rs.start // TQ
        for c in range(tm // TQ):
            toks = slice(c * TQ, (c + 1) * TQ)
            for p in range(PAIRS):
                for s in range(2):
                    hh = 2 * p + s
                    cols = slice(s * TQ, (s + 1) * TQ)
                    qt_ref[c0 + c, p, :, cols] = zero
                    qt_ref[c0 + c, p, s * QK_NOPE:(s + 1) * QK_NOPE, cols] = (
                        q_nope[hh * QK_NOPE:(hh + 1) * QK_NOPE, toks])
                    qt_ref[c0 + c, p, 2 * QK_NOPE:2 * QK_NOPE + QK_ROPE, cols] = (
                        rot[hh * QK_ROPE:(hh + 1) * QK_ROPE, toks])
    else:
        kc_ref, krb_ref, qa_ref, qr_ref = refs[n_in + 5:]
        kc_ref[rs, :] = ckv.astype(BF16)
        krb_ref[rs, :] = kr.astype(BF16)
        q = _dot(qn, wq_ref[...])
        tq = tq_ref[rs, :]
        qa_ref[rs, :] = q[:, :C_QROPE].astype(BF16)
        qr_ref[rs, :] = (q[:, C_QROPE:C_QROPE_SW] * tq[:, :nr] + q[:, C_QROPE_SW:WQ_COLS] * tq[:, nr:]).astype(BF16)


def _inproj(x3, mod3, gmix, win, qg, kvg, wq, tabq, tabk, *, q_transposed, tm, tk=None, wk=None, wvt=None):
    nb, L, d = x3.shape
    tm = min(tm, L)
    nt = L // tm
    rep, mod_spec = _mod_spec(mod3, L, tm, d)

    def tok(width):
        return pl.BlockSpec((None, tm, width), lambda b, i: (b, i, 0))

    def const(a):
        return pl.BlockSpec(a.shape, lambda b, i: (0,) * a.ndim, pipeline_mode=pl.Buffered(1))

    nr = MLA_HEADS * QK_ROPE
    out_specs = [tok(SSD_INNER), tok(CONV_DIM), tok(LANES), tok(KV_RANK), tok(QK_ROPE)]
    out_shape = [SDS((nb, L, SSD_INNER), F32), SDS((nb, L, CONV_DIM), F32), SDS((nb, L, LANES), F32),
                 SDS((nb, L, KV_RANK), F32), SDS((nb, L, QK_ROPE), F32)]
    if q_transposed:
        assert tm % TQ == 0 and tk % tm == 0
        r = tk // tm
        tabq_spec = pl.BlockSpec((2 * QK_ROPE, tm), lambda b, i: (0, i))
        out_specs += [pl.BlockSpec((None, PAIRS, tm, PAIR_K), lambda b, i: (b, 0, i, 0)),
                      pl.BlockSpec((None, None, PAIRS, PAIR_V, tm), lambda b, i: (b, i // r, 0, 0, i % r)),
                      pl.BlockSpec((None, tm // TQ, PAIRS, PAIR_K, 2 * TQ), lambda b, i: (b, i, 0, 0, 0))]
        out_shape += [SDS((nb, PAIRS, L, PAIR_K), BF16), SDS((nb, L // tk, PAIRS, PAIR_V, tk), BF16),
                      SDS((nb, L // TQ, PAIRS, PAIR_K, 2 * TQ), BF16)]
        extra = [wk, wvt]
    else:
        extra = []
        tabq_spec = pl.BlockSpec((tm, 2 * nr), lambda b, i: (i, 0))
        out_specs += [tok(KV_RANK), tok(QK_ROPE), tok(C_QROPE), tok(nr)]
        out_shape += [SDS((nb, L, KV_RANK), BF16), SDS((nb, L, QK_ROPE), BF16),
                      SDS((nb, L, C_QROPE), BF16), SDS((nb, L, nr), BF16)]
    return pl.pallas_call(
        functools.partial(_inproj_body, q_transposed=q_transposed, rep=rep),
        grid=(nb, nt),
        in_specs=[tok(d), mod_spec(0), mod_spec(1), const(gmix), const(win), const(qg), const(kvg), const(wq),
                  tabq_spec, pl.BlockSpec((tm, 2 * QK_ROPE), lambda b, i: (i, 0))] + [const(a) for a in extra],
        out_specs=out_specs,
        out_shape=out_shape,
        compiler_params=_params(("arbitrary", "arbitrary")),
        name="inproj_prompt" if q_transposed else "inproj_sample",
    )(x3, mod3, mod3, gmix, win, qg, kvg, wq, tabq, tabk, *extra)


def _ssd_reset(c, xbuf, hT):
    @pl.when(c == 0)
    def _():
        xbuf[0:8, :] = jnp.zeros((8, CONV_DIM), F32)
        hT[...] = jnp.zeros(hT.shape, F32)


def _ssd_final(c, n_chunks, hfin_ref, hT):
    @pl.when(c == n_chunks - 1)
    def _():
        for k in range(SSD_HEADS // 2):
            hfin_ref[k * LANES:(k + 1) * LANES, :] = hT[k].T


def _ssd_body(*refs, **kw):
    for _ in _ssd_steps(*refs, **kw):
        pass


def _ssd_steps(*refs, Q, seg, carry, chunk_pos=None):
    if carry:
        (xin_ref, z_ref, dt_ref, cw_ref, cb_ref, dtb_ref, alog_ref, dsk_ref, g_ref,
         y_ref, hfin_ref, xbuf, hT) = refs
    else:
        (xin_ref, dt_ref, cw_ref, cb_ref, dtb_ref, alog_ref, dsk_ref,
         ypre_ref, eacs_ref, xw_ref, dec_ref, bm_ref, cm_ref) = refs
    cw = cw_ref[...]
    cb = cb_ref[...]
    if carry:
        c, n_chunks = (pl.program_id(1), pl.num_programs(1)) if chunk_pos is None else chunk_pos
        if chunk_pos is None:
            _ssd_reset(c, xbuf, hT)
        xbuf[8:8 + Q, :] = xin_ref[...]
    xc_parts = []
    for lb in range(CONV_DIM // CONV_LANES):
        cols = slice(lb * CONV_LANES, (lb + 1) * CONV_LANES)
        acc = jnp.broadcast_to(cb[:, cols], (Q, CONV_LANES))
        for k in range(CONV_WIDTH):
            if carry:
                tap = xbuf[pl.ds(8 - (CONV_WIDTH - 1) + k, Q), cols]
            else:
                tap = xin_ref[k, :, cols]
            acc = acc + cw[k:k + 1, cols] * tap
        xc_parts.append(_silu(acc))
        yield
    if carry:
        xbuf[0:8, :] = xbuf[Q:Q + 8, :]
    xc = jnp.concatenate(xc_parts, axis=1)
    xs = xc[:, :SSD_INNER]
    gs = SSD_GROUPS * SSD_STATE
    bm = xc[:, SSD_INNER:SSD_INNER + gs]
    cm = xc[:, SSD_INNER + gs:]
    bm_b = bm.astype(BF16)
    cm_b = cm.astype(BF16)

    lane = lax.broadcasted_iota(jnp.int32, (Q, LANES), 1)
    v = dt_ref[...] + dtb_ref[...]
    dt = jnp.maximum(v, 0.0) + jnp.log1p(jnp.exp(-jnp.abs(v)))
    dt = jnp.where(lane < SSD_HEADS, dt, 0.0)
    dA = dt * (-jnp.exp(alog_ref[...]))
    ri = lax.broadcasted_iota(jnp.int32, (Q, Q), 0)
    ci = lax.broadcasted_iota(jnp.int32, (Q, Q), 1)
    if seg == Q:
        mask = ci <= ri
    else:
        same = (ri // seg) == (ci // seg)
        mask = jnp.logical_and(same, ci <= ri)
    acs = _split3_dot(jnp.where(mask, 1.0, 0.0).astype(BF16), dA)
    if seg == Q:
        acs_last = acs[Q - 1:Q, :]
    else:
        acs_last = _split3_dot(jnp.where(same, 1.0, 0.0).astype(BF16), dA)
    to_end = jnp.exp(acs_last - acs) * dt
    acsT = acs.T
    dtT = dt.T
    yield

    G = [_dot_nt(cm_b[:, g * SSD_STATE:(g + 1) * SSD_STATE], bm_b[:, g * SSD_STATE:(g + 1) * SSD_STATE])
         for g in range(SSD_GROUPS)]
    lane_lo = lane < SSD_HEAD_DIM
    heads_per_group = SSD_HEADS // SSD_GROUPS
    ypairs, epairs, xwpairs, decpairs = [], [], [], []
    for k in range(SSD_HEADS // 2):
        g = (2 * k) // heads_per_group
        xp = xs[:, k * LANES:(k + 1) * LANES]
        xhalf = (jnp.where(lane_lo, xp, 0.0).astype(BF16), jnp.where(lane_lo, 0.0, xp).astype(BF16))
        yk = jnp.zeros((Q, LANES), F32)
        for s in range(2):
            hh = 2 * k + s
            segm = acs[:, hh:hh + 1] - acsT[hh:hh + 1, :]
            m = G[g] * jnp.exp(jnp.where(mask, segm, -jnp.inf)) * dtT[hh:hh + 1, :]
            yk = yk + _dot(m.astype(BF16), xhalf[s])
            yield

        def pair(a):
            return jnp.where(lane_lo[:a.shape[0]], a[:, 2 * k:2 * k + 1], a[:, 2 * k + 1:2 * k + 2])

        e_p = jnp.exp(pair(acs))
        xw = xp * pair(to_end)
        dec = jnp.exp(pair(acs_last))
        if carry:
            h_prev = hT[k]
            yk = yk + _dot(cm_b[:, g * SSD_STATE:(g + 1) * SSD_STATE], h_prev.astype(BF16)) * e_p
            hT[k] = dec * h_prev + _dot_tn(bm_b[:, g * SSD_STATE:(g + 1) * SSD_STATE], xw.astype(BF16))
        else:
            epairs.append(e_p)
            xwpairs.append(xw)
            decpairs.append(dec)
        ypairs.append(yk)
        yield
    y = jnp.concatenate(ypairs, axis=1) + dsk_ref[...] * xs
    if carry:
        y = y * _silu(z_ref[...])
        y_ref[...] = _rms(y, g_ref[...]).astype(BF16)

        if chunk_pos is None:
            _ssd_final(c, n_chunks, hfin_ref, hT)
    else:
        ypre_ref[...] = y
        eacs_ref[...] = jnp.concatenate(epairs, axis=1)
        xw_ref[...] = jnp.concatenate(xwpairs, axis=1)
        dec_ref[...] = jnp.concatenate(decpairs, axis=1)
        bm_ref[...] = bm
        cm_ref[...] = cm


def _row(a, n):
    return jnp.pad(a.reshape(1, -1).astype(F32), ((0, 0), (0, n - a.size)))


def _ssd_prompt(xbc, z, dt, cw, cb, dtb, alog, dsk, g):
    nb, L, _ = xbc.shape
    Q = SSD_CHUNK if L % SSD_CHUNK == 0 else L
    nc = L // Q

    def tok(width):
        return pl.BlockSpec((None, Q, width), lambda b, c: (b, c, 0))

    def const(a):
        return pl.BlockSpec(a.shape, lambda b, c: (0,) * a.ndim)

    consts = (cw, cb, dtb, alog, dsk, g)
    return pl.pallas_call(
        functools.partial(_ssd_body, Q=Q, seg=Q, carry=True),
        grid=(nb, nc),
        in_specs=[tok(CONV_DIM), tok(SSD_INNER), tok(LANES)] + [const(a) for a in consts],
        out_specs=[tok(SSD_INNER), pl.BlockSpec((None, SSD_INNER, SSD_STATE), lambda b, c: (b, 0, 0))],
        out_shape=[SDS((nb, L, SSD_INNER), BF16), SDS((nb, SSD_INNER, SSD_STATE), F32)],
        scratch_shapes=[pltpu.VMEM((Q + 8, CONV_DIM), F32), pltpu.VMEM((SSD_HEADS // 2, SSD_STATE, LANES), F32)],
        compiler_params=_params(("arbitrary", "arbitrary")),
        name="ssd_prompt",
    )(xbc, z, dt, *consts)


def _ssd_sample(xsh, dt, cw, cb, dtb, alog, dsk, *, seg):
    _, T, _ = xsh.shape
    consts = (cw, cb, dtb, alog, dsk)

    def full(a):
        return pl.BlockSpec(a.shape, lambda i: (0,) * a.ndim)

    outs = [SDS((T, SSD_INNER), F32)] * 4 + [SDS((T, SSD_GROUPS * SSD_STATE), F32)] * 2
    return pl.pallas_call(
        functools.partial(_ssd_body, Q=T, seg=seg, carry=False),
        grid=(1,),
        in_specs=[full(xsh), full(dt)] + [full(a) for a in consts],
        out_specs=[full(o) for o in outs],
        out_shape=outs,
        compiler_params=_params(("arbitrary",)),
        name="ssd_sample",
    )(xsh, dt, *consts)


def _sstate_body(cm_ref, bm_ref, ypre_ref, eacs_ref, xw_ref, dec_ref, z_ref, s0_ref, g_ref, y_ref, sn_ref):
    s0 = s0_ref[...]
    s0b = s0.astype(BF16)
    cm = cm_ref[...].astype(BF16)
    bm = bm_ref[...].astype(BF16)
    rows = SSD_INNER // SSD_GROUPS
    yo = jnp.concatenate(
        [jnp.einsum("btn,bqn->btq", cm[:, :, g * SSD_STATE:(g + 1) * SSD_STATE],
                    s0b[:, g * rows:(g + 1) * rows, :], preferred_element_type=F32)
         for g in range(SSD_GROUPS)], axis=-1)
    y = (ypre_ref[...] + yo * eacs_ref[...]) * _silu(z_ref[...])
    y_ref[...] = _rms(y, g_ref[...]).astype(BF16)
    dec = dec_ref[...]
    hi = dec.astype(BF16)
    lo = (dec - hi.astype(F32)).astype(BF16)
    sel = jnp.where(lax.broadcasted_iota(jnp.int32, (dec.shape[0], dec.shape[1], SSD_STATE), 1) == 0,
                    1.0, 0.0).astype(BF16)
    dmat = (jnp.einsum("bjq,bjn->bqn", hi, sel, preferred_element_type=F32)
            + jnp.einsum("bjq,bjn->bqn", lo, sel, preferred_element_type=F32))
    xw = xw_ref[...].astype(BF16)
    upd = jnp.concatenate(
        [jnp.einsum("bjq,bjn->bqn", xw[:, :, g * rows:(g + 1) * rows],
                    bm[:, :, g * SSD_STATE:(g + 1) * SSD_STATE], preferred_element_type=F32)
         for g in range(SSD_GROUPS)], axis=1)
    sn_ref[...] = dmat * s0 + upd


def _sstate(cm, bm, ypre, eacs, xw, dec, z, s0, g, *, bs):
    nseq, t, _ = cm.shape
    bs = min(bs, nseq)

    def blk(a):
        return pl.BlockSpec((bs,) + a.shape[1:], lambda i: (i, 0, 0))

    ins = (cm, bm, ypre, eacs, xw, dec, z, s0)
    return pl.pallas_call(
        _sstate_body,
        grid=(nseq // bs,),
        in_specs=[blk(a) for a in ins] + [pl.BlockSpec(g.shape, lambda i: (0, 0))],
        out_specs=[blk(ypre), blk(s0)],
        out_shape=[SDS(ypre.shape, BF16), SDS(s0.shape, F32)],
        compiler_params=_params(("arbitrary",)),
        name="sstate",
    )(*ins, g)


def _attn_body(qt_ref, qn_ref, kp_ref, vt_ref, g_ref, o_ref, m_sc, acc_sc,
               s_a, s_b, s_c, smax_a, smax_b, smax_c, *, tk):
    i = pl.program_id(1)
    ncols = MLA_HEADS * TQ
    m_sc[...] = jnp.full(m_sc.shape, -jnp.inf, F32)
    acc_sc[...] = jnp.zeros(acc_sc.shape, F32)
    ct = 2 * TQ
    tiles = [slice(p * ct, (p + 1) * ct) for p in range(PAIRS)]

    def scores(j, s_ref, smax_ref, q_ref=qt_ref):
        k0 = pl.multiple_of(j * tk, tk)
        for p, cs in enumerate(tiles):
            s = _dot(kp_ref[p, pl.ds(k0, tk), :], q_ref[p])
            s_ref[:, cs] = s
            smax_ref[:, cs] = jnp.max(s, axis=0, keepdims=True)

    def softmax_pv(j, s_ref, smax_ref, masked):
        for p, cs in enumerate(tiles):
            s = s_ref[:, cs]
            if masked:
                key = j * tk + lax.broadcasted_iota(jnp.int32, (tk, ct), 0)
                tok = i * TQ + (lax.broadcasted_iota(jnp.int32, (tk, ct), 1) & (TQ - 1))
                s = jnp.where(key <= tok, s, -jnp.inf)
                smax = jnp.max(s, axis=0, keepdims=True)
            else:
                smax = smax_ref[:, cs]
            m_prev = m_sc[:, cs]
            m_new = jnp.maximum(m_prev, smax)
            alpha = jnp.exp2(m_prev - m_new)
            e = jnp.exp2(s - m_new)
            acc_sc[p] = alpha * acc_sc[p] + _dot(vt_ref[j, p], e.astype(BF16))
            m_sc[:, cs] = m_new

    nfull = (i * TQ) // tk

    @pl.when(i == 0)
    def _():
        scores(0, s_c, smax_c)

    def prefetch():
        scores(0, s_c, smax_c, qn_ref)

    @pl.when(nfull == 0)
    def _():
        softmax_pv(0, s_c, smax_c, True)
        prefetch()

    @pl.when(nfull >= 1)
    def _():
        scores(1, s_b, smax_b)
        softmax_pv(0, s_c, smax_c, False)

    def pair(p, carry):
        j = 2 * p + 1
        scores(j + 1, s_a, smax_a)
        softmax_pv(j, s_b, smax_b, False)
        scores(j + 2, s_b, smax_b)
        softmax_pv(j + 1, s_a, smax_a, False)
        return carry

    lax.fori_loop(0, jnp.maximum(nfull - 1, 0) // 2, pair, 0)
    odd = (nfull % 2) == 1

    @pl.when(odd)
    def _():
        prefetch()
        softmax_pv(nfull, s_b, smax_b, True)

    @pl.when(jnp.logical_and(jnp.logical_not(odd), nfull >= 2))
    def _():
        scores(nfull, s_a, smax_a)
        softmax_pv(nfull - 1, s_b, smax_b, False)
        prefetch()
        softmax_pv(nfull, s_a, smax_a, True)

    ys = []
    for hh in range(MLA_HEADS):
        p, s = divmod(hh, 2)
        cols = slice(s * TQ, (s + 1) * TQ)
        denom = acc_sc[p, 2 * V_DIM:2 * V_DIM + 1, cols]
        ys.append(acc_sc[p, s * V_DIM:(s + 1) * V_DIM, cols] * (1.0 / denom))
    yt = jnp.concatenate(ys, axis=0)
    yt = yt * lax.rsqrt(jnp.mean(yt * yt, axis=0, keepdims=True) + EPS)
    o_ref[...] = (yt.T * g_ref[...]).astype(BF16)


def _attn_prompt(qt, kp, vt, g, *, tk):
    nb, nq = qt.shape[:2]
    L = nq * TQ
    ncols = MLA_HEADS * TQ
    assert TQ & (TQ - 1) == 0 and L % tk == 0 and tk % TQ == 0

    def q_spec(index):
        return pl.BlockSpec((None, None, PAIRS, PAIR_K, 2 * TQ), lambda b, i: (b, index(i), 0, 0, 0))

    return pl.pallas_call(
        functools.partial(_attn_body, tk=tk),
        grid=(nb, nq),
        in_specs=[q_spec(lambda i: i), q_spec(lambda i: jnp.minimum(i + 1, nq - 1)),
                  pl.BlockSpec((None, PAIRS, L, PAIR_K), lambda b, i: (b, 0, 0, 0), pipeline_mode=pl.Buffered(1)),
                  pl.BlockSpec((None, L // tk, PAIRS, PAIR_V, tk), lambda b, i: (b, 0, 0, 0, 0),
                               pipeline_mode=pl.Buffered(1)),
                  pl.BlockSpec(g.shape, lambda b, i: (0, 0))],
        out_specs=pl.BlockSpec((None, TQ, MLA_INNER), lambda b, i: (b, i, 0)),
        out_shape=SDS((nb, L, MLA_INNER), BF16),
        scratch_shapes=[pltpu.VMEM((1, ncols), F32),
                        pltpu.VMEM((PAIRS, PAIR_V, 2 * TQ), F32),
                        pltpu.VMEM((tk, ncols), F32), pltpu.VMEM((tk, ncols), F32), pltpu.VMEM((tk, ncols), F32),
                        pltpu.VMEM((1, ncols), F32), pltpu.VMEM((1, ncols), F32), pltpu.VMEM((1, ncols), F32)],
        compiler_params=_params(("arbitrary", "arbitrary")),
        name="attn_prompt",
    )(qt, qt, kp, vt, g)


def _sattn_body(pt_ref, qa_ref, qr_ref, kn_ref, rn_ref, ckv_hbm, krt_hbm, o_ref, kbuf, rbuf, sem, *, npages, t_new):
    slot = _sattn_fetch(pt_ref, ckv_hbm, krt_hbm, kbuf, rbuf, sem, npages=npages)
    for _ in _sattn_steps(slot, qa_ref, qr_ref, kn_ref, rn_ref, o_ref, kbuf, rbuf, npages=npages, t_new=t_new,
                          split=SPLIT):
        pass


def _sattn_fetch(pt_ref, ckv_hbm, krt_hbm, kbuf, rbuf, sem, *, npages):
    b = pl.program_id(0)
    nseq = pl.num_programs(0)

    def copies(seq_page, slot, p):
        off = p * PAGE_SIZE
        return (pltpu.make_async_copy(ckv_hbm.at[seq_page], kbuf.at[slot, p], sem.at[0, slot]),
                pltpu.make_async_copy(krt_hbm.at[seq_page], rbuf.at[slot, :, pl.ds(off, PAGE_SIZE)],
                                      sem.at[1, slot]))

    def start_fetch(seq, slot):
        for p in range(npages):
            for cp in copies(pt_ref[seq * npages + p], slot, p):
                cp.start()

    def wait_fetch(slot):
        pltpu.make_async_copy(ckv_hbm.at[pl.ds(0, npages)], kbuf.at[slot], sem.at[0, slot]).wait()
        pltpu.make_async_copy(rbuf.at[slot], rbuf.at[slot], sem.at[1, slot]).wait()

    @pl.when(b == 0)
    def _():
        start_fetch(0, 0)

    @pl.when(b + 1 < nseq)
    def _():
        start_fetch(b + 1, (b + 1) % 2)

    slot = b % 2
    wait_fetch(slot)
    return slot


def _sattn_steps(slot, qa_ref, qr_ref, kn_ref, rn_ref, o_ref, kbuf, rbuf, *, npages, t_new, split, anchor=None):
    split = math.gcd(split, npages)
    grp = math.gcd(SPLIT, split)
    pg = npages // split
    part = pg * PAGE_SIZE
    q = qa_ref[...]
    qr = qr_ref[...]
    kn = kn_ref[...]
    rn = rn_ref[...]
    kps, s_p = [], []
    for h0 in range(0, split, grp):
        hs = range(h0, h0 + grp)
        kps += [kbuf[slot, pl.ds(h * pg, pg)].reshape(part, KV_RANK).astype(BF16) for h in hs]
        s_main = [_dot_nt(q, kps[h]) for h in hs]
        s_rope = [_dot(qr, rbuf[slot, :, h * part:(h + 1) * part].astype(BF16)) for h in hs]
        s_p += [a + b for a, b in zip(s_main, s_rope)]
        yield
    s_n = _dot_nt(q, kn) + _dot_nt(qr, rn)
    rows = t_new * MLA_HEADS
    tok = lax.broadcasted_iota(jnp.int32, (rows, t_new), 0) // MLA_HEADS
    col = lax.broadcasted_iota(jnp.int32, (rows, t_new), 1)
    s_n = jnp.where(col <= tok, s_n, -jnp.inf)
    m = jnp.max(s_n, axis=-1, keepdims=True)
    for s in s_p:
        m = jnp.maximum(m, jnp.max(s, axis=-1, keepdims=True))
    if anchor is not None:
        bits = pltpu.bitcast(anchor[0:16, 0:LANES].astype(F32), jnp.uint32)
        m = m + ((bits >> 16) >> 16).astype(F32)[0:1, 0:1]
    pn = jnp.exp(s_n - m)
    pp = [jnp.exp(s - m) for s in s_p]
    l = jnp.sum(pn, axis=-1, keepdims=True)
    for p in pp:
        l = l + jnp.sum(p, axis=-1, keepdims=True)
    inv = 1.0 / l
    o = _dot(pn.astype(BF16), kn) * inv
    yield
    for h0 in range(0, split, grp):
        parts = [_dot(pp[h].astype(BF16), kps[h]) for h in range(h0, h0 + grp)]
        for part_o in parts:
            o = o + part_o * inv
        yield
    o_ref[...] = o


def _sattn(page_table, qa, qr, kn, rn, cache_kv, cache_krt):
    nseq, npages = page_table.shape
    t_new = kn.shape[1]
    rows = t_new * MLA_HEADS
    grid_spec = pltpu.PrefetchScalarGridSpec(
        num_scalar_prefetch=1,
        grid=(nseq,),
        in_specs=[pl.BlockSpec((rows, KV_RANK), lambda b, pt: (b, 0)),
                  pl.BlockSpec((rows, QK_ROPE), lambda b, pt: (b, 0)),
                  pl.BlockSpec((None, t_new, KV_RANK), lambda b, pt: (b, 0, 0)),
                  pl.BlockSpec((None, t_new, QK_ROPE), lambda b, pt: (b, 0, 0)),
                  pl.BlockSpec(memory_space=pl.ANY),
                  pl.BlockSpec(memory_space=pl.ANY)],
        out_specs=pl.BlockSpec((rows, KV_RANK), lambda b, pt: (b, 0)),
        scratch_shapes=[pltpu.VMEM((2, npages, PAGE_SIZE, KV_RANK), F32),
                        pltpu.VMEM((2, QK_ROPE, npages * PAGE_SIZE), F32),
                        pltpu.SemaphoreType.DMA((2, 2))],
    )
    return pl.pallas_call(
        functools.partial(_sattn_body, npages=npages, t_new=t_new),
        grid_spec=grid_spec,
        out_shape=SDS((nseq * rows, KV_RANK), F32),
        compiler_params=_params(("arbitrary",)),
        name="attn_sample",
    )(page_table.reshape(-1), qa, qr, kn, rn, cache_kv, cache_krt)


def _ssd_sattn_body(pt_ref, qa_ref, qr_ref, kn_ref, rn_ref, ckv_hbm, krt_hbm,
                    xin_ref, z_ref, dt_ref, cw_ref, cb_ref, dtb_ref, alog_ref, dsk_ref, g_ref,
                    o_ref, y_ref, hfin_ref, kbuf, rbuf, sem, xbuf, hT, *, npages, t_new, Q, n_chunks):
    c = pl.program_id(0) % n_chunks
    _ssd_reset(c, xbuf, hT)
    slot = _sattn_fetch(pt_ref, ckv_hbm, krt_hbm, kbuf, rbuf, sem, npages=npages)
    ssd = _ssd_steps(xin_ref, z_ref, dt_ref, cw_ref, cb_ref, dtb_ref, alog_ref, dsk_ref, g_ref, y_ref, hfin_ref,
                     xbuf, hT, Q=Q, seg=Q, carry=True, chunk_pos=(c, n_chunks))
    att = _sattn_steps(slot, qa_ref, qr_ref, kn_ref, rn_ref, o_ref, kbuf, rbuf, npages=npages, t_new=t_new,
                       split=FUSED_SPLIT, anchor=y_ref)
    done = object()
    for _ in range(math.gcd(FUSED_SPLIT, npages) // math.gcd(SPLIT, FUSED_SPLIT, npages)):
        next(att)
        next(ssd, done)
    for _ in ssd:
        pass
    for _ in att:
        pass
    _ssd_final(c, n_chunks, hfin_ref, hT)


def _ssd_sattn(page_table, qa, qr, kn, rn, cache_kv, cache_krt, xbc, z, dt, cw, cb, dtb, alog, dsk, g):
    nseq, npages = page_table.shape
    t_new = kn.shape[1]
    rows = t_new * MLA_HEADS
    nb, L, _ = xbc.shape
    Q = SSD_CHUNK if L % SSD_CHUNK == 0 else L
    nc = L // Q
    assert nseq == nb * nc
    consts = (cw, cb, dtb, alog, dsk, g)

    def tok(width):
        return pl.BlockSpec((None, Q, width), lambda s, pt: (s // nc, s % nc, 0))

    def const(a):
        return pl.BlockSpec(a.shape, lambda s, pt: (0,) * a.ndim)

    grid_spec = pltpu.PrefetchScalarGridSpec(
        num_scalar_prefetch=1,
        grid=(nseq,),
        in_specs=[pl.BlockSpec((rows, KV_RANK), lambda s, pt: (s, 0)),
                  pl.BlockSpec((rows, QK_ROPE), lambda s, pt: (s, 0)),
                  pl.BlockSpec((None, t_new, KV_RANK), lambda s, pt: (s, 0, 0)),
                  pl.BlockSpec((None, t_new, QK_ROPE), lambda s, pt: (s, 0, 0)),
                  pl.BlockSpec(memory_space=pl.ANY),
                  pl.BlockSpec(memory_space=pl.ANY),
                  tok(CONV_DIM), tok(SSD_INNER), tok(LANES)] + [const(a) for a in consts],
        out_specs=[pl.BlockSpec((rows, KV_RANK), lambda s, pt: (s, 0)),
                   tok(SSD_INNER),
                   pl.BlockSpec((None, SSD_INNER, SSD_STATE), lambda s, pt: (s // nc, 0, 0))],
        scratch_shapes=[pltpu.VMEM((2, npages, PAGE_SIZE, KV_RANK), F32),
                        pltpu.VMEM((2, QK_ROPE, npages * PAGE_SIZE), F32),
                        pltpu.SemaphoreType.DMA((2, 2)),
                        pltpu.VMEM((Q + 8, CONV_DIM), F32),
                        pltpu.VMEM((SSD_HEADS // 2, SSD_STATE, LANES), F32)],
    )
    return pl.pallas_call(
        functools.partial(_ssd_sattn_body, npages=npages, t_new=t_new, Q=Q, n_chunks=nc),
        grid_spec=grid_spec,
        out_shape=[SDS((nseq * rows, KV_RANK), F32), SDS((nb, L, SSD_INNER), BF16),
                   SDS((nb, SSD_INNER, SSD_STATE), F32)],
        compiler_params=_params(("arbitrary",)),
        name="ssd_prompt_attn_sample",
    )(page_table.reshape(-1), qa, qr, kn, rn, cache_kv, cache_krt, xbc, z, dt, *consts)


def _apost_body(o_ref, wuv_ref, g_ref, y_ref):
    tm = o_ref.shape[0]
    o = jnp.zeros((tm, MLA_INNER), F32)
    for hh in range(MLA_HEADS):
        o = o + _dot(o_ref[:, hh * KV_RANK:(hh + 1) * KV_RANK].astype(BF16), wuv_ref[hh])
    y_ref[...] = _rms(o, g_ref[...]).astype(BF16)


def _apost(o_lat, wuv_pad, g):
    T = o_lat.shape[0]
    return pl.pallas_call(
        _apost_body,
        grid=(1,),
        in_specs=[pl.BlockSpec(o_lat.shape, lambda i: (0, 0)),
                  pl.BlockSpec(wuv_pad.shape, lambda i: (0, 0, 0)),
                  pl.BlockSpec(g.shape, lambda i: (0, 0))],
        out_specs=pl.BlockSpec((T, MLA_INNER), lambda i: (0, 0)),
        out_shape=SDS((T, MLA_INNER), BF16),
        compiler_params=_params(("arbitrary",)),
        name="attn_post",
    )(o_lat, wuv_pad, g)


def _mlp_body(x_ref, ys_ref, ya_ref, g1_ref, sh2_ref, sc2_ref, g2_ref, shf_ref, scf_ref,
              wout_ref, gmlp_ref, wup_ref, wdn_ref, gfin_ref, o_ref, *, final, tf, rep):
    yy = jnp.concatenate([ys_ref[...], ya_ref[...]], axis=-1)
    x1 = x_ref[...] + _mod_rows(g1_ref, rep) * _dot(yy, wout_ref[...])
    h2 = (_rms(x1, gmlp_ref[...]) * (1.0 + _mod_rows(sc2_ref, rep)) + _mod_rows(sh2_ref, rep)).astype(BF16)
    acc = jnp.zeros(x1.shape, F32)
    for c in range(D_FF // tf):
        u = jnp.maximum(_dot(h2, wup_ref[:, c * tf:(c + 1) * tf]), 0.0)
        acc = acc + _dot((u * u).astype(BF16), wdn_ref[c * tf:(c + 1) * tf, :])
    x2 = x1 + _mod_rows(g2_ref, rep) * acc
    if final:
        x2 = _rms(x2, gfin_ref[...]) * (1.0 + _mod_rows(scf_ref, rep)) + _mod_rows(shf_ref, rep)
    o_ref[...] = x2


def _mlp(x3, ys, ya, mod3, wout, gmlp, wup, wdn, gfin, *, final, tm, tf):
    nb, L, d = x3.shape
    tm = min(tm, L)
    rep, mod_spec = _mod_spec(mod3, L, tm, d)

    def tok(width):
        return pl.BlockSpec((None, tm, width), lambda b, i: (b, i, 0))

    def const(a):
        return pl.BlockSpec(a.shape, lambda b, i: (0,) * a.ndim, pipeline_mode=pl.Buffered(1))

    return pl.pallas_call(
        functools.partial(_mlp_body, final=final, tf=tf, rep=rep),
        grid=(nb, L // tm),
        in_specs=[tok(d), tok(SSD_INNER), tok(MLA_INNER)] + [mod_spec(k) for k in (2, 3, 4, 5, 6, 7)]
                 + [const(wout), const(gmlp), const(wup), const(wdn), const(gfin)],
        out_specs=tok(d),
        out_shape=SDS((nb, L, d), F32),
        compiler_params=_params(("arbitrary", "arbitrary")),
        name="mlp",
    )(x3, ys, ya, mod3, mod3, mod3, mod3, mod3, mod3, wout, gmlp, wup, wdn, gfin)


def _rope_tables(pos):
    inv = 1.0 / (ROPE_THETA ** (np.arange(0, QK_ROPE, 2, dtype=np.float64) / QK_ROPE))
    ang = pos.astype(np.float64)[:, None] * inv[None, :]
    cos, sin = np.cos(ang).astype(np.float32), np.sin(ang).astype(np.float32)
    c32 = np.concatenate([cos, cos], axis=-1)
    s32 = np.concatenate([-sin, sin], axis=-1)
    tab1 = np.concatenate([c32, s32], axis=-1)
    tabq = np.concatenate([np.tile(c32, (1, MLA_HEADS)), np.tile(s32, (1, MLA_HEADS))], axis=-1)
    return tab1, tabq


def _swap_halves(w):
    half = w.shape[-1] // 2
    return jnp.concatenate([w[..., half:], w[..., :half]], axis=-1)


def kernel(x_prompt, x_sample, cache_kv_latent, cache_k_rope, state_conv, state_ssm, page_table,
           c_prompt, c_sample, w_ada, b_ada, norm_mix_g, w_in, conv_w, conv_b, dt_bias, a_log,
           d_skip, norm_ssd_g, q_norm_g, kv_norm_g, w_uq, w_uk, w_uv, norm_attn_g, w_out,
           norm_mlp_g, w_up, w_down, w_ada_final, b_ada_final, norm_final_g):
    depth = w_in.shape[0]
    b_p, seq, d = x_prompt.shape
    n_seq, t_new, _ = x_sample.shape
    n_tok_s = n_seq * t_new
    past_len = page_table.shape[1] * PAGE_SIZE

    c_all = jnp.concatenate([c_prompt, c_sample], axis=0)
    ada_fin = _ada(c_all, w_ada_final, b_ada_final)
    tab1_p, _ = _rope_tables(np.arange(seq))
    tab1_s, tabq_s = _rope_tables(past_len + np.arange(t_new))
    tab1_pt = jnp.asarray(np.ascontiguousarray(tab1_p.T))
    tab1_p = jnp.asarray(tab1_p)
    tabq_s = jnp.asarray(np.tile(tabq_s, (n_seq, 1)))
    tab1_s = jnp.asarray(np.tile(tab1_s, (n_seq, 1)))

    xp = x_prompt
    xs = x_sample.reshape(1, n_tok_s, d)
    outs_p, outs_s = [], []
    for l in range(depth):
        final = l == depth - 1
        wi = w_in[l]
        c1 = SSD_INNER
        c2 = c1 + CONV_DIM
        c3 = c2 + SSD_HEADS
        c4 = c3 + Q_RANK
        c5 = c4 + KV_RANK
        w_kr = wi[:, c5:]
        win = jnp.concatenate(
            [wi[:, :c2], wi[:, c3:c5], wi[:, c2:c3], jnp.zeros((d, MISC_KR - SSD_HEADS), F32), w_kr,
             _swap_halves(w_kr), jnp.zeros((d, LANES - MISC_KRSW - QK_ROPE), F32)], axis=1).astype(BF16)
        wq_h = w_uq[l].reshape(Q_RANK, MLA_HEADS, QK_NOPE + QK_ROPE)
        w_rope = wq_h[:, :, QK_NOPE:]
        wfold = _fold(jnp.transpose(wq_h, (1, 0, 2)), jnp.transpose(w_uk[l], (1, 0, 2)))
        wq = jnp.concatenate([wfold, w_rope.reshape(Q_RANK, -1).astype(BF16),
                              _swap_halves(w_rope).reshape(Q_RANK, -1).astype(BF16)], axis=1)
        wuv_pad = jnp.zeros((MLA_HEADS, KV_RANK, MLA_HEADS, V_DIM), F32)
        wuv_pad = wuv_pad.at[jnp.arange(MLA_HEADS), :, jnp.arange(MLA_HEADS), :].set(
            jnp.transpose(w_uv[l], (1, 0, 2)))
        wuv_pad = wuv_pad.reshape(MLA_HEADS, KV_RANK, MLA_INNER).astype(BF16)
        wq_p = jnp.concatenate([wq_h[:, :, :QK_NOPE].reshape(Q_RANK, -1), w_rope.reshape(Q_RANK, -1),
                                _swap_halves(w_rope).reshape(Q_RANK, -1)], axis=1).T.astype(BF16)
        wk = w_uk[l].reshape(KV_RANK, MLA_HEADS * QK_NOPE).astype(BF16)
        wvt = jnp.transpose(w_uv[l], (1, 2, 0)).reshape(PAIRS, 2 * V_DIM, KV_RANK).astype(BF16)
        wout = w_out[l].astype(BF16)
        wup = w_up[l].astype(BF16)
        wdn = w_down[l].astype(BF16)
        gmix = norm_mix_g[l].reshape(1, d)
        gmlp = norm_mlp_g[l].reshape(1, d)
        gfin = norm_final_g.reshape(1, d)
        qg = q_norm_g[l].reshape(1, Q_RANK)
        kvg = kv_norm_g[l].reshape(1, KV_RANK)
        gssd = norm_ssd_g[l].reshape(1, SSD_INNER)
        gattn = norm_attn_g[l].reshape(1, MLA_INNER)
        cw = conv_w[l]
        cb = conv_b[l].reshape(1, CONV_DIM)
        dtb = _row(dt_bias[l], LANES)
        alog = _row(a_log[l], LANES)
        dsk = jnp.repeat(d_skip[l].astype(F32), SSD_HEAD_DIM).reshape(1, SSD_INNER)

        ada = _ada(c_all, w_ada[l], b_ada[l])
        mod = jnp.concatenate([ada, ada_fin], axis=1)
        mod_p = mod[:b_p].reshape(b_p, 1, 8 * d)
        mod_s = mod[b_p:].reshape(1, n_seq, 8 * d)

        def seqs(a):
            return a.reshape(n_seq, t_new, a.shape[-1])

        tk = min(TK_ATTN, seq)
        z_p, xbc_p, dtr_p, ckv_p, kr_p, kp, vt, qt = _inproj(
            xp, mod_p, gmix, win, qg, kvg, wq_p, tab1_pt, tab1_p, q_transposed=True, tm=TM_INPROJ_PROMPT, tk=tk,
            wk=wk, wvt=wvt)
        z_s, xbc_s, dtr_s, ckv_s, kr_s, kc, krb, qa, qr = _inproj(
            xs, mod_s, gmix, win, qg, kvg, wq, tabq_s, tab1_s, q_transposed=False, tm=TM_INPROJ_SAMPLE)
        xpad = jnp.concatenate([state_conv[l], xbc_s.reshape(n_seq, t_new, CONV_DIM)], axis=1)
        xsh = jnp.stack([xpad[:, k:k + t_new].reshape(n_tok_s, CONV_DIM) for k in range(CONV_WIDTH)])
        ypre, eacs, xw, dec, bm, cm = _ssd_sample(xsh, dtr_s[0], cw, cb, dtb, alog, dsk, seg=t_new)
        y_ssd_s, s_new = _sstate(seqs(cm), seqs(bm), seqs(ypre), seqs(eacs), seqs(xw), seqs(dec), seqs(z_s[0]),
                                 state_ssm[l].reshape(n_seq, SSD_INNER, SSD_STATE), gssd, bs=BS_STATE)

        sattn_args = (page_table, qa.reshape(n_tok_s * MLA_HEADS, KV_RANK),
                      qr.reshape(n_tok_s * MLA_HEADS, QK_ROPE), seqs(kc[0]), seqs(krb[0]),
                      cache_kv_latent[l], jnp.swapaxes(cache_k_rope[l], -1, -2))
        ssd_args = (xbc_p, z_p, dtr_p, cw, cb, dtb, alog, dsk, gssd)
        chunk = SSD_CHUNK if seq % SSD_CHUNK == 0 else seq
        if n_seq == b_p * (seq // chunk):
            o_lat, y_ssd_p, hfin = _ssd_sattn(*sattn_args, *ssd_args)
        else:
            y_ssd_p, hfin = _ssd_prompt(*ssd_args)
            o_lat = _sattn(*sattn_args)

        y_attn_p = _attn_prompt(qt, kp, vt, gattn, tk=tk)
        xp = _mlp(xp, y_ssd_p, y_attn_p, mod_p, wout, gmlp, wup, wdn, gfin, final=final, tm=TM_MLP, tf=TF_MLP)
        tail = min(seq, CONV_WIDTH - 1)
        conv_tail = jnp.concatenate([jnp.zeros((b_p, CONV_WIDTH - 1 - tail, CONV_DIM), F32),
                                     xbc_p[:, seq - tail:]], axis=1)
        outs_p.append((ckv_p, kr_p, conv_tail, hfin.reshape(b_p, SSD_HEADS, SSD_HEAD_DIM, SSD_STATE)))

        y_attn_s = _apost(o_lat.reshape(n_tok_s, MLA_HEADS * KV_RANK), wuv_pad, gattn)
        xs = _mlp(xs, y_ssd_s.reshape(1, n_tok_s, SSD_INNER), y_attn_s.reshape(1, n_tok_s, MLA_INNER), mod_s,
                  wout, gmlp, wup, wdn, gfin, final=final, tm=TM_MLP, tf=TF_MLP)
        outs_s.append((seqs(ckv_s[0]), seqs(kr_s[0]), xpad[:, t_new:],
                       s_new.reshape(n_seq, SSD_HEADS, SSD_HEAD_DIM, SSD_STATE)))

    def stack(outs, k):
        return jnp.stack([o[k] for o in outs])

    return (xp, xs.reshape(n_seq, t_new, d),
            stack(outs_p, 0), stack(outs_p, 1), stack(outs_p, 2), stack(outs_p, 3),
            stack(outs_s, 0), stack(outs_s, 1), stack(outs_s, 2), stack(outs_s, 3))
```

```python
import functools
import math

import jax
import jax.numpy as jnp
import numpy as np
from jax import lax
from jax.experimental import pallas as pl
from jax.experimental.pallas import tpu as pltpu

F32 = jnp.float32
BF16 = jnp.bfloat16
SDS = jax.ShapeDtypeStruct

D_MODEL = 1024
SSD_HEADS = 8
SSD_HEAD_DIM = 64
SSD_INNER = SSD_HEADS * SSD_HEAD_DIM
SSD_GROUPS = 2
SSD_STATE = 128
CONV_WIDTH = 4
SSD_CHUNK = 128
CONV_DIM = SSD_INNER + 2 * SSD_GROUPS * SSD_STATE
MLA_HEADS = 8
QK_NOPE = 64
QK_ROPE = 32
V_DIM = 64
KV_RANK = 256
Q_RANK = 384
MLA_INNER = MLA_HEADS * V_DIM
ROPE_THETA = 10000.0
ATTN_SCALE = 1.0 / math.sqrt(QK_NOPE + QK_ROPE)
LOG2E = math.log2(math.e)
PAGE_SIZE = 128
D_FF = 4 * D_MODEL
EPS = 1e-6

LANES = 128
TQ = 128
PAIRS = MLA_HEADS // 2
PAIR_K = 256
PAIR_V = 2 * V_DIM + 16
SPLIT = 2
FUSED_SPLIT = 32
CONV_LANES = 128
C_Z = 0
C_XBC = C_Z + SSD_INNER
C_QLAT = C_XBC + CONV_DIM
C_KVLAT = C_QLAT + Q_RANK
C_MISC = C_KVLAT + KV_RANK
W_IN_COLS = C_MISC + LANES
MISC_KR = 32
MISC_KRSW = 64
C_QROPE = MLA_HEADS * KV_RANK
C_QROPE_SW = C_QROPE + MLA_HEADS * QK_ROPE
WQ_COLS = C_QROPE_SW + MLA_HEADS * QK_ROPE

VMEM_LIMIT = 52 * 1024 * 1024

TM_INPROJ_PROMPT = 512
TM_INPROJ_SAMPLE = 256
TM_MLP = 512
TF_MLP = 2048
TK_ATTN = 512
BS_STATE = 8


def _dot(a, b):
    return jnp.dot(a, b, preferred_element_type=F32)


def _dot_nt(a, b):
    return lax.dot_general(a, b, (((1,), (1,)), ((), ())), preferred_element_type=F32)


def _dot_tn(a, b):
    return lax.dot_general(a, b, (((0,), (0,)), ((), ())), preferred_element_type=F32)


def _silu(x):
    return x * jax.nn.sigmoid(x)


def _rms(x, g):
    return x * lax.rsqrt(jnp.mean(x * x, axis=-1, keepdims=True) + EPS) * g


def _split3_dot(mask_bf16, v):
    v1 = v.astype(BF16)
    r1 = v - v1.astype(F32)
    v2 = r1.astype(BF16)
    v3 = (r1 - v2.astype(F32)).astype(BF16)
    return _dot(mask_bf16, v1) + _dot(mask_bf16, v2) + _dot(mask_bf16, v3)


def _mod_spec(mod3, L, tm, d):
    rows = mod3.shape[1]
    if rows == 1:
        return 1, lambda k: pl.BlockSpec((None, 1, d), lambda b, i: (b, 0, k))
    rep = L // rows
    return rep, lambda k: pl.BlockSpec((None, tm // rep, d), lambda b, i: (b, i, k))


def _mod_rows(ref, rep):
    m = ref[...]
    if rep == 1:
        return m
    n = m.shape[0] * rep
    sel = (lax.broadcasted_iota(jnp.int32, (n, m.shape[0]), 0) // rep
           == lax.broadcasted_iota(jnp.int32, (n, m.shape[0]), 1))
    sel = jnp.where(sel, 1.0, 0.0).astype(BF16)
    hi = m.astype(BF16)
    lo = (m - hi.astype(F32)).astype(BF16)
    return _dot(sel, hi) + _dot(sel, lo)


def _params(sem, vmem=VMEM_LIMIT):
    return pltpu.CompilerParams(dimension_semantics=sem, vmem_limit_bytes=vmem)


def _ada_body(c_ref, w_ref, b_ref, o_ref):
    s = _silu(c_ref[...]).astype(BF16)
    o_ref[...] = _dot(s, w_ref[...].astype(BF16)) + b_ref[...]


def _ada(c, w, b):
    bsz, d = c.shape
    n = w.shape[1]
    tn = 1024
    return pl.pallas_call(
        _ada_body,
        grid=(n // tn,),
        in_specs=[pl.BlockSpec((bsz, d), lambda j: (0, 0)),
                  pl.BlockSpec((d, tn), lambda j: (0, j)),
                  pl.BlockSpec((1, tn), lambda j: (0, j))],
        out_specs=pl.BlockSpec((bsz, tn), lambda j: (0, j)),
        out_shape=SDS((bsz, n), F32),
        compiler_params=_params(("arbitrary",)),
        name="ada",
    )(c, w, b.reshape(1, n))


def _fold_body(wq_ref, wk_ref, o_ref):
    a = wq_ref[:, 0:QK_NOPE].astype(BF16)
    o_ref[...] = _dot_nt(a, wk_ref[...].astype(BF16)).astype(BF16)


def _fold(wq_h, wk_h):
    return pl.pallas_call(
        _fold_body,
        grid=(MLA_HEADS,),
        in_specs=[pl.BlockSpec((None, Q_RANK, QK_NOPE + QK_ROPE), lambda h: (h, 0, 0)),
                  pl.BlockSpec((None, KV_RANK, QK_NOPE), lambda h: (h, 0, 0))],
        out_specs=pl.BlockSpec((Q_RANK, KV_RANK), lambda h: (0, h)),
        out_shape=SDS((Q_RANK, MLA_HEADS * KV_RANK), BF16),
        compiler_params=_params(("arbitrary",)),
        name="fold",
    )(wq_h, wk_h)


def _inproj_body(*refs, q_transposed, rep):
    tm = refs[0].shape[0]
    halves = 2 if (q_transposed and rep == 1 and tm % (2 * TQ) == 0) else 1
    rows = tm // halves
    gens = [_inproj_steps(refs, slice(k * rows, (k + 1) * rows), q_transposed, rep) for k in range(halves)]
    done = object()
    while gens:
        gens = [g for g in gens if next(g, done) is not done]


def _inproj_steps(refs, rs, q_transposed, rep):
    n_in = 12 if q_transposed else 10
    x_ref, sh_ref, sc_ref, gmix_ref, win_ref, qg_ref, kvg_ref, wq_ref, tq_ref, tk_ref = refs[:10]
    z_ref, xbc_ref, dt_ref, ckv_ref, kr_ref = refs[n_in:n_in + 5]
    sc, sh = _mod_rows(sc_ref, rep), _mod_rows(sh_ref, rep)
    if rep > 1:
        sc, sh = sc[rs], sh[rs]
    h = _rms(x_ref[rs, :], gmix_ref[...]) * (1.0 + sc) + sh
    proj = _dot(h.astype(BF16), win_ref[...])
    yield
    tm = proj.shape[0]
    z_ref[rs, :] = proj[:, C_Z:C_XBC]
    xbc_ref[rs, :] = proj[:, C_XBC:C_QLAT]
    q_scale = ATTN_SCALE * LOG2E if q_transposed else ATTN_SCALE
    qn = (_rms(proj[:, C_QLAT:C_KVLAT], qg_ref[...]) * q_scale).astype(BF16)
    ckv = _rms(proj[:, C_KVLAT:C_MISC], kvg_ref[...])
    ckv_ref[rs, :] = ckv
    misc = proj[:, C_MISC:W_IN_COLS]
    lane = lax.broadcasted_iota(jnp.int32, misc.shape, 1)
    dt_ref[rs, :] = jnp.where(lane < SSD_HEADS, misc, 0.0)
    tk = tk_ref[rs, :]
    kr = (misc[:, MISC_KR:MISC_KR + QK_ROPE] * tk[:, :QK_ROPE]
          + misc[:, MISC_KRSW:MISC_KRSW + QK_ROPE] * tk[:, QK_ROPE:])
    kr_ref[rs, :] = kr
    nr = MLA_HEADS * QK_ROPE
    if q_transposed:
        wk_ref, wvt_ref = refs[10:12]
        kp_ref, vt_ref, qt_ref = refs[n_in + 5:]
        ckv_b = ckv.astype(BF16)
        k_nope = _dot(ckv_b, wk_ref[...])
        kr_pad = jnp.concatenate([kr, jnp.zeros((tm, PAIR_K - 2 * QK_NOPE - QK_ROPE), F32)], axis=1).astype(BF16)
        pad_rows = PAIR_V - 2 * V_DIM
        ones_row = jnp.where(lax.broadcasted_iota(jnp.int32, (pad_rows, tm), 0) == 0, 1.0, 0.0).astype(BF16)
        for p in range(PAIRS):
            kp_ref[p, rs, 0:2 * QK_NOPE] = k_nope[:, p * 2 * QK_NOPE:(p + 1) * 2 * QK_NOPE].astype(BF16)
            kp_ref[p, rs, 2 * QK_NOPE:PAIR_K] = kr_pad
            vt_ref[p, 0:2 * V_DIM, rs] = _dot_nt(wvt_ref[p], ckv_b).astype(BF16)
            vt_ref[p, 2 * V_DIM:PAIR_V, rs] = ones_row
        yield
        qt = _dot_nt(wq_ref[...], qn)
        yield
        n0 = MLA_HEADS * QK_NOPE
        tq = tq_ref[:, rs]
        cos_t = jnp.concatenate([tq[:QK_ROPE]] * MLA_HEADS, axis=0)
        sin_t = jnp.concatenate([tq[QK_ROPE:]] * MLA_HEADS, axis=0)
        rot = (qt[n0:n0 + nr] * cos_t + qt[n0 + nr:n0 + 2 * nr] * sin_t).astype(BF16)
        q_nope = qt[:n0].astype(BF16)
        zero = jnp.zeros((PAIR_K, TQ), BF16)
        c0 = rs.start // TQ
        for c in range(tm // TQ):
            toks = slice(c * TQ, (c + 1) * TQ)
            for p in range(PAIRS):
                for s in range(2):
                    hh = 2 * p + s
                    cols = slice(s * TQ, (s + 1) * TQ)
                    qt_ref[c0 + c, p, :, cols] = zero
                    qt_ref[c0 + c, p, s * QK_NOPE:(s + 1) * QK_NOPE, cols] = (
                        q_nope[hh * QK_NOPE:(hh + 1) * QK_NOPE, toks])
                    qt_ref[c0 + c, p, 2 * QK_NOPE:2 * QK_NOPE + QK_ROPE, cols] = (
                        rot[hh * QK_ROPE:(hh + 1) * QK_ROPE, toks])
    else:
        kc_ref, krb_ref, qa_ref, qr_ref = refs[n_in + 5:]
        kc_ref[rs, :] = ckv.astype(BF16)
        krb_ref[rs, :] = kr.astype(BF16)
        q = _dot(qn, wq_ref[...])
        tq = tq_ref[rs, :]
        qa_ref[rs, :] = q[:, :C_QROPE].astype(BF16)
        qr_ref[rs, :] = (q[:, C_QROPE:C_QROPE_SW] * tq[:, :nr] + q[:, C_QROPE_SW:WQ_COLS] * tq[:, nr:]).astype(BF16)


def _inproj(x3, mod3, gmix, win, qg, kvg, wq, tabq, tabk, *, q_transposed, tm, tk=None, wk=None, wvt=None):
    nb, L, d = x3.shape
    tm = min(tm, L)
    nt = L // tm
    rep, mod_spec = _mod_spec(mod3, L, tm, d)

    def tok(width):
        return pl.BlockSpec((None, tm, width), lambda b, i: (b, i, 0))

    def const(a):
        return pl.BlockSpec(a.shape, lambda b, i: (0,) * a.ndim, pipeline_mode=pl.Buffered(1))

    nr = MLA_HEADS * QK_ROPE
    out_specs = [tok(SSD_INNER), tok(CONV_DIM), tok(LANES), tok(KV_RANK), tok(QK_ROPE)]
    out_shape = [SDS((nb, L, SSD_INNER), F32), SDS((nb, L, CONV_DIM), F32), SDS((nb, L, LANES), F32),
                 SDS((nb, L, KV_RANK), F32), SDS((nb, L, QK_ROPE), F32)]
    if q_transposed:
        assert tm % TQ == 0 and tk % tm == 0
        r = tk // tm
        tabq_spec = pl.BlockSpec((2 * QK_ROPE, tm), lambda b, i: (0, i))
        out_specs += [pl.BlockSpec((None, PAIRS, tm, PAIR_K), lambda b, i: (b, 0, i, 0)),
                      pl.BlockSpec((None, None, PAIRS, PAIR_V, tm), lambda b, i: (b, i // r, 0, 0, i % r)),
                      pl.BlockSpec((None, tm // TQ, PAIRS, PAIR_K, 2 * TQ), lambda b, i: (b, i, 0, 0, 0))]
        out_shape += [SDS((nb, PAIRS, L, PAIR_K), BF16), SDS((nb, L // tk, PAIRS, PAIR_V, tk), BF16),
                      SDS((nb, L // TQ, PAIRS, PAIR_K, 2 * TQ), BF16)]
        extra = [wk, wvt]
    else:
        extra = []
        tabq_spec = pl.BlockSpec((tm, 2 * nr), lambda b, i: (i, 0))
        out_specs += [tok(KV_RANK), tok(QK_ROPE), tok(C_QROPE), tok(nr)]
        out_shape += [SDS((nb, L, KV_RANK), BF16), SDS((nb, L, QK_ROPE), BF16),
                      SDS((nb, L, C_QROPE), BF16), SDS((nb, L, nr), BF16)]
    return pl.pallas_call(
        functools.partial(_inproj_body, q_transposed=q_transposed, rep=rep),
        grid=(nb, nt),
        in_specs=[tok(d), mod_spec(0), mod_spec(1), const(gmix), const(win), const(qg), const(kvg), const(wq),
                  tabq_spec, pl.BlockSpec((tm, 2 * QK_ROPE), lambda b, i: (i, 0))] + [const(a) for a in extra],
        out_specs=out_specs,
        out_shape=out_shape,
        compiler_params=_params(("arbitrary", "arbitrary")),
        name="inproj_prompt" if q_transposed else "inproj_sample",
    )(x3, mod3, mod3, gmix, win, qg, kvg, wq, tabq, tabk, *extra)


def _ssd_reset(c, xbuf, hT):
    @pl.when(c == 0)
    def _():
        xbuf[0:8, :] = jnp.zeros((8, CONV_DIM), F32)
        hT[...] = jnp.zeros(hT.shape, F32)


def _ssd_final(c, n_chunks, hfin_ref, hT):
    @pl.when(c == n_chunks - 1)
    def _():
        for k in range(SSD_HEADS // 2):
            hfin_ref[k * LANES:(k + 1) * LANES, :] = hT[k].T


def _ssd_body(*refs, **kw):
    for _ in _ssd_steps(*refs, **kw):
        pass


def _ssd_steps(*refs, Q, seg, carry, chunk_pos=None):
    if carry:
        (xin_ref, z_ref, dt_ref, cw_ref, cb_ref, dtb_ref, alog_ref, dsk_ref, g_ref,
         y_ref, hfin_ref, xbuf, hT) = refs
    else:
        (xin_ref, dt_ref, cw_ref, cb_ref, dtb_ref, alog_ref, dsk_ref,
         ypre_ref, eacs_ref, xw_ref, dec_ref, bm_ref, cm_ref) = refs
    cw = cw_ref[...]
    cb = cb_ref[...]
    if carry:
        c, n_chunks = (pl.program_id(1), pl.num_programs(1)) if chunk_pos is None else chunk_pos
        if chunk_pos is None:
            _ssd_reset(c, xbuf, hT)
        xbuf[8:8 + Q, :] = xin_ref[...]
    xc_parts = []
    for lb in range(CONV_DIM // CONV_LANES):
        cols = slice(lb * CONV_LANES, (lb + 1) * CONV_LANES)
        acc = jnp.broadcast_to(cb[:, cols], (Q, CONV_LANES))
        for k in range(CONV_WIDTH):
            if carry:
                tap = xbuf[pl.ds(8 - (CONV_WIDTH - 1) + k, Q), cols]
            else:
                tap = xin_ref[k, :, cols]
            acc = acc + cw[k:k + 1, cols] * tap
        xc_parts.append(_silu(acc))
        yield
    if carry:
        xbuf[0:8, :] = xbuf[Q:Q + 8, :]
    xc = jnp.concatenate(xc_parts, axis=1)
    xs = xc[:, :SSD_INNER]
    gs = SSD_GROUPS * SSD_STATE
    bm = xc[:, SSD_INNER:SSD_INNER + gs]
    cm = xc[:, SSD_INNER + gs:]
    bm_b = bm.astype(BF16)
    cm_b = cm.astype(BF16)

    lane = lax.broadcasted_iota(jnp.int32, (Q, LANES), 1)
    v = dt_ref[...] + dtb_ref[...]
    dt = jnp.maximum(v, 0.0) + jnp.log1p(jnp.exp(-jnp.abs(v)))
    dt = jnp.where(lane < SSD_HEADS, dt, 0.0)
    dA = dt * (-jnp.exp(alog_ref[...]))
    ri = lax.broadcasted_iota(jnp.int32, (Q, Q), 0)
    ci = lax.broadcasted_iota(jnp.int32, (Q, Q), 1)
    if seg == Q:
        mask = ci <= ri
    else:
        same = (ri // seg) == (ci // seg)
        mask = jnp.logical_and(same, ci <= ri)
    acs = _split3_dot(jnp.where(mask, 1.0, 0.0).astype(BF16), dA)
    if seg == Q:
        acs_last = acs[Q - 1:Q, :]
    else:
        acs_last = _split3_dot(jnp.where(same, 1.0, 0.0).astype(BF16), dA)
    to_end = jnp.exp(acs_last - acs) * dt
    acsT = acs.T
    dtT = dt.T
    yield

    G = [_dot_nt(cm_b[:, g * SSD_STATE:(g + 1) * SSD_STATE], bm_b[:, g * SSD_STATE:(g + 1) * SSD_STATE])
         for g in range(SSD_GROUPS)]
    lane_lo = lane < SSD_HEAD_DIM
    heads_per_group = SSD_HEADS // SSD_GROUPS
    ypairs, epairs, xwpairs, decpairs = [], [], [], []
    for k in range(SSD_HEADS // 2):
        g = (2 * k) // heads_per_group
        xp = xs[:, k * LANES:(k + 1) * LANES]
        xhalf = (jnp.where(lane_lo, xp, 0.0).astype(BF16), jnp.where(lane_lo, 0.0, xp).astype(BF16))
        yk = jnp.zeros((Q, LANES), F32)
        for s in range(2):
            hh = 2 * k + s
            segm = acs[:, hh:hh + 1] - acsT[hh:hh + 1, :]
            m = G[g] * jnp.exp(jnp.where(mask, segm, -jnp.inf)) * dtT[hh:hh + 1, :]
            yk = yk + _dot(m.astype(BF16), xhalf[s])
            yield

        def pair(a):
            return jnp.where(lane_lo[:a.shape[0]], a[:, 2 * k:2 * k + 1], a[:, 2 * k + 1:2 * k + 2])

        e_p = jnp.exp(pair(acs))
        xw = xp * pair(to_end)
        dec = jnp.exp(pair(acs_last))
        if carry:
            h_prev = hT[k]
            yk = yk + _dot(cm_b[:, g * SSD_STATE:(g + 1) * SSD_STATE], h_prev.astype(BF16)) * e_p
            hT[k] = dec * h_prev + _dot_tn(bm_b[:, g * SSD_STATE:(g + 1) * SSD_STATE], xw.astype(BF16))
        else:
            epairs.append(e_p)
            xwpairs.append(xw)
            decpairs.append(dec)
        ypairs.append(yk)
        yield
    y = jnp.concatenate(ypairs, axis=1) + dsk_ref[...] * xs
    if carry:
        y = y * _silu(z_ref[...])
        y_ref[...] = _rms(y, g_ref[...]).astype(BF16)

        if chunk_pos is None:
            _ssd_final(c, n_chunks, hfin_ref, hT)
    else:
        ypre_ref[...] = y
        eacs_ref[...] = jnp.concatenate(epairs, axis=1)
        xw_ref[...] = jnp.concatenate(xwpairs, axis=1)
        dec_ref[...] = jnp.concatenate(decpairs, axis=1)
        bm_ref[...] = bm
        cm_ref[...] = cm


def _row(a, n):
    return jnp.pad(a.reshape(1, -1).astype(F32), ((0, 0), (0, n - a.size)))


def _ssd_prompt(xbc, z, dt, cw, cb, dtb, alog, dsk, g):
    nb, L, _ = xbc.shape
    Q = SSD_CHUNK if L % SSD_CHUNK == 0 else L
    nc = L // Q

    def tok(width):
        return pl.BlockSpec((None, Q, width), lambda b, c: (b, c, 0))

    def const(a):
        return pl.BlockSpec(a.shape, lambda b, c: (0,) * a.ndim)

    consts = (cw, cb, dtb, alog, dsk, g)
    return pl.pallas_call(
        functools.partial(_ssd_body, Q=Q, seg=Q, carry=True),
        grid=(nb, nc),
        in_specs=[tok(CONV_DIM), tok(SSD_INNER), tok(LANES)] + [const(a) for a in consts],
        out_specs=[tok(SSD_INNER), pl.BlockSpec((None, SSD_INNER, SSD_STATE), lambda b, c: (b, 0, 0))],
        out_shape=[SDS((nb, L, SSD_INNER), BF16), SDS((nb, SSD_INNER, SSD_STATE), F32)],
        scratch_shapes=[pltpu.VMEM((Q + 8, CONV_DIM), F32), pltpu.VMEM((SSD_HEADS // 2, SSD_STATE, LANES), F32)],
        compiler_params=_params(("arbitrary", "arbitrary")),
        name="ssd_prompt",
    )(xbc, z, dt, *consts)


def _ssd_sample(xsh, dt, cw, cb, dtb, alog, dsk, *, seg):
    _, T, _ = xsh.shape
    consts = (cw, cb, dtb, alog, dsk)

    def full(a):
        return pl.BlockSpec(a.shape, lambda i: (0,) * a.ndim)

    outs = [SDS((T, SSD_INNER), F32)] * 4 + [SDS((T, SSD_GROUPS * SSD_STATE), F32)] * 2
    return pl.pallas_call(
        functools.partial(_ssd_body, Q=T, seg=seg, carry=False),
        grid=(1,),
        in_specs=[full(xsh), full(dt)] + [full(a) for a in consts],
        out_specs=[full(o) for o in outs],
        out_shape=outs,
        compiler_params=_params(("arbitrary",)),
        name="ssd_sample",
    )(xsh, dt, *consts)


def _sstate_body(cm_ref, bm_ref, ypre_ref, eacs_ref, xw_ref, dec_ref, z_ref, s0_ref, g_ref, y_ref, sn_ref):
    s0 = s0_ref[...]
    s0b = s0.astype(BF16)
    cm = cm_ref[...].astype(BF16)
    bm = bm_ref[...].astype(BF16)
    rows = SSD_INNER // SSD_GROUPS
    yo = jnp.concatenate(
        [jnp.einsum("btn,bqn->btq", cm[:, :, g * SSD_STATE:(g + 1) * SSD_STATE],
                    s0b[:, g * rows:(g + 1) * rows, :], preferred_element_type=F32)
         for g in range(SSD_GROUPS)], axis=-1)
    y = (ypre_ref[...] + yo * eacs_ref[...]) * _silu(z_ref[...])
    y_ref[...] = _rms(y, g_ref[...]).astype(BF16)
    dec = dec_ref[...]
    hi = dec.astype(BF16)
    lo = (dec - hi.astype(F32)).astype(BF16)
    sel = jnp.where(lax.broadcasted_iota(jnp.int32, (dec.shape[0], dec.shape[1], SSD_STATE), 1) == 0,
                    1.0, 0.0).astype(BF16)
    dmat = (jnp.einsum("bjq,bjn->bqn", hi, sel, preferred_element_type=F32)
            + jnp.einsum("bjq,bjn->bqn", lo, sel, preferred_element_type=F32))
    xw = xw_ref[...].astype(BF16)
    upd = jnp.concatenate(
        [jnp.einsum("bjq,bjn->bqn", xw[:, :, g * rows:(g + 1) * rows],
                    bm[:, :, g * SSD_STATE:(g + 1) * SSD_STATE], preferred_element_type=F32)
         for g in range(SSD_GROUPS)], axis=1)
    sn_ref[...] = dmat * s0 + upd


def _sstate(cm, bm, ypre, eacs, xw, dec, z, s0, g, *, bs):
    nseq, t, _ = cm.shape
    bs = min(bs, nseq)

    def blk(a):
        return pl.BlockSpec((bs,) + a.shape[1:], lambda i: (i, 0, 0))

    ins = (cm, bm, ypre, eacs, xw, dec, z, s0)
    return pl.pallas_call(
        _sstate_body,
        grid=(nseq // bs,),
        in_specs=[blk(a) for a in ins] + [pl.BlockSpec(g.shape, lambda i: (0, 0))],
        out_specs=[blk(ypre), blk(s0)],
        out_shape=[SDS(ypre.shape, BF16), SDS(s0.shape, F32)],
        compiler_params=_params(("arbitrary",)),
        name="sstate",
    )(*ins, g)


def _attn_body(qt_ref, qn_ref, kp_ref, vt_ref, g_ref, o_ref, m_sc, acc_sc,
               s_a, s_b, s_c, smax_a, smax_b, smax_c, *, tk):
    i = pl.program_id(1)
    ncols = MLA_HEADS * TQ
    m_sc[...] = jnp.full(m_sc.shape, -jnp.inf, F32)
    acc_sc[...] = jnp.zeros(acc_sc.shape, F32)
    ct = 2 * TQ
    tiles = [slice(p * ct, (p + 1) * ct) for p in range(PAIRS)]

    def scores(j, s_ref, smax_ref, q_ref=qt_ref):
        k0 = pl.multiple_of(j * tk, tk)
        for p, cs in enumerate(tiles):
            s = _dot(kp_ref[p, pl.ds(k0, tk), :], q_ref[p])
            s_ref[:, cs] = s
            smax_ref[:, cs] = jnp.max(s, axis=0, keepdims=True)

    def softmax_pv(j, s_ref, smax_ref, masked):
        for p, cs in enumerate(tiles):
            s = s_ref[:, cs]
            if masked:
                key = j * tk + lax.broadcasted_iota(jnp.int32, (tk, ct), 0)
                tok = i * TQ + (lax.broadcasted_iota(jnp.int32, (tk, ct), 1) & (TQ - 1))
                s = jnp.where(key <= tok, s, -jnp.inf)
                smax = jnp.max(s, axis=0, keepdims=True)
            else:
                smax = smax_ref[:, cs]
            m_prev = m_sc[:, cs]
            m_new = jnp.maximum(m_prev, smax)
            alpha = jnp.exp2(m_prev - m_new)
            e = jnp.exp2(s - m_new)
            acc_sc[p] = alpha * acc_sc[p] + _dot(vt_ref[j, p], e.astype(BF16))
            m_sc[:, cs] = m_new

    nfull = (i * TQ) // tk

    @pl.when(i == 0)
    def _():
        scores(0, s_c, smax_c)

    def prefetch():
        scores(0, s_c, smax_c, qn_ref)

    @pl.when(nfull == 0)
    def _():
        softmax_pv(0, s_c, smax_c, True)
        prefetch()

    @pl.when(nfull >= 1)
    def _():
        scores(1, s_b, smax_b)
        softmax_pv(0, s_c, smax_c, False)

    def pair(p, carry):
        j = 2 * p + 1
        scores(j + 1, s_a, smax_a)
        softmax_pv(j, s_b, smax_b, False)
        scores(j + 2, s_b, smax_b)
        softmax_pv(j + 1, s_a, smax_a, False)
        return carry

    lax.fori_loop(0, jnp.maximum(nfull - 1, 0) // 2, pair, 0)
    odd = (nfull % 2) == 1

    @pl.when(odd)
    def _():
        prefetch()
        softmax_pv(nfull, s_b, smax_b, True)

    @pl.when(jnp.logical_and(jnp.logical_not(odd), nfull >= 2))
    def _():
        scores(nfull, s_a, smax_a)
        softmax_pv(nfull - 1, s_b, smax_b, False)
        prefetch()
        softmax_pv(nfull, s_a, smax_a, True)

    ys = []
    for hh in range(MLA_HEADS):
        p, s = divmod(hh, 2)
        cols = slice(s * TQ, (s + 1) * TQ)
        denom = acc_sc[p, 2 * V_DIM:2 * V_DIM + 1, cols]
        ys.append(acc_sc[p, s * V_DIM:(s + 1) * V_DIM, cols] * (1.0 / denom))
    yt = jnp.concatenate(ys, axis=0)
    yt = yt * lax.rsqrt(jnp.mean(yt * yt, axis=0, keepdims=True) + EPS)
    o_ref[...] = (yt.T * g_ref[...]).astype(BF16)


def _attn_prompt(qt, kp, vt, g, *, tk):
    nb, nq = qt.shape[:2]
    L = nq * TQ
    ncols = MLA_HEADS * TQ
    assert TQ & (TQ - 1) == 0 and L % tk == 0 and tk % TQ == 0

    def q_spec(index):
        return pl.BlockSpec((None, None, PAIRS, PAIR_K, 2 * TQ), lambda b, i: (b, index(i), 0, 0, 0))

    return pl.pallas_call(
        functools.partial(_attn_body, tk=tk),
        grid=(nb, nq),
        in_specs=[q_spec(lambda i: i), q_spec(lambda i: jnp.minimum(i + 1, nq - 1)),
                  pl.BlockSpec((None, PAIRS, L, PAIR_K), lambda b, i: (b, 0, 0, 0), pipeline_mode=pl.Buffered(1)),
                  pl.BlockSpec((None, L // tk, PAIRS, PAIR_V, tk), lambda b, i: (b, 0, 0, 0, 0),
                               pipeline_mode=pl.Buffered(1)),
                  pl.BlockSpec(g.shape, lambda b, i: (0, 0))],
        out_specs=pl.BlockSpec((None, TQ, MLA_INNER), lambda b, i: (b, i, 0)),
        out_shape=SDS((nb, L, MLA_INNER), BF16),
        scratch_shapes=[pltpu.VMEM((1, ncols), F32),
                        pltpu.VMEM((PAIRS, PAIR_V, 2 * TQ), F32),
                        pltpu.VMEM((tk, ncols), F32), pltpu.VMEM((tk, ncols), F32), pltpu.VMEM((tk, ncols), F32),
                        pltpu.VMEM((1, ncols), F32), pltpu.VMEM((1, ncols), F32), pltpu.VMEM((1, ncols), F32)],
        compiler_params=_params(("arbitrary", "arbitrary")),
        name="attn_prompt",
    )(qt, qt, kp, vt, g)


def _sattn_body(pt_ref, qa_ref, qr_ref, kn_ref, rn_ref, ckv_hbm, krt_hbm, o_ref, kbuf, rbuf, sem, *, npages, t_new):
    slot = _sattn_fetch(pt_ref, ckv_hbm, krt_hbm, kbuf, rbuf, sem, npages=npages)
    for _ in _sattn_steps(slot, qa_ref, qr_ref, kn_ref, rn_ref, o_ref, kbuf, rbuf, npages=npages, t_new=t_new,
                          split=SPLIT):
        pass


def _sattn_fetch(pt_ref, ckv_hbm, krt_hbm, kbuf, rbuf, sem, *, npages):
    b = pl.program_id(0)
    nseq = pl.num_programs(0)

    def copies(seq_page, slot, p):
        off = p * PAGE_SIZE
        return (pltpu.make_async_copy(ckv_hbm.at[seq_page], kbuf.at[slot, p], sem.at[0, slot]),
                pltpu.make_async_copy(krt_hbm.at[seq_page], rbuf.at[slot, :, pl.ds(off, PAGE_SIZE)],
                                      sem.at[1, slot]))

    def start_fetch(seq, slot):
        for p in range(npages):
            for prio, cp in enumerate(copies(pt_ref[seq * npages + p], slot, p)):
                cp.start(priority=prio)

    def wait_fetch(slot):
        pltpu.make_async_copy(ckv_hbm.at[pl.ds(0, npages)], kbuf.at[slot], sem.at[0, slot]).wait()
        pltpu.make_async_copy(rbuf.at[slot], rbuf.at[slot], sem.at[1, slot]).wait()

    @pl.when(b == 0)
    def _():
        start_fetch(0, 0)

    @pl.when(b + 1 < nseq)
    def _():
        start_fetch(b + 1, (b + 1) % 2)

    slot = b % 2
    wait_fetch(slot)
    return slot


def _sattn_steps(slot, qa_ref, qr_ref, kn_ref, rn_ref, o_ref, kbuf, rbuf, *, npages, t_new, split, anchor=None):
    split = math.gcd(split, npages)
    grp = math.gcd(SPLIT, split)
    pg = npages // split
    part = pg * PAGE_SIZE
    q = qa_ref[...]
    qr = qr_ref[...]
    kn = kn_ref[...]
    rn = rn_ref[...]
    kps, s_p = [], []
    for h0 in range(0, split, grp):
        hs = range(h0, h0 + grp)
        kps += [kbuf[slot, pl.ds(h * pg, pg)].reshape(part, KV_RANK).astype(BF16) for h in hs]
        s_main = [_dot_nt(q, kps[h]) for h in hs]
        s_rope = [_dot(qr, rbuf[slot, :, h * part:(h + 1) * part].astype(BF16)) for h in hs]
        s_p += [a + b for a, b in zip(s_main, s_rope)]
        yield
    s_n = _dot_nt(q, kn) + _dot_nt(qr, rn)
    rows = t_new * MLA_HEADS
    tok = lax.broadcasted_iota(jnp.int32, (rows, t_new), 0) // MLA_HEADS
    col = lax.broadcasted_iota(jnp.int32, (rows, t_new), 1)
    s_n = jnp.where(col <= tok, s_n, -jnp.inf)
    m = jnp.max(s_n, axis=-1, keepdims=True)
    for s in s_p:
        m = jnp.maximum(m, jnp.max(s, axis=-1, keepdims=True))
    if anchor is not None:
        bits = pltpu.bitcast(anchor[0:16, 0:LANES].astype(F32), jnp.uint32)
        m = m + ((bits >> 16) >> 16).astype(F32)[0:1, 0:1]
    pn = jnp.exp(s_n - m)
    pp = [jnp.exp(s - m) for s in s_p]
    l = jnp.sum(pn, axis=-1, keepdims=True)
    for p in pp:
        l = l + jnp.sum(p, axis=-1, keepdims=True)
    inv = 1.0 / l
    o = _dot(pn.astype(BF16), kn) * inv
    yield
    for h0 in range(0, split, grp):
        parts = [_dot(pp[h].astype(BF16), kps[h]) for h in range(h0, h0 + grp)]
        for part_o in parts:
            o = o + part_o * inv
        yield
    o_ref[...] = o


def _sattn(page_table, qa, qr, kn, rn, cache_kv, cache_krt):
    nseq, npages = page_table.shape
    t_new = kn.shape[1]
    rows = t_new * MLA_HEADS
    grid_spec = pltpu.PrefetchScalarGridSpec(
        num_scalar_prefetch=1,
        grid=(nseq,),
        in_specs=[pl.BlockSpec((rows, KV_RANK), lambda b, pt: (b, 0)),
                  pl.BlockSpec((rows, QK_ROPE), lambda b, pt: (b, 0)),
                  pl.BlockSpec((None, t_new, KV_RANK), lambda b, pt: (b, 0, 0)),
                  pl.BlockSpec((None, t_new, QK_ROPE), lambda b, pt: (b, 0, 0)),
                  pl.BlockSpec(memory_space=pl.ANY),
                  pl.BlockSpec(memory_space=pl.ANY)],
        out_specs=pl.BlockSpec((rows, KV_RANK), lambda b, pt: (b, 0)),
        scratch_shapes=[pltpu.VMEM((2, npages, PAGE_SIZE, KV_RANK), F32),
                        pltpu.VMEM((2, QK_ROPE, npages * PAGE_SIZE), F32),
                        pltpu.SemaphoreType.DMA((2, 2))],
    )
    return pl.pallas_call(
        functools.partial(_sattn_body, npages=npages, t_new=t_new),
        grid_spec=grid_spec,
        out_shape=SDS((nseq * rows, KV_RANK), F32),
        compiler_params=_params(("arbitrary",)),
        name="attn_sample",
    )(page_table.reshape(-1), qa, qr, kn, rn, cache_kv, cache_krt)


def _ssd_sattn_body(pt_ref, qa_ref, qr_ref, kn_ref, rn_ref, ckv_hbm, krt_hbm,
                    xin_ref, z_ref, dt_ref, cw_ref, cb_ref, dtb_ref, alog_ref, dsk_ref, g_ref,
                    o_ref, y_ref, hfin_ref, kbuf, rbuf, sem, xbuf, hT, *, npages, t_new, Q, n_chunks):
    c = pl.program_id(0) % n_chunks
    _ssd_reset(c, xbuf, hT)
    slot = _sattn_fetch(pt_ref, ckv_hbm, krt_hbm, kbuf, rbuf, sem, npages=npages)
    ssd = _ssd_steps(xin_ref, z_ref, dt_ref, cw_ref, cb_ref, dtb_ref, alog_ref, dsk_ref, g_ref, y_ref, hfin_ref,
                     xbuf, hT, Q=Q, seg=Q, carry=True, chunk_pos=(c, n_chunks))
    att = _sattn_steps(slot, qa_ref, qr_ref, kn_ref, rn_ref, o_ref, kbuf, rbuf, npages=npages, t_new=t_new,
                       split=FUSED_SPLIT, anchor=y_ref)
    done = object()
    for _ in range(math.gcd(FUSED_SPLIT, npages) // math.gcd(SPLIT, FUSED_SPLIT, npages)):
        next(att)
        next(ssd, done)
    for _ in ssd:
        pass
    for _ in att:
        pass
    _ssd_final(c, n_chunks, hfin_ref, hT)


def _ssd_sattn(page_table, qa, qr, kn, rn, cache_kv, cache_krt, xbc, z, dt, cw, cb, dtb, alog, dsk, g):
    nseq, npages = page_table.shape
    t_new = kn.shape[1]
    rows = t_new * MLA_HEADS
    nb, L, _ = xbc.shape
    Q = SSD_CHUNK if L % SSD_CHUNK == 0 else L
    nc = L // Q
    assert nseq == nb * nc
    consts = (cw, cb, dtb, alog, dsk, g)

    def tok(width):
        return pl.BlockSpec((None, Q, width), lambda s, pt: (s // nc, s % nc, 0))

    def const(a):
        return pl.BlockSpec(a.shape, lambda s, pt: (0,) * a.ndim)

    grid_spec = pltpu.PrefetchScalarGridSpec(
        num_scalar_prefetch=1,
        grid=(nseq,),
        in_specs=[pl.BlockSpec((rows, KV_RANK), lambda s, pt: (s, 0)),
                  pl.BlockSpec((rows, QK_ROPE), lambda s, pt: (s, 0)),
                  pl.BlockSpec((None, t_new, KV_RANK), lambda s, pt: (s, 0, 0)),
                  pl.BlockSpec((None, t_new, QK_ROPE), lambda s, pt: (s, 0, 0)),
                  pl.BlockSpec(memory_space=pl.ANY),
                  pl.BlockSpec(memory_space=pl.ANY),
                  tok(CONV_DIM), tok(SSD_INNER), tok(LANES)] + [const(a) for a in consts],
        out_specs=[pl.BlockSpec((rows, KV_RANK), lambda s, pt: (s, 0)),
                   tok(SSD_INNER),
                   pl.BlockSpec((None, SSD_INNER, SSD_STATE), lambda s, pt: (s // nc, 0, 0))],
        scratch_shapes=[pltpu.VMEM((2, npages, PAGE_SIZE, KV_RANK), F32),
                        pltpu.VMEM((2, QK_ROPE, npages * PAGE_SIZE), F32),
                        pltpu.SemaphoreType.DMA((2, 2)),
                        pltpu.VMEM((Q + 8, CONV_DIM), F32),
                        pltpu.VMEM((SSD_HEADS // 2, SSD_STATE, LANES), F32)],
    )
    return pl.pallas_call(
        functools.partial(_ssd_sattn_body, npages=npages, t_new=t_new, Q=Q, n_chunks=nc),
        grid_spec=grid_spec,
        out_shape=[SDS((nseq * rows, KV_RANK), F32), SDS((nb, L, SSD_INNER), BF16),
                   SDS((nb, SSD_INNER, SSD_STATE), F32)],
        compiler_params=_params(("arbitrary",)),
        name="ssd_prompt_attn_sample",
    )(page_table.reshape(-1), qa, qr, kn, rn, cache_kv, cache_krt, xbc, z, dt, *consts)


def _apost_body(o_ref, wuv_ref, g_ref, y_ref):
    tm = o_ref.shape[0]
    o = jnp.zeros((tm, MLA_INNER), F32)
    for hh in range(MLA_HEADS):
        o = o + _dot(o_ref[:, hh * KV_RANK:(hh + 1) * KV_RANK].astype(BF16), wuv_ref[hh])
    y_ref[...] = _rms(o, g_ref[...]).astype(BF16)


def _apost(o_lat, wuv_pad, g):
    T = o_lat.shape[0]
    return pl.pallas_call(
        _apost_body,
        grid=(1,),
        in_specs=[pl.BlockSpec(o_lat.shape, lambda i: (0, 0)),
                  pl.BlockSpec(wuv_pad.shape, lambda i: (0, 0, 0)),
                  pl.BlockSpec(g.shape, lambda i: (0, 0))],
        out_specs=pl.BlockSpec((T, MLA_INNER), lambda i: (0, 0)),
        out_shape=SDS((T, MLA_INNER), BF16),
        compiler_params=_params(("arbitrary",)),
        name="attn_post",
    )(o_lat, wuv_pad, g)


def _mlp_body(x_ref, ys_ref, ya_ref, g1_ref, sh2_ref, sc2_ref, g2_ref, shf_ref, scf_ref,
              wout_ref, gmlp_ref, wup_ref, wdn_ref, gfin_ref, o_ref, *, final, tf, rep):
    yy = jnp.concatenate([ys_ref[...], ya_ref[...]], axis=-1)
    x1 = x_ref[...] + _mod_rows(g1_ref, rep) * _dot(yy, wout_ref[...])
    h2 = (_rms(x1, gmlp_ref[...]) * (1.0 + _mod_rows(sc2_ref, rep)) + _mod_rows(sh2_ref, rep)).astype(BF16)
    acc = jnp.zeros(x1.shape, F32)
    for c in range(D_FF // tf):
        u = jnp.maximum(_dot(h2, wup_ref[:, c * tf:(c + 1) * tf]), 0.0)
        acc = acc + _dot((u * u).astype(BF16), wdn_ref[c * tf:(c + 1) * tf, :])
    x2 = x1 + _mod_rows(g2_ref, rep) * acc
    if final:
        x2 = _rms(x2, gfin_ref[...]) * (1.0 + _mod_rows(scf_ref, rep)) + _mod_rows(shf_ref, rep)
    o_ref[...] = x2


def _mlp(x3, ys, ya, mod3, wout, gmlp, wup, wdn, gfin, *, final, tm, tf):
    nb, L, d = x3.shape
    tm = min(tm, L)
    rep, mod_spec = _mod_spec(mod3, L, tm, d)

    def tok(width):
        return pl.BlockSpec((None, tm, width), lambda b, i: (b, i, 0))

    def const(a):
        return pl.BlockSpec(a.shape, lambda b, i: (0,) * a.ndim, pipeline_mode=pl.Buffered(1))

    return pl.pallas_call(
        functools.partial(_mlp_body, final=final, tf=tf, rep=rep),
        grid=(nb, L // tm),
        in_specs=[tok(d), tok(SSD_INNER), tok(MLA_INNER)] + [mod_spec(k) for k in (2, 3, 4, 5, 6, 7)]
                 + [const(wout), const(gmlp), const(wup), const(wdn), const(gfin)],
        out_specs=tok(d),
        out_shape=SDS((nb, L, d), F32),
        compiler_params=_params(("arbitrary", "arbitrary")),
        name="mlp",
    )(x3, ys, ya, mod3, mod3, mod3, mod3, mod3, mod3, wout, gmlp, wup, wdn, gfin)


def _rope_tables(pos):
    inv = 1.0 / (ROPE_THETA ** (np.arange(0, QK_ROPE, 2, dtype=np.float64) / QK_ROPE))
    ang = pos.astype(np.float64)[:, None] * inv[None, :]
    cos, sin = np.cos(ang).astype(np.float32), np.sin(ang).astype(np.float32)
    c32 = np.concatenate([cos, cos], axis=-1)
    s32 = np.concatenate([-sin, sin], axis=-1)
    tab1 = np.concatenate([c32, s32], axis=-1)
    tabq = np.concatenate([np.tile(c32, (1, MLA_HEADS)), np.tile(s32, (1, MLA_HEADS))], axis=-1)
    return tab1, tabq


def _swap_halves(w):
    half = w.shape[-1] // 2
    return jnp.concatenate([w[..., half:], w[..., :half]], axis=-1)


def kernel(x_prompt, x_sample, cache_kv_latent, cache_k_rope, state_conv, state_ssm, page_table,
           c_prompt, c_sample, w_ada, b_ada, norm_mix_g, w_in, conv_w, conv_b, dt_bias, a_log,
           d_skip, norm_ssd_g, q_norm_g, kv_norm_g, w_uq, w_uk, w_uv, norm_attn_g, w_out,
           norm_mlp_g, w_up, w_down, w_ada_final, b_ada_final, norm_final_g):
    depth = w_in.shape[0]
    b_p, seq, d = x_prompt.shape
    n_seq, t_new, _ = x_sample.shape
    n_tok_s = n_seq * t_new
    past_len = page_table.shape[1] * PAGE_SIZE

    c_all = jnp.concatenate([c_prompt, c_sample], axis=0)
    ada_fin = _ada(c_all, w_ada_final, b_ada_final)
    tab1_p, _ = _rope_tables(np.arange(seq))
    tab1_s, tabq_s = _rope_tables(past_len + np.arange(t_new))
    tab1_pt = jnp.asarray(np.ascontiguousarray(tab1_p.T))
    tab1_p = jnp.asarray(tab1_p)
    tabq_s = jnp.asarray(np.tile(tabq_s, (n_seq, 1)))
    tab1_s = jnp.asarray(np.tile(tab1_s, (n_seq, 1)))

    xp = x_prompt
    xs = x_sample.reshape(1, n_tok_s, d)
    outs_p, outs_s = [], []
    for l in range(depth):
        final = l == depth - 1
        wi = w_in[l]
        c1 = SSD_INNER
        c2 = c1 + CONV_DIM
        c3 = c2 + SSD_HEADS
        c4 = c3 + Q_RANK
        c5 = c4 + KV_RANK
        w_kr = wi[:, c5:]
        win = jnp.concatenate(
            [wi[:, :c2], wi[:, c3:c5], wi[:, c2:c3], jnp.zeros((d, MISC_KR - SSD_HEADS), F32), w_kr,
             _swap_halves(w_kr), jnp.zeros((d, LANES - MISC_KRSW - QK_ROPE), F32)], axis=1).astype(BF16)
        wq_h = w_uq[l].reshape(Q_RANK, MLA_HEADS, QK_NOPE + QK_ROPE)
        w_rope = wq_h[:, :, QK_NOPE:]
        wfold = _fold(jnp.transpose(wq_h, (1, 0, 2)), jnp.transpose(w_uk[l], (1, 0, 2)))
        wq = jnp.concatenate([wfold, w_rope.reshape(Q_RANK, -1).astype(BF16),
                              _swap_halves(w_rope).reshape(Q_RANK, -1).astype(BF16)], axis=1)
        wuv_pad = jnp.zeros((MLA_HEADS, KV_RANK, MLA_HEADS, V_DIM), F32)
        wuv_pad = wuv_pad.at[jnp.arange(MLA_HEADS), :, jnp.arange(MLA_HEADS), :].set(
            jnp.transpose(w_uv[l], (1, 0, 2)))
        wuv_pad = wuv_pad.reshape(MLA_HEADS, KV_RANK, MLA_INNER).astype(BF16)
        wq_p = jnp.concatenate([wq_h[:, :, :QK_NOPE].reshape(Q_RANK, -1), w_rope.reshape(Q_RANK, -1),
                                _swap_halves(w_rope).reshape(Q_RANK, -1)], axis=1).T.astype(BF16)
        wk = w_uk[l].reshape(KV_RANK, MLA_HEADS * QK_NOPE).astype(BF16)
        wvt = jnp.transpose(w_uv[l], (1, 2, 0)).reshape(PAIRS, 2 * V_DIM, KV_RANK).astype(BF16)
        wout = w_out[l].astype(BF16)
        wup = w_up[l].astype(BF16)
        wdn = w_down[l].astype(BF16)
        gmix = norm_mix_g[l].reshape(1, d)
        gmlp = norm_mlp_g[l].reshape(1, d)
        gfin = norm_final_g.reshape(1, d)
        qg = q_norm_g[l].reshape(1, Q_RANK)
        kvg = kv_norm_g[l].reshape(1, KV_RANK)
        gssd = norm_ssd_g[l].reshape(1, SSD_INNER)
        gattn = norm_attn_g[l].reshape(1, MLA_INNER)
        cw = conv_w[l]
        cb = conv_b[l].reshape(1, CONV_DIM)
        dtb = _row(dt_bias[l], LANES)
        alog = _row(a_log[l], LANES)
        dsk = jnp.repeat(d_skip[l].astype(F32), SSD_HEAD_DIM).reshape(1, SSD_INNER)

        ada = _ada(c_all, w_ada[l], b_ada[l])
        mod = jnp.concatenate([ada, ada_fin], axis=1)
        mod_p = mod[:b_p].reshape(b_p, 1, 8 * d)
        mod_s = mod[b_p:].reshape(1, n_seq, 8 * d)

        def seqs(a):
            return a.reshape(n_seq, t_new, a.shape[-1])

        tk = min(TK_ATTN, seq)
        z_p, xbc_p, dtr_p, ckv_p, kr_p, kp, vt, qt = _inproj(
            xp, mod_p, gmix, win, qg, kvg, wq_p, tab1_pt, tab1_p, q_transposed=True, tm=TM_INPROJ_PROMPT, tk=tk,
            wk=wk, wvt=wvt)
        z_s, xbc_s, dtr_s, ckv_s, kr_s, kc, krb, qa, qr = _inproj(
            xs, mod_s, gmix, win, qg, kvg, wq, tabq_s, tab1_s, q_transposed=False, tm=TM_INPROJ_SAMPLE)
        xpad = jnp.concatenate([state_conv[l], xbc_s.reshape(n_seq, t_new, CONV_DIM)], axis=1)
        xsh = jnp.stack([xpad[:, k:k + t_new].reshape(n_tok_s, CONV_DIM) for k in range(CONV_WIDTH)])
        ypre, eacs, xw, dec, bm, cm = _ssd_sample(xsh, dtr_s[0], cw, cb, dtb, alog, dsk, seg=t_new)
        y_ssd_s, s_new = _sstate(seqs(cm), seqs(bm), seqs(ypre), seqs(eacs), seqs(xw), seqs(dec), seqs(z_s[0]),
                                 state_ssm[l].reshape(n_seq, SSD_INNER, SSD_STATE), gssd, bs=BS_STATE)

        sattn_args = (page_table, qa.reshape(n_tok_s * MLA_HEADS, KV_RANK),
                      qr.reshape(n_tok_s * MLA_HEADS, QK_ROPE), seqs(kc[0]), seqs(krb[0]),
                      cache_kv_latent[l], jnp.swapaxes(cache_k_rope[l], -1, -2))
        ssd_args = (xbc_p, z_p, dtr_p, cw, cb, dtb, alog, dsk, gssd)
        chunk = SSD_CHUNK if seq % SSD_CHUNK == 0 else seq
        if n_seq == b_p * (seq // chunk):
            o_lat, y_ssd_p, hfin = _ssd_sattn(*sattn_args, *ssd_args)
        else:
            y_ssd_p, hfin = _ssd_prompt(*ssd_args)
            o_lat = _sattn(*sattn_args)

        y_attn_p = _attn_prompt(qt, kp, vt, gattn, tk=tk)
        xp = _mlp(xp, y_ssd_p, y_attn_p, mod_p, wout, gmlp, wup, wdn, gfin, final=final, tm=TM_MLP, tf=TF_MLP)
        tail = min(seq, CONV_WIDTH - 1)
        conv_tail = jnp.concatenate([jnp.zeros((b_p, CONV_WIDTH - 1 - tail, CONV_DIM), F32),
                                     xbc_p[:, seq - tail:]], axis=1)
        outs_p.append((ckv_p, kr_p, conv_tail, hfin.reshape(b_p, SSD_HEADS, SSD_HEAD_DIM, SSD_STATE)))

        y_attn_s = _apost(o_lat.reshape(n_tok_s, MLA_HEADS * KV_RANK), wuv_pad, gattn)
        xs = _mlp(xs, y_ssd_s.reshape(1, n_tok_s, SSD_INNER), y_attn_s.reshape(1, n_tok_s, MLA_INNER), mod_s,
                  wout, gmlp, wup, wdn, gfin, final=final, tm=TM_MLP, tf=TF_MLP)
        outs_s.append((seqs(ckv_s[0]), seqs(kr_s[0]), xpad[:, t_new:],
                       s_new.reshape(n_seq, SSD_HEADS, SSD_HEAD_DIM, SSD_STATE)))

    def stack(outs, k):
        return jnp.stack([o[k] for o in outs])

    return (xp, xs.reshape(n_seq, t_new, d),
            stack(outs_p, 0), stack(outs_p, 1), stack(outs_p, 2), stack(outs_p, 3),
            stack(outs_s, 0), stack(outs_s, 1), stack(outs_s, 2), stack(outs_s, 3))
```

```python
import functools
import math

import jax
import jax.numpy as jnp
import numpy as np
from jax import lax
from jax.experimental import pallas as pl
from jax.experimental.pallas import tpu as pltpu

F32 = jnp.float32
BF16 = jnp.bfloat16
SDS = jax.ShapeDtypeStruct

D_MODEL = 1024
SSD_HEADS = 8
SSD_HEAD_DIM = 64
SSD_INNER = SSD_HEADS * SSD_HEAD_DIM
SSD_GROUPS = 2
SSD_STATE = 128
CONV_WIDTH = 4
SSD_CHUNK = 128
CONV_DIM = SSD_INNER + 2 * SSD_GROUPS * SSD_STATE
MLA_HEADS = 8
QK_NOPE = 64
QK_ROPE = 32
V_DIM = 64
KV_RANK = 256
Q_RANK = 384
MLA_INNER = MLA_HEADS * V_DIM
ROPE_THETA = 10000.0
ATTN_SCALE = 1.0 / math.sqrt(QK_NOPE + QK_ROPE)
LOG2E = math.log2(math.e)
PAGE_SIZE = 128
D_FF = 4 * D_MODEL
EPS = 1e-6

LANES = 128
TQ = 128
PAIRS = MLA_HEADS // 2
PAIR_K = 256
PAIR_V = 2 * V_DIM + 16
SPLIT = 2
FUSED_SPLIT = 32
CONV_LANES = 128
C_Z = 0
C_XBC = C_Z + SSD_INNER
C_QLAT = C_XBC + CONV_DIM
C_KVLAT = C_QLAT + Q_RANK
C_MISC = C_KVLAT + KV_RANK
W_IN_COLS = C_MISC + LANES
MISC_KR = 32
MISC_KRSW = 64
C_QROPE = MLA_HEADS * KV_RANK
C_QROPE_SW = C_QROPE + MLA_HEADS * QK_ROPE
WQ_COLS = C_QROPE_SW + MLA_HEADS * QK_ROPE

VMEM_LIMIT = 52 * 1024 * 1024

TM_INPROJ_PROMPT = 512
TM_INPROJ_SAMPLE = 256
TM_MLP = 512
TF_MLP = 2048
TK_ATTN = 512
BS_STATE = 8


def _dot(a, b):
    return jnp.dot(a, b, preferred_element_type=F32)


def _dot_nt(a, b):
    return lax.dot_general(a, b, (((1,), (1,)), ((), ())), preferred_element_type=F32)


def _dot_tn(a, b):
    return lax.dot_general(a, b, (((0,), (0,)), ((), ())), preferred_element_type=F32)


def _silu(x):
    return x * jax.nn.sigmoid(x)


def _rms(x, g):
    return x * lax.rsqrt(jnp.mean(x * x, axis=-1, keepdims=True) + EPS) * g


def _split3_dot(mask_bf16, v):
    v1 = v.astype(BF16)
    r1 = v - v1.astype(F32)
    v2 = r1.astype(BF16)
    v3 = (r1 - v2.astype(F32)).astype(BF16)
    return _dot(mask_bf16, v1) + _dot(mask_bf16, v2) + _dot(mask_bf16, v3)


def _mod_spec(mod3, L, tm, d):
    rows = mod3.shape[1]
    if rows == 1:
        return 1, lambda k: pl.BlockSpec((None, 1, d), lambda b, i: (b, 0, k))
    rep = L // rows
    return rep, lambda k: pl.BlockSpec((None, tm // rep, d), lambda b, i: (b, i, k))


def _mod_rows(ref, rep):
    m = ref[...]
    if rep == 1:
        return m
    n = m.shape[0] * rep
    sel = (lax.broadcasted_iota(jnp.int32, (n, m.shape[0]), 0) // rep
           == lax.broadcasted_iota(jnp.int32, (n, m.shape[0]), 1))
    sel = jnp.where(sel, 1.0, 0.0).astype(BF16)
    hi = m.astype(BF16)
    lo = (m - hi.astype(F32)).astype(BF16)
    return _dot(sel, hi) + _dot(sel, lo)


def _params(sem, vmem=VMEM_LIMIT):
    return pltpu.CompilerParams(dimension_semantics=sem, vmem_limit_bytes=vmem)


def _ada_body(c_ref, w_ref, b_ref, o_ref):
    s = _silu(c_ref[...]).astype(BF16)
    o_ref[...] = _dot(s, w_ref[...].astype(BF16)) + b_ref[...]


def _ada(c, w, b):
    bsz, d = c.shape
    n = w.shape[1]
    tn = 1024
    return pl.pallas_call(
        _ada_body,
        grid=(n // tn,),
        in_specs=[pl.BlockSpec((bsz, d), lambda j: (0, 0)),
                  pl.BlockSpec((d, tn), lambda j: (0, j)),
                  pl.BlockSpec((1, tn), lambda j: (0, j))],
        out_specs=pl.BlockSpec((bsz, tn), lambda j: (0, j)),
        out_shape=SDS((bsz, n), F32),
        compiler_params=_params(("arbitrary",)),
        name="ada",
    )(c, w, b.reshape(1, n))


def _fold_body(wq_ref, wk_ref, o_ref):
    a = wq_ref[:, 0:QK_NOPE].astype(BF16)
    o_ref[...] = _dot_nt(a, wk_ref[...].astype(BF16)).astype(BF16)


def _fold(wq_h, wk_h):
    return pl.pallas_call(
        _fold_body,
        grid=(MLA_HEADS,),
        in_specs=[pl.BlockSpec((None, Q_RANK, QK_NOPE + QK_ROPE), lambda h: (h, 0, 0)),
                  pl.BlockSpec((None, KV_RANK, QK_NOPE), lambda h: (h, 0, 0))],
        out_specs=pl.BlockSpec((Q_RANK, KV_RANK), lambda h: (0, h)),
        out_shape=SDS((Q_RANK, MLA_HEADS * KV_RANK), BF16),
        compiler_params=_params(("arbitrary",)),
        name="fold",
    )(wq_h, wk_h)


def _inproj_body(*refs, q_transposed, rep):
    tm = refs[0].shape[0]
    halves = 2 if (q_transposed and rep == 1 and tm % (2 * TQ) == 0) else 1
    rows = tm // halves
    gens = [_inproj_steps(refs, slice(k * rows, (k + 1) * rows), q_transposed, rep) for k in range(halves)]
    done = object()
    while gens:
        gens = [g for g in gens if next(g, done) is not done]


def _inproj_steps(refs, rs, q_transposed, rep):
    n_in = 12 if q_transposed else 10
    x_ref, sh_ref, sc_ref, gmix_ref, win_ref, qg_ref, kvg_ref, wq_ref, tq_ref, tk_ref = refs[:10]
    z_ref, xbc_ref, dt_ref, ckv_ref, kr_ref = refs[n_in:n_in + 5]
    sc, sh = _mod_rows(sc_ref, rep), _mod_rows(sh_ref, rep)
    if rep > 1:
        sc, sh = sc[rs], sh[rs]
    h = _rms(x_ref[rs, :], gmix_ref[...]) * (1.0 + sc) + sh
    proj = _dot(h.astype(BF16), win_ref[...])
    yield
    tm = proj.shape[0]
    z_ref[rs, :] = proj[:, C_Z:C_XBC]
    xbc_ref[rs, :] = proj[:, C_XBC:C_QLAT]
    q_scale = ATTN_SCALE * LOG2E if q_transposed else ATTN_SCALE
    qn = (_rms(proj[:, C_QLAT:C_KVLAT], qg_ref[...]) * q_scale).astype(BF16)
    ckv = _rms(proj[:, C_KVLAT:C_MISC], kvg_ref[...])
    ckv_ref[rs, :] = ckv
    misc = proj[:, C_MISC:W_IN_COLS]
    lane = lax.broadcasted_iota(jnp.int32, misc.shape, 1)
    dt_ref[rs, :] = jnp.where(lane < SSD_HEADS, misc, 0.0)
    tk = tk_ref[rs, :]
    kr = (misc[:, MISC_KR:MISC_KR + QK_ROPE] * tk[:, :QK_ROPE]
          + misc[:, MISC_KRSW:MISC_KRSW + QK_ROPE] * tk[:, QK_ROPE:])
    kr_ref[rs, :] = kr
    nr = MLA_HEADS * QK_ROPE
    if q_transposed:
        wk_ref, wvt_ref = refs[10:12]
        kp_ref, vt_ref, qt_ref = refs[n_in + 5:]
        ckv_b = ckv.astype(BF16)
        k_nope = _dot(ckv_b, wk_ref[...])
        kr_pad = jnp.concatenate([kr, jnp.zeros((tm, PAIR_K - 2 * QK_NOPE - QK_ROPE), F32)], axis=1).astype(BF16)
        pad_rows = PAIR_V - 2 * V_DIM
        ones_row = jnp.where(lax.broadcasted_iota(jnp.int32, (pad_rows, tm), 0) == 0, 1.0, 0.0).astype(BF16)
        for p in range(PAIRS):
            kp_ref[p, rs, 0:2 * QK_NOPE] = k_nope[:, p * 2 * QK_NOPE:(p + 1) * 2 * QK_NOPE].astype(BF16)
            kp_ref[p, rs, 2 * QK_NOPE:PAIR_K] = kr_pad
            vt_ref[p, 0:2 * V_DIM, rs] = _dot_nt(wvt_ref[p], ckv_b).astype(BF16)
            vt_ref[p, 2 * V_DIM:PAIR_V, rs] = ones_row
        yield
        qt = _dot_nt(wq_ref[...], qn)
        yield
        n0 = MLA_HEADS * QK_NOPE
        tq = tq_ref[:, rs]
        cos_t = jnp.concatenate([tq[:QK_ROPE]] * MLA_HEADS, axis=0)
        sin_t = jnp.concatenate([tq[QK_ROPE:]] * MLA_HEADS, axis=0)
        rot = (qt[n0:n0 + nr] * cos_t + qt[n0 + nr:n0 + 2 * nr] * sin_t).astype(BF16)
        q_nope = qt[:n0].astype(BF16)
        zero = jnp.zeros((PAIR_K, TQ), BF16)
        c0 = rs.start // TQ
        for c in range(tm // TQ):
            toks = slice(c * TQ, (c + 1) * TQ)
            for p in range(PAIRS):
                for s in range(2):
                    hh = 2 * p + s
                    cols = slice(s * TQ, (s + 1) * TQ)
                    qt_ref[c0 + c, p, :, cols] = zero
                    qt_ref[c0 + c, p, s * QK_NOPE:(s + 1) * QK_NOPE, cols] = (
                        q_nope[hh * QK_NOPE:(hh + 1) * QK_NOPE, toks])
                    qt_ref[c0 + c, p, 2 * QK_NOPE:2 * QK_NOPE + QK_ROPE, cols] = (
                        rot[hh * QK_ROPE:(hh + 1) * QK_ROPE, toks])
    else:
        kc_ref, krb_ref, qa_ref, qr_ref = refs[n_in + 5:]
        kc_ref[rs, :] = ckv.astype(BF16)
        krb_ref[rs, :] = kr.astype(BF16)
        q = _dot(qn, wq_ref[...])
        tq = tq_ref[rs, :]
        qa_ref[rs, :] = q[:, :C_QROPE].astype(BF16)
        qr_ref[rs, :] = (q[:, C_QROPE:C_QROPE_SW] * tq[:, :nr] + q[:, C_QROPE_SW:WQ_COLS] * tq[:, nr:]).astype(BF16)


def _inproj(x3, mod3, gmix, win, qg, kvg, wq, tabq, tabk, *, q_transposed, tm, tk=None, wk=None, wvt=None):
    nb, L, d = x3.shape
    tm = min(tm, L)
    nt = L // tm
    rep, mod_spec = _mod_spec(mod3, L, tm, d)

    def tok(width):
        return pl.BlockSpec((None, tm, width), lambda b, i: (b, i, 0))

    def const(a):
        return pl.BlockSpec(a.shape, lambda b, i: (0,) * a.ndim, pipeline_mode=pl.Buffered(1))

    nr = MLA_HEADS * QK_ROPE
    out_specs = [tok(SSD_INNER), tok(CONV_DIM), tok(LANES), tok(KV_RANK), tok(QK_ROPE)]
    out_shape = [SDS((nb, L, SSD_INNER), F32), SDS((nb, L, CONV_DIM), F32), SDS((nb, L, LANES), F32),
                 SDS((nb, L, KV_RANK), F32), SDS((nb, L, QK_ROPE), F32)]
    if q_transposed:
        assert tm % TQ == 0 and tk % tm == 0
        r = tk // tm
        tabq_spec = pl.BlockSpec((2 * QK_ROPE, tm), lambda b, i: (0, i))
        out_specs += [pl.BlockSpec((None, PAIRS, tm, PAIR_K), lambda b, i: (b, 0, i, 0)),
                      pl.BlockSpec((None, None, PAIRS, PAIR_V, tm), lambda b, i: (b, i // r, 0, 0, i % r)),
                      pl.BlockSpec((None, tm // TQ, PAIRS, PAIR_K, 2 * TQ), lambda b, i: (b, i, 0, 0, 0))]
        out_shape += [SDS((nb, PAIRS, L, PAIR_K), BF16), SDS((nb, L // tk, PAIRS, PAIR_V, tk), BF16),
                      SDS((nb, L // TQ, PAIRS, PAIR_K, 2 * TQ), BF16)]
        extra = [wk, wvt]
    else:
        extra = []
        tabq_spec = pl.BlockSpec((tm, 2 * nr), lambda b, i: (i, 0))
        out_specs += [tok(KV_RANK), tok(QK_ROPE), tok(C_QROPE), tok(nr)]
        out_shape += [SDS((nb, L, KV_RANK), BF16), SDS((nb, L, QK_ROPE), BF16),
                      SDS((nb, L, C_QROPE), BF16), SDS((nb, L, nr), BF16)]
    return pl.pallas_call(
        functools.partial(_inproj_body, q_transposed=q_transposed, rep=rep),
        grid=(nb, nt),
        in_specs=[tok(d), mod_spec(0), mod_spec(1), const(gmix), const(win), const(qg), const(kvg), const(wq),
                  tabq_spec, pl.BlockSpec((tm, 2 * QK_ROPE), lambda b, i: (i, 0))] + [const(a) for a in extra],
        out_specs=out_specs,
        out_shape=out_shape,
        compiler_params=_params(("arbitrary", "arbitrary")),
        name="inproj_prompt" if q_transposed else "inproj_sample",
    )(x3, mod3, mod3, gmix, win, qg, kvg, wq, tabq, tabk, *extra)


def _ssd_reset(c, xbuf, hT):
    @pl.when(c == 0)
    def _():
        xbuf[0:8, :] = jnp.zeros((8, CONV_DIM), F32)
        hT[...] = jnp.zeros(hT.shape, F32)


def _ssd_final(c, n_chunks, hfin_ref, hT):
    @pl.when(c == n_chunks - 1)
    def _():
        for k in range(SSD_HEADS // 2):
            hfin_ref[k * LANES:(k + 1) * LANES, :] = hT[k].T


def _ssd_body(*refs, **kw):
    for _ in _ssd_steps(*refs, **kw):
        pass


def _ssd_steps(*refs, Q, seg, carry, chunk_pos=None):
    if carry:
        (xin_ref, z_ref, dt_ref, cw_ref, cb_ref, dtb_ref, alog_ref, dsk_ref, g_ref,
         y_ref, hfin_ref, xbuf, hT) = refs
    else:
        (xin_ref, dt_ref, cw_ref, cb_ref, dtb_ref, alog_ref, dsk_ref,
         ypre_ref, eacs_ref, xw_ref, dec_ref, bm_ref, cm_ref) = refs
    cw = cw_ref[...]
    cb = cb_ref[...]
    if carry:
        c, n_chunks = (pl.program_id(1), pl.num_programs(1)) if chunk_pos is None else chunk_pos
        if chunk_pos is None:
            _ssd_reset(c, xbuf, hT)
        xbuf[8:8 + Q, :] = xin_ref[...]
    xc_parts = []
    for lb in range(CONV_DIM // CONV_LANES):
        cols = slice(lb * CONV_LANES, (lb + 1) * CONV_LANES)
        acc = jnp.broadcast_to(cb[:, cols], (Q, CONV_LANES))
        for k in range(CONV_WIDTH):
            if carry:
                tap = xbuf[pl.ds(8 - (CONV_WIDTH - 1) + k, Q), cols]
            else:
                tap = xin_ref[k, :, cols]
            acc = acc + cw[k:k + 1, cols] * tap
        xc_parts.append(_silu(acc))
        yield
    if carry:
        xbuf[0:8, :] = xbuf[Q:Q + 8, :]
    xc = jnp.concatenate(xc_parts, axis=1)
    xs = xc[:, :SSD_INNER]
    gs = SSD_GROUPS * SSD_STATE
    bm = xc[:, SSD_INNER:SSD_INNER + gs]
    cm = xc[:, SSD_INNER + gs:]
    bm_b = bm.astype(BF16)
    cm_b = cm.astype(BF16)

    lane = lax.broadcasted_iota(jnp.int32, (Q, LANES), 1)
    v = dt_ref[...] + dtb_ref[...]
    dt = jnp.maximum(v, 0.0) + jnp.log1p(jnp.exp(-jnp.abs(v)))
    dt = jnp.where(lane < SSD_HEADS, dt, 0.0)
    dA = dt * (-jnp.exp(alog_ref[...]))
    ri = lax.broadcasted_iota(jnp.int32, (Q, Q), 0)
    ci = lax.broadcasted_iota(jnp.int32, (Q, Q), 1)
    if seg == Q:
        mask = ci <= ri
    else:
        same = (ri // seg) == (ci // seg)
        mask = jnp.logical_and(same, ci <= ri)
    acs = _split3_dot(jnp.where(mask, 1.0, 0.0).astype(BF16), dA)
    if seg == Q:
        acs_last = acs[Q - 1:Q, :]
    else:
        acs_last = _split3_dot(jnp.where(same, 1.0, 0.0).astype(BF16), dA)
    to_end = jnp.exp(acs_last - acs) * dt
    acsT = acs.T
    dtT = dt.T
    yield

    G = [_dot_nt(cm_b[:, g * SSD_STATE:(g + 1) * SSD_STATE], bm_b[:, g * SSD_STATE:(g + 1) * SSD_STATE])
         for g in range(SSD_GROUPS)]
    lane_lo = lane < SSD_HEAD_DIM
    heads_per_group = SSD_HEADS // SSD_GROUPS
    ypairs, epairs, xwpairs, decpairs = [], [], [], []
    for k in range(SSD_HEADS // 2):
        g = (2 * k) // heads_per_group
        xp = xs[:, k * LANES:(k + 1) * LANES]
        xhalf = (jnp.where(lane_lo, xp, 0.0).astype(BF16), jnp.where(lane_lo, 0.0, xp).astype(BF16))
        yk = jnp.zeros((Q, LANES), F32)
        for s in range(2):
            hh = 2 * k + s
            segm = acs[:, hh:hh + 1] - acsT[hh:hh + 1, :]
            m = G[g] * jnp.exp(jnp.where(mask, segm, -jnp.inf)) * dtT[hh:hh + 1, :]
            yk = yk + _dot(m.astype(BF16), xhalf[s])
            yield

        def pair(a):
            return jnp.where(lane_lo[:a.shape[0]], a[:, 2 * k:2 * k + 1], a[:, 2 * k + 1:2 * k + 2])

        e_p = jnp.exp(pair(acs))
        xw = xp * pair(to_end)
        dec = jnp.exp(pair(acs_last))
        if carry:
            h_prev = hT[k]
            yk = yk + _dot(cm_b[:, g * SSD_STATE:(g + 1) * SSD_STATE], h_prev.astype(BF16)) * e_p
            hT[k] = dec * h_prev + _dot_tn(bm_b[:, g * SSD_STATE:(g + 1) * SSD_STATE], xw.astype(BF16))
        else:
            epairs.append(e_p)
            xwpairs.append(xw)
            decpairs.append(dec)
        ypairs.append(yk)
        yield
    y = jnp.concatenate(ypairs, axis=1) + dsk_ref[...] * xs
    if carry:
        y = y * _silu(z_ref[...])
        y_ref[...] = _rms(y, g_ref[...]).astype(BF16)

        if chunk_pos is None:
            _ssd_final(c, n_chunks, hfin_ref, hT)
    else:
        ypre_ref[...] = y
        eacs_ref[...] = jnp.concatenate(epairs, axis=1)
        xw_ref[...] = jnp.concatenate(xwpairs, axis=1)
        dec_ref[...] = jnp.concatenate(decpairs, axis=1)
        bm_ref[...] = bm
        cm_ref[...] = cm


def _row(a, n):
    return jnp.pad(a.reshape(1, -1).astype(F32), ((0, 0), (0, n - a.size)))


def _ssd_prompt(xbc, z, dt, cw, cb, dtb, alog, dsk, g):
    nb, L, _ = xbc.shape
    Q = SSD_CHUNK if L % SSD_CHUNK == 0 else L
    nc = L // Q

    def tok(width):
        return pl.BlockSpec((None, Q, width), lambda b, c: (b, c, 0))

    def const(a):
        return pl.BlockSpec(a.shape, lambda b, c: (0,) * a.ndim)

    consts = (cw, cb, dtb, alog, dsk, g)
    return pl.pallas_call(
        functools.partial(_ssd_body, Q=Q, seg=Q, carry=True),
        grid=(nb, nc),
        in_specs=[tok(CONV_DIM), tok(SSD_INNER), tok(LANES)] + [const(a) for a in consts],
        out_specs=[tok(SSD_INNER), pl.BlockSpec((None, SSD_INNER, SSD_STATE), lambda b, c: (b, 0, 0))],
        out_shape=[SDS((nb, L, SSD_INNER), BF16), SDS((nb, SSD_INNER, SSD_STATE), F32)],
        scratch_shapes=[pltpu.VMEM((Q + 8, CONV_DIM), F32), pltpu.VMEM((SSD_HEADS // 2, SSD_STATE, LANES), F32)],
        compiler_params=_params(("arbitrary", "arbitrary")),
        name="ssd_prompt",
    )(xbc, z, dt, *consts)


def _ssd_sample(xsh, dt, cw, cb, dtb, alog, dsk, *, seg):
    _, T, _ = xsh.shape
    consts = (cw, cb, dtb, alog, dsk)

    def full(a):
        return pl.BlockSpec(a.shape, lambda i: (0,) * a.ndim)

    outs = [SDS((T, SSD_INNER), F32)] * 4 + [SDS((T, SSD_GROUPS * SSD_STATE), F32)] * 2
    return pl.pallas_call(
        functools.partial(_ssd_body, Q=T, seg=seg, carry=False),
        grid=(1,),
        in_specs=[full(xsh), full(dt)] + [full(a) for a in consts],
        out_specs=[full(o) for o in outs],
        out_shape=outs,
        compiler_params=_params(("arbitrary",)),
        name="ssd_sample",
    )(xsh, dt, *consts)


def _sstate_body(cm_ref, bm_ref, ypre_ref, eacs_ref, xw_ref, dec_ref, z_ref, s0_ref, g_ref, y_ref, sn_ref):
    cm = cm_ref[...].astype(BF16)
    bm = bm_ref[...].astype(BF16)
    rows = SSD_INNER // SSD_GROUPS
    yo = jnp.concatenate(
        [jnp.einsum("btn,bqn->btq", cm[:, :, g * SSD_STATE:(g + 1) * SSD_STATE],
                    s0_ref[:, g * rows:(g + 1) * rows, :].astype(BF16), preferred_element_type=F32)
         for g in range(SSD_GROUPS)], axis=-1)
    y = (ypre_ref[...] + yo * eacs_ref[...]) * _silu(z_ref[...])
    y_ref[...] = _rms(y, g_ref[...]).astype(BF16)
    dec = dec_ref[...]
    hi = dec.astype(BF16)
    lo = (dec - hi.astype(F32)).astype(BF16)
    sel = jnp.where(lax.broadcasted_iota(jnp.int32, (dec.shape[0], dec.shape[1], SSD_STATE), 1) == 0,
                    1.0, 0.0).astype(BF16)
    xw = xw_ref[...].astype(BF16)
    for g in range(SSD_GROUPS):
        rs = slice(g * rows, (g + 1) * rows)
        dmat = (jnp.einsum("bjq,bjn->bqn", hi[:, :, rs], sel, preferred_element_type=F32)
                + jnp.einsum("bjq,bjn->bqn", lo[:, :, rs], sel, preferred_element_type=F32))
        upd = jnp.einsum("bjq,bjn->bqn", xw[:, :, rs], bm[:, :, g * SSD_STATE:(g + 1) * SSD_STATE],
                         preferred_element_type=F32)
        sn_ref[:, rs, :] = dmat * s0_ref[:, rs, :] + upd


def _sstate(cm, bm, ypre, eacs, xw, dec, z, s0, g, *, bs):
    nseq, t, _ = cm.shape
    bs = min(bs, nseq)

    def blk(a):
        return pl.BlockSpec((bs,) + a.shape[1:], lambda i: (i, 0, 0))

    ins = (cm, bm, ypre, eacs, xw, dec, z, s0)
    return pl.pallas_call(
        _sstate_body,
        grid=(nseq // bs,),
        in_specs=[blk(a) for a in ins] + [pl.BlockSpec(g.shape, lambda i: (0, 0))],
        out_specs=[blk(ypre), blk(s0)],
        out_shape=[SDS(ypre.shape, BF16), SDS(s0.shape, F32)],
        compiler_params=_params(("arbitrary",)),
        name="sstate",
    )(*ins, g)


def _attn_body(qt_ref, qn_ref, kp_ref, vt_ref, g_ref, o_ref, m_sc, acc_sc,
               s_a, s_b, s_c, smax_a, smax_b, smax_c, *, tk):
    i = pl.program_id(1)
    ncols = MLA_HEADS * TQ
    m_sc[...] = jnp.full(m_sc.shape, -jnp.inf, F32)
    acc_sc[...] = jnp.zeros(acc_sc.shape, F32)
    ct = 2 * TQ
    tiles = [slice(p * ct, (p + 1) * ct) for p in range(PAIRS)]

    def scores(j, s_ref, smax_ref, q_ref=qt_ref):
        k0 = pl.multiple_of(j * tk, tk)
        for p, cs in enumerate(tiles):
            s = _dot(kp_ref[p, pl.ds(k0, tk), :], q_ref[p])
            s_ref[:, cs] = s
            smax_ref[:, cs] = jnp.max(s, axis=0, keepdims=True)

    def softmax_pv(j, s_ref, smax_ref, masked):
        for p, cs in enumerate(tiles):
            s = s_ref[:, cs]
            if masked:
                key = j * tk + lax.broadcasted_iota(jnp.int32, (tk, ct), 0)
                tok = i * TQ + (lax.broadcasted_iota(jnp.int32, (tk, ct), 1) & (TQ - 1))
                s = jnp.where(key <= tok, s, -jnp.inf)
                smax = jnp.max(s, axis=0, keepdims=True)
            else:
                smax = smax_ref[:, cs]
            m_prev = m_sc[:, cs]
            m_new = jnp.maximum(m_prev, smax)
            alpha = jnp.exp2(m_prev - m_new)
            e = jnp.exp2(s - m_new)
            acc_sc[p] = alpha * acc_sc[p] + _dot(vt_ref[j, p], e.astype(BF16))
            m_sc[:, cs] = m_new

    nfull = (i * TQ) // tk

    @pl.when(i == 0)
    def _():
        scores(0, s_c, smax_c)

    def prefetch():
        scores(0, s_c, smax_c, qn_ref)

    @pl.when(nfull == 0)
    def _():
        softmax_pv(0, s_c, smax_c, True)
        prefetch()

    @pl.when(nfull >= 1)
    def _():
        scores(1, s_b, smax_b)
        softmax_pv(0, s_c, smax_c, False)

    def pair(p, carry):
        j = 2 * p + 1
        scores(j + 1, s_a, smax_a)
        softmax_pv(j, s_b, smax_b, False)
        scores(j + 2, s_b, smax_b)
        softmax_pv(j + 1, s_a, smax_a, False)
        return carry

    lax.fori_loop(0, jnp.maximum(nfull - 1, 0) // 2, pair, 0)
    odd = (nfull % 2) == 1

    @pl.when(odd)
    def _():
        prefetch()
        softmax_pv(nfull, s_b, smax_b, True)

    @pl.when(jnp.logical_and(jnp.logical_not(odd), nfull >= 2))
    def _():
        scores(nfull, s_a, smax_a)
        softmax_pv(nfull - 1, s_b, smax_b, False)
        prefetch()
        softmax_pv(nfull, s_a, smax_a, True)

    ys = []
    for hh in range(MLA_HEADS):
        p, s = divmod(hh, 2)
        cols = slice(s * TQ, (s + 1) * TQ)
        denom = acc_sc[p, 2 * V_DIM:2 * V_DIM + 1, cols]
        ys.append(acc_sc[p, s * V_DIM:(s + 1) * V_DIM, cols] * (1.0 / denom))
    yt = jnp.concatenate(ys, axis=0)
    yt = yt * lax.rsqrt(jnp.mean(yt * yt, axis=0, keepdims=True) + EPS)
    o_ref[...] = (yt.T * g_ref[...]).astype(BF16)


def _attn_prompt(qt, kp, vt, g, *, tk):
    nb, nq = qt.shape[:2]
    L = nq * TQ
    ncols = MLA_HEADS * TQ
    assert TQ & (TQ - 1) == 0 and L % tk == 0 and tk % TQ == 0

    def q_spec(index):
        return pl.BlockSpec((None, None, PAIRS, PAIR_K, 2 * TQ), lambda b, i: (b, index(i), 0, 0, 0))

    return pl.pallas_call(
        functools.partial(_attn_body, tk=tk),
        grid=(nb, nq),
        in_specs=[q_spec(lambda i: i), q_spec(lambda i: jnp.minimum(i + 1, nq - 1)),
                  pl.BlockSpec((None, PAIRS, L, PAIR_K), lambda b, i: (b, 0, 0, 0), pipeline_mode=pl.Buffered(1)),
                  pl.BlockSpec((None, L // tk, PAIRS, PAIR_V, tk), lambda b, i: (b, 0, 0, 0, 0),
                               pipeline_mode=pl.Buffered(1)),
                  pl.BlockSpec(g.shape, lambda b, i: (0, 0))],
        out_specs=pl.BlockSpec((None, TQ, MLA_INNER), lambda b, i: (b, i, 0)),
        out_shape=SDS((nb, L, MLA_INNER), BF16),
        scratch_shapes=[pltpu.VMEM((1, ncols), F32),
                        pltpu.VMEM((PAIRS, PAIR_V, 2 * TQ), F32),
                        pltpu.VMEM((tk, ncols), F32), pltpu.VMEM((tk, ncols), F32), pltpu.VMEM((tk, ncols), F32),
                        pltpu.VMEM((1, ncols), F32), pltpu.VMEM((1, ncols), F32), pltpu.VMEM((1, ncols), F32)],
        compiler_params=_params(("arbitrary", "arbitrary")),
        name="attn_prompt",
    )(qt, qt, kp, vt, g)


def _sattn_body(pt_ref, qa_ref, qr_ref, kn_ref, rn_ref, ckv_hbm, krt_hbm, o_ref, kbuf, rbuf, sem, *, npages, t_new):
    slot = _sattn_fetch(pt_ref, ckv_hbm, krt_hbm, kbuf, rbuf, sem, npages=npages)
    for _ in _sattn_steps(slot, qa_ref, qr_ref, kn_ref, rn_ref, o_ref, kbuf, rbuf, npages=npages, t_new=t_new,
                          split=SPLIT):
        pass


def _sattn_fetch(pt_ref, ckv_hbm, krt_hbm, kbuf, rbuf, sem, *, npages):
    b = pl.program_id(0)
    nseq = pl.num_programs(0)

    def copies(seq_page, slot, p):
        off = p * PAGE_SIZE
        return (pltpu.make_async_copy(ckv_hbm.at[seq_page], kbuf.at[slot, p], sem.at[0, slot]),
                pltpu.make_async_copy(krt_hbm.at[seq_page], rbuf.at[slot, :, pl.ds(off, PAGE_SIZE)],
                                      sem.at[1, slot]))

    def start_fetch(seq, slot):
        for p in range(npages):
            for cp in copies(pt_ref[seq * npages + p], slot, p):
                cp.start()

    def wait_fetch(slot):
        pltpu.make_async_copy(ckv_hbm.at[pl.ds(0, npages)], kbuf.at[slot], sem.at[0, slot]).wait()
        pltpu.make_async_copy(rbuf.at[slot], rbuf.at[slot], sem.at[1, slot]).wait()

    @pl.when(b == 0)
    def _():
        start_fetch(0, 0)

    @pl.when(b + 1 < nseq)
    def _():
        start_fetch(b + 1, (b + 1) % 2)

    slot = b % 2
    wait_fetch(slot)
    return slot


def _sattn_steps(slot, qa_ref, qr_ref, kn_ref, rn_ref, o_ref, kbuf, rbuf, *, npages, t_new, split, anchor=None):
    split = math.gcd(split, npages)
    grp = math.gcd(SPLIT, split)
    pg = npages // split
    part = pg * PAGE_SIZE
    q = qa_ref[...]
    qr = qr_ref[...]
    kn = kn_ref[...]
    rn = rn_ref[...]
    kps, s_p = [], []
    for h0 in range(0, split, grp):
        hs = range(h0, h0 + grp)
        kps += [kbuf[slot, pl.ds(h * pg, pg)].reshape(part, KV_RANK).astype(BF16) for h in hs]
        s_main = [_dot_nt(q, kps[h]) for h in hs]
        s_rope = [_dot(qr, rbuf[slot, :, h * part:(h + 1) * part].astype(BF16)) for h in hs]
        s_p += [a + b for a, b in zip(s_main, s_rope)]
        yield
    s_n = _dot_nt(q, kn) + _dot_nt(qr, rn)
    rows = t_new * MLA_HEADS
    tok = lax.broadcasted_iota(jnp.int32, (rows, t_new), 0) // MLA_HEADS
    col = lax.broadcasted_iota(jnp.int32, (rows, t_new), 1)
    s_n = jnp.where(col <= tok, s_n, -jnp.inf)
    m = jnp.max(s_n, axis=-1, keepdims=True)
    for s in s_p:
        m = jnp.maximum(m, jnp.max(s, axis=-1, keepdims=True))
    if anchor is not None:
        bits = pltpu.bitcast(anchor[0:16, 0:LANES].astype(F32), jnp.uint32)
        m = m + ((bits >> 16) >> 16).astype(F32)[0:1, 0:1]
    pn = jnp.exp(s_n - m)
    pp = [jnp.exp(s - m) for s in s_p]
    l = jnp.sum(pn, axis=-1, keepdims=True)
    for p in pp:
        l = l + jnp.sum(p, axis=-1, keepdims=True)
    inv = 1.0 / l
    o = _dot(pn.astype(BF16), kn) * inv
    yield
    for h0 in range(0, split, grp):
        parts = [_dot(pp[h].astype(BF16), kps[h]) for h in range(h0, h0 + grp)]
        for part_o in parts:
            o = o + part_o * inv
        yield
    o_ref[...] = o


def _sattn(page_table, qa, qr, kn, rn, cache_kv, cache_krt):
    nseq, npages = page_table.shape
    t_new = kn.shape[1]
    rows = t_new * MLA_HEADS
    grid_spec = pltpu.PrefetchScalarGridSpec(
        num_scalar_prefetch=1,
        grid=(nseq,),
        in_specs=[pl.BlockSpec((rows, KV_RANK), lambda b, pt: (b, 0)),
                  pl.BlockSpec((rows, QK_ROPE), lambda b, pt: (b, 0)),
                  pl.BlockSpec((None, t_new, KV_RANK), lambda b, pt: (b, 0, 0)),
                  pl.BlockSpec((None, t_new, QK_ROPE), lambda b, pt: (b, 0, 0)),
                  pl.BlockSpec(memory_space=pl.ANY),
                  pl.BlockSpec(memory_space=pl.ANY)],
        out_specs=pl.BlockSpec((rows, KV_RANK), lambda b, pt: (b, 0)),
        scratch_shapes=[pltpu.VMEM((2, npages, PAGE_SIZE, KV_RANK), F32),
                        pltpu.VMEM((2, QK_ROPE, npages * PAGE_SIZE), F32),
                        pltpu.SemaphoreType.DMA((2, 2))],
    )
    return pl.pallas_call(
        functools.partial(_sattn_body, npages=npages, t_new=t_new),
        grid_spec=grid_spec,
        out_shape=SDS((nseq * rows, KV_RANK), F32),
        compiler_params=_params(("arbitrary",)),
        name="attn_sample",
    )(page_table.reshape(-1), qa, qr, kn, rn, cache_kv, cache_krt)


def _ssd_sattn_body(pt_ref, qa_ref, qr_ref, kn_ref, rn_ref, ckv_hbm, krt_hbm,
                    xin_ref, z_ref, dt_ref, cw_ref, cb_ref, dtb_ref, alog_ref, dsk_ref, g_ref,
                    o_ref, y_ref, hfin_ref, kbuf, rbuf, sem, xbuf, hT, *, npages, t_new, Q, n_chunks):
    c = pl.program_id(0) % n_chunks
    _ssd_reset(c, xbuf, hT)
    slot = _sattn_fetch(pt_ref, ckv_hbm, krt_hbm, kbuf, rbuf, sem, npages=npages)
    ssd = _ssd_steps(xin_ref, z_ref, dt_ref, cw_ref, cb_ref, dtb_ref, alog_ref, dsk_ref, g_ref, y_ref, hfin_ref,
                     xbuf, hT, Q=Q, seg=Q, carry=True, chunk_pos=(c, n_chunks))
    att = _sattn_steps(slot, qa_ref, qr_ref, kn_ref, rn_ref, o_ref, kbuf, rbuf, npages=npages, t_new=t_new,
                       split=FUSED_SPLIT, anchor=y_ref)
    done = object()
    for _ in range(math.gcd(FUSED_SPLIT, npages) // math.gcd(SPLIT, FUSED_SPLIT, npages)):
        next(att)
        next(ssd, done)
    for _ in ssd:
        pass
    for _ in att:
        pass
    _ssd_final(c, n_chunks, hfin_ref, hT)


def _ssd_sattn(page_table, qa, qr, kn, rn, cache_kv, cache_krt, xbc, z, dt, cw, cb, dtb, alog, dsk, g):
    nseq, npages = page_table.shape
    t_new = kn.shape[1]
    rows = t_new * MLA_HEADS
    nb, L, _ = xbc.shape
    Q = SSD_CHUNK if L % SSD_CHUNK == 0 else L
    nc = L // Q
    assert nseq == nb * nc
    consts = (cw, cb, dtb, alog, dsk, g)

    def tok(width):
        return pl.BlockSpec((None, Q, width), lambda s, pt: (s // nc, s % nc, 0))

    def const(a):
        return pl.BlockSpec(a.shape, lambda s, pt: (0,) * a.ndim)

    grid_spec = pltpu.PrefetchScalarGridSpec(
        num_scalar_prefetch=1,
        grid=(nseq,),
        in_specs=[pl.BlockSpec((rows, KV_RANK), lambda s, pt: (s, 0)),
                  pl.BlockSpec((rows, QK_ROPE), lambda s, pt: (s, 0)),
                  pl.BlockSpec((None, t_new, KV_RANK), lambda s, pt: (s, 0, 0)),
                  pl.BlockSpec((None, t_new, QK_ROPE), lambda s, pt: (s, 0, 0)),
                  pl.BlockSpec(memory_space=pl.ANY),
                  pl.BlockSpec(memory_space=pl.ANY),
                  tok(CONV_DIM), tok(SSD_INNER), tok(LANES)] + [const(a) for a in consts],
        out_specs=[pl.BlockSpec((rows, KV_RANK), lambda s, pt: (s, 0)),
                   tok(SSD_INNER),
                   pl.BlockSpec((None, SSD_INNER, SSD_STATE), lambda s, pt: (s // nc, 0, 0))],
        scratch_shapes=[pltpu.VMEM((2, npages, PAGE_SIZE, KV_RANK), F32),
                        pltpu.VMEM((2, QK_ROPE, npages * PAGE_SIZE), F32),
                        pltpu.SemaphoreType.DMA((2, 2)),
                        pltpu.VMEM((Q + 8, CONV_DIM), F32),
                        pltpu.VMEM((SSD_HEADS // 2, SSD_STATE, LANES), F32)],
    )
    return pl.pallas_call(
        functools.partial(_ssd_sattn_body, npages=npages, t_new=t_new, Q=Q, n_chunks=nc),
        grid_spec=grid_spec,
        out_shape=[SDS((nseq * rows, KV_RANK), F32), SDS((nb, L, SSD_INNER), BF16),
                   SDS((nb, SSD_INNER, SSD_STATE), F32)],
        compiler_params=_params(("arbitrary",)),
        name="ssd_prompt_attn_sample",
    )(page_table.reshape(-1), qa, qr, kn, rn, cache_kv, cache_krt, xbc, z, dt, *consts)


def _apost_body(o_ref, wuv_ref, g_ref, y_ref):
    tm = o_ref.shape[0]
    o = jnp.zeros((tm, MLA_INNER), F32)
    for hh in range(MLA_HEADS):
        o = o + _dot(o_ref[:, hh * KV_RANK:(hh + 1) * KV_RANK].astype(BF16), wuv_ref[hh])
    y_ref[...] = _rms(o, g_ref[...]).astype(BF16)


def _apost(o_lat, wuv_pad, g):
    T = o_lat.shape[0]
    return pl.pallas_call(
        _apost_body,
        grid=(1,),
        in_specs=[pl.BlockSpec(o_lat.shape, lambda i: (0, 0)),
                  pl.BlockSpec(wuv_pad.shape, lambda i: (0, 0, 0)),
                  pl.BlockSpec(g.shape, lambda i: (0, 0))],
        out_specs=pl.BlockSpec((T, MLA_INNER), lambda i: (0, 0)),
        out_shape=SDS((T, MLA_INNER), BF16),
        compiler_params=_params(("arbitrary",)),
        name="attn_post",
    )(o_lat, wuv_pad, g)


def _mlp_body(x_ref, ys_ref, ya_ref, g1_ref, sh2_ref, sc2_ref, g2_ref, shf_ref, scf_ref,
              wout_ref, gmlp_ref, wup_ref, wdn_ref, gfin_ref, o_ref, *, final, tf, rep):
    yy = jnp.concatenate([ys_ref[...], ya_ref[...]], axis=-1)
    x1 = x_ref[...] + _mod_rows(g1_ref, rep) * _dot(yy, wout_ref[...])
    h2 = (_rms(x1, gmlp_ref[...]) * (1.0 + _mod_rows(sc2_ref, rep)) + _mod_rows(sh2_ref, rep)).astype(BF16)
    acc = jnp.zeros(x1.shape, F32)
    for c in range(D_FF // tf):
        u = jnp.maximum(_dot(h2, wup_ref[:, c * tf:(c + 1) * tf]), 0.0)
        acc = acc + _dot((u * u).astype(BF16), wdn_ref[c * tf:(c + 1) * tf, :])
    x2 = x1 + _mod_rows(g2_ref, rep) * acc
    if final:
        x2 = _rms(x2, gfin_ref[...]) * (1.0 + _mod_rows(scf_ref, rep)) + _mod_rows(shf_ref, rep)
    o_ref[...] = x2


def _mlp(x3, ys, ya, mod3, wout, gmlp, wup, wdn, gfin, *, final, tm, tf):
    nb, L, d = x3.shape
    tm = min(tm, L)
    rep, mod_spec = _mod_spec(mod3, L, tm, d)

    def tok(width):
        return pl.BlockSpec((None, tm, width), lambda b, i: (b, i, 0))

    def const(a):
        return pl.BlockSpec(a.shape, lambda b, i: (0,) * a.ndim, pipeline_mode=pl.Buffered(1))

    return pl.pallas_call(
        functools.partial(_mlp_body, final=final, tf=tf, rep=rep),
        grid=(nb, L // tm),
        in_specs=[tok(d), tok(SSD_INNER), tok(MLA_INNER)] + [mod_spec(k) for k in (2, 3, 4, 5, 6, 7)]
                 + [const(wout), const(gmlp), const(wup), const(wdn), const(gfin)],
        out_specs=tok(d),
        out_shape=SDS((nb, L, d), F32),
        compiler_params=_params(("arbitrary", "arbitrary")),
        name="mlp",
    )(x3, ys, ya, mod3, mod3, mod3, mod3, mod3, mod3, wout, gmlp, wup, wdn, gfin)


def _rope_tables(pos):
    inv = 1.0 / (ROPE_THETA ** (np.arange(0, QK_ROPE, 2, dtype=np.float64) / QK_ROPE))
    ang = pos.astype(np.float64)[:, None] * inv[None, :]
    cos, sin = np.cos(ang).astype(np.float32), np.sin(ang).astype(np.float32)
    c32 = np.concatenate([cos, cos], axis=-1)
    s32 = np.concatenate([-sin, sin], axis=-1)
    tab1 = np.concatenate([c32, s32], axis=-1)
    tabq = np.concatenate([np.tile(c32, (1, MLA_HEADS)), np.tile(s32, (1, MLA_HEADS))], axis=-1)
    return tab1, tabq


def _swap_halves(w):
    half = w.shape[-1] // 2
    return jnp.concatenate([w[..., half:], w[..., :half]], axis=-1)


def kernel(x_prompt, x_sample, cache_kv_latent, cache_k_rope, state_conv, state_ssm, page_table,
           c_prompt, c_sample, w_ada, b_ada, norm_mix_g, w_in, conv_w, conv_b, dt_bias, a_log,
           d_skip, norm_ssd_g, q_norm_g, kv_norm_g, w_uq, w_uk, w_uv, norm_attn_g, w_out,
           norm_mlp_g, w_up, w_down, w_ada_final, b_ada_final, norm_final_g):
    depth = w_in.shape[0]
    b_p, seq, d = x_prompt.shape
    n_seq, t_new, _ = x_sample.shape
    n_tok_s = n_seq * t_new
    past_len = page_table.shape[1] * PAGE_SIZE

    c_all = jnp.concatenate([c_prompt, c_sample], axis=0)
    ada_fin = _ada(c_all, w_ada_final, b_ada_final)
    tab1_p, _ = _rope_tables(np.arange(seq))
    tab1_s, tabq_s = _rope_tables(past_len + np.arange(t_new))
    tab1_pt = jnp.asarray(np.ascontiguousarray(tab1_p.T))
    tab1_p = jnp.asarray(tab1_p)
    tabq_s = jnp.asarray(np.tile(tabq_s, (n_seq, 1)))
    tab1_s = jnp.asarray(np.tile(tab1_s, (n_seq, 1)))

    xp = x_prompt
    xs = x_sample.reshape(1, n_tok_s, d)
    outs_p, outs_s = [], []
    for l in range(depth):
        final = l == depth - 1
        wi = w_in[l]
        c1 = SSD_INNER
        c2 = c1 + CONV_DIM
        c3 = c2 + SSD_HEADS
        c4 = c3 + Q_RANK
        c5 = c4 + KV_RANK
        w_kr = wi[:, c5:]
        win = jnp.concatenate(
            [wi[:, :c2], wi[:, c3:c5], wi[:, c2:c3], jnp.zeros((d, MISC_KR - SSD_HEADS), F32), w_kr,
             _swap_halves(w_kr), jnp.zeros((d, LANES - MISC_KRSW - QK_ROPE), F32)], axis=1).astype(BF16)
        wq_h = w_uq[l].reshape(Q_RANK, MLA_HEADS, QK_NOPE + QK_ROPE)
        w_rope = wq_h[:, :, QK_NOPE:]
        wfold = _fold(jnp.transpose(wq_h, (1, 0, 2)), jnp.transpose(w_uk[l], (1, 0, 2)))
        wq = jnp.concatenate([wfold, w_rope.reshape(Q_RANK, -1).astype(BF16),
                              _swap_halves(w_rope).reshape(Q_RANK, -1).astype(BF16)], axis=1)
        wuv_pad = jnp.zeros((MLA_HEADS, KV_RANK, MLA_HEADS, V_DIM), F32)
        wuv_pad = wuv_pad.at[jnp.arange(MLA_HEADS), :, jnp.arange(MLA_HEADS), :].set(
            jnp.transpose(w_uv[l], (1, 0, 2)))
        wuv_pad = wuv_pad.reshape(MLA_HEADS, KV_RANK, MLA_INNER).astype(BF16)
        wq_p = jnp.concatenate([wq_h[:, :, :QK_NOPE].reshape(Q_RANK, -1), w_rope.reshape(Q_RANK, -1),
                                _swap_halves(w_rope).reshape(Q_RANK, -1)], axis=1).T.astype(BF16)
        wk = w_uk[l].reshape(KV_RANK, MLA_HEADS * QK_NOPE).astype(BF16)
        wvt = jnp.transpose(w_uv[l], (1, 2, 0)).reshape(PAIRS, 2 * V_DIM, KV_RANK).astype(BF16)
        wout = w_out[l].astype(BF16)
        wup = w_up[l].astype(BF16)
        wdn = w_down[l].astype(BF16)
        gmix = norm_mix_g[l].reshape(1, d)
        gmlp = norm_mlp_g[l].reshape(1, d)
        gfin = norm_final_g.reshape(1, d)
        qg = q_norm_g[l].reshape(1, Q_RANK)
        kvg = kv_norm_g[l].reshape(1, KV_RANK)
        gssd = norm_ssd_g[l].reshape(1, SSD_INNER)
        gattn = norm_attn_g[l].reshape(1, MLA_INNER)
        cw = conv_w[l]
        cb = conv_b[l].reshape(1, CONV_DIM)
        dtb = _row(dt_bias[l], LANES)
        alog = _row(a_log[l], LANES)
        dsk = jnp.repeat(d_skip[l].astype(F32), SSD_HEAD_DIM).reshape(1, SSD_INNER)

        ada = _ada(c_all, w_ada[l], b_ada[l])
        mod = jnp.concatenate([ada, ada_fin], axis=1)
        mod_p = mod[:b_p].reshape(b_p, 1, 8 * d)
        mod_s = mod[b_p:].reshape(1, n_seq, 8 * d)

        def seqs(a):
            return a.reshape(n_seq, t_new, a.shape[-1])

        tk = min(TK_ATTN, seq)
        z_p, xbc_p, dtr_p, ckv_p, kr_p, kp, vt, qt = _inproj(
            xp, mod_p, gmix, win, qg, kvg, wq_p, tab1_pt, tab1_p, q_transposed=True, tm=TM_INPROJ_PROMPT, tk=tk,
            wk=wk, wvt=wvt)
        z_s, xbc_s, dtr_s, ckv_s, kr_s, kc, krb, qa, qr = _inproj(
            xs, mod_s, gmix, win, qg, kvg, wq, tabq_s, tab1_s, q_transposed=False, tm=TM_INPROJ_SAMPLE)
        xpad = jnp.concatenate([state_conv[l], xbc_s.reshape(n_seq, t_new, CONV_DIM)], axis=1)
        xsh = jnp.stack([xpad[:, k:k + t_new].reshape(n_tok_s, CONV_DIM) for k in range(CONV_WIDTH)])
        ypre, eacs, xw, dec, bm, cm = _ssd_sample(xsh, dtr_s[0], cw, cb, dtb, alog, dsk, seg=t_new)
        y_ssd_s, s_new = _sstate(seqs(cm), seqs(bm), seqs(ypre), seqs(eacs), seqs(xw), seqs(dec), seqs(z_s[0]),
                                 state_ssm[l].reshape(n_seq, SSD_INNER, SSD_STATE), gssd, bs=BS_STATE)

        sattn_args = (page_table, qa.reshape(n_tok_s * MLA_HEADS, KV_RANK),
                      qr.reshape(n_tok_s * MLA_HEADS, QK_ROPE), seqs(kc[0]), seqs(krb[0]),
                      cache_kv_latent[l], jnp.swapaxes(cache_k_rope[l], -1, -2))
        ssd_args = (xbc_p, z_p, dtr_p, cw, cb, dtb, alog, dsk, gssd)
        chunk = SSD_CHUNK if seq % SSD_CHUNK == 0 else seq
        if n_seq == b_p * (seq // chunk):
            o_lat, y_ssd_p, hfin = _ssd_sattn(*sattn_args, *ssd_args)
        else:
            y_ssd_p, hfin = _ssd_prompt(*ssd_args)
            o_lat = _sattn(*sattn_args)

        y_attn_p = _attn_prompt(qt, kp, vt, gattn, tk=tk)
        xp = _mlp(xp, y_ssd_p, y_attn_p, mod_p, wout, gmlp, wup, wdn, gfin, final=final, tm=TM_MLP, tf=TF_MLP)
        tail = min(seq, CONV_WIDTH - 1)
        conv_tail = jnp.concatenate([jnp.zeros((b_p, CONV_WIDTH - 1 - tail, CONV_DIM), F32),
                                     xbc_p[:, seq - tail:]], axis=1)
        outs_p.append((ckv_p, kr_p, conv_tail, hfin.reshape(b_p, SSD_HEADS, SSD_HEAD_DIM, SSD_STATE)))

        y_attn_s = _apost(o_lat.reshape(n_tok_s, MLA_HEADS * KV_RANK), wuv_pad, gattn)
        xs = _mlp(xs, y_ssd_s.reshape(1, n_tok_s, SSD_INNER), y_attn_s.reshape(1, n_tok_s, MLA_INNER), mod_s,
                  wout, gmlp, wup, wdn, gfin, final=final, tm=TM_MLP, tf=TF_MLP)
        outs_s.append((seqs(ckv_s[0]), seqs(kr_s[0]), xpad[:, t_new:],
                       s_new.reshape(n_seq, SSD_HEADS, SSD_HEAD_DIM, SSD_STATE)))

    def stack(outs, k):
        return jnp.stack([o[k] for o in outs])

    return (xp, xs.reshape(n_seq, t_new, d),
            stack(outs_p, 0), stack(outs_p, 1), stack(outs_p, 2), stack(outs_p, 3),
            stack(outs_s, 0), stack(outs_s, 1), stack(outs_s, 2), stack(outs_s, 3))
```
